```python
import numpy as np
import jax
import jax.numpy as jnp
from jax import lax

D_MODEL = 2048
BATCH = 4
SEQ = 4096
DEPTH = 2

HEAD_DIM = 128
MIX_WIDTH = D_MODEL
NSA_WIDTH = MIX_WIDTH // 2
GLA_WIDTH = MIX_WIDTH - NSA_WIDTH
NSA_HEADS = NSA_WIDTH // HEAD_DIM
NSA_KV_HEADS = NSA_HEADS // 4
NSA_GROUP = NSA_HEADS // NSA_KV_HEADS
CMP_BLOCK = 32
CMP_STRIDE = 16
CMP_HIDDEN = 256
SLC_BLOCK = 64
SLC_TOPN = 16
WINDOW = 512
NSA_QBLOCK = 64
FORCE_SCORE = 1e9
GLA_DV = 256
GLA_HEADS = GLA_WIDTH // GLA_DV
GLA_DK = GLA_DV // 2
GLA_GATE_RANK = 16
GLA_GATE_NORM = 16.0
GLA_CHUNK = 64
ROPE_THETA = 500000.0
ROPE_DIMS = HEAD_DIM // 4
D_FF = 256 * ((8 * D_MODEL // 3 + 255) // 256)
N_EXPERTS = 8
TOP_K = 2
D_FF_EXPERT = 7 * D_MODEL // 2
MOE_BLOCK = 512
N_DENSE = (DEPTH + 1) // 2
N_MOE = DEPTH // 2
LN_EPS = 1e-5
NORM_EPS = 1e-6
DEEPNORM_ALPHA = (2 * DEPTH) ** 0.25
DEEPNORM_BETA = (8 * DEPTH) ** -0.25
ADA_INIT_SCALE = 0.1
SPLIT_SIZES = (NSA_HEADS * HEAD_DIM,) + (NSA_KV_HEADS * HEAD_DIM,) * 6 + (NSA_HEADS * 3, GLA_HEADS * GLA_DK, GLA_HEADS * GLA_DK, GLA_HEADS * GLA_DV, GLA_GATE_RANK, GLA_HEADS * GLA_DV)
IN_WIDTH = sum(SPLIT_SIZES)

kernel_name = 'hybrid_nsa_gla_moe_deepnorm_adaln'


def layer_norm(x, g, b):
    xf = x.astype(jnp.float32)
    mu = jnp.mean(xf, axis=-1, keepdims=True)
    var = jnp.mean(jnp.square(xf - mu), axis=-1, keepdims=True)
    return ((xf - mu) * lax.rsqrt(var + LN_EPS) * g + b).astype(x.dtype)


def rotary_partial(x, pos):
    half = ROPE_DIMS // 2
    inv = ROPE_THETA ** (-jnp.arange(half, dtype=jnp.float32) * 2.0 / ROPE_DIMS)
    ang = pos.astype(jnp.float32)[:, None] * inv[None, :]
    cos = jnp.cos(ang)[:, None, :]
    sin = jnp.sin(ang)[:, None, :]
    x1 = x[..., :half].astype(jnp.float32)
    x2 = x[..., half:ROPE_DIMS].astype(jnp.float32)
    r1 = (x1 * cos - x2 * sin).astype(x.dtype)
    r2 = (x2 * cos + x1 * sin).astype(x.dtype)
    return jnp.concatenate([r1, r2, x[..., ROPE_DIMS:]], axis=-1)


def masked_softmax(s, mask):
    s = jnp.where(mask, s.astype(jnp.float32), -1e30)
    p = jax.nn.softmax(s, axis=-1)
    return jnp.where(mask, p, 0.0)


def nsa_attention(q, k_cmp, v_cmp, k_slc, v_slc, k_win, v_win, gates,
                  cmp_pos_k, cmp_w1_k, cmp_w2_k, cmp_pos_v, cmp_w1_v, cmp_w2_v):
    B, T = q.shape[0], q.shape[1]
    dt = q.dtype
    Hkv, G, Dh, QB = NSA_KV_HEADS, NSA_GROUP, HEAD_DIM, NSA_QBLOCK
    pos = jnp.arange(T)
    scale = HEAD_DIM ** -0.5

    n_cmp = (T - CMP_BLOCK) // CMP_STRIDE + 1
    cmp_idx = jnp.arange(n_cmp)[:, None] * CMP_STRIDE + jnp.arange(CMP_BLOCK)[None, :]

    def compress(a, pe, w1, w2):
        blocks = a[:, cmp_idx] + pe[None, None, :, None, :]
        blocks = blocks.transpose(0, 1, 3, 2, 4).reshape(B, n_cmp, Hkv, CMP_BLOCK * Dh)
        return jax.nn.gelu(blocks @ w1) @ w2

    kc = compress(k_cmp, cmp_pos_k, cmp_w1_k, cmp_w2_k)
    vc = compress(v_cmp, cmp_pos_v, cmp_w1_v, cmp_w2_v)
    cmp_start = jnp.arange(n_cmp) * CMP_STRIDE
    cmp_end = cmp_start + CMP_BLOCK - 1

    n_slc = T // SLC_BLOCK
    n_sel = min(SLC_TOPN, n_slc)
    blk_start = jnp.arange(n_slc) * SLC_BLOCK
    overlap = ((cmp_start[:, None] < blk_start[None, :] + SLC_BLOCK)
               & (cmp_start[:, None] + CMP_BLOCK > blk_start[None, :])).astype(jnp.float32)

    q_rot = rotary_partial(q, pos) * scale
    q_nope = q * scale
    k_blocks = rotary_partial(k_slc, pos).reshape(B, n_slc, SLC_BLOCK, Hkv, Dh).transpose(0, 3, 1, 2, 4)
    v_blocks = v_slc.reshape(B, n_slc, SLC_BLOCK, Hkv, Dh).transpose(0, 3, 1, 2, 4)
    pad = ((0, 0), (WINDOW, 0), (0, 0), (0, 0))
    kw = jnp.pad(rotary_partial(k_win, pos), pad)
    vw = jnp.pad(v_win, pad)
    gate = jax.nn.sigmoid(gates.astype(jnp.float32))

    n_qb = T // QB
    bi = jnp.arange(B)[:, None, None, None]
    hi = jnp.arange(Hkv)[None, :, None, None]
    j_blk = jnp.arange(n_slc)

    def to_blocks(a):
        return a.reshape((B, n_qb, QB) + a.shape[2:]).swapaxes(0, 1)

    def block_fn(args):
        i, qr, qn, g = args
        t = i * QB + jnp.arange(QB)
        qr = qr.reshape(B, QB, Hkv, G, Dh)
        qn = qn.reshape(B, QB, Hkv, G, Dh)
        s_c = jnp.einsum('bqhgd,bchd->bhgqc', qn, kc)
        p_c = masked_softmax(s_c, cmp_end[None, :] <= t[:, None])
        o_c = jnp.einsum('bhgqc,bchd->bqhgd', p_c.astype(dt), vc)
        imp = jnp.einsum('bhgqc,cn->bhqn', p_c, overlap)
        cur = t // SLC_BLOCK
        forced = (j_blk[None, :] == 0) | (j_blk[None, :] == cur[:, None]) | (j_blk[None, :] == cur[:, None] - 1)
        future = blk_start[None, :] > t[:, None]
        imp = jnp.where(future, -jnp.inf, jnp.where(forced, FORCE_SCORE, imp))
        _, sel = lax.top_k(imp, n_sel)
        k_sel = k_blocks[bi, hi, sel]
        v_sel = v_blocks[bi, hi, sel]
        s_s = jnp.einsum('bqhgd,bhqnkd->bhgqnk', qr, k_sel)
        key_pos = sel[..., None] * SLC_BLOCK + jnp.arange(SLC_BLOCK)
        mask_s = (key_pos <= t[None, None, :, None, None]).reshape(B, Hkv, 1, QB, n_sel * SLC_BLOCK)
        p_s = masked_softmax(s_s.reshape(B, Hkv, G, QB, n_sel * SLC_BLOCK), mask_s).reshape(s_s.shape)
        o_s = jnp.einsum('bhgqnk,bhqnkd->bqhgd', p_s.astype(dt), v_sel)
        start = i * QB
        k_band = lax.dynamic_slice_in_dim(kw, start, WINDOW + QB, axis=1)
        v_band = lax.dynamic_slice_in_dim(vw, start, WINDOW + QB, axis=1)
        wpos = start - WINDOW + jnp.arange(WINDOW + QB)
        diff = t[:, None] - wpos[None, :]
        mask_w = (wpos[None, :] >= 0) & (diff >= 0) & (diff < WINDOW)
        s_w = jnp.einsum('bqhgd,bkhd->bhgqk', qr, k_band)
        p_w = masked_softmax(s_w, mask_w)
        o_w = jnp.einsum('bhgqk,bkhd->bqhgd', p_w.astype(dt), v_band)
        g = g.reshape(B, QB, Hkv, G, 3)
        o = g[..., 0:1] * o_c + g[..., 1:2] * o_s + g[..., 2:3] * o_w
        return o.reshape(B, QB, NSA_HEADS * Dh).astype(dt)

    out = lax.map(block_fn, (jnp.arange(n_qb), to_blocks(q_rot), to_blocks(q_nope), to_blocks(gate)))
    return out.swapaxes(0, 1).reshape(B, T, NSA_HEADS * Dh)


def gla_chunked(q, k, v, log_a):
    B, T, H, Dk = q.shape
    Dv = v.shape[-1]
    C = GLA_CHUNK
    n = T // C

    def chunks(a):
        return a.astype(jnp.float32).reshape(B, n, C, H, a.shape[-1])

    qf = chunks(q) * (Dk ** -0.5)
    kf = chunks(k)
    vf = chunks(v)
    b = jnp.cumsum(chunks(log_a), axis=2)
    b_last = b[:, :, -1]
    q_dec = qf * jnp.exp(b)
    k_intra = kf * jnp.exp(-b)
    k_state = kf * jnp.exp(b_last[:, :, None] - b)
    causal = jnp.tril(jnp.ones((C, C), dtype=bool))
    A = jnp.where(causal, jnp.einsum('bnihd,bnjhd->bnhij', q_dec, k_intra), 0.0)
    o_intra = jnp.einsum('bnhij,bnjhe->bnihe', A, vf)

    def step(S, inp):
        qd, ksd, vv, bl = inp
        o = jnp.einsum('bihd,bhde->bihe', qd, S)
        S = S * jnp.exp(bl)[..., None] + jnp.einsum('bjhd,bjhe->bhde', ksd, vv)
        return S, o

    S0 = jnp.zeros((B, H, Dk, Dv), jnp.float32)
    _, o_inter = lax.scan(step, S0, (q_dec.swapaxes(0, 1), k_state.swapaxes(0, 1), vf.swapaxes(0, 1), b_last.swapaxes(0, 1)))
    o = o_intra + o_inter.swapaxes(0, 1)
    return o.reshape(B, T, H, Dv)


def hybrid_mixer(h, w_in, cmp_pos_k, cmp_w1_k, cmp_w2_k, cmp_pos_v, cmp_w1_v, cmp_w2_v,
                 gla_w_a2, gla_b_a, gla_norm_w, w_out):
    B, T, _ = h.shape
    idx = np.cumsum(SPLIT_SIZES)[:-1].tolist()
    (nq, kc, vc, ks, vs, kw, vw, ng, gq, gk, gv, ga, gg) = jnp.split(h @ w_in, idx, axis=-1)

    def heads(a, n_h):
        return a.reshape(B, T, n_h, a.shape[-1] // n_h)

    nsa_out = nsa_attention(heads(nq, NSA_HEADS), heads(kc, NSA_KV_HEADS), heads(vc, NSA_KV_HEADS),
                            heads(ks, NSA_KV_HEADS), heads(vs, NSA_KV_HEADS), heads(kw, NSA_KV_HEADS),
                            heads(vw, NSA_KV_HEADS), heads(ng, NSA_HEADS),
                            cmp_pos_k, cmp_w1_k, cmp_w2_k, cmp_pos_v, cmp_w1_v, cmp_w2_v)
    log_a = jax.nn.log_sigmoid((ga @ gla_w_a2 + gla_b_a).astype(jnp.float32)) / GLA_GATE_NORM
    o = gla_chunked(heads(gq, GLA_HEADS), heads(gk, GLA_HEADS), heads(gv, GLA_HEADS), heads(log_a, GLA_HEADS))
    o = o * lax.rsqrt(jnp.mean(jnp.square(o), axis=-1, keepdims=True) + NORM_EPS) * gla_norm_w
    gla_out = (o * jax.nn.silu(heads(gg, GLA_HEADS).astype(jnp.float32))).reshape(B, T, GLA_WIDTH).astype(h.dtype)
    return jnp.concatenate([nsa_out, gla_out], axis=-1) @ w_out


def swiglu(h, w_gate, w_up, w_down):
    return (jax.nn.silu(h @ w_gate) * (h @ w_up)) @ w_down


def moe_swiglu(h, w_router, w_gate, w_up, w_down):
    B, T, D = h.shape
    xt = h.reshape(-1, D)
    N = xt.shape[0]
    A = N * TOP_K
    logits = (xt @ w_router).astype(jnp.float32)
    top_val, top_idx = lax.top_k(logits, TOP_K)
    comb = jax.nn.softmax(top_val, axis=-1)
    flat_e = top_idx.reshape(-1).astype(jnp.int32)
    flat_tok = jnp.repeat(jnp.arange(N, dtype=jnp.int32), TOP_K)
    order = jnp.argsort(flat_e)
    e_sorted = flat_e[order]
    tok_sorted = flat_tok[order]
    w_sorted = comb.reshape(-1)[order]
    counts = jnp.bincount(flat_e, length=N_EXPERTS)
    padded = (counts + MOE_BLOCK - 1) // MOE_BLOCK * MOE_BLOCK
    pad_end = jnp.cumsum(padded)
    pad_start = pad_end - padded
    seg_start = jnp.cumsum(counts) - counts
    dest = pad_start[e_sorted] + jnp.arange(A, dtype=jnp.int32) - seg_start[e_sorted]
    n_blocks = -(-A // MOE_BLOCK) + N_EXPERTS
    slot_tok = jnp.full((n_blocks * MOE_BLOCK,), N, jnp.int32).at[dest].set(tok_sorted)
    block_e = jnp.minimum(jnp.searchsorted(pad_end, jnp.arange(n_blocks) * MOE_BLOCK, side='right'), N_EXPERTS - 1)
    x_pad = jnp.concatenate([xt, jnp.zeros((1, D), xt.dtype)], axis=0)

    def expert_block(args):
        tok, e = args
        xb = x_pad[tok]
        return (jax.nn.silu(xb @ w_gate[e]) * (xb @ w_up[e])) @ w_down[e]

    y = lax.map(expert_block, (slot_tok.reshape(n_blocks, MOE_BLOCK), block_e)).reshape(-1, D)
    y_assign = y[dest] * w_sorted[:, None].astype(y.dtype)
    out = jax.ops.segment_sum(y_assign, tok_sorted, num_segments=N)
    return out.reshape(B, T, D)


def setup_inputs(seed: int = 0) -> dict:
    key = jax.random.key(seed)
    ks = jax.random.split(key, 26)
    L, D = DEPTH, D_MODEL

    def nrm(k, shape, s):
        return jax.random.normal(k, shape, jnp.float32) * s

    return {
        'x': nrm(ks[0], (BATCH, SEQ, D), 1.0),
        'c': nrm(ks[1], (BATCH, D), 1.0),
        'w_ada': nrm(ks[2], (L, D, 6 * D), ADA_INIT_SCALE * D ** -0.5),
        'b_ada': nrm(ks[3], (L, 6 * D), 0.01),
        'w_in': nrm(ks[4], (L, D, IN_WIDTH), D ** -0.5),
        'cmp_pos_k': nrm(ks[5], (L, CMP_BLOCK, HEAD_DIM), 0.02),
        'cmp_w1_k': nrm(ks[6], (L, CMP_BLOCK * HEAD_DIM, CMP_HIDDEN), (CMP_BLOCK * HEAD_DIM) ** -0.5),
        'cmp_w2_k': nrm(ks[7], (L, CMP_HIDDEN, HEAD_DIM), CMP_HIDDEN ** -0.5),
        'cmp_pos_v': nrm(ks[8], (L, CMP_BLOCK, HEAD_DIM), 0.02),
        'cmp_w1_v': nrm(ks[9], (L, CMP_BLOCK * HEAD_DIM, CMP_HIDDEN), (CMP_BLOCK * HEAD_DIM) ** -0.5),
        'cmp_w2_v': nrm(ks[10], (L, CMP_HIDDEN, HEAD_DIM), CMP_HIDDEN ** -0.5),
        'gla_w_a2': nrm(ks[11], (L, GLA_GATE_RANK, GLA_HEADS * GLA_DK), GLA_GATE_RANK ** -0.5),
        'gla_b_a': nrm(ks[12], (L, GLA_HEADS * GLA_DK), 0.01),
        'gla_norm_w': 1.0 + nrm(ks[13], (L, GLA_DV), 0.02),
        'w_out': nrm(ks[14], (L, MIX_WIDTH, D), DEEPNORM_BETA * MIX_WIDTH ** -0.5),
        'ln_mix_g': 1.0 + nrm(ks[15], (L, D), 0.02),
        'ln_mix_b': nrm(ks[16], (L, D), 0.02),
        'ln_ffn_g': 1.0 + nrm(ks[17], (L, D), 0.02),
        'ln_ffn_b': nrm(ks[18], (L, D), 0.02),
        'ffn_w_gate': nrm(ks[19], (N_DENSE, D, D_FF), D ** -0.5),
        'ffn_w_up': nrm(ks[20], (N_DENSE, D, D_FF), D ** -0.5),
        'ffn_w_down': nrm(ks[21], (N_DENSE, D_FF, D), DEEPNORM_BETA * D_FF ** -0.5),
        'moe_router': nrm(ks[22], (N_MOE, D, N_EXPERTS), D ** -0.5),
        'moe_w_gate': nrm(ks[23], (N_MOE, N_EXPERTS, D, D_FF_EXPERT), D ** -0.5),
        'moe_w_up': nrm(ks[24], (N_MOE, N_EXPERTS, D, D_FF_EXPERT), D ** -0.5),
        'moe_w_down': nrm(ks[25], (N_MOE, N_EXPERTS, D_FF_EXPERT, D), DEEPNORM_BETA * D_FF_EXPERT ** -0.5),
    }


def reference(x, c, w_ada, b_ada, w_in, cmp_pos_k, cmp_w1_k, cmp_w2_k, cmp_pos_v, cmp_w1_v, cmp_w2_v,
              gla_w_a2, gla_b_a, gla_norm_w, w_out, ln_mix_g, ln_mix_b, ln_ffn_g, ln_ffn_b,
              ffn_w_gate, ffn_w_up, ffn_w_down, moe_router, moe_w_gate, moe_w_up, moe_w_down):
    cond = jax.nn.silu(c)
    for layer in range(DEPTH):
        mod = cond @ w_ada[layer] + b_ada[layer]
        sh_a, sc_a, g_a, sh_f, sc_f, g_f = jnp.split(mod, 6, axis=-1)
        h = x * (1.0 + sc_a[:, None, :]) + sh_a[:, None, :]
        y = hybrid_mixer(h, w_in[layer], cmp_pos_k[layer], cmp_w1_k[layer], cmp_w2_k[layer],
                         cmp_pos_v[layer], cmp_w1_v[layer], cmp_w2_v[layer],
                         gla_w_a2[layer], gla_b_a[layer], gla_norm_w[layer], w_out[layer])
        x = layer_norm(DEEPNORM_ALPHA * x + (1.0 + g_a[:, None, :]) * y, ln_mix_g[layer], ln_mix_b[layer])
        h = x * (1.0 + sc_f[:, None, :]) + sh_f[:, None, :]
        if layer % 2 == 0:
            i = layer // 2
            y = swiglu(h, ffn_w_gate[i], ffn_w_up[i], ffn_w_down[i])
        else:
            i = layer // 2
            y = moe_swiglu(h, moe_router[i], moe_w_gate[i], moe_w_up[i], moe_w_down[i])
        x = layer_norm(DEEPNORM_ALPHA * x + (1.0 + g_f[:, None, :]) * y, ln_ffn_g[layer], ln_ffn_b[layer])
    return x
```

```python
import functools

import numpy as np
import jax
import jax.numpy as jnp
from jax import lax
from jax.experimental import pallas as pl
from jax.experimental.pallas import tpu as pltpu

F32 = jnp.float32
BF16 = jnp.bfloat16

D_MODEL = 2048
DEPTH = 2
HEAD_DIM = 128
NSA_HEADS = 8
NSA_KV_HEADS = 2
NSA_GROUP = 4
CMP_BLOCK = 32
CMP_STRIDE = 16
CMP_HIDDEN = 256
SLC_BLOCK = 64
SLC_TOPN = 16
WINDOW = 512
FORCE_SCORE = 1e9
GLA_DV = 256
GLA_HEADS = 4
GLA_DK = 128
GLA_GATE_RANK = 16
GLA_GATE_NORM = 16.0
GLA_CHUNK = 64
ROPE_THETA = 500000.0
ROPE_DIMS = 32
N_EXPERTS = 8
TOP_K = 2
LN_EPS = 1e-5
NORM_EPS = 1e-6
DEEPNORM_ALPHA = (2 * DEPTH) ** 0.25
NEG_BIG = -1e30

VMEM_LIMIT_BYTES = 56 * 1024 * 1024
LANES = 128

C_NQ = 0
C_GV = 1024
C_GG = 2048
C_KC = 3072
C_VC = 3328
C_KS = 3584
C_VS = 3840
C_KW = 4096
C_VW = 4352
C_GQ = 4608
C_GK = 5120
C_MISC = 5632
PROJ_W = 5760
MISC_GA = 24

NT_DIMS = (((1,), (1,)), ((), ()))


def _params(*sem):
    return pltpu.CompilerParams(dimension_semantics=sem, vmem_limit_bytes=VMEM_LIMIT_BYTES)


def _layer_norm(z, g, b):
    mu = jnp.mean(z, axis=-1, keepdims=True)
    zc = z - mu
    var = jnp.mean(zc * zc, axis=-1, keepdims=True)
    return zc * lax.rsqrt(var + LN_EPS) * g + b


def _ada_kernel(c_ref, w_ref, b_ref, o_ref):
    c = c_ref[...]
    cond = c * jax.nn.sigmoid(c)
    o_ref[0] = jnp.dot(cond.astype(BF16), w_ref[0].astype(BF16),
                       preferred_element_type=F32) + b_ref[0]


def ada_mod(c_pad, w_ada, b_ada):
    L, D, N = w_ada.shape
    tn = 1024
    return pl.pallas_call(
        _ada_kernel,
        grid=(L, N // tn),
        in_specs=[pl.BlockSpec((8, D), lambda l, j: (0, 0)),
                  pl.BlockSpec((1, D, tn), lambda l, j: (l, 0, j)),
                  pl.BlockSpec((1, 1, tn), lambda l, j: (l, 0, j))],
        out_specs=pl.BlockSpec((1, 8, tn), lambda l, j: (l, 0, j)),
        out_shape=jax.ShapeDtypeStruct((L, 8, N), F32),
        compiler_params=_params("parallel", "parallel"),
        name="ada_mod",
    )(c_pad, w_ada, b_ada)


def _inproj_kernel(x_ref, m_ref, w_ref, o_ref):
    h = x_ref[0] * (1.0 + m_ref[0, 1:2, :]) + m_ref[0, 0:1, :]
    o_ref[0] = jnp.dot(h.astype(BF16), w_ref[...], preferred_element_type=F32)


def in_proj(x, mod, w):
    B, T, D = x.shape
    N = w.shape[1]
    tm = min(512, T)
    tn = N // 3
    return pl.pallas_call(
        _inproj_kernel,
        grid=(N // tn, B, T // tm),
        in_specs=[pl.BlockSpec((1, tm, D), lambda j, b, i: (b, i, 0)),
                  pl.BlockSpec((1, 6, D), lambda j, b, i: (b, 0, 0)),
                  pl.BlockSpec((D, tn), lambda j, b, i: (0, j))],
        out_specs=pl.BlockSpec((1, tm, tn), lambda j, b, i: (b, i, j)),
        out_shape=jax.ShapeDtypeStruct((B, T, N), F32),
        compiler_params=_params("parallel", "parallel", "parallel"),
        name="in_proj",
    )(x, mod, w)


def _prep_kernel(q_ref, kv_ref, misc_ref, rc_ref, ra_ref, rb_ref,
                 qr_ref, qn_ref, ks_ref, vs_ref, kw_ref, vw_ref, g_ref):
    rc = rc_ref[...]
    ra = ra_ref[...]
    rb = rb_ref[...]
    scale = HEAD_DIM ** -0.5

    def rope(xh):
        return (xh * rc + pltpu.roll(xh, LANES - ROPE_DIMS // 2, 1) * ra
                + pltpu.roll(xh, ROPE_DIMS // 2, 1) * rb)

    for hq in range(NSA_HEADS):
        sl = slice(hq * HEAD_DIM, (hq + 1) * HEAD_DIM)
        xh = q_ref[0, :, sl]
        qr_ref[0, :, sl] = (rope(xh) * scale).astype(BF16)
        qn_ref[0, :, sl] = (xh * scale).astype(BF16)
    misc = misc_ref[0]
    for h in range(NSA_KV_HEADS):
        sl = slice(h * HEAD_DIM, (h + 1) * HEAD_DIM)
        ks_ref[0, h] = rope(kv_ref[0, :, (C_KS - C_KC) + h * HEAD_DIM:(C_KS - C_KC) + (h + 1) * HEAD_DIM]).astype(BF16)
        vs_ref[0, h] = kv_ref[0, :, (C_VS - C_KC) + h * HEAD_DIM:(C_VS - C_KC) + (h + 1) * HEAD_DIM].astype(BF16)
        kw_ref[0, h] = rope(kv_ref[0, :, (C_KW - C_KC) + h * HEAD_DIM:(C_KW - C_KC) + (h + 1) * HEAD_DIM]).astype(BF16)
        vw_ref[0, h] = kv_ref[0, :, (C_VW - C_KC) + h * HEAD_DIM:(C_VW - C_KC) + (h + 1) * HEAD_DIM].astype(BF16)
        shift = (LANES - 3 * NSA_GROUP * h) % LANES
        gm = misc if shift == 0 else pltpu.roll(misc, shift, 1)
        g_ref[0, h] = jax.nn.sigmoid(gm)
        del sl


def nsa_prep(proj, rc, ra, rb):
    B, T, _ = proj.shape
    tm = min(512, T)
    kvw = C_GQ - C_KC
    qspec = pl.BlockSpec((1, tm, NSA_HEADS * HEAD_DIM), lambda b, i: (b, i, 0))
    kvspec = pl.BlockSpec((1, NSA_KV_HEADS, tm, HEAD_DIM), lambda b, i: (b, 0, i, 0))
    tspec = pl.BlockSpec((tm, LANES), lambda b, i: (i, 0))
    kv_shape = jax.ShapeDtypeStruct((B, NSA_KV_HEADS, T, HEAD_DIM), BF16)
    return pl.pallas_call(
        _prep_kernel,
        grid=(B, T // tm),
        in_specs=[pl.BlockSpec((1, tm, NSA_HEADS * HEAD_DIM), lambda b, i: (b, i, C_NQ // (NSA_HEADS * HEAD_DIM))),
                  pl.BlockSpec((1, tm, kvw), lambda b, i: (b, i, C_KC // kvw)),
                  pl.BlockSpec((1, tm, LANES), lambda b, i: (b, i, C_MISC // LANES)),
                  tspec, tspec, tspec],
        out_specs=[qspec, qspec, kvspec, kvspec, kvspec, kvspec, kvspec],
        out_shape=[jax.ShapeDtypeStruct((B, T, NSA_HEADS * HEAD_DIM), BF16),
                   jax.ShapeDtypeStruct((B, T, NSA_HEADS * HEAD_DIM), BF16),
                   kv_shape, kv_shape, kv_shape, kv_shape,
                   jax.ShapeDtypeStruct((B, NSA_KV_HEADS, T, LANES), F32)],
        compiler_params=_params("parallel", "parallel"),
        name="nsa_prep",
    )(proj, proj, proj, rc, ra, rb)


def _compress_kernel(a_ref, pe_ref, w1_ref, w2_ref, o_ref, *, n_half):
    half = CMP_STRIDE

    def part(l0):
        acc = jnp.zeros((n_half, CMP_HIDDEN), F32)
        for l in range(half):
            rows = a_ref[0, pl.ds(l, n_half, stride=half), :] + pe_ref[0, l0 + l:l0 + l + 1, :]
            acc += jnp.dot(rows.astype(BF16), w1_ref[0, (l0 + l) * HEAD_DIM:(l0 + l + 1) * HEAD_DIM, :],
                           preferred_element_type=F32)
        return acc

    first = part(0)
    second = part(half)
    hid = first + pltpu.roll(second, n_half - 1, 0)
    row = lax.broadcasted_iota(jnp.int32, (n_half, 1), 0)
    hid = jnp.where(row < n_half - 1, hid, 0.0)
    act = jax.nn.gelu(hid)
    o_ref[0, 0, 0] = jnp.dot(act.astype(BF16), w2_ref[0], preferred_element_type=F32).astype(BF16)


def nsa_compress(proj, pe, w1, w2):
    B, T, _ = proj.shape
    n_half = T // CMP_STRIDE
    return pl.pallas_call(
        functools.partial(_compress_kernel, n_half=n_half),
        grid=(B, 2, NSA_KV_HEADS),
        in_specs=[pl.BlockSpec((1, T, HEAD_DIM), lambda b, s, h: (b, 0, C_KC // HEAD_DIM + s * NSA_KV_HEADS + h)),
                  pl.BlockSpec((1, CMP_BLOCK, HEAD_DIM), lambda b, s, h: (s, 0, 0)),
                  pl.BlockSpec((1, CMP_BLOCK * HEAD_DIM, CMP_HIDDEN), lambda b, s, h: (s, 0, 0)),
                  pl.BlockSpec((1, CMP_HIDDEN, HEAD_DIM), lambda b, s, h: (s, 0, 0))],
        out_specs=pl.BlockSpec((1, 1, 1, n_half, HEAD_DIM), lambda b, s, h: (b, s, h, 0, 0)),
        out_shape=jax.ShapeDtypeStruct((B, 2, NSA_KV_HEADS, n_half, HEAD_DIM), BF16),
        compiler_params=_params("parallel", "parallel", "parallel"),
        name="nsa_compress",
    )(proj, pe, w1, w2)


def _nsa_kernel(qr_ref, qn_ref, g_ref, kc_ref, vc_ref, ks_ref, vs_ref, kw_ref, vw_ref, ov_ref, o_ref,
                *, TQ, TK, WK, NC, NB):
    G = NSA_GROUP
    R = G * TQ
    i = pl.program_id(2)
    t0 = i * TQ

    def stack(ref):
        x = ref[0]
        return jnp.concatenate([x[:, g * HEAD_DIM:(g + 1) * HEAD_DIM] for g in range(G)], axis=0)

    qr = stack(qr_ref)
    qn = stack(qn_ref)
    t_tok = t0 + lax.broadcasted_iota(jnp.int32, (TQ, 1), 0)

    kc = kc_ref[0, 0, 0]
    vc = vc_ref[0, 0, 0]
    s_c = lax.dot_general(qn, kc, NT_DIMS, preferred_element_type=F32).reshape(G, TQ, NC)
    cend = lax.broadcasted_iota(jnp.int32, (1, NC), 1) * CMP_STRIDE + (CMP_BLOCK - 1)
    ok_c = (cend <= t_tok)[None]
    s_c = jnp.where(ok_c, s_c, NEG_BIG)
    e_c = jnp.exp(s_c - jnp.max(s_c, axis=-1, keepdims=True))
    p_c = jnp.where(ok_c, e_c / jnp.sum(e_c, axis=-1, keepdims=True), 0.0)
    p_cb = p_c.reshape(R, NC).astype(BF16)
    o_c = jnp.dot(p_cb, vc, preferred_element_type=F32)
    imp4 = jnp.dot(p_cb, ov_ref[...], preferred_element_type=F32)
    imp = imp4[0:TQ]
    for g in range(1, G):
        imp = imp + imp4[g * TQ:(g + 1) * TQ]

    NBP = ((NB + 7) // 8) * 8
    impT = imp.T[0:NBP]
    tT = t0 + lax.broadcasted_iota(jnp.int32, (1, TQ), 1)
    cur = tT // SLC_BLOCK
    jj = lax.broadcasted_iota(jnp.int32, (NBP, TQ), 0)
    forced = jnp.where(jj == 0, 1, jnp.where(jj == cur, 1, jnp.where(jj == cur - 1, 1, 0)))
    val = jnp.where(jj > cur, -jnp.inf, jnp.where(forced > 0, FORCE_SCORE, impT))
    rank = jnp.zeros((NBP, TQ), jnp.int32)
    for j2 in range(NB):
        row = val[j2:j2 + 1, :]
        later = jnp.where(jj > j2, 1, 0)
        rank = rank + jnp.where(row > val, 1, jnp.where(row == val, later, 0))
    n_sel = min(SLC_TOPN, NB)
    selT = jnp.where(jj > cur, 0.0, jnp.where(rank < n_sel, 1.0, 0.0))
    if NBP < LANES:
        selT = jnp.concatenate([selT, jnp.zeros((LANES - NBP, TQ), F32)], axis=0)
    sel = selT.T.astype(BF16)

    jrow = lax.broadcasted_iota(jnp.int32, (LANES, TK), 0)
    kcol = lax.broadcasted_iota(jnp.int32, (LANES, TK), 1)
    kline = lax.broadcasted_iota(jnp.int32, (1, TK), 1)

    def sel_body(kt, carry):
        m, l, acc = carry
        k0 = pl.multiple_of(kt * TK, TK)
        k = ks_ref[0, 0, pl.ds(k0, TK), :]
        v = vs_ref[0, 0, pl.ds(k0, TK), :]
        s = lax.dot_general(qr, k, NT_DIMS, preferred_element_type=F32).reshape(G, TQ, TK)
        expand = jnp.where((kcol + k0) // SLC_BLOCK == jrow, 1.0, 0.0).astype(BF16)
        selk = jnp.dot(sel, expand, preferred_element_type=F32)
        ok = (jnp.where(kline + k0 <= t_tok, selk, 0.0) > 0.5)[None]
        s = jnp.where(ok, s, NEG_BIG)
        m_new = jnp.maximum(m, jnp.max(s, axis=-1, keepdims=True))
        alpha = jnp.exp(m - m_new)
        p = jnp.where(ok, jnp.exp(s - m_new), 0.0)
        l = alpha * l + jnp.sum(p, axis=-1, keepdims=True)
        pv = jnp.dot(p.reshape(R, TK).astype(BF16), v, preferred_element_type=F32)
        acc = alpha.reshape(R, 1) * acc + pv
        return m_new, l, acc

    n_kt = (t0 + TQ + TK - 1) // TK
    m0 = jnp.full((G, TQ, 1), NEG_BIG, F32)
    l0 = jnp.zeros((G, TQ, 1), F32)
    a0 = jnp.zeros((R, HEAD_DIM), F32)
    _, l_s, acc_s = lax.fori_loop(0, n_kt, sel_body, (m0, l0, a0))
    o_s = acc_s / l_s.reshape(R, 1)

    ws = pl.multiple_of(jnp.maximum(t0 + TQ - WK, 0), LANES)
    kwin = kw_ref[0, 0, pl.ds(ws, WK), :]
    vwin = vw_ref[0, 0, pl.ds(ws, WK), :]
    s_w = lax.dot_general(qr, kwin, NT_DIMS, preferred_element_type=F32).reshape(G, TQ, WK)
    dist = t_tok - (ws + lax.broadcasted_iota(jnp.int32, (1, WK), 1))
    ok_w = (jnp.where(dist >= 0, jnp.where(dist < WINDOW, 1, 0), 0) > 0)[None]
    s_w = jnp.where(ok_w, s_w, NEG_BIG)
    e_w = jnp.where(ok_w, jnp.exp(s_w - jnp.max(s_w, axis=-1, keepdims=True)), 0.0)
    p_w = e_w / jnp.sum(e_w, axis=-1, keepdims=True)
    o_w = jnp.dot(p_w.reshape(R, WK).astype(BF16), vwin, preferred_element_type=F32)

    gate = g_ref[0, 0]
    for g in range(G):
        rows = slice(g * TQ, (g + 1) * TQ)
        o = (gate[:, 3 * g:3 * g + 1] * o_c[rows] + gate[:, 3 * g + 1:3 * g + 2] * o_s[rows]
             + gate[:, 3 * g + 2:3 * g + 3] * o_w[rows])
        o_ref[0, :, g * HEAD_DIM:(g + 1) * HEAD_DIM] = o.astype(BF16)


def nsa_attention(qr, qn, gates, kcvc, ks, vs, kw, vw, overlap):
    B, T, _ = qr.shape
    TQ = 128
    TK = min(512, T)
    WK = min(WINDOW + TQ, T)
    NC = kcvc.shape[3]
    NB = T // SLC_BLOCK
    gw = NSA_GROUP * HEAD_DIM
    qspec = pl.BlockSpec((1, TQ, gw), lambda b, h, i: (b, i, h))
    kvspec = pl.BlockSpec((1, 1, T, HEAD_DIM), lambda b, h, i: (b, h, 0, 0))
    return pl.pallas_call(
        functools.partial(_nsa_kernel, TQ=TQ, TK=TK, WK=WK, NC=NC, NB=NB),
        grid=(B, NSA_KV_HEADS, T // TQ),
        in_specs=[qspec, qspec,
                  pl.BlockSpec((1, 1, TQ, LANES), lambda b, h, i: (b, h, i, 0)),
                  pl.BlockSpec((1, 1, 1, NC, HEAD_DIM), lambda b, h, i: (b, 0, h, 0, 0)),
                  pl.BlockSpec((1, 1, 1, NC, HEAD_DIM), lambda b, h, i: (b, 1, h, 0, 0)),
                  kvspec, kvspec, kvspec, kvspec,
                  pl.BlockSpec((NC, LANES), lambda b, h, i: (0, 0))],
        out_specs=pl.BlockSpec((1, TQ, gw), lambda b, h, i: (b, i, h)),
        out_shape=jax.ShapeDtypeStruct((B, T, NSA_HEADS * HEAD_DIM), BF16),
        compiler_params=_params("parallel", "parallel", "arbitrary"),
        name="nsa_attention",
    )(qr, qn, gates, kcvc, kcvc, ks, vs, kw, vw, overlap)


def _gla_kernel(q_ref, k_ref, v_ref, gg_ref, misc_ref, wa_ref, ba_ref, nw_ref, o_ref, s_ref, la_ref, *, TC):
    C = GLA_CHUNK

    @pl.when(pl.program_id(1) == 0)
    def _():
        s_ref[...] = jnp.zeros_like(s_ref)

    z = jnp.dot(misc_ref[0].astype(BF16), wa_ref[...], preferred_element_type=F32) + ba_ref[...]
    la_ref[...] = (jnp.minimum(z, 0.0) - jnp.log1p(jnp.exp(-jnp.abs(z)))) * (1.0 / GLA_GATE_NORM)
    ri = lax.broadcasted_iota(jnp.int32, (C, C), 0)
    ci = lax.broadcasted_iota(jnp.int32, (C, C), 1)
    causal = ri >= ci
    tri = jnp.where(causal, 1.0, 0.0)
    nw = nw_ref[...]

    def chunk(c, carry):
        r0 = pl.multiple_of(c * C, C)
        b_all = jnp.dot(tri, la_ref[pl.ds(r0, C), :], preferred_element_type=F32,
                        precision=lax.Precision.HIGHEST)
        for h in range(GLA_HEADS):
            b = b_all[:, h * GLA_DK:(h + 1) * GLA_DK]
            bl = b[C - 1:C, :]
            q = q_ref[0, pl.ds(r0, C), h * GLA_DK:(h + 1) * GLA_DK] * (GLA_DK ** -0.5)
            k = k_ref[0, pl.ds(r0, C), h * GLA_DK:(h + 1) * GLA_DK]
            v = v_ref[0, pl.ds(r0, C), h * GLA_DV:(h + 1) * GLA_DV].astype(BF16)
            qd = (q * jnp.exp(b)).astype(BF16)
            ki = (k * jnp.exp(-b)).astype(BF16)
            kst = (k * jnp.exp(bl - b)).T.astype(BF16)
            decay = jnp.exp(b.T[:, C - 1:C])
            a = lax.dot_general(qd, ki, NT_DIMS, preferred_element_type=F32)
            a = jnp.where(causal, a, 0.0).astype(BF16)
            s_old = s_ref[h]
            o = (jnp.dot(a, v, preferred_element_type=F32)
                 + jnp.dot(qd, s_old.astype(BF16), preferred_element_type=F32))
            s_ref[h] = s_old * decay + jnp.dot(kst, v, preferred_element_type=F32)
            o = o * lax.rsqrt(jnp.mean(o * o, axis=-1, keepdims=True) + NORM_EPS) * nw
            gg = gg_ref[0, pl.ds(r0, C), h * GLA_DV:(h + 1) * GLA_DV]
            o_ref[0, pl.ds(r0, C), h * GLA_DV:(h + 1) * GLA_DV] = (o * (gg * jax.nn.sigmoid(gg))).astype(BF16)
        return carry

    lax.fori_loop(0, TC // C, chunk, 0)


def gla_mixer(proj, wa_pad, ba, nw):
    B, T, _ = proj.shape
    TC = min(512, T)
    qk_w = GLA_HEADS * GLA_DK
    v_w = GLA_HEADS * GLA_DV
    return pl.pallas_call(
        functools.partial(_gla_kernel, TC=TC),
        grid=(B, T // TC),
        in_specs=[pl.BlockSpec((1, TC, qk_w), lambda b, i: (b, i, C_GQ // qk_w)),
                  pl.BlockSpec((1, TC, qk_w), lambda b, i: (b, i, C_GK // qk_w)),
                  pl.BlockSpec((1, TC, v_w), lambda b, i: (b, i, C_GV // v_w)),
                  pl.BlockSpec((1, TC, v_w), lambda b, i: (b, i, C_GG // v_w)),
                  pl.BlockSpec((1, TC, LANES), lambda b, i: (b, i, C_MISC // LANES)),
                  pl.BlockSpec((LANES, qk_w), lambda b, i: (0, 0)),
                  pl.BlockSpec((1, qk_w), lambda b, i: (0, 0)),
                  pl.BlockSpec((1, GLA_DV), lambda b, i: (0, 0))],
        out_specs=pl.BlockSpec((1, TC, v_w), lambda b, i: (b, i, 0)),
        out_shape=jax.ShapeDtypeStruct((B, T, v_w), BF16),
        scratch_shapes=[pltpu.VMEM((GLA_HEADS, GLA_DK, GLA_DV), F32),
                        pltpu.VMEM((TC, qk_w), F32)],
        compiler_params=_params("parallel", "arbitrary"),
        name="gla_mixer",
    )(proj, proj, proj, proj, proj, wa_pad, ba, nw)


def _outproj_kernel(n_ref, g_ref, w_ref, x_ref, m_ref, lg_ref, lb_ref, o_ref, *, half):
    y = (jnp.dot(n_ref[0], w_ref[0:half, :], preferred_element_type=F32)
         + jnp.dot(g_ref[0], w_ref[half:, :], preferred_element_type=F32))
    z = DEEPNORM_ALPHA * x_ref[0] + (1.0 + m_ref[0, 2:3, :]) * y
    o_ref[0] = _layer_norm(z, lg_ref[...], lb_ref[...])


def out_proj_ln(nsa_o, gla_o, w, x, mod, lg, lb):
    B, T, D = x.shape
    half = nsa_o.shape[-1]
    tm = min(512, T)
    vec = pl.BlockSpec((1, D), lambda b, i: (0, 0))
    return pl.pallas_call(
        functools.partial(_outproj_kernel, half=half),
        grid=(B, T // tm),
        in_specs=[pl.BlockSpec((1, tm, half), lambda b, i: (b, i, 0)),
                  pl.BlockSpec((1, tm, gla_o.shape[-1]), lambda b, i: (b, i, 0)),
                  pl.BlockSpec(w.shape, lambda b, i: (0, 0)),
                  pl.BlockSpec((1, tm, D), lambda b, i: (b, i, 0)),
                  pl.BlockSpec((1, 6, D), lambda b, i: (b, 0, 0)),
                  vec, vec],
        out_specs=pl.BlockSpec((1, tm, D), lambda b, i: (b, i, 0)),
        out_shape=jax.ShapeDtypeStruct((B, T, D), F32),
        compiler_params=_params("parallel", "parallel"),
        name="out_proj_ln",
    )(nsa_o, gla_o, w, x, mod, lg, lb)


def _ffn_kernel(x_ref, m_ref, wg_ref, wu_ref, wd_ref, lg_ref, lb_ref, o_ref, h_sc, acc_sc):
    j = pl.program_id(2)

    @pl.when(j == 0)
    def _():
        h_sc[...] = (x_ref[0] * (1.0 + m_ref[0, 4:5, :]) + m_ref[0, 3:4, :]).astype(BF16)
        acc_sc[...] = jnp.zeros_like(acc_sc)

    h = h_sc[...]
    a = jnp.dot(h, wg_ref[...], preferred_element_type=F32)
    u = jnp.dot(h, wu_ref[...], preferred_element_type=F32)
    acc_sc[...] += jnp.dot((a * jax.nn.sigmoid(a) * u).astype(BF16), wd_ref[...], preferred_element_type=F32)

    @pl.when(j == pl.num_programs(2) - 1)
    def _():
        z = DEEPNORM_ALPHA * x_ref[0] + (1.0 + m_ref[0, 5:6, :]) * acc_sc[...]
        o_ref[0] = _layer_norm(z, lg_ref[...], lb_ref[...])


def ffn_ln(x, mod, wg, wu, wd, lg, lb):
    B, T, D = x.shape
    F = wg.shape[1]
    tm = min(512, T)
    tf = 512
    vec = pl.BlockSpec((1, D), lambda b, i, j: (0, 0))
    return pl.pallas_call(
        _ffn_kernel,
        grid=(B, T // tm, F // tf),
        in_specs=[pl.BlockSpec((1, tm, D), lambda b, i, j: (b, i, 0)),
                  pl.BlockSpec((1, 6, D), lambda b, i, j: (b, 0, 0)),
                  pl.BlockSpec((D, tf), lambda b, i, j: (0, j)),
                  pl.BlockSpec((D, tf), lambda b, i, j: (0, j)),
                  pl.BlockSpec((tf, D), lambda b, i, j: (j, 0)),
                  vec, vec],
        out_specs=pl.BlockSpec((1, tm, D), lambda b, i, j: (b, i, 0)),
        out_shape=jax.ShapeDtypeStruct((B, T, D), F32),
        scratch_shapes=[pltpu.VMEM((tm, D), BF16), pltpu.VMEM((tm, D), F32)],
        compiler_params=_params("parallel", "parallel", "arbitrary"),
        name="ffn_ln",
    )(x, mod, wg, wu, wd, lg, lb)


def _router_kernel(x_ref, m_ref, wr_ref, h_ref, lg_ref):
    h = x_ref[0] * (1.0 + m_ref[0, 4:5, :]) + m_ref[0, 3:4, :]
    h_ref[0] = h.astype(BF16)
    lg_ref[0] = jnp.dot(h, wr_ref[...], preferred_element_type=F32, precision=lax.Precision.HIGHEST)


def moe_router(x, mod, wr_pad):
    B, T, D = x.shape
    tm = min(512, T)
    return pl.pallas_call(
        _router_kernel,
        grid=(B, T // tm),
        in_specs=[pl.BlockSpec((1, tm, D), lambda b, i: (b, i, 0)),
                  pl.BlockSpec((1, 6, D), lambda b, i: (b, 0, 0)),
                  pl.BlockSpec((D, LANES), lambda b, i: (0, 0))],
        out_specs=[pl.BlockSpec((1, tm, D), lambda b, i: (b, i, 0)),
                   pl.BlockSpec((1, tm, LANES), lambda b, i: (b, i, 0))],
        out_shape=[jax.ShapeDtypeStruct((B, T, D), BF16),
                   jax.ShapeDtypeStruct((B, T, LANES), F32)],
        compiler_params=_params("parallel", "parallel"),
        name="moe_router",
    )(x, mod, wr_pad)


def _moe_kernel(be_ref, nu_ref, x_ref, wg_ref, wu_ref, wd_ref, o_ref, acc_sc):
    i = pl.program_id(0)
    j = pl.program_id(1)
    last = pl.num_programs(1) - 1
    used = i < nu_ref[0]

    @pl.when(j == 0)
    def _():
        acc_sc[...] = jnp.zeros_like(acc_sc)

    @pl.when(used)
    def _():
        h = x_ref[...]
        a = jnp.dot(h, wg_ref[0], preferred_element_type=F32)
        u = jnp.dot(h, wu_ref[0], preferred_element_type=F32)
        acc_sc[...] += jnp.dot((a * jax.nn.sigmoid(a) * u).astype(BF16), wd_ref[0],
                               preferred_element_type=F32)

    @pl.when(j == last)
    def _():
        o_ref[...] = acc_sc[...]


def moe_experts(block_e, n_used, xg, wg, wu, wd, tm):
    M, D = xg.shape
    F = wg.shape[2]
    tf = 512
    nj = F // tf
    n_blocks = M // tm

    def wj(i, j, be, nu):
        return jnp.where(i < nu[0], j, nj - 1)

    return pl.pallas_call(
        _moe_kernel,
        grid_spec=pltpu.PrefetchScalarGridSpec(
            num_scalar_prefetch=2,
            grid=(n_blocks, nj),
            in_specs=[pl.BlockSpec((tm, D), lambda i, j, be, nu: (i, 0)),
                      pl.BlockSpec((1, D, tf), lambda i, j, be, nu: (be[i], 0, wj(i, j, be, nu))),
                      pl.BlockSpec((1, D, tf), lambda i, j, be, nu: (be[i], 0, wj(i, j, be, nu))),
                      pl.BlockSpec((1, tf, D), lambda i, j, be, nu: (be[i], wj(i, j, be, nu), 0))],
            out_specs=pl.BlockSpec((tm, D), lambda i, j, be, nu: (i, 0)),
            scratch_shapes=[pltpu.VMEM((tm, D), F32)]),
        out_shape=jax.ShapeDtypeStruct((M, D), F32),
        compiler_params=_params("parallel", "arbitrary"),
        name="moe_experts",
    )(block_e, n_used, xg, wg, wu, wd)


def _combine_kernel(y0_ref, y1_ref, cw_ref, x_ref, m_ref, lg_ref, lb_ref, o_ref):
    cw = cw_ref[0]
    y = y0_ref[0] * cw[:, 0:1] + y1_ref[0] * cw[:, 1:2]
    z = DEEPNORM_ALPHA * x_ref[0] + (1.0 + m_ref[0, 5:6, :]) * y
    o_ref[0] = _layer_norm(z, lg_ref[...], lb_ref[...])


def moe_combine_ln(y0, y1, cw, x, mod, lg, lb):
    B, T, D = x.shape
    tm = min(512, T)
    row = pl.BlockSpec((1, tm, D), lambda b, i: (b, i, 0))
    vec = pl.BlockSpec((1, D), lambda b, i: (0, 0))
    return pl.pallas_call(
        _combine_kernel,
        grid=(B, T // tm),
        in_specs=[row, row, pl.BlockSpec((1, tm, LANES), lambda b, i: (b, i, 0)), row,
                  pl.BlockSpec((1, 6, D), lambda b, i: (b, 0, 0)), vec, vec],
        out_specs=row,
        out_shape=jax.ShapeDtypeStruct((B, T, D), F32),
        compiler_params=_params("parallel", "parallel"),
        name="moe_combine_ln",
    )(y0, y1, cw, x, mod, lg, lb)


MOE_TM = 512


def moe_layer(x, mod, w_router, wg, wu, wd, lg, lb):
    B, T, D = x.shape
    N = B * T
    A = N * TOP_K
    tm = MOE_TM
    wr_pad = jnp.zeros((D, LANES), F32).at[:, :N_EXPERTS].set(w_router)
    h, logits = moe_router(x, mod, wr_pad)
    logits = logits.reshape(N, LANES)[:, :N_EXPERTS]
    top_val, top_idx = lax.top_k(logits, TOP_K)
    comb = jax.nn.softmax(top_val, axis=-1)
    flat_e = top_idx.reshape(-1).astype(jnp.int32)
    order = jnp.argsort(flat_e)
    e_sorted = flat_e[order]
    counts = jnp.bincount(flat_e, length=N_EXPERTS).astype(jnp.int32)
    padded = (counts + tm - 1) // tm * tm
    pad_end = jnp.cumsum(padded)
    pad_start = pad_end - padded
    seg_start = jnp.cumsum(counts) - counts
    dest = pad_start[e_sorted] + jnp.arange(A, dtype=jnp.int32) - seg_start[e_sorted]
    n_blocks = -(-A // tm) + N_EXPERTS
    slot_of = jnp.zeros((A,), jnp.int32).at[order].set(dest).reshape(N, TOP_K)
    slot_tok = jnp.full((n_blocks * tm,), N, jnp.int32).at[dest].set(order // TOP_K)
    block_e = jnp.minimum(jnp.searchsorted(pad_end, jnp.arange(n_blocks, dtype=jnp.int32) * tm, side='right'),
                          N_EXPERTS - 1).astype(jnp.int32)
    n_used = (pad_end[-1] // tm).astype(jnp.int32).reshape(1)
    block_e = jnp.where(jnp.arange(n_blocks) < n_used[0], block_e, block_e[jnp.maximum(n_used[0] - 1, 0)])
    h_pad = jnp.concatenate([h.reshape(N, D), jnp.zeros((1, D), BF16)], axis=0)
    xg = h_pad[slot_tok]
    y = moe_experts(block_e, n_used, xg, wg, wu, wd, tm)
    y0 = y[slot_of[:, 0]].reshape(B, T, D)
    y1 = y[slot_of[:, 1]].reshape(B, T, D)
    cw = jnp.zeros((N, LANES), F32).at[:, :TOP_K].set(comb).reshape(B, T, LANES)
    return moe_combine_ln(y0, y1, cw, x, mod, lg, lb)


def _rope_tables(T):
    half = ROPE_DIMS // 2
    inv = ROPE_THETA ** (-jnp.arange(half, dtype=F32) * 2.0 / ROPE_DIMS)
    ang = jnp.arange(T).astype(F32)[:, None] * inv[None, :]
    cos, sin = jnp.cos(ang), jnp.sin(ang)
    z = jnp.zeros((T, LANES - ROPE_DIMS), F32)
    zh = jnp.zeros((T, half), F32)
    rc = jnp.concatenate([cos, cos, jnp.ones((T, LANES - ROPE_DIMS), F32)], axis=1)
    ra = jnp.concatenate([-sin, zh, z], axis=1)
    rb = jnp.concatenate([zh, sin, z], axis=1)
    return rc, ra, rb


def _overlap_matrix(NC, NB):
    c0 = np.arange(NC)[:, None] * CMP_STRIDE
    b0 = np.arange(LANES)[None, :] * SLC_BLOCK
    ov = (c0 < b0 + SLC_BLOCK) & (c0 + CMP_BLOCK > b0) & (np.arange(LANES)[None, :] < NB)
    return jnp.asarray(ov.astype(np.float32), dtype=BF16)


def _reorder_w_in(w):
    D = w.shape[0]
    o_ng = 1024 + 6 * 256
    o_gq = o_ng + 24
    o_gk = o_gq + 512
    o_gv = o_gk + 512
    o_ga = o_gv + 1024
    o_gg = o_ga + GLA_GATE_RANK
    misc = jnp.concatenate([w[:, o_ng:o_gq], w[:, o_ga:o_gg],
                            jnp.zeros((D, LANES - 24 - GLA_GATE_RANK), w.dtype)], axis=1)
    cols = [w[:, 0:1024], w[:, o_gv:o_ga], w[:, o_gg:o_gg + 1024], w[:, 1024:o_ng],
            w[:, o_gq:o_gk], w[:, o_gk:o_gv], misc]
    return jnp.concatenate(cols, axis=1).astype(BF16)


def hybrid_mixer_ln(x, mod, w_in_r, cmp_pe, cmp_w1, cmp_w2, wa_pad, ba, nw, w_out, lg, lb, tables, overlap):
    proj = in_proj(x, mod, w_in_r)
    qr, qn, ks, vs, kw, vw, gates = nsa_prep(proj, *tables)
    kcvc = nsa_compress(proj, cmp_pe, cmp_w1, cmp_w2)
    nsa_o = nsa_attention(qr, qn, gates, kcvc, ks, vs, kw, vw, overlap)
    gla_o = gla_mixer(proj, wa_pad, ba, nw)
    return out_proj_ln(nsa_o, gla_o, w_out, x, mod, lg, lb)


def kernel(x, c, w_ada, b_ada, w_in, cmp_pos_k, cmp_w1_k, cmp_w2_k, cmp_pos_v, cmp_w1_v, cmp_w2_v, gla_w_a2, gla_b_a, gla_norm_w, w_out, ln_mix_g, ln_mix_b, ln_ffn_g, ln_ffn_b, ffn_w_gate, ffn_w_up, ffn_w_down, moe_router, moe_w_gate, moe_w_up, moe_w_down):
    B, T, D = x.shape
    L = w_ada.shape[0]
    c_pad = jnp.zeros((8, D), F32).at[:B].set(c)
    mod_all = ada_mod(c_pad, w_ada, b_ada.reshape(L, 1, 6 * D))[:, :B].reshape(L, B, 6, D)
    tables = _rope_tables(T)
    overlap = _overlap_matrix(T // CMP_STRIDE, T // SLC_BLOCK)
    for layer in range(L):
        mod = mod_all[layer]
        wa_pad = jnp.zeros((LANES, GLA_HEADS * GLA_DK), F32).at[MISC_GA:MISC_GA + GLA_GATE_RANK].set(
            gla_w_a2[layer]).astype(BF16)
        x = hybrid_mixer_ln(
            x, mod, _reorder_w_in(w_in[layer]),
            jnp.stack([cmp_pos_k[layer], cmp_pos_v[layer]]),
            jnp.stack([cmp_w1_k[layer], cmp_w1_v[layer]]).astype(BF16),
            jnp.stack([cmp_w2_k[layer], cmp_w2_v[layer]]).astype(BF16),
            wa_pad, gla_b_a[layer].reshape(1, -1), gla_norm_w[layer].reshape(1, -1),
            w_out[layer].astype(BF16), ln_mix_g[layer].reshape(1, D), ln_mix_b[layer].reshape(1, D),
            tables, overlap)
        lg = ln_ffn_g[layer].reshape(1, D)
        lb = ln_ffn_b[layer].reshape(1, D)
        i = layer // 2
        if layer % 2 == 0:
            x = ffn_ln(x, mod, ffn_w_gate[i].astype(BF16), ffn_w_up[i].astype(BF16),
                       ffn_w_down[i].astype(BF16), lg, lb)
        else:
            x = moe_layer(x, mod, moe_router[i], moe_w_gate[i].astype(BF16), moe_w_up[i].astype(BF16),
                          moe_w_down[i].astype(BF16), lg, lb)
    return x
```

```python
import functools

import numpy as np
import jax
import jax.numpy as jnp
from jax import lax
from jax.experimental import pallas as pl
from jax.experimental.pallas import tpu as pltpu

F32 = jnp.float32
BF16 = jnp.bfloat16

D_MODEL = 2048
DEPTH = 2
HEAD_DIM = 128
NSA_HEADS = 8
NSA_KV_HEADS = 2
NSA_GROUP = 4
CMP_BLOCK = 32
CMP_STRIDE = 16
CMP_HIDDEN = 256
SLC_BLOCK = 64
SLC_TOPN = 16
WINDOW = 512
FORCE_SCORE = 1e9
GLA_DV = 256
GLA_HEADS = 4
GLA_DK = 128
GLA_GATE_RANK = 16
GLA_GATE_NORM = 16.0
GLA_CHUNK = 64
ROPE_THETA = 500000.0
ROPE_DIMS = 32
N_EXPERTS = 8
TOP_K = 2
LN_EPS = 1e-5
NORM_EPS = 1e-6
DEEPNORM_ALPHA = (2 * DEPTH) ** 0.25
NEG_BIG = -1e30

VMEM_LIMIT_BYTES = 56 * 1024 * 1024
LANES = 128

C_NQ = 0
C_GV = 1024
C_GG = 2048
C_KC = 3072
C_VC = 3328
C_KS = 3584
C_VS = 3840
C_KW = 4096
C_VW = 4352
C_GQ = 4608
C_GK = 5120
C_MISC = 5632
PROJ_W = 5760
MISC_GA = 24

NT_DIMS = (((1,), (1,)), ((), ()))


def _params(*sem):
    return pltpu.CompilerParams(dimension_semantics=sem, vmem_limit_bytes=VMEM_LIMIT_BYTES)


def _layer_norm(z, g, b):
    mu = jnp.mean(z, axis=-1, keepdims=True)
    zc = z - mu
    var = jnp.mean(zc * zc, axis=-1, keepdims=True)
    return zc * lax.rsqrt(var + LN_EPS) * g + b


def _ada_kernel(c_ref, w_ref, b_ref, o_ref):
    c = c_ref[...]
    cond = c * jax.nn.sigmoid(c)
    o_ref[0] = jnp.dot(cond.astype(BF16), w_ref[0].astype(BF16),
                       preferred_element_type=F32) + b_ref[0]


def ada_mod(c_pad, w_ada, b_ada):
    L, D, N = w_ada.shape
    tn = 1024
    return pl.pallas_call(
        _ada_kernel,
        grid=(L, N // tn),
        in_specs=[pl.BlockSpec((8, D), lambda l, j: (0, 0)),
                  pl.BlockSpec((1, D, tn), lambda l, j: (l, 0, j)),
                  pl.BlockSpec((1, 1, tn), lambda l, j: (l, 0, j))],
        out_specs=pl.BlockSpec((1, 8, tn), lambda l, j: (l, 0, j)),
        out_shape=jax.ShapeDtypeStruct((L, 8, N), F32),
        compiler_params=_params("parallel", "parallel"),
        name="ada_mod",
    )(c_pad, w_ada, b_ada)


def _inproj_kernel(x_ref, m_ref, w_ref, o_ref):
    h = x_ref[0] * (1.0 + m_ref[0, 1:2, :]) + m_ref[0, 0:1, :]
    o_ref[0] = jnp.dot(h.astype(BF16), w_ref[...], preferred_element_type=F32)


def in_proj(x, mod, w):
    B, T, D = x.shape
    N = w.shape[1]
    tm = min(512, T)
    tn = N // 3
    return pl.pallas_call(
        _inproj_kernel,
        grid=(N // tn, B, T // tm),
        in_specs=[pl.BlockSpec((1, tm, D), lambda j, b, i: (b, i, 0)),
                  pl.BlockSpec((1, 6, D), lambda j, b, i: (b, 0, 0)),
                  pl.BlockSpec((D, tn), lambda j, b, i: (0, j))],
        out_specs=pl.BlockSpec((1, tm, tn), lambda j, b, i: (b, i, j)),
        out_shape=jax.ShapeDtypeStruct((B, T, N), F32),
        compiler_params=_params("parallel", "parallel", "parallel"),
        name="in_proj",
    )(x, mod, w)


def _prep_kernel(q_ref, kv_ref, misc_ref, rc_ref, ra_ref, rb_ref,
                 qr_ref, qn_ref, ks_ref, vs_ref, kw_ref, vw_ref, g_ref):
    rc = rc_ref[...]
    ra = ra_ref[...]
    rb = rb_ref[...]
    scale = HEAD_DIM ** -0.5

    def rope(xh):
        return (xh * rc + pltpu.roll(xh, LANES - ROPE_DIMS // 2, 1) * ra
                + pltpu.roll(xh, ROPE_DIMS // 2, 1) * rb)

    for hq in range(NSA_HEADS):
        sl = slice(hq * HEAD_DIM, (hq + 1) * HEAD_DIM)
        xh = q_ref[0, :, sl]
        qr_ref[0, :, sl] = (rope(xh) * scale).astype(BF16)
        qn_ref[0, :, sl] = (xh * scale).astype(BF16)
    misc = misc_ref[0]
    for h in range(NSA_KV_HEADS):
        sl = slice(h * HEAD_DIM, (h + 1) * HEAD_DIM)
        ks_ref[0, h] = rope(kv_ref[0, :, (C_KS - C_KC) + h * HEAD_DIM:(C_KS - C_KC) + (h + 1) * HEAD_DIM]).astype(BF16)
        vs_ref[0, h] = kv_ref[0, :, (C_VS - C_KC) + h * HEAD_DIM:(C_VS - C_KC) + (h + 1) * HEAD_DIM].T.astype(BF16)
        kw_ref[0, h] = rope(kv_ref[0, :, (C_KW - C_KC) + h * HEAD_DIM:(C_KW - C_KC) + (h + 1) * HEAD_DIM]).astype(BF16)
        vw_ref[0, h] = kv_ref[0, :, (C_VW - C_KC) + h * HEAD_DIM:(C_VW - C_KC) + (h + 1) * HEAD_DIM].T.astype(BF16)
        shift = (LANES - 3 * NSA_GROUP * h) % LANES
        gm = misc if shift == 0 else pltpu.roll(misc, shift, 1)
        g_ref[0, h] = jax.nn.sigmoid(gm)
        del sl


def nsa_prep(proj, rc, ra, rb):
    B, T, _ = proj.shape
    tm = min(512, T)
    kvw = C_GQ - C_KC
    qspec = pl.BlockSpec((1, tm, NSA_HEADS * HEAD_DIM), lambda b, i: (b, i, 0))
    kvspec = pl.BlockSpec((1, NSA_KV_HEADS, tm, HEAD_DIM), lambda b, i: (b, 0, i, 0))
    vtspec = pl.BlockSpec((1, NSA_KV_HEADS, HEAD_DIM, tm), lambda b, i: (b, 0, 0, i))
    tspec = pl.BlockSpec((tm, LANES), lambda b, i: (i, 0))
    kv_shape = jax.ShapeDtypeStruct((B, NSA_KV_HEADS, T, HEAD_DIM), BF16)
    vt_shape = jax.ShapeDtypeStruct((B, NSA_KV_HEADS, HEAD_DIM, T), BF16)
    return pl.pallas_call(
        _prep_kernel,
        grid=(B, T // tm),
        in_specs=[pl.BlockSpec((1, tm, NSA_HEADS * HEAD_DIM), lambda b, i: (b, i, C_NQ // (NSA_HEADS * HEAD_DIM))),
                  pl.BlockSpec((1, tm, kvw), lambda b, i: (b, i, C_KC // kvw)),
                  pl.BlockSpec((1, tm, LANES), lambda b, i: (b, i, C_MISC // LANES)),
                  tspec, tspec, tspec],
        out_specs=[qspec, qspec, kvspec, vtspec, kvspec, vtspec, kvspec],
        out_shape=[jax.ShapeDtypeStruct((B, T, NSA_HEADS * HEAD_DIM), BF16),
                   jax.ShapeDtypeStruct((B, T, NSA_HEADS * HEAD_DIM), BF16),
                   kv_shape, vt_shape, kv_shape, vt_shape,
                   jax.ShapeDtypeStruct((B, NSA_KV_HEADS, T, LANES), F32)],
        compiler_params=_params("parallel", "parallel"),
        name="nsa_prep",
    )(proj, proj, proj, rc, ra, rb)


def _compress_kernel(a_ref, pe_ref, w1_ref, w2_ref, o_ref, ot_ref, *, n_half):
    half = CMP_STRIDE

    def part(l0):
        acc = jnp.zeros((n_half, CMP_HIDDEN), F32)
        for l in range(half):
            rows = a_ref[0, pl.ds(l, n_half, stride=half), :] + pe_ref[0, l0 + l:l0 + l + 1, :]
            acc += jnp.dot(rows.astype(BF16), w1_ref[0, (l0 + l) * HEAD_DIM:(l0 + l + 1) * HEAD_DIM, :],
                           preferred_element_type=F32)
        return acc

    first = part(0)
    second = part(half)
    hid = first + pltpu.roll(second, n_half - 1, 0)
    row = lax.broadcasted_iota(jnp.int32, (n_half, 1), 0)
    hid = jnp.where(row < n_half - 1, hid, 0.0)
    act = jax.nn.gelu(hid)
    out = jnp.dot(act.astype(BF16), w2_ref[0], preferred_element_type=F32)
    o_ref[0, 0, 0] = out.astype(BF16)
    ot_ref[0, 0, 0] = out.T.astype(BF16)


def nsa_compress(proj, pe, w1, w2):
    B, T, _ = proj.shape
    n_half = T // CMP_STRIDE
    return pl.pallas_call(
        functools.partial(_compress_kernel, n_half=n_half),
        grid=(B, 2, NSA_KV_HEADS),
        in_specs=[pl.BlockSpec((1, T, HEAD_DIM), lambda b, s, h: (b, 0, C_KC // HEAD_DIM + s * NSA_KV_HEADS + h)),
                  pl.BlockSpec((1, CMP_BLOCK, HEAD_DIM), lambda b, s, h: (s, 0, 0)),
                  pl.BlockSpec((1, CMP_BLOCK * HEAD_DIM, CMP_HIDDEN), lambda b, s, h: (s, 0, 0)),
                  pl.BlockSpec((1, CMP_HIDDEN, HEAD_DIM), lambda b, s, h: (s, 0, 0))],
        out_specs=[pl.BlockSpec((1, 1, 1, n_half, HEAD_DIM), lambda b, s, h: (b, s, h, 0, 0)),
                   pl.BlockSpec((1, 1, 1, HEAD_DIM, n_half), lambda b, s, h: (b, s, h, 0, 0))],
        out_shape=[jax.ShapeDtypeStruct((B, 2, NSA_KV_HEADS, n_half, HEAD_DIM), BF16),
                   jax.ShapeDtypeStruct((B, 2, NSA_KV_HEADS, HEAD_DIM, n_half), BF16)],
        compiler_params=_params("parallel", "parallel", "parallel"),
        name="nsa_compress",
    )(proj, pe, w1, w2)


def _nsa_kernel(qr_ref, qn_ref, g_ref, kc_ref, vct_ref, ks_ref, vst_ref, kw_ref, vwt_ref, ovt_ref, o_ref,
                *, TQ, TK, WK, NC, NB):
    G = NSA_GROUP
    R = G * TQ
    NBP = ovt_ref.shape[0]
    i = pl.program_id(2)
    t0 = i * TQ
    m_floor = 0.5 * NEG_BIG

    def stack(ref):
        x = ref[0]
        return jnp.concatenate([x[:, g * HEAD_DIM:(g + 1) * HEAD_DIM] for g in range(G)], axis=0)

    def tile_g(x):
        return jnp.concatenate([x] * G, axis=1)

    qr = stack(qr_ref)
    qn = stack(qn_ref)
    tT = t0 + lax.broadcasted_iota(jnp.int32, (1, TQ), 1)

    s_c = lax.dot_general(kc_ref[0, 0, 0], qn, NT_DIMS, preferred_element_type=F32)
    cend = lax.broadcasted_iota(jnp.int32, (NC, 1), 0) * CMP_STRIDE + (CMP_BLOCK - 1)
    s_c = s_c + tile_g(jnp.where(cend <= tT, 0.0, NEG_BIG))
    m_c = jnp.maximum(jnp.max(s_c, axis=0, keepdims=True), m_floor)
    e_c = jnp.exp(s_c - m_c)
    den = jnp.sum(e_c, axis=0, keepdims=True)
    p_cb = (e_c * jnp.where(den > 0.0, 1.0 / den, 0.0)).astype(BF16)
    o_c = jnp.dot(vct_ref[0, 0, 0], p_cb, preferred_element_type=F32)
    imp4 = jnp.dot(ovt_ref[...], p_cb, preferred_element_type=F32)
    impT = imp4[:, 0:TQ]
    for g in range(1, G):
        impT = impT + imp4[:, g * TQ:(g + 1) * TQ]

    cur = tT // SLC_BLOCK
    jj = lax.broadcasted_iota(jnp.int32, (NBP, TQ), 0)
    forced = jnp.where(jj == 0, 1, jnp.where(jj == cur, 1, jnp.where(jj == cur - 1, 1, 0)))
    val = jnp.where(jj > cur, -jnp.inf, jnp.where(forced > 0, FORCE_SCORE, impT))
    rank = jnp.zeros((NBP, TQ), jnp.int32)
    for j2 in range(NB):
        row = val[j2:j2 + 1, :]
        later = jnp.where(jj > j2, 1, 0)
        rank = rank + jnp.where(row > val, 1, jnp.where(row == val, later, 0))
    n_sel = min(SLC_TOPN, NB)
    sel = jnp.where(jj > cur, 0.0, jnp.where(rank < n_sel, 1.0, 0.0)).astype(BF16)

    krow = lax.broadcasted_iota(jnp.int32, (TK, 1), 0)
    kblk = lax.broadcasted_iota(jnp.int32, (TK, NBP), 0) // SLC_BLOCK
    jcol = lax.broadcasted_iota(jnp.int32, (TK, NBP), 1)

    def sel_body(kt, carry):
        m, l, acc = carry
        k0 = pl.multiple_of(kt * TK, TK)
        k = ks_ref[0, 0, pl.ds(k0, TK), :]
        vt = vst_ref[0, 0, :, pl.ds(k0, TK)]
        s = lax.dot_general(k, qr, NT_DIMS, preferred_element_type=F32)
        expand = jnp.where(kblk + kt * (TK // SLC_BLOCK) == jcol, 1.0, 0.0).astype(BF16)
        selk = jnp.dot(expand, sel, preferred_element_type=F32)
        bias = jnp.where(krow + k0 <= tT, (selk - 1.0) * (-NEG_BIG), NEG_BIG)
        s = s + tile_g(bias)
        m_new = jnp.maximum(m, jnp.max(s, axis=0, keepdims=True))
        alpha = jnp.exp(m - m_new)
        p = jnp.exp(s - m_new)
        l = alpha * l + jnp.sum(p, axis=0, keepdims=True)
        acc = alpha * acc + jnp.dot(vt, p.astype(BF16), preferred_element_type=F32)
        return m_new, l, acc

    n_kt = (t0 + TQ + TK - 1) // TK
    m0 = jnp.full((1, R), m_floor, F32)
    l0 = jnp.zeros((1, R), F32)
    a0 = jnp.zeros((HEAD_DIM, R), F32)
    _, l_s, acc_s = lax.fori_loop(0, n_kt, sel_body, (m0, l0, a0))
    o_s = acc_s * (1.0 / l_s)

    ws = pl.multiple_of(jnp.maximum(t0 + TQ - WK, 0), LANES)
    s_w = lax.dot_general(kw_ref[0, 0, pl.ds(ws, WK), :], qr, NT_DIMS, preferred_element_type=F32)
    dist = tT - (ws + lax.broadcasted_iota(jnp.int32, (WK, 1), 0))
    s_w = s_w + tile_g(jnp.where(dist >= 0, jnp.where(dist < WINDOW, 0.0, NEG_BIG), NEG_BIG))
    e_w = jnp.exp(s_w - jnp.max(s_w, axis=0, keepdims=True))
    l_w = jnp.sum(e_w, axis=0, keepdims=True)
    o_w = jnp.dot(vwt_ref[0, 0, :, pl.ds(ws, WK)], e_w.astype(BF16), preferred_element_type=F32) * (1.0 / l_w)

    gate = g_ref[0, 0].T
    for g in range(G):
        cols = slice(g * TQ, (g + 1) * TQ)
        o = (gate[3 * g:3 * g + 1, :] * o_c[:, cols] + gate[3 * g + 1:3 * g + 2, :] * o_s[:, cols]
             + gate[3 * g + 2:3 * g + 3, :] * o_w[:, cols])
        o_ref[0, :, g * HEAD_DIM:(g + 1) * HEAD_DIM] = o.T.astype(BF16)


def nsa_attention(qr, qn, gates, kc, vct, ks, vst, kw, vwt, overlap_t):
    B, T, _ = qr.shape
    TQ = 128
    TK = min(512, T)
    WK = min(WINDOW + TQ, T)
    NC = kc.shape[3]
    NB = T // SLC_BLOCK
    gw = NSA_GROUP * HEAD_DIM
    qspec = pl.BlockSpec((1, TQ, gw), lambda b, h, i: (b, i, h))
    kspec = pl.BlockSpec((1, 1, T, HEAD_DIM), lambda b, h, i: (b, h, 0, 0))
    vtspec = pl.BlockSpec((1, 1, HEAD_DIM, T), lambda b, h, i: (b, h, 0, 0))
    return pl.pallas_call(
        functools.partial(_nsa_kernel, TQ=TQ, TK=TK, WK=WK, NC=NC, NB=NB),
        grid=(B, NSA_KV_HEADS, T // TQ),
        in_specs=[qspec, qspec,
                  pl.BlockSpec((1, 1, TQ, LANES), lambda b, h, i: (b, h, i, 0)),
                  pl.BlockSpec((1, 1, 1, NC, HEAD_DIM), lambda b, h, i: (b, 0, h, 0, 0)),
                  pl.BlockSpec((1, 1, 1, HEAD_DIM, NC), lambda b, h, i: (b, 1, h, 0, 0)),
                  kspec, vtspec, kspec, vtspec,
                  pl.BlockSpec(overlap_t.shape, lambda b, h, i: (0, 0))],
        out_specs=pl.BlockSpec((1, TQ, gw), lambda b, h, i: (b, i, h)),
        out_shape=jax.ShapeDtypeStruct((B, T, NSA_HEADS * HEAD_DIM), BF16),
        compiler_params=_params("parallel", "parallel", "arbitrary"),
        name="nsa_attention",
    )(qr, qn, gates, kc, vct, ks, vst, kw, vwt, overlap_t)


def _gla_kernel(q_ref, k_ref, v_ref, gg_ref, misc_ref, wa_ref, ba_ref, nw_ref, o_ref, s_ref, la_ref, *, TC):
    C = GLA_CHUNK

    @pl.when(pl.program_id(1) == 0)
    def _():
        s_ref[...] = jnp.zeros_like(s_ref)

    z = jnp.dot(misc_ref[0].astype(BF16), wa_ref[...], preferred_element_type=F32) + ba_ref[...]
    la_ref[...] = (jnp.minimum(z, 0.0) - jnp.log1p(jnp.exp(-jnp.abs(z)))) * (1.0 / GLA_GATE_NORM)
    ri = lax.broadcasted_iota(jnp.int32, (C, C), 0)
    ci = lax.broadcasted_iota(jnp.int32, (C, C), 1)
    causal = ri >= ci
    tri = jnp.where(causal, 1.0, 0.0)
    nw = nw_ref[...]

    def chunk(c, carry):
        r0 = pl.multiple_of(c * C, C)
        b_all = jnp.dot(tri, la_ref[pl.ds(r0, C), :], preferred_element_type=F32,
                        precision=lax.Precision.HIGHEST)
        for h in range(GLA_HEADS):
            b = b_all[:, h * GLA_DK:(h + 1) * GLA_DK]
            bl = b[C - 1:C, :]
            q = q_ref[0, pl.ds(r0, C), h * GLA_DK:(h + 1) * GLA_DK] * (GLA_DK ** -0.5)
            k = k_ref[0, pl.ds(r0, C), h * GLA_DK:(h + 1) * GLA_DK]
            v = v_ref[0, pl.ds(r0, C), h * GLA_DV:(h + 1) * GLA_DV].astype(BF16)
            qd = (q * jnp.exp(b)).astype(BF16)
            ki = (k * jnp.exp(-b)).astype(BF16)
            kst = (k * jnp.exp(bl - b)).T.astype(BF16)
            decay = jnp.exp(b.T[:, C - 1:C])
            a = lax.dot_general(qd, ki, NT_DIMS, preferred_element_type=F32)
            a = jnp.where(causal, a, 0.0).astype(BF16)
            s_old = s_ref[h]
            o = (jnp.dot(a, v, preferred_element_type=F32)
                 + jnp.dot(qd, s_old.astype(BF16), preferred_element_type=F32))
            s_ref[h] = s_old * decay + jnp.dot(kst, v, preferred_element_type=F32)
            o = o * lax.rsqrt(jnp.mean(o * o, axis=-1, keepdims=True) + NORM_EPS) * nw
            gg = gg_ref[0, pl.ds(r0, C), h * GLA_DV:(h + 1) * GLA_DV]
            o_ref[0, pl.ds(r0, C), h * GLA_DV:(h + 1) * GLA_DV] = (o * (gg * jax.nn.sigmoid(gg))).astype(BF16)
        return carry

    lax.fori_loop(0, TC // C, chunk, 0)


def gla_mixer(proj, wa_pad, ba, nw):
    B, T, _ = proj.shape
    TC = min(512, T)
    qk_w = GLA_HEADS * GLA_DK
    v_w = GLA_HEADS * GLA_DV
    return pl.pallas_call(
        functools.partial(_gla_kernel, TC=TC),
        grid=(B, T // TC),
        in_specs=[pl.BlockSpec((1, TC, qk_w), lambda b, i: (b, i, C_GQ // qk_w)),
                  pl.BlockSpec((1, TC, qk_w), lambda b, i: (b, i, C_GK // qk_w)),
                  pl.BlockSpec((1, TC, v_w), lambda b, i: (b, i, C_GV // v_w)),
                  pl.BlockSpec((1, TC, v_w), lambda b, i: (b, i, C_GG // v_w)),
                  pl.BlockSpec((1, TC, LANES), lambda b, i: (b, i, C_MISC // LANES)),
                  pl.BlockSpec((LANES, qk_w), lambda b, i: (0, 0)),
                  pl.BlockSpec((1, qk_w), lambda b, i: (0, 0)),
                  pl.BlockSpec((1, GLA_DV), lambda b, i: (0, 0))],
        out_specs=pl.BlockSpec((1, TC, v_w), lambda b, i: (b, i, 0)),
        out_shape=jax.ShapeDtypeStruct((B, T, v_w), BF16),
        scratch_shapes=[pltpu.VMEM((GLA_HEADS, GLA_DK, GLA_DV), F32),
                        pltpu.VMEM((TC, qk_w), F32)],
        compiler_params=_params("parallel", "arbitrary"),
        name="gla_mixer",
    )(proj, proj, proj, proj, proj, wa_pad, ba, nw)


def _outproj_kernel(n_ref, g_ref, w_ref, x_ref, m_ref, lg_ref, lb_ref, o_ref, *, half):
    y = (jnp.dot(n_ref[0], w_ref[0:half, :], preferred_element_type=F32)
         + jnp.dot(g_ref[0], w_ref[half:, :], preferred_element_type=F32))
    z = DEEPNORM_ALPHA * x_ref[0] + (1.0 + m_ref[0, 2:3, :]) * y
    o_ref[0] = _layer_norm(z, lg_ref[...], lb_ref[...])


def out_proj_ln(nsa_o, gla_o, w, x, mod, lg, lb):
    B, T, D = x.shape
    half = nsa_o.shape[-1]
    tm = min(512, T)
    vec = pl.BlockSpec((1, D), lambda b, i: (0, 0))
    return pl.pallas_call(
        functools.partial(_outproj_kernel, half=half),
        grid=(B, T // tm),
        in_specs=[pl.BlockSpec((1, tm, half), lambda b, i: (b, i, 0)),
                  pl.BlockSpec((1, tm, gla_o.shape[-1]), lambda b, i: (b, i, 0)),
                  pl.BlockSpec(w.shape, lambda b, i: (0, 0)),
                  pl.BlockSpec((1, tm, D), lambda b, i: (b, i, 0)),
                  pl.BlockSpec((1, 6, D), lambda b, i: (b, 0, 0)),
                  vec, vec],
        out_specs=pl.BlockSpec((1, tm, D), lambda b, i: (b, i, 0)),
        out_shape=jax.ShapeDtypeStruct((B, T, D), F32),
        compiler_params=_params("parallel", "parallel"),
        name="out_proj_ln",
    )(nsa_o, gla_o, w, x, mod, lg, lb)


def _ffn_kernel(x_ref, m_ref, wg_ref, wu_ref, wd_ref, lg_ref, lb_ref, o_ref, h_sc, acc_sc):
    j = pl.program_id(2)

    @pl.when(j == 0)
    def _():
        h_sc[...] = (x_ref[0] * (1.0 + m_ref[0, 4:5, :]) + m_ref[0, 3:4, :]).astype(BF16)
        acc_sc[...] = jnp.zeros_like(acc_sc)

    h = h_sc[...]
    a = jnp.dot(h, wg_ref[...], preferred_element_type=F32)
    u = jnp.dot(h, wu_ref[...], preferred_element_type=F32)
    acc_sc[...] += jnp.dot((a * jax.nn.sigmoid(a) * u).astype(BF16), wd_ref[...], preferred_element_type=F32)

    @pl.when(j == pl.num_programs(2) - 1)
    def _():
        z = DEEPNORM_ALPHA * x_ref[0] + (1.0 + m_ref[0, 5:6, :]) * acc_sc[...]
        o_ref[0] = _layer_norm(z, lg_ref[...], lb_ref[...])


def ffn_ln(x, mod, wg, wu, wd, lg, lb):
    B, T, D = x.shape
    F = wg.shape[1]
    tm = min(512, T)
    tf = 512
    vec = pl.BlockSpec((1, D), lambda b, i, j: (0, 0))
    return pl.pallas_call(
        _ffn_kernel,
        grid=(B, T // tm, F // tf),
        in_specs=[pl.BlockSpec((1, tm, D), lambda b, i, j: (b, i, 0)),
                  pl.BlockSpec((1, 6, D), lambda b, i, j: (b, 0, 0)),
                  pl.BlockSpec((D, tf), lambda b, i, j: (0, j)),
                  pl.BlockSpec((D, tf), lambda b, i, j: (0, j)),
                  pl.BlockSpec((tf, D), lambda b, i, j: (j, 0)),
                  vec, vec],
        out_specs=pl.BlockSpec((1, tm, D), lambda b, i, j: (b, i, 0)),
        out_shape=jax.ShapeDtypeStruct((B, T, D), F32),
        scratch_shapes=[pltpu.VMEM((tm, D), BF16), pltpu.VMEM((tm, D), F32)],
        compiler_params=_params("parallel", "parallel", "arbitrary"),
        name="ffn_ln",
    )(x, mod, wg, wu, wd, lg, lb)


def _router_kernel(x_ref, m_ref, wr_ref, h_ref, lg_ref):
    h = x_ref[0] * (1.0 + m_ref[0, 4:5, :]) + m_ref[0, 3:4, :]
    h_ref[0] = h.astype(BF16)
    lg_ref[0] = jnp.dot(h, wr_ref[...], preferred_element_type=F32, precision=lax.Precision.HIGHEST)


def moe_router(x, mod, wr_pad):
    B, T, D = x.shape
    tm = min(512, T)
    return pl.pallas_call(
        _router_kernel,
        grid=(B, T // tm),
        in_specs=[pl.BlockSpec((1, tm, D), lambda b, i: (b, i, 0)),
                  pl.BlockSpec((1, 6, D), lambda b, i: (b, 0, 0)),
                  pl.BlockSpec((D, LANES), lambda b, i: (0, 0))],
        out_specs=[pl.BlockSpec((1, tm, D), lambda b, i: (b, i, 0)),
                   pl.BlockSpec((1, tm, LANES), lambda b, i: (b, i, 0))],
        out_shape=[jax.ShapeDtypeStruct((B, T, D), BF16),
                   jax.ShapeDtypeStruct((B, T, LANES), F32)],
        compiler_params=_params("parallel", "parallel"),
        name="moe_router",
    )(x, mod, wr_pad)


def _moe_kernel(te_ref, nv_ref, x_ref, wg_ref, wu_ref, wd_ref, o_ref, *, sub):
    i = pl.program_id(0)
    j = pl.program_id(1)
    nv = nv_ref[i]

    @pl.when(j == 0)
    def _():
        o_ref[...] = jnp.zeros_like(o_ref)

    @pl.when(nv > 0)
    def _():
        wg = wg_ref[0].astype(BF16)
        wu = wu_ref[0].astype(BF16)
        wd = wd_ref[0].astype(BF16)
        for sb in range(o_ref.shape[0] // sub):
            @pl.when(sb < nv)
            def _():
                rows = slice(sb * sub, (sb + 1) * sub)
                h = x_ref[rows, :]
                a = jnp.dot(h, wg, preferred_element_type=F32)
                u = jnp.dot(h, wu, preferred_element_type=F32)
                o_ref[rows, :] += jnp.dot((a * jax.nn.sigmoid(a) * u).astype(BF16), wd,
                                          preferred_element_type=F32)


def moe_experts(tile_e, tile_nv, xg, wg, wu, wd, tm, sub):
    M, D = xg.shape
    F = wg.shape[2]
    tf = 512
    nj = F // tf
    n_tiles = M // tm

    def wj(i, j, nv):
        return jnp.where(nv[i] > 0, j, nj - 1)

    return pl.pallas_call(
        functools.partial(_moe_kernel, sub=sub),
        grid_spec=pltpu.PrefetchScalarGridSpec(
            num_scalar_prefetch=2,
            grid=(n_tiles, nj),
            in_specs=[pl.BlockSpec((tm, D), lambda i, j, te, nv: (i, 0)),
                      pl.BlockSpec((1, D, tf), lambda i, j, te, nv: (te[i], 0, wj(i, j, nv))),
                      pl.BlockSpec((1, D, tf), lambda i, j, te, nv: (te[i], 0, wj(i, j, nv))),
                      pl.BlockSpec((1, tf, D), lambda i, j, te, nv: (te[i], wj(i, j, nv), 0))],
            out_specs=pl.BlockSpec((tm, D), lambda i, j, te, nv: (i, 0))),
        out_shape=jax.ShapeDtypeStruct((M, D), F32),
        compiler_params=pltpu.CompilerParams(dimension_semantics=("parallel", "arbitrary"),
                                             vmem_limit_bytes=MOE_VMEM_LIMIT_BYTES),
        name="moe_experts",
    )(tile_e, tile_nv, xg, wg, wu, wd)


def _combine_kernel(y0_ref, y1_ref, cw_ref, x_ref, m_ref, lg_ref, lb_ref, o_ref):
    cw = cw_ref[0]
    y = y0_ref[0] * cw[:, 0:1] + y1_ref[0] * cw[:, 1:2]
    z = DEEPNORM_ALPHA * x_ref[0] + (1.0 + m_ref[0, 5:6, :]) * y
    o_ref[0] = _layer_norm(z, lg_ref[...], lb_ref[...])


def moe_combine_ln(y0, y1, cw, x, mod, lg, lb):
    B, T, D = x.shape
    tm = min(512, T)
    row = pl.BlockSpec((1, tm, D), lambda b, i: (b, i, 0))
    vec = pl.BlockSpec((1, D), lambda b, i: (0, 0))
    return pl.pallas_call(
        _combine_kernel,
        grid=(B, T // tm),
        in_specs=[row, row, pl.BlockSpec((1, tm, LANES), lambda b, i: (b, i, 0)), row,
                  pl.BlockSpec((1, 6, D), lambda b, i: (b, 0, 0)), vec, vec],
        out_specs=row,
        out_shape=jax.ShapeDtypeStruct((B, T, D), F32),
        compiler_params=_params("parallel", "parallel"),
        name="moe_combine_ln",
    )(y0, y1, cw, x, mod, lg, lb)


MOE_TM = 1024
MOE_SUB = 256
MOE_VMEM_LIMIT_BYTES = 60 * 1024 * 1024


def moe_layer(x, mod, w_router, wg, wu, wd, lg, lb):
    B, T, D = x.shape
    N = B * T
    A = N * TOP_K
    tm = MOE_TM
    wr_pad = jnp.zeros((D, LANES), F32).at[:, :N_EXPERTS].set(w_router)
    h, logits = moe_router(x, mod, wr_pad)
    logits = logits.reshape(N, LANES)[:, :N_EXPERTS]
    top_val, top_idx = lax.top_k(logits, TOP_K)
    comb = jax.nn.softmax(top_val, axis=-1)
    flat_e = top_idx.reshape(-1).astype(jnp.int32)
    onehot = (flat_e[:, None] == jnp.arange(N_EXPERTS, dtype=jnp.int32)[None, :]).astype(jnp.int32)
    csum = jnp.cumsum(onehot, axis=0)
    pos = jnp.sum(csum * onehot, axis=1) - 1
    counts = csum[-1]
    padded = (counts + tm - 1) // tm * tm
    pad_end = jnp.cumsum(padded)
    pad_start = pad_end - padded
    slot = pad_start[flat_e] + pos
    n_tiles = -(-A // tm) + N_EXPERTS
    slot_tok = jnp.full((n_tiles * tm,), N, jnp.int32).at[slot].set(jnp.arange(A, dtype=jnp.int32) // TOP_K)
    tile_start = jnp.arange(n_tiles, dtype=jnp.int32) * tm
    tile_e = jnp.minimum(jnp.searchsorted(pad_end, tile_start, side='right'), N_EXPERTS - 1).astype(jnp.int32)
    valid = jnp.clip(pad_start[tile_e] + counts[tile_e] - tile_start, 0, tm)
    tile_nv = ((valid + MOE_SUB - 1) // MOE_SUB).astype(jnp.int32)
    n_used = pad_end[-1] // tm
    tile_e = jnp.where(jnp.arange(n_tiles) < n_used, tile_e, tile_e[jnp.maximum(n_used - 1, 0)])
    h_pad = jnp.concatenate([h.reshape(N, D), jnp.zeros((1, D), BF16)], axis=0)
    xg = h_pad[slot_tok]
    y = moe_experts(tile_e, tile_nv, xg, wg, wu, wd, tm, MOE_SUB)
    slot_of = slot.reshape(N, TOP_K)
    y0 = y[slot_of[:, 0]].reshape(B, T, D)
    y1 = y[slot_of[:, 1]].reshape(B, T, D)
    cw = jnp.zeros((N, LANES), F32).at[:, :TOP_K].set(comb).reshape(B, T, LANES)
    return moe_combine_ln(y0, y1, cw, x, mod, lg, lb)


def _rope_tables(T):
    half = ROPE_DIMS // 2
    inv = ROPE_THETA ** (-jnp.arange(half, dtype=F32) * 2.0 / ROPE_DIMS)
    ang = jnp.arange(T).astype(F32)[:, None] * inv[None, :]
    cos, sin = jnp.cos(ang), jnp.sin(ang)
    z = jnp.zeros((T, LANES - ROPE_DIMS), F32)
    zh = jnp.zeros((T, half), F32)
    rc = jnp.concatenate([cos, cos, jnp.ones((T, LANES - ROPE_DIMS), F32)], axis=1)
    ra = jnp.concatenate([-sin, zh, z], axis=1)
    rb = jnp.concatenate([zh, sin, z], axis=1)
    return rc, ra, rb


def _overlap_matrix_t(NC, NB):
    nbp = -(-NB // 16) * 16
    c0 = np.arange(NC)[None, :] * CMP_STRIDE
    b0 = np.arange(nbp)[:, None] * SLC_BLOCK
    ov = (c0 < b0 + SLC_BLOCK) & (c0 + CMP_BLOCK > b0) & (np.arange(nbp)[:, None] < NB)
    return jnp.asarray(ov.astype(np.float32), dtype=BF16)


def _reorder_w_in(w):
    D = w.shape[0]
    o_ng = 1024 + 6 * 256
    o_gq = o_ng + 24
    o_gk = o_gq + 512
    o_gv = o_gk + 512
    o_ga = o_gv + 1024
    o_gg = o_ga + GLA_GATE_RANK
    misc = jnp.concatenate([w[:, o_ng:o_gq], w[:, o_ga:o_gg],
                            jnp.zeros((D, LANES - 24 - GLA_GATE_RANK), w.dtype)], axis=1)
    cols = [w[:, 0:1024], w[:, o_gv:o_ga], w[:, o_gg:o_gg + 1024], w[:, 1024:o_ng],
            w[:, o_gq:o_gk], w[:, o_gk:o_gv], misc]
    return jnp.concatenate(cols, axis=1).astype(BF16)


def hybrid_mixer_ln(x, mod, w_in_r, cmp_pe, cmp_w1, cmp_w2, wa_pad, ba, nw, w_out, lg, lb, tables, overlap):
    proj = in_proj(x, mod, w_in_r)
    qr, qn, ks, vs, kw, vw, gates = nsa_prep(proj, *tables)
    kc, vct = nsa_compress(proj, cmp_pe, cmp_w1, cmp_w2)
    nsa_o = nsa_attention(qr, qn, gates, kc, vct, ks, vs, kw, vw, overlap)
    gla_o = gla_mixer(proj, wa_pad, ba, nw)
    return out_proj_ln(nsa_o, gla_o, w_out, x, mod, lg, lb)


def kernel(x, c, w_ada, b_ada, w_in, cmp_pos_k, cmp_w1_k, cmp_w2_k, cmp_pos_v, cmp_w1_v, cmp_w2_v, gla_w_a2, gla_b_a, gla_norm_w, w_out, ln_mix_g, ln_mix_b, ln_ffn_g, ln_ffn_b, ffn_w_gate, ffn_w_up, ffn_w_down, moe_router, moe_w_gate, moe_w_up, moe_w_down):
    B, T, D = x.shape
    L = w_ada.shape[0]
    c_pad = jnp.zeros((8, D), F32).at[:B].set(c)
    mod_all = ada_mod(c_pad, w_ada, b_ada.reshape(L, 1, 6 * D))[:, :B].reshape(L, B, 6, D)
    tables = _rope_tables(T)
    overlap = _overlap_matrix_t(T // CMP_STRIDE, T // SLC_BLOCK)
    for layer in range(L):
        mod = mod_all[layer]
        wa_pad = jnp.zeros((LANES, GLA_HEADS * GLA_DK), F32).at[MISC_GA:MISC_GA + GLA_GATE_RANK].set(
            gla_w_a2[layer]).astype(BF16)
        x = hybrid_mixer_ln(
            x, mod, _reorder_w_in(w_in[layer]),
            jnp.stack([cmp_pos_k[layer], cmp_pos_v[layer]]),
            jnp.stack([cmp_w1_k[layer], cmp_w1_v[layer]]).astype(BF16),
            jnp.stack([cmp_w2_k[layer], cmp_w2_v[layer]]).astype(BF16),
            wa_pad, gla_b_a[layer].reshape(1, -1), gla_norm_w[layer].reshape(1, -1),
            w_out[layer].astype(BF16), ln_mix_g[layer].reshape(1, D), ln_mix_b[layer].reshape(1, D),
            tables, overlap)
        lg = ln_ffn_g[layer].reshape(1, D)
        lb = ln_ffn_b[layer].reshape(1, D)
        i = layer // 2
        if layer % 2 == 0:
            x = ffn_ln(x, mod, ffn_w_gate[i].astype(BF16), ffn_w_up[i].astype(BF16),
                       ffn_w_down[i].astype(BF16), lg, lb)
        else:
            x = moe_layer(x, mod, moe_router[i], moe_w_gate[i], moe_w_up[i], moe_w_down[i], lg, lb)
    return x
```

```python
import functools

import numpy as np
import jax
import jax.numpy as jnp
from jax import lax
from jax.experimental import pallas as pl
from jax.experimental.pallas import tpu as pltpu

F32 = jnp.float32
BF16 = jnp.bfloat16

D_MODEL = 2048
DEPTH = 2
HEAD_DIM = 128
NSA_HEADS = 8
NSA_KV_HEADS = 2
NSA_GROUP = 4
CMP_BLOCK = 32
CMP_STRIDE = 16
CMP_HIDDEN = 256
SLC_BLOCK = 64
SLC_TOPN = 16
WINDOW = 512
FORCE_SCORE = 1e9
GLA_DV = 256
GLA_HEADS = 4
GLA_DK = 128
GLA_GATE_RANK = 16
GLA_GATE_NORM = 16.0
GLA_CHUNK = 64
ROPE_THETA = 500000.0
ROPE_DIMS = 32
N_EXPERTS = 8
TOP_K = 2
LN_EPS = 1e-5
NORM_EPS = 1e-6
DEEPNORM_ALPHA = (2 * DEPTH) ** 0.25
NEG_BIG = -1e30
MASK_BIG = 2.0 ** 100
V_AUG_ROWS = HEAD_DIM + 16
NSA_HEADS_PER_STEP = 2
LOG2_E = 1.4426950408889634

VMEM_LIMIT_BYTES = 56 * 1024 * 1024
LANES = 128

C_NQ = 0
C_GV = 1024
C_GG = 2048
C_KC = 3072
C_VC = 3328
C_KS = 3584
C_VS = 3840
C_KW = 4096
C_VW = 4352
C_GQ = 4608
C_GK = 5120
C_MISC = 5632
PROJ_W = 5760
MISC_GA = 24

NT_DIMS = (((1,), (1,)), ((), ()))


def _params(*sem):
    return pltpu.CompilerParams(dimension_semantics=sem, vmem_limit_bytes=VMEM_LIMIT_BYTES)


def _layer_norm(z, g, b):
    mu = jnp.mean(z, axis=-1, keepdims=True)
    zc = z - mu
    var = jnp.mean(zc * zc, axis=-1, keepdims=True)
    return zc * lax.rsqrt(var + LN_EPS) * g + b


def _ada_kernel(c_ref, w_ref, b_ref, o_ref):
    c = c_ref[...]
    cond = c * jax.nn.sigmoid(c)
    o_ref[0] = jnp.dot(cond.astype(BF16), w_ref[0].astype(BF16),
                       preferred_element_type=F32) + b_ref[0]


def ada_mod(c_pad, w_ada, b_ada):
    L, D, N = w_ada.shape
    tn = 1024
    return pl.pallas_call(
        _ada_kernel,
        grid=(L, N // tn),
        in_specs=[pl.BlockSpec((8, D), lambda l, j: (0, 0)),
                  pl.BlockSpec((1, D, tn), lambda l, j: (l, 0, j)),
                  pl.BlockSpec((1, 1, tn), lambda l, j: (l, 0, j))],
        out_specs=pl.BlockSpec((1, 8, tn), lambda l, j: (l, 0, j)),
        out_shape=jax.ShapeDtypeStruct((L, 8, N), F32),
        compiler_params=_params("parallel", "parallel"),
        name="ada_mod",
    )(c_pad, w_ada, b_ada)


def _inproj_kernel(x_ref, m_ref, w_ref, o_ref):
    h = x_ref[0] * (1.0 + m_ref[0, 1:2, :]) + m_ref[0, 0:1, :]
    o_ref[0] = jnp.dot(h.astype(BF16), w_ref[...], preferred_element_type=F32)


def in_proj(x, mod, w):
    B, T, D = x.shape
    N = w.shape[1]
    tm = min(512, T)
    tn = N // 3
    return pl.pallas_call(
        _inproj_kernel,
        grid=(N // tn, B, T // tm),
        in_specs=[pl.BlockSpec((1, tm, D), lambda j, b, i: (b, i, 0)),
                  pl.BlockSpec((1, 6, D), lambda j, b, i: (b, 0, 0)),
                  pl.BlockSpec((D, tn), lambda j, b, i: (0, j))],
        out_specs=pl.BlockSpec((1, tm, tn), lambda j, b, i: (b, i, j)),
        out_shape=jax.ShapeDtypeStruct((B, T, N), F32),
        compiler_params=_params("parallel", "parallel", "parallel"),
        name="in_proj",
    )(x, mod, w)


def _prep_kernel(q_ref, kv_ref, misc_ref, rc_ref, ra_ref, rb_ref,
                 qr_ref, qn_ref, ks_ref, vs_ref, kw_ref, vw_ref, g_ref):
    rc = rc_ref[...]
    ra = ra_ref[...]
    rb = rb_ref[...]
    scale = HEAD_DIM ** -0.5 * LOG2_E

    def rope(xh):
        return (xh * rc + pltpu.roll(xh, LANES - ROPE_DIMS // 2, 1) * ra
                + pltpu.roll(xh, ROPE_DIMS // 2, 1) * rb)

    for hq in range(NSA_HEADS):
        sl = slice(hq * HEAD_DIM, (hq + 1) * HEAD_DIM)
        xh = q_ref[0, :, sl]
        qr_ref[0, :, sl] = (rope(xh) * scale).astype(BF16)
        qn_ref[0, :, sl] = (xh * scale).astype(BF16)
    misc = misc_ref[0]
    tm = misc.shape[0]
    key_blk = (pl.program_id(1) * tm + lax.broadcasted_iota(jnp.int32, (tm, LANES), 0)) // SLC_BLOCK
    blk_onehot = jnp.where(key_blk == lax.broadcasted_iota(jnp.int32, (tm, LANES), 1), 1.0, 0.0).astype(BF16)
    for h in range(NSA_KV_HEADS):
        sl = slice(h * HEAD_DIM, (h + 1) * HEAD_DIM)
        ks_ref[0, h, :, 0:HEAD_DIM] = rope(
            kv_ref[0, :, (C_KS - C_KC) + h * HEAD_DIM:(C_KS - C_KC) + (h + 1) * HEAD_DIM]).astype(BF16)
        ks_ref[0, h, :, HEAD_DIM:] = blk_onehot
        vs_ref[0, h, 0:HEAD_DIM, :] = kv_ref[0, :, (C_VS - C_KC) + h * HEAD_DIM:(C_VS - C_KC) + (h + 1) * HEAD_DIM].T.astype(BF16)
        vs_ref[0, h, HEAD_DIM:, :] = jnp.ones((V_AUG_ROWS - HEAD_DIM, tm), BF16)
        kw_ref[0, h] = rope(kv_ref[0, :, (C_KW - C_KC) + h * HEAD_DIM:(C_KW - C_KC) + (h + 1) * HEAD_DIM]).astype(BF16)
        vw_ref[0, h] = kv_ref[0, :, (C_VW - C_KC) + h * HEAD_DIM:(C_VW - C_KC) + (h + 1) * HEAD_DIM].T.astype(BF16)
        shift = (LANES - 3 * NSA_GROUP * h) % LANES
        gm = misc if shift == 0 else pltpu.roll(misc, shift, 1)
        g_ref[0, h] = jax.nn.sigmoid(gm)
        del sl


def nsa_prep(proj, rc, ra, rb):
    B, T, _ = proj.shape
    tm = min(512, T)
    kvw = C_GQ - C_KC
    qspec = pl.BlockSpec((1, tm, NSA_HEADS * HEAD_DIM), lambda b, i: (b, i, 0))
    kvspec = pl.BlockSpec((1, NSA_KV_HEADS, tm, HEAD_DIM), lambda b, i: (b, 0, i, 0))
    vtspec = pl.BlockSpec((1, NSA_KV_HEADS, HEAD_DIM, tm), lambda b, i: (b, 0, 0, i))
    tspec = pl.BlockSpec((tm, LANES), lambda b, i: (i, 0))
    kv_shape = jax.ShapeDtypeStruct((B, NSA_KV_HEADS, T, HEAD_DIM), BF16)
    vt_shape = jax.ShapeDtypeStruct((B, NSA_KV_HEADS, HEAD_DIM, T), BF16)
    assert T // SLC_BLOCK <= LANES
    ks_spec = pl.BlockSpec((1, NSA_KV_HEADS, tm, HEAD_DIM + LANES), lambda b, i: (b, 0, i, 0))
    ks_shape = jax.ShapeDtypeStruct((B, NSA_KV_HEADS, T, HEAD_DIM + LANES), BF16)
    vs_spec = pl.BlockSpec((1, NSA_KV_HEADS, V_AUG_ROWS, tm), lambda b, i: (b, 0, 0, i))
    vs_shape = jax.ShapeDtypeStruct((B, NSA_KV_HEADS, V_AUG_ROWS, T), BF16)
    return pl.pallas_call(
        _prep_kernel,
        grid=(B, T // tm),
        in_specs=[pl.BlockSpec((1, tm, NSA_HEADS * HEAD_DIM), lambda b, i: (b, i, C_NQ // (NSA_HEADS * HEAD_DIM))),
                  pl.BlockSpec((1, tm, kvw), lambda b, i: (b, i, C_KC // kvw)),
                  pl.BlockSpec((1, tm, LANES), lambda b, i: (b, i, C_MISC // LANES)),
                  tspec, tspec, tspec],
        out_specs=[qspec, qspec, ks_spec, vs_spec, kvspec, vtspec, kvspec],
        out_shape=[jax.ShapeDtypeStruct((B, T, NSA_HEADS * HEAD_DIM), BF16),
                   jax.ShapeDtypeStruct((B, T, NSA_HEADS * HEAD_DIM), BF16),
                   ks_shape, vs_shape, kv_shape, vt_shape,
                   jax.ShapeDtypeStruct((B, NSA_KV_HEADS, T, LANES), F32)],
        compiler_params=_params("parallel", "parallel"),
        name="nsa_prep",
    )(proj, proj, proj, rc, ra, rb)


def _compress_kernel(a_ref, pe_ref, w1_ref, w2_ref, o_ref, ot_ref, *, n_half):
    half = CMP_STRIDE

    def part(l0):
        acc = jnp.zeros((n_half, CMP_HIDDEN), F32)
        for l in range(half):
            rows = a_ref[0, pl.ds(l, n_half, stride=half), :] + pe_ref[0, l0 + l:l0 + l + 1, :]
            acc += jnp.dot(rows.astype(BF16), w1_ref[0, (l0 + l) * HEAD_DIM:(l0 + l + 1) * HEAD_DIM, :],
                           preferred_element_type=F32)
        return acc

    first = part(0)
    second = part(half)
    hid = first + pltpu.roll(second, n_half - 1, 0)
    row = lax.broadcasted_iota(jnp.int32, (n_half, 1), 0)
    hid = jnp.where(row < n_half - 1, hid, 0.0)
    act = jax.nn.gelu(hid)
    out = jnp.dot(act.astype(BF16), w2_ref[0], preferred_element_type=F32)
    o_ref[0, 0, 0] = out.astype(BF16)
    ot_ref[0, 0, 0] = out.T.astype(BF16)


def nsa_compress(proj, pe, w1, w2):
    B, T, _ = proj.shape
    n_half = T // CMP_STRIDE
    return pl.pallas_call(
        functools.partial(_compress_kernel, n_half=n_half),
        grid=(B, 2, NSA_KV_HEADS),
        in_specs=[pl.BlockSpec((1, T, HEAD_DIM), lambda b, s, h: (b, 0, C_KC // HEAD_DIM + s * NSA_KV_HEADS + h)),
                  pl.BlockSpec((1, CMP_BLOCK, HEAD_DIM), lambda b, s, h: (s, 0, 0)),
                  pl.BlockSpec((1, CMP_BLOCK * HEAD_DIM, CMP_HIDDEN), lambda b, s, h: (s, 0, 0)),
                  pl.BlockSpec((1, CMP_HIDDEN, HEAD_DIM), lambda b, s, h: (s, 0, 0))],
        out_specs=[pl.BlockSpec((1, 1, 1, n_half, HEAD_DIM), lambda b, s, h: (b, s, h, 0, 0)),
                   pl.BlockSpec((1, 1, 1, HEAD_DIM, n_half), lambda b, s, h: (b, s, h, 0, 0))],
        out_shape=[jax.ShapeDtypeStruct((B, 2, NSA_KV_HEADS, n_half, HEAD_DIM), BF16),
                   jax.ShapeDtypeStruct((B, 2, NSA_KV_HEADS, HEAD_DIM, n_half), BF16)],
        compiler_params=_params("parallel", "parallel", "parallel"),
        name="nsa_compress",
    )(proj, pe, w1, w2)


def _nsa_kernel(qr_ref, qn_ref, g_ref, kc_ref, vct_ref, ks_ref, vst_ref, kw_ref, vwt_ref, ovt_ref, o_ref,
                *, TQ, TK, WK, NC, NB, NH):
    G = NSA_GROUP
    R = G * TQ
    NBP = ovt_ref.shape[0]
    i = pl.program_id(2)
    t0 = i * TQ
    m_floor = 0.5 * NEG_BIG

    def stack(ref, h):
        x = ref[0]
        return jnp.concatenate([x[:, (h * G + g) * HEAD_DIM:(h * G + g + 1) * HEAD_DIM] for g in range(G)], axis=0)

    def tile_g(x):
        return jnp.concatenate([x] * G, axis=1)

    tT = t0 + lax.broadcasted_iota(jnp.int32, (1, TQ), 1)
    cur = tT // SLC_BLOCK
    n_sel = min(SLC_TOPN, NB)

    def head_front(h):
        qr = stack(qr_ref, h)
        qn = stack(qn_ref, h)

        s_c = lax.dot_general(kc_ref[0, 0, h], qn, NT_DIMS, preferred_element_type=F32)
        cend = lax.broadcasted_iota(jnp.int32, (NC, 1), 0) * CMP_STRIDE + (CMP_BLOCK - 1)
        s_c = s_c + tile_g(jnp.where(cend <= tT, 0.0, NEG_BIG))
        m_c = jnp.maximum(jnp.max(s_c, axis=0, keepdims=True), m_floor)
        e_c = jnp.exp2(s_c - m_c)
        den = jnp.sum(e_c, axis=0, keepdims=True)
        p_cb = (e_c * jnp.where(den > 0.0, 1.0 / den, 0.0)).astype(BF16)
        o_c = jnp.dot(vct_ref[0, 0, h], p_cb, preferred_element_type=F32)
        imp4 = jnp.dot(ovt_ref[...], p_cb, preferred_element_type=F32)
        impT = imp4[:, 0:TQ]
        for g in range(1, G):
            impT = impT + imp4[:, g * TQ:(g + 1) * TQ]

        ws = pl.multiple_of(jnp.maximum(t0 + TQ - WK, 0), LANES)
        s_w = lax.dot_general(kw_ref[0, h, pl.ds(ws, WK), :], qr, NT_DIMS, preferred_element_type=F32)
        dist = tT - (ws + lax.broadcasted_iota(jnp.int32, (WK, 1), 0))
        s_w = s_w + tile_g(jnp.where(dist >= 0, jnp.where(dist < WINDOW, 0.0, NEG_BIG), NEG_BIG))
        e_w = jnp.exp2(s_w - jnp.max(s_w, axis=0, keepdims=True))
        l_w = jnp.sum(e_w, axis=0, keepdims=True)
        o_w = jnp.dot(vwt_ref[0, h, :, pl.ds(ws, WK)], e_w.astype(BF16),
                      preferred_element_type=F32) * (1.0 / l_w)

        jj = lax.broadcasted_iota(jnp.int32, (NBP, TQ), 0)
        forced = jnp.where(jj == 0, 1, jnp.where(jj == cur, 1, jnp.where(jj == cur - 1, 1, 0)))
        val = jnp.where(jj > cur, -jnp.inf, jnp.where(forced > 0, FORCE_SCORE, impT))
        rank = jnp.zeros((NBP, TQ), jnp.int32)
        for j2 in range(NB):
            row = val[j2:j2 + 1, :]
            later = jnp.where(jj > j2, 1, 0)
            rank = rank + jnp.where(row > val, 1, jnp.where(row == val, later, 0))
        sel_neg = jnp.where(jj > cur, -MASK_BIG, jnp.where(rank < n_sel, 0.0, -MASK_BIG))
        if NBP < LANES:
            sel_neg = jnp.concatenate([sel_neg, jnp.zeros((LANES - NBP, TQ), F32)], axis=0)
        sel_q = sel_neg.T.astype(BF16)
        q_aug = jnp.concatenate([qr, jnp.concatenate([sel_q] * G, axis=0)], axis=1)
        return o_c, o_w, q_aug

    fronts = [head_front(h) for h in range(NH)]

    def sel_scores(h, kt):
        k0 = pl.multiple_of(kt * TK, TK)
        k = ks_ref[0, h, pl.ds(k0, TK), :]
        s = lax.dot_general(k, fronts[h][2], NT_DIMS, preferred_element_type=F32)
        return s, jnp.max(s, axis=0, keepdims=True)

    def sel_accumulate(h, kt, s, smax, m, acc):
        k0 = pl.multiple_of(kt * TK, TK)
        vt = vst_ref[0, h, :, pl.ds(k0, TK)]
        m_new = jnp.maximum(m, smax)
        p = jnp.exp2(s - m_new).astype(BF16)
        acc = jnp.exp2(m - m_new) * acc + jnp.dot(vt, p, preferred_element_type=F32)
        return m_new, acc

    def sel_body(kt, carry):
        out = []
        for h in range(NH):
            s, smax, m, acc = carry[h]
            s_next, smax_next = sel_scores(h, kt + 1)
            m, acc = sel_accumulate(h, kt, s, smax, m, acc)
            out.append((s_next, smax_next, m, acc))
        return tuple(out)

    n_kt = (t0 + TQ + TK - 1) // TK
    m0 = jnp.full((1, R), m_floor, F32)
    a0 = jnp.zeros((V_AUG_ROWS, R), F32)
    state = lax.fori_loop(0, n_kt - 1, sel_body, tuple(sel_scores(h, 0) + (m0, a0) for h in range(NH)))
    krow = (n_kt - 1) * TK + lax.broadcasted_iota(jnp.int32, (TK, 1), 0)
    causal = tile_g(jnp.where(krow <= tT, 0.0, NEG_BIG))
    for h in range(NH):
        s_last, _, m_s, acc_s = state[h]
        s_last = s_last + causal
        _, acc_s = sel_accumulate(h, n_kt - 1, s_last, jnp.max(s_last, axis=0, keepdims=True), m_s, acc_s)
        o_s = acc_s[0:HEAD_DIM] * (1.0 / acc_s[HEAD_DIM:HEAD_DIM + 1])

        o_c, o_w, _ = fronts[h]
        gate = g_ref[0, h].T
        for g in range(G):
            cols = slice(g * TQ, (g + 1) * TQ)
            o = (gate[3 * g:3 * g + 1, :] * o_c[:, cols] + gate[3 * g + 1:3 * g + 2, :] * o_s[:, cols]
                 + gate[3 * g + 2:3 * g + 3, :] * o_w[:, cols])
            o_ref[0, :, (h * G + g) * HEAD_DIM:(h * G + g + 1) * HEAD_DIM] = o.T.astype(BF16)


def nsa_attention(qr, qn, gates, kc, vct, ks, vst, kw, vwt, overlap_t):
    B, T, _ = qr.shape
    TQ = 128
    TK = min(512, T)
    WK = min(WINDOW + TQ, T)
    NC = kc.shape[3]
    NB = T // SLC_BLOCK
    NH = NSA_HEADS_PER_STEP
    gw = NH * NSA_GROUP * HEAD_DIM
    qspec = pl.BlockSpec((1, TQ, gw), lambda b, h, i: (b, i, h))
    kspec = pl.BlockSpec((1, NH, T, HEAD_DIM), lambda b, h, i: (b, h, 0, 0))
    vtspec = pl.BlockSpec((1, NH, HEAD_DIM, T), lambda b, h, i: (b, h, 0, 0))
    return pl.pallas_call(
        functools.partial(_nsa_kernel, TQ=TQ, TK=TK, WK=WK, NC=NC, NB=NB, NH=NH),
        grid=(B, NSA_KV_HEADS // NH, T // TQ),
        in_specs=[qspec, qspec,
                  pl.BlockSpec((1, NH, TQ, LANES), lambda b, h, i: (b, h, i, 0)),
                  pl.BlockSpec((1, 1, NH, NC, HEAD_DIM), lambda b, h, i: (b, 0, h, 0, 0)),
                  pl.BlockSpec((1, 1, NH, HEAD_DIM, NC), lambda b, h, i: (b, 1, h, 0, 0)),
                  pl.BlockSpec((1, NH, T, HEAD_DIM + LANES), lambda b, h, i: (b, h, 0, 0)),
                  pl.BlockSpec((1, NH, V_AUG_ROWS, T), lambda b, h, i: (b, h, 0, 0)),
                  kspec, vtspec,
                  pl.BlockSpec(overlap_t.shape, lambda b, h, i: (0, 0))],
        out_specs=pl.BlockSpec((1, TQ, gw), lambda b, h, i: (b, i, h)),
        out_shape=jax.ShapeDtypeStruct((B, T, NSA_HEADS * HEAD_DIM), BF16),
        compiler_params=_params("parallel", "parallel", "arbitrary"),
        name="nsa_attention",
    )(qr, qn, gates, kc, vct, ks, vst, kw, vwt, overlap_t)


def _gla_kernel(q_ref, k_ref, v_ref, gg_ref, misc_ref, wa_ref, ba_ref, nw_ref, o_ref, s_ref, la_ref, *, TC):
    C = GLA_CHUNK

    @pl.when(pl.program_id(1) == 0)
    def _():
        s_ref[...] = jnp.zeros_like(s_ref)

    z = jnp.dot(misc_ref[0].astype(BF16), wa_ref[...], preferred_element_type=F32) + ba_ref[...]
    la_ref[...] = (jnp.minimum(z, 0.0) - jnp.log1p(jnp.exp(-jnp.abs(z)))) * (1.0 / GLA_GATE_NORM)
    ri = lax.broadcasted_iota(jnp.int32, (C, C), 0)
    ci = lax.broadcasted_iota(jnp.int32, (C, C), 1)
    causal = ri >= ci
    tri = jnp.where(causal, 1.0, 0.0)
    nw = nw_ref[...]

    def chunk(c, carry):
        r0 = pl.multiple_of(c * C, C)
        b_all = jnp.dot(tri, la_ref[pl.ds(r0, C), :], preferred_element_type=F32,
                        precision=lax.Precision.HIGHEST)
        for h in range(GLA_HEADS):
            b = b_all[:, h * GLA_DK:(h + 1) * GLA_DK]
            bl = b[C - 1:C, :]
            q = q_ref[0, pl.ds(r0, C), h * GLA_DK:(h + 1) * GLA_DK] * (GLA_DK ** -0.5)
            k = k_ref[0, pl.ds(r0, C), h * GLA_DK:(h + 1) * GLA_DK]
            v = v_ref[0, pl.ds(r0, C), h * GLA_DV:(h + 1) * GLA_DV].astype(BF16)
            qd = (q * jnp.exp(b)).astype(BF16)
            ki = (k * jnp.exp(-b)).astype(BF16)
            kst = (k * jnp.exp(bl - b)).T.astype(BF16)
            decay = jnp.exp(b.T[:, C - 1:C])
            a = lax.dot_general(qd, ki, NT_DIMS, preferred_element_type=F32)
            a = jnp.where(causal, a, 0.0).astype(BF16)
            s_old = s_ref[h]
            o = (jnp.dot(a, v, preferred_element_type=F32)
                 + jnp.dot(qd, s_old.astype(BF16), preferred_element_type=F32))
            s_ref[h] = s_old * decay + jnp.dot(kst, v, preferred_element_type=F32)
            o = o * lax.rsqrt(jnp.mean(o * o, axis=-1, keepdims=True) + NORM_EPS) * nw
            gg = gg_ref[0, pl.ds(r0, C), h * GLA_DV:(h + 1) * GLA_DV]
            o_ref[0, pl.ds(r0, C), h * GLA_DV:(h + 1) * GLA_DV] = (o * (gg * jax.nn.sigmoid(gg))).astype(BF16)
        return carry

    lax.fori_loop(0, TC // C, chunk, 0)


def gla_mixer(proj, wa_pad, ba, nw):
    B, T, _ = proj.shape
    TC = min(512, T)
    qk_w = GLA_HEADS * GLA_DK
    v_w = GLA_HEADS * GLA_DV
    return pl.pallas_call(
        functools.partial(_gla_kernel, TC=TC),
        grid=(B, T // TC),
        in_specs=[pl.BlockSpec((1, TC, qk_w), lambda b, i: (b, i, C_GQ // qk_w)),
                  pl.BlockSpec((1, TC, qk_w), lambda b, i: (b, i, C_GK // qk_w)),
                  pl.BlockSpec((1, TC, v_w), lambda b, i: (b, i, C_GV // v_w)),
                  pl.BlockSpec((1, TC, v_w), lambda b, i: (b, i, C_GG // v_w)),
                  pl.BlockSpec((1, TC, LANES), lambda b, i: (b, i, C_MISC // LANES)),
                  pl.BlockSpec((LANES, qk_w), lambda b, i: (0, 0)),
                  pl.BlockSpec((1, qk_w), lambda b, i: (0, 0)),
                  pl.BlockSpec((1, GLA_DV), lambda b, i: (0, 0))],
        out_specs=pl.BlockSpec((1, TC, v_w), lambda b, i: (b, i, 0)),
        out_shape=jax.ShapeDtypeStruct((B, T, v_w), BF16),
        scratch_shapes=[pltpu.VMEM((GLA_HEADS, GLA_DK, GLA_DV), F32),
                        pltpu.VMEM((TC, qk_w), F32)],
        compiler_params=_params("parallel", "arbitrary"),
        name="gla_mixer",
    )(proj, proj, proj, proj, proj, wa_pad, ba, nw)


def _outproj_kernel(n_ref, g_ref, w_ref, x_ref, m_ref, lg_ref, lb_ref, o_ref, *, half):
    y = (jnp.dot(n_ref[0], w_ref[0:half, :], preferred_element_type=F32)
         + jnp.dot(g_ref[0], w_ref[half:, :], preferred_element_type=F32))
    z = DEEPNORM_ALPHA * x_ref[0] + (1.0 + m_ref[0, 2:3, :]) * y
    o_ref[0] = _layer_norm(z, lg_ref[...], lb_ref[...])


def out_proj_ln(nsa_o, gla_o, w, x, mod, lg, lb):
    B, T, D = x.shape
    half = nsa_o.shape[-1]
    tm = min(512, T)
    vec = pl.BlockSpec((1, D), lambda b, i: (0, 0))
    return pl.pallas_call(
        functools.partial(_outproj_kernel, half=half),
        grid=(B, T // tm),
        in_specs=[pl.BlockSpec((1, tm, half), lambda b, i: (b, i, 0)),
                  pl.BlockSpec((1, tm, gla_o.shape[-1]), lambda b, i: (b, i, 0)),
                  pl.BlockSpec(w.shape, lambda b, i: (0, 0)),
                  pl.BlockSpec((1, tm, D), lambda b, i: (b, i, 0)),
                  pl.BlockSpec((1, 6, D), lambda b, i: (b, 0, 0)),
                  vec, vec],
        out_specs=pl.BlockSpec((1, tm, D), lambda b, i: (b, i, 0)),
        out_shape=jax.ShapeDtypeStruct((B, T, D), F32),
        compiler_params=_params("parallel", "parallel"),
        name="out_proj_ln",
    )(nsa_o, gla_o, w, x, mod, lg, lb)


def _ffn_kernel(x_ref, m_ref, wg_ref, wu_ref, wd_ref, lg_ref, lb_ref, o_ref, h_sc, acc_sc):
    j = pl.program_id(2)

    @pl.when(j == 0)
    def _():
        h_sc[...] = (x_ref[0] * (1.0 + m_ref[0, 4:5, :]) + m_ref[0, 3:4, :]).astype(BF16)
        acc_sc[...] = jnp.zeros_like(acc_sc)

    h = h_sc[...]
    a = jnp.dot(h, wg_ref[...], preferred_element_type=F32)
    u = jnp.dot(h, wu_ref[...], preferred_element_type=F32)
    acc_sc[...] += jnp.dot((a * jax.nn.sigmoid(a) * u).astype(BF16), wd_ref[...], preferred_element_type=F32)

    @pl.when(j == pl.num_programs(2) - 1)
    def _():
        z = DEEPNORM_ALPHA * x_ref[0] + (1.0 + m_ref[0, 5:6, :]) * acc_sc[...]
        o_ref[0] = _layer_norm(z, lg_ref[...], lb_ref[...])


def ffn_ln(x, mod, wg, wu, wd, lg, lb):
    B, T, D = x.shape
    F = wg.shape[1]
    tm = min(512, T)
    tf = 512
    vec = pl.BlockSpec((1, D), lambda b, i, j: (0, 0))
    return pl.pallas_call(
        _ffn_kernel,
        grid=(B, T // tm, F // tf),
        in_specs=[pl.BlockSpec((1, tm, D), lambda b, i, j: (b, i, 0)),
                  pl.BlockSpec((1, 6, D), lambda b, i, j: (b, 0, 0)),
                  pl.BlockSpec((D, tf), lambda b, i, j: (0, j)),
                  pl.BlockSpec((D, tf), lambda b, i, j: (0, j)),
                  pl.BlockSpec((tf, D), lambda b, i, j: (j, 0)),
                  vec, vec],
        out_specs=pl.BlockSpec((1, tm, D), lambda b, i, j: (b, i, 0)),
        out_shape=jax.ShapeDtypeStruct((B, T, D), F32),
        scratch_shapes=[pltpu.VMEM((tm, D), BF16), pltpu.VMEM((tm, D), F32)],
        compiler_params=_params("parallel", "parallel", "arbitrary"),
        name="ffn_ln",
    )(x, mod, wg, wu, wd, lg, lb)


def _router_kernel(x_ref, m_ref, wr_ref, h_ref, lg_ref):
    h = x_ref[0] * (1.0 + m_ref[0, 4:5, :]) + m_ref[0, 3:4, :]
    h_ref[0] = h.astype(BF16)
    lg_ref[0] = jnp.dot(h, wr_ref[...], preferred_element_type=F32, precision=lax.Precision.HIGHEST)


def moe_router(x, mod, wr_pad):
    B, T, D = x.shape
    tm = min(512, T)
    return pl.pallas_call(
        _router_kernel,
        grid=(B, T // tm),
        in_specs=[pl.BlockSpec((1, tm, D), lambda b, i: (b, i, 0)),
                  pl.BlockSpec((1, 6, D), lambda b, i: (b, 0, 0)),
                  pl.BlockSpec((D, LANES), lambda b, i: (0, 0))],
        out_specs=[pl.BlockSpec((1, tm, D), lambda b, i: (b, i, 0)),
                   pl.BlockSpec((1, tm, LANES), lambda b, i: (b, i, 0))],
        out_shape=[jax.ShapeDtypeStruct((B, T, D), BF16),
                   jax.ShapeDtypeStruct((B, T, LANES), F32)],
        compiler_params=_params("parallel", "parallel"),
        name="moe_router",
    )(x, mod, wr_pad)


def _moe_kernel(te_ref, nv_ref, x_ref, wg_ref, wu_ref, wd_ref, o_ref, *, sub):
    i = pl.program_id(0)
    j = pl.program_id(1)
    nv = nv_ref[i]

    @pl.when(j == 0)
    def _():
        o_ref[...] = jnp.zeros_like(o_ref)

    n_sub = o_ref.shape[0] // sub

    def swiglu_rows(rows, wg, wu, wd):
        h = x_ref[rows, :]
        a = jnp.dot(h, wg, preferred_element_type=F32)
        u = jnp.dot(h, wu, preferred_element_type=F32)
        o_ref[rows, :] += jnp.dot((a * jax.nn.sigmoid(a) * u).astype(BF16), wd, preferred_element_type=F32)

    @pl.when(nv == n_sub)
    def _():
        wg = wg_ref[0].astype(BF16)
        wu = wu_ref[0].astype(BF16)
        wd = wd_ref[0].astype(BF16)
        for sb in range(n_sub):
            swiglu_rows(slice(sb * sub, (sb + 1) * sub), wg, wu, wd)

    @pl.when(jnp.logical_and(nv > 0, nv < n_sub))
    def _():
        wg = wg_ref[0].astype(BF16)
        wu = wu_ref[0].astype(BF16)
        wd = wd_ref[0].astype(BF16)
        for sb in range(n_sub - 1):
            @pl.when(sb < nv)
            def _():
                swiglu_rows(slice(sb * sub, (sb + 1) * sub), wg, wu, wd)


def moe_experts(tile_e, tile_nv, xg, wg, wu, wd, tm, sub):
    M, D = xg.shape
    F = wg.shape[2]
    tf = 512
    nj = F // tf
    n_tiles = M // tm

    def wj(i, j, nv):
        return jnp.where(nv[i] > 0, j, nj - 1)

    return pl.pallas_call(
        functools.partial(_moe_kernel, sub=sub),
        grid_spec=pltpu.PrefetchScalarGridSpec(
            num_scalar_prefetch=2,
            grid=(n_tiles, nj),
            in_specs=[pl.BlockSpec((tm, D), lambda i, j, te, nv: (i, 0)),
                      pl.BlockSpec((1, D, tf), lambda i, j, te, nv: (te[i], 0, wj(i, j, nv))),
                      pl.BlockSpec((1, D, tf), lambda i, j, te, nv: (te[i], 0, wj(i, j, nv))),
                      pl.BlockSpec((1, tf, D), lambda i, j, te, nv: (te[i], wj(i, j, nv), 0))],
            out_specs=pl.BlockSpec((tm, D), lambda i, j, te, nv: (i, 0))),
        out_shape=jax.ShapeDtypeStruct((M, D), F32),
        compiler_params=pltpu.CompilerParams(dimension_semantics=("parallel", "arbitrary"),
                                             vmem_limit_bytes=MOE_VMEM_LIMIT_BYTES),
        name="moe_experts",
    )(tile_e, tile_nv, xg, wg, wu, wd)


def _combine_kernel(y0_ref, y1_ref, cw_ref, x_ref, m_ref, lg_ref, lb_ref, o_ref):
    cw = cw_ref[0]
    y = y0_ref[0] * cw[:, 0:1] + y1_ref[0] * cw[:, 1:2]
    z = DEEPNORM_ALPHA * x_ref[0] + (1.0 + m_ref[0, 5:6, :]) * y
    o_ref[0] = _layer_norm(z, lg_ref[...], lb_ref[...])


def moe_combine_ln(y0, y1, cw, x, mod, lg, lb):
    B, T, D = x.shape
    tm = min(512, T)
    row = pl.BlockSpec((1, tm, D), lambda b, i: (b, i, 0))
    vec = pl.BlockSpec((1, D), lambda b, i: (0, 0))
    return pl.pallas_call(
        _combine_kernel,
        grid=(B, T // tm),
        in_specs=[row, row, pl.BlockSpec((1, tm, LANES), lambda b, i: (b, i, 0)), row,
                  pl.BlockSpec((1, 6, D), lambda b, i: (b, 0, 0)), vec, vec],
        out_specs=row,
        out_shape=jax.ShapeDtypeStruct((B, T, D), F32),
        compiler_params=_params("parallel", "parallel"),
        name="moe_combine_ln",
    )(y0, y1, cw, x, mod, lg, lb)


MOE_TM = 1024
MOE_SUB = 256
MOE_VMEM_LIMIT_BYTES = 60 * 1024 * 1024


def moe_layer(x, mod, w_router, wg, wu, wd, lg, lb):
    B, T, D = x.shape
    N = B * T
    A = N * TOP_K
    tm = MOE_TM
    wr_pad = jnp.zeros((D, LANES), F32).at[:, :N_EXPERTS].set(w_router)
    h, logits = moe_router(x, mod, wr_pad)
    logits = logits.reshape(N, LANES)[:, :N_EXPERTS]
    top_val, top_idx = lax.top_k(logits, TOP_K)
    comb = jax.nn.softmax(top_val, axis=-1)
    flat_e = top_idx.reshape(-1).astype(jnp.int32)
    onehot = (flat_e[:, None] == jnp.arange(N_EXPERTS, dtype=jnp.int32)[None, :]).astype(jnp.int32)
    csum = jnp.cumsum(onehot, axis=0)
    counts = csum[-1]
    padded = (counts + tm - 1) // tm * tm
    pad_end = jnp.cumsum(padded)
    pad_start = pad_end - padded
    slot = jnp.sum(onehot * (csum + pad_start[None, :]), axis=1) - 1
    n_tiles = -(-A // tm) + N_EXPERTS
    slot_tok = jnp.full((n_tiles * tm,), N, jnp.int32).at[slot].set(jnp.arange(A, dtype=jnp.int32) // TOP_K)
    tile_start = jnp.arange(n_tiles, dtype=jnp.int32) * tm
    tile_e = jnp.minimum(jnp.searchsorted(pad_end, tile_start, side='right'), N_EXPERTS - 1).astype(jnp.int32)
    valid = jnp.clip(pad_start[tile_e] + counts[tile_e] - tile_start, 0, tm)
    tile_nv = ((valid + MOE_SUB - 1) // MOE_SUB).astype(jnp.int32)
    n_used = pad_end[-1] // tm
    tile_e = jnp.where(jnp.arange(n_tiles) < n_used, tile_e, tile_e[jnp.maximum(n_used - 1, 0)])
    h_pad = jnp.concatenate([h.reshape(N, D), jnp.zeros((1, D), BF16)], axis=0)
    xg = h_pad[slot_tok]
    y = moe_experts(tile_e, tile_nv, xg, wg, wu, wd, tm, MOE_SUB)
    slot_of = slot.reshape(N, TOP_K)
    y0 = y[slot_of[:, 0]].reshape(B, T, D)
    y1 = y[slot_of[:, 1]].reshape(B, T, D)
    cw = jnp.zeros((N, LANES), F32).at[:, :TOP_K].set(comb).reshape(B, T, LANES)
    return moe_combine_ln(y0, y1, cw, x, mod, lg, lb)


def _rope_tables(T):
    half = ROPE_DIMS // 2
    inv = ROPE_THETA ** (-jnp.arange(half, dtype=F32) * 2.0 / ROPE_DIMS)
    ang = jnp.arange(T).astype(F32)[:, None] * inv[None, :]
    cos, sin = jnp.cos(ang), jnp.sin(ang)
    z = jnp.zeros((T, LANES - ROPE_DIMS), F32)
    zh = jnp.zeros((T, half), F32)
    rc = jnp.concatenate([cos, cos, jnp.ones((T, LANES - ROPE_DIMS), F32)], axis=1)
    ra = jnp.concatenate([-sin, zh, z], axis=1)
    rb = jnp.concatenate([zh, sin, z], axis=1)
    return rc, ra, rb


def _overlap_matrix_t(NC, NB):
    nbp = -(-NB // 16) * 16
    c0 = np.arange(NC)[None, :] * CMP_STRIDE
    b0 = np.arange(nbp)[:, None] * SLC_BLOCK
    ov = (c0 < b0 + SLC_BLOCK) & (c0 + CMP_BLOCK > b0) & (np.arange(nbp)[:, None] < NB)
    return jnp.asarray(ov.astype(np.float32), dtype=BF16)


def _reorder_w_in(w):
    D = w.shape[0]
    o_ng = 1024 + 6 * 256
    o_gq = o_ng + 24
    o_gk = o_gq + 512
    o_gv = o_gk + 512
    o_ga = o_gv + 1024
    o_gg = o_ga + GLA_GATE_RANK
    misc = jnp.concatenate([w[:, o_ng:o_gq], w[:, o_ga:o_gg],
                            jnp.zeros((D, LANES - 24 - GLA_GATE_RANK), w.dtype)], axis=1)
    cols = [w[:, 0:1024], w[:, o_gv:o_ga], w[:, o_gg:o_gg + 1024], w[:, 1024:o_ng],
            w[:, o_gq:o_gk], w[:, o_gk:o_gv], misc]
    return jnp.concatenate(cols, axis=1).astype(BF16)


def hybrid_mixer_ln(x, mod, w_in_r, cmp_pe, cmp_w1, cmp_w2, wa_pad, ba, nw, w_out, lg, lb, tables, overlap):
    proj = in_proj(x, mod, w_in_r)
    qr, qn, ks, vs, kw, vw, gates = nsa_prep(proj, *tables)
    kc, vct = nsa_compress(proj, cmp_pe, cmp_w1, cmp_w2)
    nsa_o = nsa_attention(qr, qn, gates, kc, vct, ks, vs, kw, vw, overlap)
    gla_o = gla_mixer(proj, wa_pad, ba, nw)
    return out_proj_ln(nsa_o, gla_o, w_out, x, mod, lg, lb)


def kernel(x, c, w_ada, b_ada, w_in, cmp_pos_k, cmp_w1_k, cmp_w2_k, cmp_pos_v, cmp_w1_v, cmp_w2_v, gla_w_a2, gla_b_a, gla_norm_w, w_out, ln_mix_g, ln_mix_b, ln_ffn_g, ln_ffn_b, ffn_w_gate, ffn_w_up, ffn_w_down, moe_router, moe_w_gate, moe_w_up, moe_w_down):
    B, T, D = x.shape
    L = w_ada.shape[0]
    c_pad = jnp.zeros((8, D), F32).at[:B].set(c)
    mod_all = ada_mod(c_pad, w_ada, b_ada.reshape(L, 1, 6 * D))[:, :B].reshape(L, B, 6, D)
    tables = _rope_tables(T)
    overlap = _overlap_matrix_t(T // CMP_STRIDE, T // SLC_BLOCK)
    for layer in range(L):
        mod = mod_all[layer]
        wa_pad = jnp.zeros((LANES, GLA_HEADS * GLA_DK), F32).at[MISC_GA:MISC_GA + GLA_GATE_RANK].set(
            gla_w_a2[layer]).astype(BF16)
        x = hybrid_mixer_ln(
            x, mod, _reorder_w_in(w_in[layer]),
            jnp.stack([cmp_pos_k[layer], cmp_pos_v[layer]]),
            jnp.stack([cmp_w1_k[layer], cmp_w1_v[layer]]).astype(BF16),
            jnp.stack([cmp_w2_k[layer], cmp_w2_v[layer]]).astype(BF16),
            wa_pad, gla_b_a[layer].reshape(1, -1), gla_norm_w[layer].reshape(1, -1),
            w_out[layer].astype(BF16), ln_mix_g[layer].reshape(1, D), ln_mix_b[layer].reshape(1, D),
            tables, overlap)
        lg = ln_ffn_g[layer].reshape(1, D)
        lb = ln_ffn_b[layer].reshape(1, D)
        i = layer // 2
        if layer % 2 == 0:
            x = ffn_ln(x, mod, ffn_w_gate[i].astype(BF16), ffn_w_up[i].astype(BF16),
                       ffn_w_down[i].astype(BF16), lg, lb)
        else:
            x = moe_layer(x, mod, moe_router[i], moe_w_gate[i], moe_w_up[i], moe_w_down[i], lg, lb)
    return x
```

```python
import functools

import numpy as np
import jax
import jax.numpy as jnp
from jax import lax
from jax.experimental import pallas as pl
from jax.experimental.pallas import tpu as pltpu

F32 = jnp.float32
BF16 = jnp.bfloat16

D_MODEL = 2048
DEPTH = 2
HEAD_DIM = 128
NSA_HEADS = 8
NSA_KV_HEADS = 2
NSA_GROUP = 4
CMP_BLOCK = 32
CMP_STRIDE = 16
CMP_HIDDEN = 256
SLC_BLOCK = 64
SLC_TOPN = 16
WINDOW = 512
FORCE_SCORE = 1e9
GLA_DV = 256
GLA_HEADS = 4
GLA_DK = 128
GLA_GATE_RANK = 16
GLA_GATE_NORM = 16.0
GLA_CHUNK = 64
ROPE_THETA = 500000.0
ROPE_DIMS = 32
N_EXPERTS = 8
TOP_K = 2
LN_EPS = 1e-5
NORM_EPS = 1e-6
DEEPNORM_ALPHA = (2 * DEPTH) ** 0.25
NEG_BIG = -1e30
MASK_BIG = 2.0 ** 100
V_AUG_ROWS = HEAD_DIM + 16
NSA_HEADS_PER_STEP = 2
LOG2_E = 1.4426950408889634

VMEM_LIMIT_BYTES = 56 * 1024 * 1024
BIG_VMEM_LIMIT_BYTES = 60 * 1024 * 1024
LANES = 128

C_NQ = 0
C_GV = 1024
C_GG = 2048
C_KC = 3072
C_VC = 3328
C_KS = 3584
C_VS = 3840
C_KW = 4096
C_VW = 4352
C_GQ = 4608
C_GK = 5120
C_MISC = 5632
PROJ_W = 5760
MISC_GA = 24

NT_DIMS = (((1,), (1,)), ((), ()))


def _params(*sem):
    return pltpu.CompilerParams(dimension_semantics=sem, vmem_limit_bytes=VMEM_LIMIT_BYTES)


def _layer_norm(z, g, b):
    mu = jnp.mean(z, axis=-1, keepdims=True)
    zc = z - mu
    var = jnp.mean(zc * zc, axis=-1, keepdims=True)
    return zc * lax.rsqrt(var + LN_EPS) * g + b


def _ada_kernel(c_ref, w_ref, b_ref, o_ref):
    c = c_ref[...]
    cond = c * jax.nn.sigmoid(c)
    o_ref[0] = jnp.dot(cond.astype(BF16), w_ref[0].astype(BF16),
                       preferred_element_type=F32) + b_ref[0]


def ada_mod(c_pad, w_ada, b_ada):
    L, D, N = w_ada.shape
    tn = 1024
    return pl.pallas_call(
        _ada_kernel,
        grid=(L, N // tn),
        in_specs=[pl.BlockSpec((8, D), lambda l, j: (0, 0)),
                  pl.BlockSpec((1, D, tn), lambda l, j: (l, 0, j)),
                  pl.BlockSpec((1, 1, tn), lambda l, j: (l, 0, j))],
        out_specs=pl.BlockSpec((1, 8, tn), lambda l, j: (l, 0, j)),
        out_shape=jax.ShapeDtypeStruct((L, 8, N), F32),
        compiler_params=_params("parallel", "parallel"),
        name="ada_mod",
    )(c_pad, w_ada, b_ada)


def _inproj_kernel(x_ref, m_ref, w_ref, o_ref):
    h = x_ref[0] * (1.0 + m_ref[0, 1:2, :]) + m_ref[0, 0:1, :]
    o_ref[0] = jnp.dot(h.astype(BF16), w_ref[...], preferred_element_type=F32)


def in_proj(x, mod, w):
    B, T, D = x.shape
    N = w.shape[1]
    tm = min(512, T)
    tn = N // 3
    return pl.pallas_call(
        _inproj_kernel,
        grid=(N // tn, B, T // tm),
        in_specs=[pl.BlockSpec((1, tm, D), lambda j, b, i: (b, i, 0)),
                  pl.BlockSpec((1, 6, D), lambda j, b, i: (b, 0, 0)),
                  pl.BlockSpec((D, tn), lambda j, b, i: (0, j))],
        out_specs=pl.BlockSpec((1, tm, tn), lambda j, b, i: (b, i, j)),
        out_shape=jax.ShapeDtypeStruct((B, T, N), F32),
        compiler_params=_params("parallel", "parallel", "parallel"),
        name="in_proj",
    )(x, mod, w)


def _prep_kernel(q_ref, kv_ref, misc_ref, rc_ref, ra_ref, rb_ref,
                 qr_ref, qn_ref, ks_ref, vs_ref, kw_ref, vw_ref, g_ref):
    rc = rc_ref[...]
    ra = ra_ref[...]
    rb = rb_ref[...]
    scale = HEAD_DIM ** -0.5 * LOG2_E

    def rope(xh):
        return (xh * rc + pltpu.roll(xh, LANES - ROPE_DIMS // 2, 1) * ra
                + pltpu.roll(xh, ROPE_DIMS // 2, 1) * rb)

    for hq in range(NSA_HEADS):
        sl = slice(hq * HEAD_DIM, (hq + 1) * HEAD_DIM)
        xh = q_ref[0, :, sl]
        qr_ref[0, :, sl] = (rope(xh) * scale).astype(BF16)
        qn_ref[0, :, sl] = (xh * scale).astype(BF16)
    misc = misc_ref[0]
    tm = misc.shape[0]
    key_blk = (pl.program_id(1) * tm + lax.broadcasted_iota(jnp.int32, (tm, LANES), 0)) // SLC_BLOCK
    blk_onehot = jnp.where(key_blk == lax.broadcasted_iota(jnp.int32, (tm, LANES), 1), 1.0, 0.0).astype(BF16)
    for h in range(NSA_KV_HEADS):
        sl = slice(h * HEAD_DIM, (h + 1) * HEAD_DIM)
        ks_ref[0, h, :, 0:HEAD_DIM] = rope(
            kv_ref[0, :, (C_KS - C_KC) + h * HEAD_DIM:(C_KS - C_KC) + (h + 1) * HEAD_DIM]).astype(BF16)
        ks_ref[0, h, :, HEAD_DIM:] = blk_onehot
        vs_ref[0, h, 0:HEAD_DIM, :] = kv_ref[0, :, (C_VS - C_KC) + h * HEAD_DIM:(C_VS - C_KC) + (h + 1) * HEAD_DIM].T.astype(BF16)
        vs_ref[0, h, HEAD_DIM:, :] = jnp.ones((V_AUG_ROWS - HEAD_DIM, tm), BF16)
        kw_ref[0, h] = rope(kv_ref[0, :, (C_KW - C_KC) + h * HEAD_DIM:(C_KW - C_KC) + (h + 1) * HEAD_DIM]).astype(BF16)
        vw_ref[0, h] = kv_ref[0, :, (C_VW - C_KC) + h * HEAD_DIM:(C_VW - C_KC) + (h + 1) * HEAD_DIM].T.astype(BF16)
        shift = (LANES - 3 * NSA_GROUP * h) % LANES
        gm = misc if shift == 0 else pltpu.roll(misc, shift, 1)
        g_ref[0, h] = jax.nn.sigmoid(gm)
        del sl


def nsa_prep(proj, rc, ra, rb):
    B, T, _ = proj.shape
    tm = min(512, T)
    kvw = C_GQ - C_KC
    qspec = pl.BlockSpec((1, tm, NSA_HEADS * HEAD_DIM), lambda b, i: (b, i, 0))
    kvspec = pl.BlockSpec((1, NSA_KV_HEADS, tm, HEAD_DIM), lambda b, i: (b, 0, i, 0))
    vtspec = pl.BlockSpec((1, NSA_KV_HEADS, HEAD_DIM, tm), lambda b, i: (b, 0, 0, i))
    tspec = pl.BlockSpec((tm, LANES), lambda b, i: (i, 0))
    kv_shape = jax.ShapeDtypeStruct((B, NSA_KV_HEADS, T, HEAD_DIM), BF16)
    vt_shape = jax.ShapeDtypeStruct((B, NSA_KV_HEADS, HEAD_DIM, T), BF16)
    assert T // SLC_BLOCK <= LANES
    ks_spec = pl.BlockSpec((1, NSA_KV_HEADS, tm, HEAD_DIM + LANES), lambda b, i: (b, 0, i, 0))
    ks_shape = jax.ShapeDtypeStruct((B, NSA_KV_HEADS, T, HEAD_DIM + LANES), BF16)
    vs_spec = pl.BlockSpec((1, NSA_KV_HEADS, V_AUG_ROWS, tm), lambda b, i: (b, 0, 0, i))
    vs_shape = jax.ShapeDtypeStruct((B, NSA_KV_HEADS, V_AUG_ROWS, T), BF16)
    return pl.pallas_call(
        _prep_kernel,
        grid=(B, T // tm),
        in_specs=[pl.BlockSpec((1, tm, NSA_HEADS * HEAD_DIM), lambda b, i: (b, i, C_NQ // (NSA_HEADS * HEAD_DIM))),
                  pl.BlockSpec((1, tm, kvw), lambda b, i: (b, i, C_KC // kvw)),
                  pl.BlockSpec((1, tm, LANES), lambda b, i: (b, i, C_MISC // LANES)),
                  tspec, tspec, tspec],
        out_specs=[qspec, qspec, ks_spec, vs_spec, kvspec, vtspec, kvspec],
        out_shape=[jax.ShapeDtypeStruct((B, T, NSA_HEADS * HEAD_DIM), BF16),
                   jax.ShapeDtypeStruct((B, T, NSA_HEADS * HEAD_DIM), BF16),
                   ks_shape, vs_shape, kv_shape, vt_shape,
                   jax.ShapeDtypeStruct((B, NSA_KV_HEADS, T, LANES), F32)],
        compiler_params=_params("parallel", "parallel"),
        name="nsa_prep",
    )(proj, proj, proj, rc, ra, rb)


def _compress_kernel(a_ref, pe_ref, w1_ref, w2_ref, o_ref, ot_ref, *, n_half):
    half = CMP_STRIDE

    def part(l0):
        acc = jnp.zeros((n_half, CMP_HIDDEN), F32)
        for l in range(half):
            rows = a_ref[0, pl.ds(l, n_half, stride=half), :] + pe_ref[0, l0 + l:l0 + l + 1, :]
            acc += jnp.dot(rows.astype(BF16), w1_ref[0, (l0 + l) * HEAD_DIM:(l0 + l + 1) * HEAD_DIM, :],
                           preferred_element_type=F32)
        return acc

    first = part(0)
    second = part(half)
    hid = first + pltpu.roll(second, n_half - 1, 0)
    row = lax.broadcasted_iota(jnp.int32, (n_half, 1), 0)
    hid = jnp.where(row < n_half - 1, hid, 0.0)
    act = jax.nn.gelu(hid)
    out = jnp.dot(act.astype(BF16), w2_ref[0], preferred_element_type=F32)
    o_ref[0, 0, 0] = out.astype(BF16)
    ot_ref[0, 0, 0] = out.T.astype(BF16)


def nsa_compress(proj, pe, w1, w2):
    B, T, _ = proj.shape
    n_half = T // CMP_STRIDE
    return pl.pallas_call(
        functools.partial(_compress_kernel, n_half=n_half),
        grid=(B, 2, NSA_KV_HEADS),
        in_specs=[pl.BlockSpec((1, T, HEAD_DIM), lambda b, s, h: (b, 0, C_KC // HEAD_DIM + s * NSA_KV_HEADS + h)),
                  pl.BlockSpec((1, CMP_BLOCK, HEAD_DIM), lambda b, s, h: (s, 0, 0)),
                  pl.BlockSpec((1, CMP_BLOCK * HEAD_DIM, CMP_HIDDEN), lambda b, s, h: (s, 0, 0)),
                  pl.BlockSpec((1, CMP_HIDDEN, HEAD_DIM), lambda b, s, h: (s, 0, 0))],
        out_specs=[pl.BlockSpec((1, 1, 1, n_half, HEAD_DIM), lambda b, s, h: (b, s, h, 0, 0)),
                   pl.BlockSpec((1, 1, 1, HEAD_DIM, n_half), lambda b, s, h: (b, s, h, 0, 0))],
        out_shape=[jax.ShapeDtypeStruct((B, 2, NSA_KV_HEADS, n_half, HEAD_DIM), BF16),
                   jax.ShapeDtypeStruct((B, 2, NSA_KV_HEADS, HEAD_DIM, n_half), BF16)],
        compiler_params=_params("parallel", "parallel", "parallel"),
        name="nsa_compress",
    )(proj, pe, w1, w2)


def _nsa_kernel(qr_ref, qn_ref, g_ref, kc_ref, vct_ref, ks_ref, vst_ref, kw_ref, vwt_ref, ovt_ref, o_ref,
                *, TQ, TK, WK, NC, NB, NH):
    G = NSA_GROUP
    R = G * TQ
    NBP = ovt_ref.shape[0]
    i = pl.program_id(2)
    t0 = i * TQ
    m_floor = 0.5 * NEG_BIG

    def stack(ref, h):
        x = ref[0]
        return jnp.concatenate([x[:, (h * G + g) * HEAD_DIM:(h * G + g + 1) * HEAD_DIM] for g in range(G)], axis=0)

    def tile_g(x):
        return jnp.concatenate([x] * G, axis=1)

    tT = t0 + lax.broadcasted_iota(jnp.int32, (1, TQ), 1)
    cur = tT // SLC_BLOCK
    n_sel = min(SLC_TOPN, NB)

    def head_front(h):
        qr = stack(qr_ref, h)
        qn = stack(qn_ref, h)

        s_c = lax.dot_general(kc_ref[0, 0, h], qn, NT_DIMS, preferred_element_type=F32)
        cend = lax.broadcasted_iota(jnp.int32, (NC, 1), 0) * CMP_STRIDE + (CMP_BLOCK - 1)
        s_c = s_c + tile_g(jnp.where(cend <= tT, 0.0, NEG_BIG))
        m_c = jnp.maximum(jnp.max(s_c, axis=0, keepdims=True), m_floor)
        e_c = jnp.exp2(s_c - m_c)
        den = jnp.sum(e_c, axis=0, keepdims=True)
        p_cb = (e_c * jnp.where(den > 0.0, 1.0 / den, 0.0)).astype(BF16)
        o_c = jnp.dot(vct_ref[0, 0, h], p_cb, preferred_element_type=F32)
        imp4 = jnp.dot(ovt_ref[...], p_cb, preferred_element_type=F32)
        impT = imp4[:, 0:TQ]
        for g in range(1, G):
            impT = impT + imp4[:, g * TQ:(g + 1) * TQ]

        ws = pl.multiple_of(jnp.maximum(t0 + TQ - WK, 0), LANES)
        s_w = lax.dot_general(kw_ref[0, h, pl.ds(ws, WK), :], qr, NT_DIMS, preferred_element_type=F32)
        dist = tT - (ws + lax.broadcasted_iota(jnp.int32, (WK, 1), 0))
        s_w = s_w + tile_g(jnp.where(dist >= 0, jnp.where(dist < WINDOW, 0.0, NEG_BIG), NEG_BIG))
        e_w = jnp.exp2(s_w - jnp.max(s_w, axis=0, keepdims=True))
        l_w = jnp.sum(e_w, axis=0, keepdims=True)
        o_w = jnp.dot(vwt_ref[0, h, :, pl.ds(ws, WK)], e_w.astype(BF16),
                      preferred_element_type=F32) * (1.0 / l_w)

        jj = lax.broadcasted_iota(jnp.int32, (NBP, TQ), 0)
        forced = jnp.where(jj == 0, 1, jnp.where(jj == cur, 1, jnp.where(jj == cur - 1, 1, 0)))
        val = jnp.where(jj > cur, -jnp.inf, jnp.where(forced > 0, FORCE_SCORE, impT))
        rank = jnp.zeros((NBP, TQ), jnp.int32)
        for j2 in range(NB):
            row = val[j2:j2 + 1, :]
            later = jnp.where(jj > j2, 1, 0)
            rank = rank + jnp.where(row > val, 1, jnp.where(row == val, later, 0))
        sel_neg = jnp.where(jj > cur, -MASK_BIG, jnp.where(rank < n_sel, 0.0, -MASK_BIG))
        if NBP < LANES:
            sel_neg = jnp.concatenate([sel_neg, jnp.zeros((LANES - NBP, TQ), F32)], axis=0)
        sel_q = sel_neg.T.astype(BF16)
        q_aug = jnp.concatenate([qr, jnp.concatenate([sel_q] * G, axis=0)], axis=1)
        return o_c, o_w, q_aug

    fronts = [head_front(h) for h in range(NH)]

    def sel_scores(h, kt):
        k0 = pl.multiple_of(kt * TK, TK)
        k = ks_ref[0, h, pl.ds(k0, TK), :]
        s = lax.dot_general(k, fronts[h][2], NT_DIMS, preferred_element_type=F32)
        return s, jnp.max(s, axis=0, keepdims=True)

    def sel_accumulate(h, kt, s, smax, m, acc):
        k0 = pl.multiple_of(kt * TK, TK)
        vt = vst_ref[0, h, :, pl.ds(k0, TK)]
        m_new = jnp.maximum(m, smax)
        p = jnp.exp2(s - m_new).astype(BF16)
        acc = jnp.exp2(m - m_new) * acc + jnp.dot(vt, p, preferred_element_type=F32)
        return m_new, acc

    def sel_body(kt, carry):
        out = []
        for h in range(NH):
            s, smax, m, acc = carry[h]
            nxt = sel_scores(h, kt + 1)
            out.append(nxt + sel_accumulate(h, kt, s, smax, m, acc))
        return tuple(out)

    n_kt = (t0 + TQ + TK - 1) // TK
    m0 = jnp.full((1, R), m_floor, F32)
    a0 = jnp.zeros((V_AUG_ROWS, R), F32)
    state = lax.fori_loop(0, n_kt - 1, sel_body, tuple(sel_scores(h, 0) + (m0, a0) for h in range(NH)))
    krow = (n_kt - 1) * TK + lax.broadcasted_iota(jnp.int32, (TK, 1), 0)
    causal = tile_g(jnp.where(krow <= tT, 0.0, NEG_BIG))
    for h in range(NH):
        s_last, _, m_s, acc_s = state[h]
        s_last = s_last + causal
        _, acc_s = sel_accumulate(h, n_kt - 1, s_last, jnp.max(s_last, axis=0, keepdims=True), m_s, acc_s)
        o_s = acc_s[0:HEAD_DIM] * (1.0 / acc_s[HEAD_DIM:HEAD_DIM + 1])

        o_c, o_w, _ = fronts[h]
        gate = g_ref[0, h].T
        for g in range(G):
            cols = slice(g * TQ, (g + 1) * TQ)
            o = (gate[3 * g:3 * g + 1, :] * o_c[:, cols] + gate[3 * g + 1:3 * g + 2, :] * o_s[:, cols]
                 + gate[3 * g + 2:3 * g + 3, :] * o_w[:, cols])
            o_ref[0, :, (h * G + g) * HEAD_DIM:(h * G + g + 1) * HEAD_DIM] = o.T.astype(BF16)


def nsa_attention(qr, qn, gates, kc, vct, ks, vst, kw, vwt, overlap_t):
    B, T, _ = qr.shape
    TQ = 128
    TK = min(512, T)
    WK = min(WINDOW + TQ, T)
    NC = kc.shape[3]
    NB = T // SLC_BLOCK
    NH = NSA_HEADS_PER_STEP
    gw = NH * NSA_GROUP * HEAD_DIM
    qspec = pl.BlockSpec((1, TQ, gw), lambda b, h, i: (b, i, h))
    kspec = pl.BlockSpec((1, NH, T, HEAD_DIM), lambda b, h, i: (b, h, 0, 0))
    vtspec = pl.BlockSpec((1, NH, HEAD_DIM, T), lambda b, h, i: (b, h, 0, 0))
    return pl.pallas_call(
        functools.partial(_nsa_kernel, TQ=TQ, TK=TK, WK=WK, NC=NC, NB=NB, NH=NH),
        grid=(B, NSA_KV_HEADS // NH, T // TQ),
        in_specs=[qspec, qspec,
                  pl.BlockSpec((1, NH, TQ, LANES), lambda b, h, i: (b, h, i, 0)),
                  pl.BlockSpec((1, 1, NH, NC, HEAD_DIM), lambda b, h, i: (b, 0, h, 0, 0)),
                  pl.BlockSpec((1, 1, NH, HEAD_DIM, NC), lambda b, h, i: (b, 1, h, 0, 0)),
                  pl.BlockSpec((1, NH, T, HEAD_DIM + LANES), lambda b, h, i: (b, h, 0, 0)),
                  pl.BlockSpec((1, NH, V_AUG_ROWS, T), lambda b, h, i: (b, h, 0, 0)),
                  kspec, vtspec,
                  pl.BlockSpec(overlap_t.shape, lambda b, h, i: (0, 0))],
        out_specs=pl.BlockSpec((1, TQ, gw), lambda b, h, i: (b, i, h)),
        out_shape=jax.ShapeDtypeStruct((B, T, NSA_HEADS * HEAD_DIM), BF16),
        compiler_params=_params("parallel", "parallel", "arbitrary"),
        name="nsa_attention",
    )(qr, qn, gates, kc, vct, ks, vst, kw, vwt, overlap_t)


def _gla_kernel(q_ref, k_ref, v_ref, gg_ref, misc_ref, wa_ref, ba_ref, nw_ref, o_ref, s_ref, la_ref, *, TC):
    C = GLA_CHUNK

    @pl.when(pl.program_id(1) == 0)
    def _():
        s_ref[...] = jnp.zeros_like(s_ref)

    z = jnp.dot(misc_ref[0].astype(BF16), wa_ref[...], preferred_element_type=F32) + ba_ref[...]
    la_ref[...] = (jnp.minimum(z, 0.0) - jnp.log1p(jnp.exp(-jnp.abs(z)))) * (1.0 / GLA_GATE_NORM)
    ri = lax.broadcasted_iota(jnp.int32, (C, C), 0)
    ci = lax.broadcasted_iota(jnp.int32, (C, C), 1)
    causal = ri >= ci
    tri = jnp.where(causal, 1.0, 0.0)
    nw = nw_ref[...]

    state = [s_ref[h] for h in range(GLA_HEADS)]
    for c in range(TC // C):
        rows = slice(c * C, (c + 1) * C)
        b_all = jnp.dot(tri, la_ref[rows, :], preferred_element_type=F32,
                        precision=lax.Precision.HIGHEST)
        for h in range(GLA_HEADS):
            b = b_all[:, h * GLA_DK:(h + 1) * GLA_DK]
            bl = b[C - 1:C, :]
            q = q_ref[0, rows, h * GLA_DK:(h + 1) * GLA_DK] * (GLA_DK ** -0.5)
            k = k_ref[0, rows, h * GLA_DK:(h + 1) * GLA_DK]
            v = v_ref[0, rows, h * GLA_DV:(h + 1) * GLA_DV].astype(BF16)
            qd = (q * jnp.exp(b)).astype(BF16)
            ki = (k * jnp.exp(-b)).astype(BF16)
            kst = (k * jnp.exp(bl - b)).T.astype(BF16)
            decay = jnp.exp(b.T[:, C - 1:C])
            a = lax.dot_general(qd, ki, NT_DIMS, preferred_element_type=F32)
            a = jnp.where(causal, a, 0.0).astype(BF16)
            o = (jnp.dot(a, v, preferred_element_type=F32)
                 + jnp.dot(qd, state[h].astype(BF16), preferred_element_type=F32))
            state[h] = state[h] * decay + jnp.dot(kst, v, preferred_element_type=F32)
            o = o * lax.rsqrt(jnp.mean(o * o, axis=-1, keepdims=True) + NORM_EPS) * nw
            gg = gg_ref[0, rows, h * GLA_DV:(h + 1) * GLA_DV]
            o_ref[0, rows, h * GLA_DV:(h + 1) * GLA_DV] = (o * (gg * jax.nn.sigmoid(gg))).astype(BF16)
    for h in range(GLA_HEADS):
        s_ref[h] = state[h]


def gla_mixer(proj, wa_pad, ba, nw):
    B, T, _ = proj.shape
    TC = min(512, T)
    qk_w = GLA_HEADS * GLA_DK
    v_w = GLA_HEADS * GLA_DV
    return pl.pallas_call(
        functools.partial(_gla_kernel, TC=TC),
        grid=(B, T // TC),
        in_specs=[pl.BlockSpec((1, TC, qk_w), lambda b, i: (b, i, C_GQ // qk_w)),
                  pl.BlockSpec((1, TC, qk_w), lambda b, i: (b, i, C_GK // qk_w)),
                  pl.BlockSpec((1, TC, v_w), lambda b, i: (b, i, C_GV // v_w)),
                  pl.BlockSpec((1, TC, v_w), lambda b, i: (b, i, C_GG // v_w)),
                  pl.BlockSpec((1, TC, LANES), lambda b, i: (b, i, C_MISC // LANES)),
                  pl.BlockSpec((LANES, qk_w), lambda b, i: (0, 0)),
                  pl.BlockSpec((1, qk_w), lambda b, i: (0, 0)),
                  pl.BlockSpec((1, GLA_DV), lambda b, i: (0, 0))],
        out_specs=pl.BlockSpec((1, TC, v_w), lambda b, i: (b, i, 0)),
        out_shape=jax.ShapeDtypeStruct((B, T, v_w), BF16),
        scratch_shapes=[pltpu.VMEM((GLA_HEADS, GLA_DK, GLA_DV), F32),
                        pltpu.VMEM((TC, qk_w), F32)],
        compiler_params=_params("parallel", "arbitrary"),
        name="gla_mixer",
    )(proj, proj, proj, proj, proj, wa_pad, ba, nw)


def _outproj_kernel(n_ref, g_ref, w_ref, x_ref, m_ref, lg_ref, lb_ref, o_ref, *, half):
    sub = 128
    for r in range(o_ref.shape[1] // sub):
        rows = slice(r * sub, (r + 1) * sub)
        y = (jnp.dot(n_ref[0, rows, :], w_ref[0:half, :], preferred_element_type=F32)
             + jnp.dot(g_ref[0, rows, :], w_ref[half:, :], preferred_element_type=F32))
        z = DEEPNORM_ALPHA * x_ref[0, rows, :] + (1.0 + m_ref[0, 2:3, :]) * y
        o_ref[0, rows, :] = _layer_norm(z, lg_ref[...], lb_ref[...])


def out_proj_ln(nsa_o, gla_o, w, x, mod, lg, lb):
    B, T, D = x.shape
    half = nsa_o.shape[-1]
    tm = min(512, T)
    vec = pl.BlockSpec((1, D), lambda b, i: (0, 0))
    return pl.pallas_call(
        functools.partial(_outproj_kernel, half=half),
        grid=(B, T // tm),
        in_specs=[pl.BlockSpec((1, tm, half), lambda b, i: (b, i, 0)),
                  pl.BlockSpec((1, tm, gla_o.shape[-1]), lambda b, i: (b, i, 0)),
                  pl.BlockSpec(w.shape, lambda b, i: (0, 0)),
                  pl.BlockSpec((1, tm, D), lambda b, i: (b, i, 0)),
                  pl.BlockSpec((1, 6, D), lambda b, i: (b, 0, 0)),
                  vec, vec],
        out_specs=pl.BlockSpec((1, tm, D), lambda b, i: (b, i, 0)),
        out_shape=jax.ShapeDtypeStruct((B, T, D), F32),
        compiler_params=_params("parallel", "parallel"),
        name="out_proj_ln",
    )(nsa_o, gla_o, w, x, mod, lg, lb)


FFN_TM = 1024
FFN_TF = 256
FFN_SUB = 512


def _ffn_kernel(x_ref, m_ref, wg_ref, wu_ref, wd_ref, lg_ref, lb_ref, o_ref, h_sc, *, sub):
    j = pl.program_id(2)
    n_sub = o_ref.shape[1] // sub

    @pl.when(j == 0)
    def _():
        for sb in range(n_sub):
            rows = slice(sb * sub, (sb + 1) * sub)
            h_sc[rows, :] = (x_ref[0, rows, :] * (1.0 + m_ref[0, 4:5, :]) + m_ref[0, 3:4, :]).astype(BF16)
        o_ref[...] = jnp.zeros_like(o_ref)

    wg = wg_ref[...].astype(BF16)
    wu = wu_ref[...].astype(BF16)
    wd = wd_ref[...].astype(BF16)
    for sb in range(n_sub):
        rows = slice(sb * sub, (sb + 1) * sub)
        h = h_sc[rows, :]
        a = jnp.dot(h, wg, preferred_element_type=F32)
        u = jnp.dot(h, wu, preferred_element_type=F32)
        o_ref[0, rows, :] += jnp.dot((a * jax.nn.sigmoid(a) * u).astype(BF16), wd, preferred_element_type=F32)

    @pl.when(j == pl.num_programs(2) - 1)
    def _():
        for sb in range(n_sub):
            rows = slice(sb * sub, (sb + 1) * sub)
            z = DEEPNORM_ALPHA * x_ref[0, rows, :] + (1.0 + m_ref[0, 5:6, :]) * o_ref[0, rows, :]
            o_ref[0, rows, :] = _layer_norm(z, lg_ref[...], lb_ref[...])


def ffn_ln(x, mod, wg, wu, wd, lg, lb):
    B, T, D = x.shape
    F = wg.shape[1]
    tm = min(FFN_TM, T)
    tf = FFN_TF
    vec = pl.BlockSpec((1, D), lambda b, i, j: (0, 0))
    return pl.pallas_call(
        functools.partial(_ffn_kernel, sub=min(FFN_SUB, tm)),
        grid=(B, T // tm, F // tf),
        in_specs=[pl.BlockSpec((1, tm, D), lambda b, i, j: (b, i, 0)),
                  pl.BlockSpec((1, 6, D), lambda b, i, j: (b, 0, 0)),
                  pl.BlockSpec((D, tf), lambda b, i, j: (0, j)),
                  pl.BlockSpec((D, tf), lambda b, i, j: (0, j)),
                  pl.BlockSpec((tf, D), lambda b, i, j: (j, 0)),
                  vec, vec],
        out_specs=pl.BlockSpec((1, tm, D), lambda b, i, j: (b, i, 0)),
        out_shape=jax.ShapeDtypeStruct((B, T, D), F32),
        scratch_shapes=[pltpu.VMEM((tm, D), BF16)],
        compiler_params=pltpu.CompilerParams(dimension_semantics=("parallel", "parallel", "arbitrary"),
                                             vmem_limit_bytes=BIG_VMEM_LIMIT_BYTES),
        name="ffn_ln",
    )(x, mod, wg, wu, wd, lg, lb)


def _router_kernel(x_ref, m_ref, wr_ref, h_ref, lg_ref):
    h = x_ref[0] * (1.0 + m_ref[0, 4:5, :]) + m_ref[0, 3:4, :]
    h_ref[0] = h.astype(BF16)
    lg_ref[0] = jnp.dot(h, wr_ref[...], preferred_element_type=F32, precision=lax.Precision.HIGHEST)


def moe_router(x, mod, wr_pad):
    B, T, D = x.shape
    tm = min(512, T)
    return pl.pallas_call(
        _router_kernel,
        grid=(B, T // tm),
        in_specs=[pl.BlockSpec((1, tm, D), lambda b, i: (b, i, 0)),
                  pl.BlockSpec((1, 6, D), lambda b, i: (b, 0, 0)),
                  pl.BlockSpec((D, LANES), lambda b, i: (0, 0))],
        out_specs=[pl.BlockSpec((1, tm, D), lambda b, i: (b, i, 0)),
                   pl.BlockSpec((1, tm, LANES), lambda b, i: (b, i, 0))],
        out_shape=[jax.ShapeDtypeStruct((B, T, D), BF16),
                   jax.ShapeDtypeStruct((B, T, LANES), F32)],
        compiler_params=_params("parallel", "parallel"),
        name="moe_router",
    )(x, mod, wr_pad)


def _moe_kernel(te_ref, nv_ref, x_ref, wg_ref, wu_ref, wd_ref, o_ref, *, sub):
    i = pl.program_id(0)
    j = pl.program_id(1)
    nv = nv_ref[i]

    @pl.when(j == 0)
    def _():
        o_ref[...] = jnp.zeros_like(o_ref)

    n_sub = o_ref.shape[0] // sub

    def swiglu_rows(rows, wg, wu, wd):
        h = x_ref[rows, :]
        a = jnp.dot(h, wg, preferred_element_type=F32)
        u = jnp.dot(h, wu, preferred_element_type=F32)
        o_ref[rows, :] += jnp.dot((a * jax.nn.sigmoid(a) * u).astype(BF16), wd, preferred_element_type=F32)

    @pl.when(nv == n_sub)
    def _():
        wg = wg_ref[0].astype(BF16)
        wu = wu_ref[0].astype(BF16)
        wd = wd_ref[0].astype(BF16)
        for sb in range(n_sub):
            swiglu_rows(slice(sb * sub, (sb + 1) * sub), wg, wu, wd)

    @pl.when(jnp.logical_and(nv > 0, nv < n_sub))
    def _():
        wg = wg_ref[0].astype(BF16)
        wu = wu_ref[0].astype(BF16)
        wd = wd_ref[0].astype(BF16)
        for sb in range(n_sub - 1):
            @pl.when(sb < nv)
            def _():
                swiglu_rows(slice(sb * sub, (sb + 1) * sub), wg, wu, wd)


def moe_experts(tile_e, tile_nv, xg, wg, wu, wd, tm, sub):
    M, D = xg.shape
    F = wg.shape[2]
    tf = 512
    nj = F // tf
    n_tiles = M // tm

    def wj(i, j, nv):
        return jnp.where(nv[i] > 0, j, nj - 1)

    return pl.pallas_call(
        functools.partial(_moe_kernel, sub=sub),
        grid_spec=pltpu.PrefetchScalarGridSpec(
            num_scalar_prefetch=2,
            grid=(n_tiles, nj),
            in_specs=[pl.BlockSpec((tm, D), lambda i, j, te, nv: (i, 0)),
                      pl.BlockSpec((1, D, tf), lambda i, j, te, nv: (te[i], 0, wj(i, j, nv))),
                      pl.BlockSpec((1, D, tf), lambda i, j, te, nv: (te[i], 0, wj(i, j, nv))),
                      pl.BlockSpec((1, tf, D), lambda i, j, te, nv: (te[i], wj(i, j, nv), 0))],
            out_specs=pl.BlockSpec((tm, D), lambda i, j, te, nv: (i, 0))),
        out_shape=jax.ShapeDtypeStruct((M, D), F32),
        compiler_params=pltpu.CompilerParams(dimension_semantics=("parallel", "arbitrary"),
                                             vmem_limit_bytes=BIG_VMEM_LIMIT_BYTES),
        name="moe_experts",
    )(tile_e, tile_nv, xg, wg, wu, wd)


def _combine_kernel(y0_ref, y1_ref, cw_ref, x_ref, m_ref, lg_ref, lb_ref, o_ref):
    cw = cw_ref[0]
    y = y0_ref[0] * cw[:, 0:1] + y1_ref[0] * cw[:, 1:2]
    z = DEEPNORM_ALPHA * x_ref[0] + (1.0 + m_ref[0, 5:6, :]) * y
    o_ref[0] = _layer_norm(z, lg_ref[...], lb_ref[...])


def moe_combine_ln(y0, y1, cw, x, mod, lg, lb):
    B, T, D = x.shape
    tm = min(512, T)
    row = pl.BlockSpec((1, tm, D), lambda b, i: (b, i, 0))
    vec = pl.BlockSpec((1, D), lambda b, i: (0, 0))
    return pl.pallas_call(
        _combine_kernel,
        grid=(B, T // tm),
        in_specs=[row, row, pl.BlockSpec((1, tm, LANES), lambda b, i: (b, i, 0)), row,
                  pl.BlockSpec((1, 6, D), lambda b, i: (b, 0, 0)), vec, vec],
        out_specs=row,
        out_shape=jax.ShapeDtypeStruct((B, T, D), F32),
        compiler_params=_params("parallel", "parallel"),
        name="moe_combine_ln",
    )(y0, y1, cw, x, mod, lg, lb)


MOE_TM = 1024
MOE_SUB = 256


def moe_layer(x, mod, w_router, wg, wu, wd, lg, lb):
    B, T, D = x.shape
    N = B * T
    A = N * TOP_K
    tm = MOE_TM
    wr_pad = jnp.zeros((D, LANES), F32).at[:, :N_EXPERTS].set(w_router)
    h, logits = moe_router(x, mod, wr_pad)
    logits = logits.reshape(N, LANES)[:, :N_EXPERTS]
    top_val, top_idx = lax.top_k(logits, TOP_K)
    comb = jax.nn.softmax(top_val, axis=-1)
    flat_e = top_idx.reshape(-1).astype(jnp.int32)
    onehot = (flat_e[:, None] == jnp.arange(N_EXPERTS, dtype=jnp.int32)[None, :]).astype(jnp.int32)
    csum = jnp.cumsum(onehot, axis=0)
    counts = csum[-1]
    padded = (counts + tm - 1) // tm * tm
    pad_end = jnp.cumsum(padded)
    pad_start = pad_end - padded
    slot = jnp.sum(onehot * (csum + pad_start[None, :]), axis=1) - 1
    n_tiles = -(-A // tm) + N_EXPERTS
    slot_tok = (jnp.arange(n_tiles * tm, dtype=jnp.int32) % N).at[slot].set(jnp.arange(A, dtype=jnp.int32) // TOP_K)
    tile_start = jnp.arange(n_tiles, dtype=jnp.int32) * tm
    tile_e = jnp.minimum(jnp.searchsorted(pad_end, tile_start, side='right'), N_EXPERTS - 1).astype(jnp.int32)
    valid = jnp.clip(pad_start[tile_e] + counts[tile_e] - tile_start, 0, tm)
    tile_nv = ((valid + MOE_SUB - 1) // MOE_SUB).astype(jnp.int32)
    n_used = pad_end[-1] // tm
    tile_e = jnp.where(jnp.arange(n_tiles) < n_used, tile_e, tile_e[jnp.maximum(n_used - 1, 0)])
    xg = h.reshape(N, D)[slot_tok]
    y = moe_experts(tile_e, tile_nv, xg, wg, wu, wd, tm, MOE_SUB)
    slot_of = slot.reshape(N, TOP_K)
    y0 = y[slot_of[:, 0]].reshape(B, T, D)
    y1 = y[slot_of[:, 1]].reshape(B, T, D)
    cw = jnp.zeros((N, LANES), F32).at[:, :TOP_K].set(comb).reshape(B, T, LANES)
    return moe_combine_ln(y0, y1, cw, x, mod, lg, lb)


def _rope_tables(T):
    half = ROPE_DIMS // 2
    inv = ROPE_THETA ** (-jnp.arange(half, dtype=F32) * 2.0 / ROPE_DIMS)
    ang = jnp.arange(T).astype(F32)[:, None] * inv[None, :]
    cos, sin = jnp.cos(ang), jnp.sin(ang)
    z = jnp.zeros((T, LANES - ROPE_DIMS), F32)
    zh = jnp.zeros((T, half), F32)
    rc = jnp.concatenate([cos, cos, jnp.ones((T, LANES - ROPE_DIMS), F32)], axis=1)
    ra = jnp.concatenate([-sin, zh, z], axis=1)
    rb = jnp.concatenate([zh, sin, z], axis=1)
    return rc, ra, rb


def _overlap_matrix_t(NC, NB):
    nbp = -(-NB // 16) * 16
    c0 = np.arange(NC)[None, :] * CMP_STRIDE
    b0 = np.arange(nbp)[:, None] * SLC_BLOCK
    ov = (c0 < b0 + SLC_BLOCK) & (c0 + CMP_BLOCK > b0) & (np.arange(nbp)[:, None] < NB)
    return jnp.asarray(ov.astype(np.float32), dtype=BF16)


def _reorder_w_in(w):
    D = w.shape[0]
    o_ng = 1024 + 6 * 256
    o_gq = o_ng + 24
    o_gk = o_gq + 512
    o_gv = o_gk + 512
    o_ga = o_gv + 1024
    o_gg = o_ga + GLA_GATE_RANK
    misc = jnp.concatenate([w[:, o_ng:o_gq], w[:, o_ga:o_gg],
                            jnp.zeros((D, LANES - 24 - GLA_GATE_RANK), w.dtype)], axis=1)
    cols = [w[:, 0:1024], w[:, o_gv:o_ga], w[:, o_gg:o_gg + 1024], w[:, 1024:o_ng],
            w[:, o_gq:o_gk], w[:, o_gk:o_gv], misc]
    return jnp.concatenate(cols, axis=1).astype(BF16)


def hybrid_mixer_ln(x, mod, w_in_r, cmp_pe, cmp_w1, cmp_w2, wa_pad, ba, nw, w_out, lg, lb, tables, overlap):
    proj = in_proj(x, mod, w_in_r)
    qr, qn, ks, vs, kw, vw, gates = nsa_prep(proj, *tables)
    kc, vct = nsa_compress(proj, cmp_pe, cmp_w1, cmp_w2)
    nsa_o = nsa_attention(qr, qn, gates, kc, vct, ks, vs, kw, vw, overlap)
    gla_o = gla_mixer(proj, wa_pad, ba, nw)
    return out_proj_ln(nsa_o, gla_o, w_out, x, mod, lg, lb)


def kernel(x, c, w_ada, b_ada, w_in, cmp_pos_k, cmp_w1_k, cmp_w2_k, cmp_pos_v, cmp_w1_v, cmp_w2_v, gla_w_a2, gla_b_a, gla_norm_w, w_out, ln_mix_g, ln_mix_b, ln_ffn_g, ln_ffn_b, ffn_w_gate, ffn_w_up, ffn_w_down, moe_router, moe_w_gate, moe_w_up, moe_w_down):
    B, T, D = x.shape
    L = w_ada.shape[0]
    c_pad = jnp.zeros((8, D), F32).at[:B].set(c)
    mod_all = ada_mod(c_pad, w_ada, b_ada.reshape(L, 1, 6 * D))[:, :B].reshape(L, B, 6, D)
    tables = _rope_tables(T)
    overlap = _overlap_matrix_t(T // CMP_STRIDE, T // SLC_BLOCK)
    for layer in range(L):
        mod = mod_all[layer]
        wa_pad = jnp.zeros((LANES, GLA_HEADS * GLA_DK), F32).at[MISC_GA:MISC_GA + GLA_GATE_RANK].set(
            gla_w_a2[layer]).astype(BF16)
        x = hybrid_mixer_ln(
            x, mod, _reorder_w_in(w_in[layer]),
            jnp.stack([cmp_pos_k[layer], cmp_pos_v[layer]]),
            jnp.stack([cmp_w1_k[layer], cmp_w1_v[layer]]).astype(BF16),
            jnp.stack([cmp_w2_k[layer], cmp_w2_v[layer]]).astype(BF16),
            wa_pad, gla_b_a[layer].reshape(1, -1), gla_norm_w[layer].reshape(1, -1),
            w_out[layer].astype(BF16), ln_mix_g[layer].reshape(1, D), ln_mix_b[layer].reshape(1, D),
            tables, overlap)
        lg = ln_ffn_g[layer].reshape(1, D)
        lb = ln_ffn_b[layer].reshape(1, D)
        i = layer // 2
        if layer % 2 == 0:
            x = ffn_ln(x, mod, ffn_w_gate[i], ffn_w_up[i], ffn_w_down[i], lg, lb)
        else:
            x = moe_layer(x, mod, moe_router[i], moe_w_gate[i], moe_w_up[i], moe_w_down[i], lg, lb)
    return x
```

```python
import functools

import numpy as np
import jax
import jax.numpy as jnp
from jax import lax
from jax.experimental import pallas as pl
from jax.experimental.pallas import tpu as pltpu

F32 = jnp.float32
BF16 = jnp.bfloat16

D_MODEL = 2048
DEPTH = 2
HEAD_DIM = 128
NSA_HEADS = 8
NSA_KV_HEADS = 2
NSA_GROUP = 4
CMP_BLOCK = 32
CMP_STRIDE = 16
CMP_HIDDEN = 256
SLC_BLOCK = 64
SLC_TOPN = 16
WINDOW = 512
FORCE_SCORE = 1e9
GLA_DV = 256
GLA_HEADS = 4
GLA_DK = 128
GLA_GATE_RANK = 16
GLA_GATE_NORM = 16.0
GLA_CHUNK = 64
ROPE_THETA = 500000.0
ROPE_DIMS = 32
N_EXPERTS = 8
TOP_K = 2
LN_EPS = 1e-5
NORM_EPS = 1e-6
DEEPNORM_ALPHA = (2 * DEPTH) ** 0.25
NEG_BIG = -1e30
MASK_BIG = 2.0 ** 100
V_AUG_ROWS = HEAD_DIM + 16
NSA_HEADS_PER_STEP = 2
NSA_TK = 512
LOG2_E = 1.4426950408889634

VMEM_LIMIT_BYTES = 56 * 1024 * 1024
BIG_VMEM_LIMIT_BYTES = 60 * 1024 * 1024
LANES = 128

C_NQ = 0
C_GV = 1024
C_GG = 2048
C_KC = 3072
C_VC = 3328
C_KS = 3584
C_VS = 3840
C_KW = 4096
C_VW = 4352
C_GQ = 4608
C_GK = 5120
C_MISC = 5632
PROJ_W = 5760
MISC_GA = 24

NT_DIMS = (((1,), (1,)), ((), ()))


def _params(*sem):
    return pltpu.CompilerParams(dimension_semantics=sem, vmem_limit_bytes=VMEM_LIMIT_BYTES)


def _layer_norm(z, g, b):
    mu = jnp.mean(z, axis=-1, keepdims=True)
    zc = z - mu
    var = jnp.mean(zc * zc, axis=-1, keepdims=True)
    return zc * lax.rsqrt(var + LN_EPS) * g + b


def _ada_kernel(c_ref, w_ref, b_ref, o_ref):
    c = c_ref[...]
    cond = c * jax.nn.sigmoid(c)
    o_ref[0] = jnp.dot(cond.astype(BF16), w_ref[0].astype(BF16),
                       preferred_element_type=F32) + b_ref[0]


def ada_mod(c_pad, w_ada, b_ada):
    L, D, N = w_ada.shape
    tn = 1024
    return pl.pallas_call(
        _ada_kernel,
        grid=(L, N // tn),
        in_specs=[pl.BlockSpec((8, D), lambda l, j: (0, 0)),
                  pl.BlockSpec((1, D, tn), lambda l, j: (l, 0, j)),
                  pl.BlockSpec((1, 1, tn), lambda l, j: (l, 0, j))],
        out_specs=pl.BlockSpec((1, 8, tn), lambda l, j: (l, 0, j)),
        out_shape=jax.ShapeDtypeStruct((L, 8, N), F32),
        compiler_params=_params("parallel", "parallel"),
        name="ada_mod",
    )(c_pad, w_ada, b_ada)


def _inproj_kernel(x_ref, m_ref, w_ref, o_ref):
    h = x_ref[0] * (1.0 + m_ref[0, 1:2, :]) + m_ref[0, 0:1, :]
    o_ref[0] = jnp.dot(h.astype(BF16), w_ref[...], preferred_element_type=F32)


def in_proj(x, mod, w):
    B, T, D = x.shape
    N = w.shape[1]
    tm = min(512, T)
    tn = N // 3
    return pl.pallas_call(
        _inproj_kernel,
        grid=(N // tn, B, T // tm),
        in_specs=[pl.BlockSpec((1, tm, D), lambda j, b, i: (b, i, 0)),
                  pl.BlockSpec((1, 6, D), lambda j, b, i: (b, 0, 0)),
                  pl.BlockSpec((D, tn), lambda j, b, i: (0, j))],
        out_specs=pl.BlockSpec((1, tm, tn), lambda j, b, i: (b, i, j)),
        out_shape=jax.ShapeDtypeStruct((B, T, N), F32),
        compiler_params=_params("parallel", "parallel", "parallel"),
        name="in_proj",
    )(x, mod, w)


def _prep_kernel(q_ref, kv_ref, misc_ref, rc_ref, ra_ref, rb_ref,
                 qr_ref, qn_ref, ks_ref, vs_ref, kw_ref, vw_ref, g_ref):
    rc = rc_ref[...]
    ra = ra_ref[...]
    rb = rb_ref[...]
    scale = HEAD_DIM ** -0.5 * LOG2_E

    def rope(xh):
        return (xh * rc + pltpu.roll(xh, LANES - ROPE_DIMS // 2, 1) * ra
                + pltpu.roll(xh, ROPE_DIMS // 2, 1) * rb)

    for hq in range(NSA_HEADS):
        sl = slice(hq * HEAD_DIM, (hq + 1) * HEAD_DIM)
        xh = q_ref[0, :, sl]
        qr_ref[0, :, sl] = (rope(xh) * scale).astype(BF16)
        qn_ref[0, :, sl] = (xh * scale).astype(BF16)
    misc = misc_ref[0]
    tm = misc.shape[0]
    key_blk = (pl.program_id(1) * tm + lax.broadcasted_iota(jnp.int32, (tm, LANES), 0)) // SLC_BLOCK
    blk_onehot = jnp.where(key_blk == lax.broadcasted_iota(jnp.int32, (tm, LANES), 1), 1.0, 0.0).astype(BF16)
    for h in range(NSA_KV_HEADS):
        sl = slice(h * HEAD_DIM, (h + 1) * HEAD_DIM)
        ks_ref[0, h, :, 0:HEAD_DIM] = rope(
            kv_ref[0, :, (C_KS - C_KC) + h * HEAD_DIM:(C_KS - C_KC) + (h + 1) * HEAD_DIM]).astype(BF16)
        ks_ref[0, h, :, HEAD_DIM:] = blk_onehot
        vs_ref[0, h, 0:HEAD_DIM, :] = kv_ref[0, :, (C_VS - C_KC) + h * HEAD_DIM:(C_VS - C_KC) + (h + 1) * HEAD_DIM].T.astype(BF16)
        vs_ref[0, h, HEAD_DIM:, :] = jnp.ones((V_AUG_ROWS - HEAD_DIM, tm), BF16)
        kw_ref[0, h] = rope(kv_ref[0, :, (C_KW - C_KC) + h * HEAD_DIM:(C_KW - C_KC) + (h + 1) * HEAD_DIM]).astype(BF16)
        vw_ref[0, h] = kv_ref[0, :, (C_VW - C_KC) + h * HEAD_DIM:(C_VW - C_KC) + (h + 1) * HEAD_DIM].T.astype(BF16)
        shift = (LANES - 3 * NSA_GROUP * h) % LANES
        gm = misc if shift == 0 else pltpu.roll(misc, shift, 1)
        g_ref[0, h] = jax.nn.sigmoid(gm)
        del sl


def nsa_prep(proj, rc, ra, rb):
    B, T, _ = proj.shape
    tm = min(512, T)
    kvw = C_GQ - C_KC
    qspec = pl.BlockSpec((1, tm, NSA_HEADS * HEAD_DIM), lambda b, i: (b, i, 0))
    kvspec = pl.BlockSpec((1, NSA_KV_HEADS, tm, HEAD_DIM), lambda b, i: (b, 0, i, 0))
    vtspec = pl.BlockSpec((1, NSA_KV_HEADS, HEAD_DIM, tm), lambda b, i: (b, 0, 0, i))
    tspec = pl.BlockSpec((tm, LANES), lambda b, i: (i, 0))
    kv_shape = jax.ShapeDtypeStruct((B, NSA_KV_HEADS, T, HEAD_DIM), BF16)
    vt_shape = jax.ShapeDtypeStruct((B, NSA_KV_HEADS, HEAD_DIM, T), BF16)
    assert T // SLC_BLOCK <= LANES
    ks_spec = pl.BlockSpec((1, NSA_KV_HEADS, tm, HEAD_DIM + LANES), lambda b, i: (b, 0, i, 0))
    ks_shape = jax.ShapeDtypeStruct((B, NSA_KV_HEADS, T, HEAD_DIM + LANES), BF16)
    vs_spec = pl.BlockSpec((1, NSA_KV_HEADS, V_AUG_ROWS, tm), lambda b, i: (b, 0, 0, i))
    vs_shape = jax.ShapeDtypeStruct((B, NSA_KV_HEADS, V_AUG_ROWS, T), BF16)
    return pl.pallas_call(
        _prep_kernel,
        grid=(B, T // tm),
        in_specs=[pl.BlockSpec((1, tm, NSA_HEADS * HEAD_DIM), lambda b, i: (b, i, C_NQ // (NSA_HEADS * HEAD_DIM))),
                  pl.BlockSpec((1, tm, kvw), lambda b, i: (b, i, C_KC // kvw)),
                  pl.BlockSpec((1, tm, LANES), lambda b, i: (b, i, C_MISC // LANES)),
                  tspec, tspec, tspec],
        out_specs=[qspec, qspec, ks_spec, vs_spec, kvspec, vtspec, kvspec],
        out_shape=[jax.ShapeDtypeStruct((B, T, NSA_HEADS * HEAD_DIM), BF16),
                   jax.ShapeDtypeStruct((B, T, NSA_HEADS * HEAD_DIM), BF16),
                   ks_shape, vs_shape, kv_shape, vt_shape,
                   jax.ShapeDtypeStruct((B, NSA_KV_HEADS, T, LANES), F32)],
        compiler_params=_params("parallel", "parallel"),
        name="nsa_prep",
    )(proj, proj, proj, rc, ra, rb)


def _compress_kernel(a_ref, pe_ref, w1_ref, w2_ref, o_ref, ot_ref, *, n_half):
    half = CMP_STRIDE

    def part(l0):
        acc = jnp.zeros((n_half, CMP_HIDDEN), F32)
        for l in range(half):
            rows = a_ref[0, pl.ds(l, n_half, stride=half), :] + pe_ref[0, l0 + l:l0 + l + 1, :]
            acc += jnp.dot(rows.astype(BF16), w1_ref[0, (l0 + l) * HEAD_DIM:(l0 + l + 1) * HEAD_DIM, :],
                           preferred_element_type=F32)
        return acc

    first = part(0)
    second = part(half)
    hid = first + pltpu.roll(second, n_half - 1, 0)
    row = lax.broadcasted_iota(jnp.int32, (n_half, 1), 0)
    hid = jnp.where(row < n_half - 1, hid, 0.0)
    act = jax.nn.gelu(hid)
    out = jnp.dot(act.astype(BF16), w2_ref[0], preferred_element_type=F32)
    o_ref[0, 0, 0] = out.astype(BF16)
    ot_ref[0, 0, 0] = out.T.astype(BF16)


def nsa_compress(proj, pe, w1, w2):
    B, T, _ = proj.shape
    n_half = T // CMP_STRIDE
    return pl.pallas_call(
        functools.partial(_compress_kernel, n_half=n_half),
        grid=(B, 2, NSA_KV_HEADS),
        in_specs=[pl.BlockSpec((1, T, HEAD_DIM), lambda b, s, h: (b, 0, C_KC // HEAD_DIM + s * NSA_KV_HEADS + h)),
                  pl.BlockSpec((1, CMP_BLOCK, HEAD_DIM), lambda b, s, h: (s, 0, 0)),
                  pl.BlockSpec((1, CMP_BLOCK * HEAD_DIM, CMP_HIDDEN), lambda b, s, h: (s, 0, 0)),
                  pl.BlockSpec((1, CMP_HIDDEN, HEAD_DIM), lambda b, s, h: (s, 0, 0))],
        out_specs=[pl.BlockSpec((1, 1, 1, n_half, HEAD_DIM), lambda b, s, h: (b, s, h, 0, 0)),
                   pl.BlockSpec((1, 1, 1, HEAD_DIM, n_half), lambda b, s, h: (b, s, h, 0, 0))],
        out_shape=[jax.ShapeDtypeStruct((B, 2, NSA_KV_HEADS, n_half, HEAD_DIM), BF16),
                   jax.ShapeDtypeStruct((B, 2, NSA_KV_HEADS, HEAD_DIM, n_half), BF16)],
        compiler_params=_params("parallel", "parallel", "parallel"),
        name="nsa_compress",
    )(proj, pe, w1, w2)


def _nsa_kernel(qr_ref, qn_ref, g_ref, kc_ref, vct_ref, ks_ref, vst_ref, kw_ref, vwt_ref, ovt_ref, o_ref,
                *, TQ, TK, WK, NC, NB, NH):
    G = NSA_GROUP
    R = G * TQ
    NBP = ovt_ref.shape[0]
    i = pl.program_id(2)
    t0 = i * TQ
    m_floor = 0.5 * NEG_BIG

    def stack(ref, h):
        x = ref[0]
        return jnp.concatenate([x[:, (h * G + g) * HEAD_DIM:(h * G + g + 1) * HEAD_DIM] for g in range(G)], axis=0)

    def tile_g(x):
        return jnp.concatenate([x] * G, axis=1)

    tT = t0 + lax.broadcasted_iota(jnp.int32, (1, TQ), 1)
    cur = tT // SLC_BLOCK
    n_sel = min(SLC_TOPN, NB)

    HS = range(NH)
    qr = [stack(qr_ref, h) for h in HS]
    qn = [stack(qn_ref, h) for h in HS]

    ws = pl.multiple_of(jnp.maximum(t0 + TQ - WK, 0), LANES)
    s_c = [lax.dot_general(kc_ref[0, 0, h], qn[h], NT_DIMS, preferred_element_type=F32) for h in HS]
    s_w = [lax.dot_general(kw_ref[0, h, pl.ds(ws, WK), :], qr[h], NT_DIMS, preferred_element_type=F32)
           for h in HS]
    cend = lax.broadcasted_iota(jnp.int32, (NC, 1), 0) * CMP_STRIDE + (CMP_BLOCK - 1)
    bias_c = tile_g(jnp.where(cend <= tT, 0.0, NEG_BIG))
    dist = tT - (ws + lax.broadcasted_iota(jnp.int32, (WK, 1), 0))
    bias_w = tile_g(jnp.where(dist >= 0, jnp.where(dist < WINDOW, 0.0, NEG_BIG), NEG_BIG))

    p_cb, e_wb, l_w = [], [], []
    for h in HS:
        sc = s_c[h] + bias_c
        m_c = jnp.maximum(jnp.max(sc, axis=0, keepdims=True), m_floor)
        e_c = jnp.exp2(sc - m_c)
        den = jnp.sum(e_c, axis=0, keepdims=True)
        p_cb.append((e_c * jnp.where(den > 0.0, 1.0 / den, 0.0)).astype(BF16))
        sw = s_w[h] + bias_w
        e_w = jnp.exp2(sw - jnp.max(sw, axis=0, keepdims=True))
        l_w.append(jnp.sum(e_w, axis=0, keepdims=True))
        e_wb.append(e_w.astype(BF16))

    o_c = [jnp.dot(vct_ref[0, 0, h], p_cb[h], preferred_element_type=F32) for h in HS]
    imp4 = [jnp.dot(ovt_ref[...], p_cb[h], preferred_element_type=F32) for h in HS]
    o_w = [jnp.dot(vwt_ref[0, h, :, pl.ds(ws, WK)], e_wb[h], preferred_element_type=F32) * (1.0 / l_w[h])
           for h in HS]

    jj = lax.broadcasted_iota(jnp.int32, (NBP, TQ), 0)
    forced = jnp.where(jj == 0, 1, jnp.where(jj == cur, 1, jnp.where(jj == cur - 1, 1, 0)))
    val = []
    for h in HS:
        impT = imp4[h][:, 0:TQ]
        for g in range(1, G):
            impT = impT + imp4[h][:, g * TQ:(g + 1) * TQ]
        val.append(jnp.where(jj > cur, -jnp.inf, jnp.where(forced > 0, FORCE_SCORE, impT)))
    rank = [jnp.zeros((NBP, TQ), jnp.int32) for h in HS]
    for j2 in range(NB):
        later = jnp.where(jj > j2, 1, 0)
        for h in HS:
            row = val[h][j2:j2 + 1, :]
            rank[h] = rank[h] + jnp.where(row > val[h], 1, jnp.where(row == val[h], later, 0))
    fronts = []
    for h in HS:
        sel_neg = jnp.where(jj > cur, -MASK_BIG, jnp.where(rank[h] < n_sel, 0.0, -MASK_BIG))
        if NBP < LANES:
            sel_neg = jnp.concatenate([sel_neg, jnp.zeros((LANES - NBP, TQ), F32)], axis=0)
        sel_q = sel_neg.T.astype(BF16)
        q_aug = jnp.concatenate([qr[h], jnp.concatenate([sel_q] * G, axis=0)], axis=1)
        fronts.append((o_c[h], o_w[h], q_aug))

    def sel_scores(h, kt):
        k0 = pl.multiple_of(kt * TK, TK)
        k = ks_ref[0, h, pl.ds(k0, TK), :]
        s = lax.dot_general(k, fronts[h][2], NT_DIMS, preferred_element_type=F32)
        return s, jnp.max(s, axis=0, keepdims=True)

    def sel_accumulate(h, kt, s, smax, m, acc):
        k0 = pl.multiple_of(kt * TK, TK)
        vt = vst_ref[0, h, :, pl.ds(k0, TK)]
        m_new = jnp.maximum(m, smax)
        p = jnp.exp2(s - m_new).astype(BF16)
        acc = jnp.exp2(m - m_new) * acc + jnp.dot(vt, p, preferred_element_type=F32)
        return m_new, acc

    def sel_body(kt, carry):
        out = []
        for h in range(NH):
            s, smax, m, acc = carry[h]
            nxt = sel_scores(h, kt + 1)
            out.append(nxt + sel_accumulate(h, kt, s, smax, m, acc))
        return tuple(out)

    n_kt = (t0 + TQ + TK - 1) // TK
    m0 = jnp.full((1, R), m_floor, F32)
    a0 = jnp.zeros((V_AUG_ROWS, R), F32)
    state = lax.fori_loop(0, n_kt - 1, sel_body, tuple(sel_scores(h, 0) + (m0, a0) for h in range(NH)))
    krow = (n_kt - 1) * TK + lax.broadcasted_iota(jnp.int32, (TK, 1), 0)
    causal = tile_g(jnp.where(krow <= tT, 0.0, NEG_BIG))
    for h in range(NH):
        s_last, _, m_s, acc_s = state[h]
        s_last = s_last + causal
        _, acc_s = sel_accumulate(h, n_kt - 1, s_last, jnp.max(s_last, axis=0, keepdims=True), m_s, acc_s)
        o_s = acc_s[0:HEAD_DIM] * (1.0 / acc_s[HEAD_DIM:HEAD_DIM + 1])

        o_c, o_w, _ = fronts[h]
        gate = g_ref[0, h].T
        for g in range(G):
            cols = slice(g * TQ, (g + 1) * TQ)
            o = (gate[3 * g:3 * g + 1, :] * o_c[:, cols] + gate[3 * g + 1:3 * g + 2, :] * o_s[:, cols]
                 + gate[3 * g + 2:3 * g + 3, :] * o_w[:, cols])
            o_ref[0, :, (h * G + g) * HEAD_DIM:(h * G + g + 1) * HEAD_DIM] = o.T.astype(BF16)


def nsa_attention(qr, qn, gates, kc, vct, ks, vst, kw, vwt, overlap_t):
    B, T, _ = qr.shape
    TQ = 128
    TK = min(NSA_TK, T)
    WK = min(WINDOW + TQ, T)
    NC = kc.shape[3]
    NB = T // SLC_BLOCK
    NH = NSA_HEADS_PER_STEP
    gw = NH * NSA_GROUP * HEAD_DIM
    qspec = pl.BlockSpec((1, TQ, gw), lambda b, h, i: (b, i, h))
    kspec = pl.BlockSpec((1, NH, T, HEAD_DIM), lambda b, h, i: (b, h, 0, 0))
    vtspec = pl.BlockSpec((1, NH, HEAD_DIM, T), lambda b, h, i: (b, h, 0, 0))
    return pl.pallas_call(
        functools.partial(_nsa_kernel, TQ=TQ, TK=TK, WK=WK, NC=NC, NB=NB, NH=NH),
        grid=(B, NSA_KV_HEADS // NH, T // TQ),
        in_specs=[qspec, qspec,
                  pl.BlockSpec((1, NH, TQ, LANES), lambda b, h, i: (b, h, i, 0)),
                  pl.BlockSpec((1, 1, NH, NC, HEAD_DIM), lambda b, h, i: (b, 0, h, 0, 0)),
                  pl.BlockSpec((1, 1, NH, HEAD_DIM, NC), lambda b, h, i: (b, 1, h, 0, 0)),
                  pl.BlockSpec((1, NH, T, HEAD_DIM + LANES), lambda b, h, i: (b, h, 0, 0)),
                  pl.BlockSpec((1, NH, V_AUG_ROWS, T), lambda b, h, i: (b, h, 0, 0)),
                  kspec, vtspec,
                  pl.BlockSpec(overlap_t.shape, lambda b, h, i: (0, 0))],
        out_specs=pl.BlockSpec((1, TQ, gw), lambda b, h, i: (b, i, h)),
        out_shape=jax.ShapeDtypeStruct((B, T, NSA_HEADS * HEAD_DIM), BF16),
        compiler_params=_params("parallel", "parallel", "arbitrary"),
        name="nsa_attention",
    )(qr, qn, gates, kc, vct, ks, vst, kw, vwt, overlap_t)


def _gla_kernel(q_ref, k_ref, v_ref, gg_ref, misc_ref, wa_ref, ba_ref, nw_ref, o_ref, s_ref, la_ref, *, TC):
    C = GLA_CHUNK

    @pl.when(pl.program_id(1) == 0)
    def _():
        s_ref[...] = jnp.zeros_like(s_ref)

    z = jnp.dot(misc_ref[0].astype(BF16), wa_ref[...], preferred_element_type=F32) + ba_ref[...]
    la_ref[...] = (jnp.minimum(z, 0.0) - jnp.log1p(jnp.exp(-jnp.abs(z)))) * (1.0 / GLA_GATE_NORM)
    ri = lax.broadcasted_iota(jnp.int32, (C, C), 0)
    ci = lax.broadcasted_iota(jnp.int32, (C, C), 1)
    causal = ri >= ci
    tri = jnp.where(causal, 1.0, 0.0)
    nw = nw_ref[...]

    n_c = TC // C
    pairs = [(c, h) for c in range(n_c) for h in range(GLA_HEADS)]
    rows = [slice(c * C, (c + 1) * C) for c in range(n_c)]
    b_all = [jnp.dot(tri, la_ref[rows[c], :], preferred_element_type=F32, precision=lax.Precision.HIGHEST)
             for c in range(n_c)]
    qd, kv, decay, intra = {}, {}, {}, {}
    for c, h in pairs:
        b = b_all[c][:, h * GLA_DK:(h + 1) * GLA_DK]
        bl = b[C - 1:C, :]
        q = q_ref[0, rows[c], h * GLA_DK:(h + 1) * GLA_DK] * (GLA_DK ** -0.5)
        k = k_ref[0, rows[c], h * GLA_DK:(h + 1) * GLA_DK]
        v = v_ref[0, rows[c], h * GLA_DV:(h + 1) * GLA_DV].astype(BF16)
        qd[c, h] = (q * jnp.exp(b)).astype(BF16)
        ki = (k * jnp.exp(-b)).astype(BF16)
        kst = (k * jnp.exp(bl - b)).T.astype(BF16)
        decay[c, h] = jnp.exp(b.T[:, C - 1:C])
        a = lax.dot_general(qd[c, h], ki, NT_DIMS, preferred_element_type=F32)
        a = jnp.where(causal, a, 0.0).astype(BF16)
        intra[c, h] = jnp.dot(a, v, preferred_element_type=F32)
        kv[c, h] = jnp.dot(kst, v, preferred_element_type=F32)
    state = [s_ref[h] for h in range(GLA_HEADS)]
    for c, h in pairs:
        o = intra[c, h] + jnp.dot(qd[c, h], state[h].astype(BF16), preferred_element_type=F32)
        state[h] = state[h] * decay[c, h] + kv[c, h]
        o = o * lax.rsqrt(jnp.mean(o * o, axis=-1, keepdims=True) + NORM_EPS) * nw
        gg = gg_ref[0, rows[c], h * GLA_DV:(h + 1) * GLA_DV]
        o_ref[0, rows[c], h * GLA_DV:(h + 1) * GLA_DV] = (o * (gg * jax.nn.sigmoid(gg))).astype(BF16)
    for h in range(GLA_HEADS):
        s_ref[h] = state[h]


def gla_mixer(proj, wa_pad, ba, nw):
    B, T, _ = proj.shape
    TC = min(512, T)
    qk_w = GLA_HEADS * GLA_DK
    v_w = GLA_HEADS * GLA_DV
    return pl.pallas_call(
        functools.partial(_gla_kernel, TC=TC),
        grid=(B, T // TC),
        in_specs=[pl.BlockSpec((1, TC, qk_w), lambda b, i: (b, i, C_GQ // qk_w)),
                  pl.BlockSpec((1, TC, qk_w), lambda b, i: (b, i, C_GK // qk_w)),
                  pl.BlockSpec((1, TC, v_w), lambda b, i: (b, i, C_GV // v_w)),
                  pl.BlockSpec((1, TC, v_w), lambda b, i: (b, i, C_GG // v_w)),
                  pl.BlockSpec((1, TC, LANES), lambda b, i: (b, i, C_MISC // LANES)),
                  pl.BlockSpec((LANES, qk_w), lambda b, i: (0, 0)),
                  pl.BlockSpec((1, qk_w), lambda b, i: (0, 0)),
                  pl.BlockSpec((1, GLA_DV), lambda b, i: (0, 0))],
        out_specs=pl.BlockSpec((1, TC, v_w), lambda b, i: (b, i, 0)),
        out_shape=jax.ShapeDtypeStruct((B, T, v_w), BF16),
        scratch_shapes=[pltpu.VMEM((GLA_HEADS, GLA_DK, GLA_DV), F32),
                        pltpu.VMEM((TC, qk_w), F32)],
        compiler_params=_params("parallel", "arbitrary"),
        name="gla_mixer",
    )(proj, proj, proj, proj, proj, wa_pad, ba, nw)


def _outproj_kernel(n_ref, g_ref, w_ref, x_ref, m_ref, lg_ref, lb_ref, o_ref, *, half):
    sub = 128
    for r in range(o_ref.shape[1] // sub):
        rows = slice(r * sub, (r + 1) * sub)
        y = (jnp.dot(n_ref[0, rows, :], w_ref[0:half, :], preferred_element_type=F32)
             + jnp.dot(g_ref[0, rows, :], w_ref[half:, :], preferred_element_type=F32))
        z = DEEPNORM_ALPHA * x_ref[0, rows, :] + (1.0 + m_ref[0, 2:3, :]) * y
        o_ref[0, rows, :] = _layer_norm(z, lg_ref[...], lb_ref[...])


def out_proj_ln(nsa_o, gla_o, w, x, mod, lg, lb):
    B, T, D = x.shape
    half = nsa_o.shape[-1]
    tm = min(512, T)
    vec = pl.BlockSpec((1, D), lambda b, i: (0, 0))
    return pl.pallas_call(
        functools.partial(_outproj_kernel, half=half),
        grid=(B, T // tm),
        in_specs=[pl.BlockSpec((1, tm, half), lambda b, i: (b, i, 0)),
                  pl.BlockSpec((1, tm, gla_o.shape[-1]), lambda b, i: (b, i, 0)),
                  pl.BlockSpec(w.shape, lambda b, i: (0, 0)),
                  pl.BlockSpec((1, tm, D), lambda b, i: (b, i, 0)),
                  pl.BlockSpec((1, 6, D), lambda b, i: (b, 0, 0)),
                  vec, vec],
        out_specs=pl.BlockSpec((1, tm, D), lambda b, i: (b, i, 0)),
        out_shape=jax.ShapeDtypeStruct((B, T, D), F32),
        compiler_params=_params("parallel", "parallel"),
        name="out_proj_ln",
    )(nsa_o, gla_o, w, x, mod, lg, lb)


FFN_TM = 1024
FFN_TF = 256
FFN_SUB = 512


def _ffn_kernel(x_ref, m_ref, wg_ref, wu_ref, wd_ref, lg_ref, lb_ref, o_ref, h_sc, *, sub):
    j = pl.program_id(2)
    n_sub = o_ref.shape[1] // sub

    @pl.when(j == 0)
    def _():
        for sb in range(n_sub):
            rows = slice(sb * sub, (sb + 1) * sub)
            h_sc[rows, :] = (x_ref[0, rows, :] * (1.0 + m_ref[0, 4:5, :]) + m_ref[0, 3:4, :]).astype(BF16)
        o_ref[...] = jnp.zeros_like(o_ref)

    wg = wg_ref[...].astype(BF16)
    wu = wu_ref[...].astype(BF16)
    wd = wd_ref[...].astype(BF16)
    for sb in range(n_sub):
        rows = slice(sb * sub, (sb + 1) * sub)
        h = h_sc[rows, :]
        a = jnp.dot(h, wg, preferred_element_type=F32)
        u = jnp.dot(h, wu, preferred_element_type=F32)
        o_ref[0, rows, :] += jnp.dot((a * jax.nn.sigmoid(a) * u).astype(BF16), wd, preferred_element_type=F32)

    @pl.when(j == pl.num_programs(2) - 1)
    def _():
        for sb in range(n_sub):
            rows = slice(sb * sub, (sb + 1) * sub)
            z = DEEPNORM_ALPHA * x_ref[0, rows, :] + (1.0 + m_ref[0, 5:6, :]) * o_ref[0, rows, :]
            o_ref[0, rows, :] = _layer_norm(z, lg_ref[...], lb_ref[...])


def ffn_ln(x, mod, wg, wu, wd, lg, lb):
    B, T, D = x.shape
    F = wg.shape[1]
    tm = min(FFN_TM, T)
    tf = FFN_TF
    vec = pl.BlockSpec((1, D), lambda b, i, j: (0, 0))
    return pl.pallas_call(
        functools.partial(_ffn_kernel, sub=min(FFN_SUB, tm)),
        grid=(B, T // tm, F // tf),
        in_specs=[pl.BlockSpec((1, tm, D), lambda b, i, j: (b, i, 0)),
                  pl.BlockSpec((1, 6, D), lambda b, i, j: (b, 0, 0)),
                  pl.BlockSpec((D, tf), lambda b, i, j: (0, j)),
                  pl.BlockSpec((D, tf), lambda b, i, j: (0, j)),
                  pl.BlockSpec((tf, D), lambda b, i, j: (j, 0)),
                  vec, vec],
        out_specs=pl.BlockSpec((1, tm, D), lambda b, i, j: (b, i, 0)),
        out_shape=jax.ShapeDtypeStruct((B, T, D), F32),
        scratch_shapes=[pltpu.VMEM((tm, D), BF16)],
        compiler_params=pltpu.CompilerParams(dimension_semantics=("parallel", "parallel", "arbitrary"),
                                             vmem_limit_bytes=BIG_VMEM_LIMIT_BYTES),
        name="ffn_ln",
    )(x, mod, wg, wu, wd, lg, lb)


def _router_kernel(x_ref, m_ref, wr_ref, h_ref, lg_ref):
    h = x_ref[0] * (1.0 + m_ref[0, 4:5, :]) + m_ref[0, 3:4, :]
    h_hi = h.astype(BF16)
    h_lo = (h - h_hi.astype(F32)).astype(BF16)
    h_ref[0] = h_hi
    lg_ref[0] = (jnp.dot(h_hi, wr_ref[0], preferred_element_type=F32)
                 + jnp.dot(h_lo, wr_ref[0], preferred_element_type=F32)
                 + jnp.dot(h_hi, wr_ref[1], preferred_element_type=F32))


def moe_router(x, mod, wr_pad):
    B, T, D = x.shape
    tm = min(512, T)
    return pl.pallas_call(
        _router_kernel,
        grid=(B, T // tm),
        in_specs=[pl.BlockSpec((1, tm, D), lambda b, i: (b, i, 0)),
                  pl.BlockSpec((1, 6, D), lambda b, i: (b, 0, 0)),
                  pl.BlockSpec((2, D, LANES), lambda b, i: (0, 0, 0))],
        out_specs=[pl.BlockSpec((1, tm, D), lambda b, i: (b, i, 0)),
                   pl.BlockSpec((1, tm, LANES), lambda b, i: (b, i, 0))],
        out_shape=[jax.ShapeDtypeStruct((B, T, D), BF16),
                   jax.ShapeDtypeStruct((B, T, LANES), F32)],
        compiler_params=_params("parallel", "parallel"),
        name="moe_router",
    )(x, mod, wr_pad)


def _moe_kernel(te_ref, nv_ref, x_ref, wg_ref, wu_ref, wd_ref, o_ref, acc_ref, *, sub):
    i = pl.program_id(0)
    j = pl.program_id(1)
    nv = nv_ref[i]

    @pl.when(j == 0)
    def _():
        acc_ref[...] = jnp.zeros_like(acc_ref)

    n_sub = acc_ref.shape[0] // sub

    def swiglu_rows(rows, wg, wu, wd):
        h = x_ref[rows, :]
        a = jnp.dot(h, wg, preferred_element_type=F32)
        u = jnp.dot(h, wu, preferred_element_type=F32)
        acc_ref[rows, :] += jnp.dot((a * jax.nn.sigmoid(a) * u).astype(BF16), wd, preferred_element_type=F32)

    for n_real in range(1, n_sub + 1):
        @pl.when(nv == n_real)
        def _():
            wg = wg_ref[0].astype(BF16)
            wu = wu_ref[0].astype(BF16)
            wd = wd_ref[0].astype(BF16)
            for sb in range(n_real):
                swiglu_rows(slice(sb * sub, (sb + 1) * sub), wg, wu, wd)

    @pl.when(j == pl.num_programs(1) - 1)
    def _():
        o_ref[...] = acc_ref[...].astype(o_ref.dtype)


def moe_experts(tile_e, tile_nv, xg, wg, wu, wd, tm, sub):
    M, D = xg.shape
    F = wg.shape[2]
    tf = 512
    nj = F // tf
    n_tiles = M // tm

    def wj(i, j, nv):
        return jnp.where(nv[i] > 0, j, nj - 1)

    return pl.pallas_call(
        functools.partial(_moe_kernel, sub=sub),
        grid_spec=pltpu.PrefetchScalarGridSpec(
            num_scalar_prefetch=2,
            grid=(n_tiles, nj),
            in_specs=[pl.BlockSpec((tm, D), lambda i, j, te, nv: (i, 0)),
                      pl.BlockSpec((1, D, tf), lambda i, j, te, nv: (te[i], 0, wj(i, j, nv))),
                      pl.BlockSpec((1, D, tf), lambda i, j, te, nv: (te[i], 0, wj(i, j, nv))),
                      pl.BlockSpec((1, tf, D), lambda i, j, te, nv: (te[i], wj(i, j, nv), 0))],
            out_specs=pl.BlockSpec((tm, D), lambda i, j, te, nv: (i, 0)),
            scratch_shapes=[pltpu.VMEM((tm, D), F32)]),
        out_shape=jax.ShapeDtypeStruct((M, D), BF16),
        compiler_params=pltpu.CompilerParams(dimension_semantics=("parallel", "arbitrary"),
                                             vmem_limit_bytes=BIG_VMEM_LIMIT_BYTES),
        name="moe_experts",
    )(tile_e, tile_nv, xg, wg, wu, wd)


def _combine_kernel(y0_ref, y1_ref, cw_ref, x_ref, m_ref, lg_ref, lb_ref, o_ref):
    cw = cw_ref[0]
    y = y0_ref[0].astype(F32) * cw[:, 0:1] + y1_ref[0].astype(F32) * cw[:, 1:2]
    z = DEEPNORM_ALPHA * x_ref[0] + (1.0 + m_ref[0, 5:6, :]) * y
    o_ref[0] = _layer_norm(z, lg_ref[...], lb_ref[...])


def moe_combine_ln(y0, y1, cw, x, mod, lg, lb):
    B, T, D = x.shape
    tm = min(512, T)
    row = pl.BlockSpec((1, tm, D), lambda b, i: (b, i, 0))
    vec = pl.BlockSpec((1, D), lambda b, i: (0, 0))
    return pl.pallas_call(
        _combine_kernel,
        grid=(B, T // tm),
        in_specs=[row, row, pl.BlockSpec((1, tm, LANES), lambda b, i: (b, i, 0)), row,
                  pl.BlockSpec((1, 6, D), lambda b, i: (b, 0, 0)), vec, vec],
        out_specs=row,
        out_shape=jax.ShapeDtypeStruct((B, T, D), F32),
        compiler_params=_params("parallel", "parallel"),
        name="moe_combine_ln",
    )(y0, y1, cw, x, mod, lg, lb)


MOE_TM = 1024
MOE_SUB = 256


def moe_layer(x, mod, w_router, wg, wu, wd, lg, lb):
    B, T, D = x.shape
    N = B * T
    A = N * TOP_K
    tm = MOE_TM
    wr_pad = jnp.zeros((D, LANES), F32).at[:, :N_EXPERTS].set(w_router)
    wr_hi = wr_pad.astype(BF16)
    wr_lo = (wr_pad - wr_hi.astype(F32)).astype(BF16)
    h, logits = moe_router(x, mod, jnp.stack([wr_hi, wr_lo]))
    logits = logits.reshape(N, LANES)[:, :N_EXPERTS]
    top_val, top_idx = lax.top_k(logits, TOP_K)
    comb = jax.nn.softmax(top_val, axis=-1)
    flat_e = top_idx.reshape(-1).astype(jnp.int32)
    onehot = (flat_e[:, None] == jnp.arange(N_EXPERTS, dtype=jnp.int32)[None, :]).astype(jnp.int32)
    csum = jnp.cumsum(onehot, axis=0)
    counts = csum[-1]
    padded = (counts + tm - 1) // tm * tm
    pad_end = jnp.cumsum(padded)
    pad_start = pad_end - padded
    slot = jnp.sum(onehot * (csum + pad_start[None, :]), axis=1) - 1
    n_tiles = -(-A // tm) + N_EXPERTS
    tile_start = jnp.arange(n_tiles, dtype=jnp.int32) * tm
    tile_e = jnp.minimum(jnp.searchsorted(pad_end, tile_start, side='right'), N_EXPERTS - 1).astype(jnp.int32)
    valid = jnp.clip(pad_start[tile_e] + counts[tile_e] - tile_start, 0, tm)
    tile_nv = ((valid + MOE_SUB - 1) // MOE_SUB).astype(jnp.int32)
    order = jnp.argsort(flat_e)
    seg_start = jnp.cumsum(counts) - counts
    slot_id = jnp.arange(n_tiles * tm, dtype=jnp.int32)
    within = slot_id - jnp.repeat(tile_start, tm)
    src = jnp.repeat(seg_start[tile_e] + tile_start - pad_start[tile_e], tm) + within
    slot_tok = jnp.where(within < jnp.repeat(valid, tm),
                         order[jnp.clip(src, 0, A - 1)].astype(jnp.int32) // TOP_K, slot_id % N)
    n_used = pad_end[-1] // tm
    tile_e = jnp.where(jnp.arange(n_tiles) < n_used, tile_e, tile_e[jnp.maximum(n_used - 1, 0)])
    xg = h.reshape(N, D)[slot_tok]
    y = moe_experts(tile_e, tile_nv, xg, wg, wu, wd, tm, MOE_SUB)
    slot_of = slot.reshape(N, TOP_K)
    y0 = y[slot_of[:, 0]].reshape(B, T, D)
    y1 = y[slot_of[:, 1]].reshape(B, T, D)
    cw = jnp.zeros((N, LANES), F32).at[:, :TOP_K].set(comb).reshape(B, T, LANES)
    return moe_combine_ln(y0, y1, cw, x, mod, lg, lb)


def _rope_tables(T):
    half = ROPE_DIMS // 2
    inv = ROPE_THETA ** (-jnp.arange(half, dtype=F32) * 2.0 / ROPE_DIMS)
    ang = jnp.arange(T).astype(F32)[:, None] * inv[None, :]
    cos, sin = jnp.cos(ang), jnp.sin(ang)
    z = jnp.zeros((T, LANES - ROPE_DIMS), F32)
    zh = jnp.zeros((T, half), F32)
    rc = jnp.concatenate([cos, cos, jnp.ones((T, LANES - ROPE_DIMS), F32)], axis=1)
    ra = jnp.concatenate([-sin, zh, z], axis=1)
    rb = jnp.concatenate([zh, sin, z], axis=1)
    return rc, ra, rb


def _overlap_matrix_t(NC, NB):
    nbp = -(-NB // 16) * 16
    c0 = np.arange(NC)[None, :] * CMP_STRIDE
    b0 = np.arange(nbp)[:, None] * SLC_BLOCK
    ov = (c0 < b0 + SLC_BLOCK) & (c0 + CMP_BLOCK > b0) & (np.arange(nbp)[:, None] < NB)
    return jnp.asarray(ov.astype(np.float32), dtype=BF16)


def _reorder_w_in(w):
    D = w.shape[0]
    o_ng = 1024 + 6 * 256
    o_gq = o_ng + 24
    o_gk = o_gq + 512
    o_gv = o_gk + 512
    o_ga = o_gv + 1024
    o_gg = o_ga + GLA_GATE_RANK
    misc = jnp.concatenate([w[:, o_ng:o_gq], w[:, o_ga:o_gg],
                            jnp.zeros((D, LANES - 24 - GLA_GATE_RANK), w.dtype)], axis=1)
    cols = [w[:, 0:1024], w[:, o_gv:o_ga], w[:, o_gg:o_gg + 1024], w[:, 1024:o_ng],
            w[:, o_gq:o_gk], w[:, o_gk:o_gv], misc]
    return jnp.concatenate(cols, axis=1).astype(BF16)


def hybrid_mixer_ln(x, mod, w_in_r, cmp_pe, cmp_w1, cmp_w2, wa_pad, ba, nw, w_out, lg, lb, tables, overlap):
    proj = in_proj(x, mod, w_in_r)
    qr, qn, ks, vs, kw, vw, gates = nsa_prep(proj, *tables)
    kc, vct = nsa_compress(proj, cmp_pe, cmp_w1, cmp_w2)
    nsa_o = nsa_attention(qr, qn, gates, kc, vct, ks, vs, kw, vw, overlap)
    gla_o = gla_mixer(proj, wa_pad, ba, nw)
    return out_proj_ln(nsa_o, gla_o, w_out, x, mod, lg, lb)


def kernel(x, c, w_ada, b_ada, w_in, cmp_pos_k, cmp_w1_k, cmp_w2_k, cmp_pos_v, cmp_w1_v, cmp_w2_v, gla_w_a2, gla_b_a, gla_norm_w, w_out, ln_mix_g, ln_mix_b, ln_ffn_g, ln_ffn_b, ffn_w_gate, ffn_w_up, ffn_w_down, moe_router, moe_w_gate, moe_w_up, moe_w_down):
    B, T, D = x.shape
    L = w_ada.shape[0]
    c_pad = jnp.zeros((8, D), F32).at[:B].set(c)
    mod_all = ada_mod(c_pad, w_ada, b_ada.reshape(L, 1, 6 * D))[:, :B].reshape(L, B, 6, D)
    tables = _rope_tables(T)
    overlap = _overlap_matrix_t(T // CMP_STRIDE, T // SLC_BLOCK)
    for layer in range(L):
        mod = mod_all[layer]
        wa_pad = jnp.zeros((LANES, GLA_HEADS * GLA_DK), F32).at[MISC_GA:MISC_GA + GLA_GATE_RANK].set(
            gla_w_a2[layer]).astype(BF16)
        x = hybrid_mixer_ln(
            x, mod, _reorder_w_in(w_in[layer]),
            jnp.stack([cmp_pos_k[layer], cmp_pos_v[layer]]),
            jnp.stack([cmp_w1_k[layer], cmp_w1_v[layer]]).astype(BF16),
            jnp.stack([cmp_w2_k[layer], cmp_w2_v[layer]]).astype(BF16),
            wa_pad, gla_b_a[layer].reshape(1, -1), gla_norm_w[layer].reshape(1, -1),
            w_out[layer].astype(BF16), ln_mix_g[layer].reshape(1, D), ln_mix_b[layer].reshape(1, D),
            tables, overlap)
        lg = ln_ffn_g[layer].reshape(1, D)
        lb = ln_ffn_b[layer].reshape(1, D)
        i = layer // 2
        if layer % 2 == 0:
            x = ffn_ln(x, mod, ffn_w_gate[i], ffn_w_up[i], ffn_w_down[i], lg, lb)
        else:
            x = moe_layer(x, mod, moe_router[i], moe_w_gate[i], moe_w_up[i], moe_w_down[i], lg, lb)
    return x
```

```python
import functools

import numpy as np
import jax
import jax.numpy as jnp
from jax import lax
from jax.experimental import pallas as pl
from jax.experimental.pallas import tpu as pltpu

F32 = jnp.float32
BF16 = jnp.bfloat16

D_MODEL = 2048
DEPTH = 2
HEAD_DIM = 128
NSA_HEADS = 8
NSA_KV_HEADS = 2
NSA_GROUP = 4
CMP_BLOCK = 32
CMP_STRIDE = 16
CMP_HIDDEN = 256
SLC_BLOCK = 64
SLC_TOPN = 16
WINDOW = 512
FORCE_SCORE = 1e9
GLA_DV = 256
GLA_HEADS = 4
GLA_DK = 128
GLA_GATE_RANK = 16
GLA_GATE_NORM = 16.0
GLA_CHUNK = 64
ROPE_THETA = 500000.0
ROPE_DIMS = 32
N_EXPERTS = 8
TOP_K = 2
LN_EPS = 1e-5
NORM_EPS = 1e-6
DEEPNORM_ALPHA = (2 * DEPTH) ** 0.25
NEG_BIG = -1e30
MASK_BIG = 2.0 ** 100
V_AUG_ROWS = HEAD_DIM + 16
NSA_HEADS_PER_STEP = 2
NSA_TK = 512
LOG2_E = 1.4426950408889634

VMEM_LIMIT_BYTES = 56 * 1024 * 1024
BIG_VMEM_LIMIT_BYTES = 60 * 1024 * 1024
LANES = 128

A_NQ = 0
A_KS = 1024
A_KW = 1280
A_VS = 1536
A_VW = 1792
A_W = 2048
C_GV = 0
C_GG = 1024
C_GQ = 2048
C_GK = 2560
C_KC = 3072
C_VC = 3328
C_MISC = 3584
PROJ_W = 3712
MISC_GA = 24

NT_DIMS = (((1,), (1,)), ((), ()))


def _params(*sem):
    return pltpu.CompilerParams(dimension_semantics=sem, vmem_limit_bytes=VMEM_LIMIT_BYTES)


def _layer_norm(z, g, b):
    mu = jnp.mean(z, axis=-1, keepdims=True)
    zc = z - mu
    var = jnp.mean(zc * zc, axis=-1, keepdims=True)
    return zc * lax.rsqrt(var + LN_EPS) * g + b


def _ada_kernel(c_ref, w_ref, b_ref, o_ref):
    c = c_ref[...]
    cond = c * jax.nn.sigmoid(c)
    o_ref[0] = jnp.dot(cond.astype(BF16), w_ref[0].astype(BF16),
                       preferred_element_type=F32) + b_ref[0]


def ada_mod(c_pad, w_ada, b_ada):
    L, D, N = w_ada.shape
    tn = 1024
    return pl.pallas_call(
        _ada_kernel,
        grid=(L, N // tn),
        in_specs=[pl.BlockSpec((8, D), lambda l, j: (0, 0)),
                  pl.BlockSpec((1, D, tn), lambda l, j: (l, 0, j)),
                  pl.BlockSpec((1, 1, tn), lambda l, j: (l, 0, j))],
        out_specs=pl.BlockSpec((1, 8, tn), lambda l, j: (l, 0, j)),
        out_shape=jax.ShapeDtypeStruct((L, 8, N), F32),
        compiler_params=_params("parallel", "parallel"),
        name="ada_mod",
    )(c_pad, w_ada, b_ada)


def _inproj_kernel(x_ref, m_ref, w_ref, o_ref):
    h = x_ref[0] * (1.0 + m_ref[0, 1:2, :]) + m_ref[0, 0:1, :]
    o_ref[0] = jnp.dot(h.astype(BF16), w_ref[...], preferred_element_type=F32)


def in_proj(x, mod, w):
    B, T, D = x.shape
    N = w.shape[1]
    tm = min(512, T)
    return pl.pallas_call(
        _inproj_kernel,
        grid=(B, T // tm),
        in_specs=[pl.BlockSpec((1, tm, D), lambda b, i: (b, i, 0)),
                  pl.BlockSpec((1, 6, D), lambda b, i: (b, 0, 0)),
                  pl.BlockSpec((D, N), lambda b, i: (0, 0))],
        out_specs=pl.BlockSpec((1, tm, N), lambda b, i: (b, i, 0)),
        out_shape=jax.ShapeDtypeStruct((B, T, N), F32),
        compiler_params=_params("parallel", "parallel"),
        name="in_proj",
    )(x, mod, w)


def _nsa_proj_kernel(x_ref, m_ref, w_ref, rc_ref, ra_ref, rb_ref,
                     qr_ref, qn_ref, ks_ref, vs_ref, kw_ref, vw_ref, *, sub):
    tm = x_ref.shape[1]
    scale = HEAD_DIM ** -0.5 * LOG2_E
    for r in range(tm // sub):
        rows = slice(r * sub, (r + 1) * sub)
        rc = rc_ref[rows, :]
        ra = ra_ref[rows, :]
        rb = rb_ref[rows, :]

        def rope(xh):
            return (xh * rc + pltpu.roll(xh, LANES - ROPE_DIMS // 2, 1) * ra
                    + pltpu.roll(xh, ROPE_DIMS // 2, 1) * rb)

        h = x_ref[0, rows, :] * (1.0 + m_ref[0, 1:2, :]) + m_ref[0, 0:1, :]
        p = jnp.dot(h.astype(BF16), w_ref[...], preferred_element_type=F32)
        for hq in range(NSA_HEADS):
            xh = p[:, A_NQ + hq * HEAD_DIM:A_NQ + (hq + 1) * HEAD_DIM]
            qr_ref[0, rows, hq * HEAD_DIM:(hq + 1) * HEAD_DIM] = (rope(xh) * scale).astype(BF16)
            qn_ref[0, rows, hq * HEAD_DIM:(hq + 1) * HEAD_DIM] = (xh * scale).astype(BF16)
        key_blk = (pl.program_id(1) * tm + r * sub + lax.broadcasted_iota(jnp.int32, (sub, LANES), 0)) // SLC_BLOCK
        blk_onehot = jnp.where(key_blk == lax.broadcasted_iota(jnp.int32, (sub, LANES), 1), 1.0, 0.0).astype(BF16)
        for hk in range(NSA_KV_HEADS):
            ks_ref[0, hk, rows, 0:HEAD_DIM] = rope(p[:, A_KS + hk * HEAD_DIM:A_KS + (hk + 1) * HEAD_DIM]).astype(BF16)
            ks_ref[0, hk, rows, HEAD_DIM:] = blk_onehot
            kw_ref[0, hk, rows, :] = rope(p[:, A_KW + hk * HEAD_DIM:A_KW + (hk + 1) * HEAD_DIM]).astype(BF16)
            vs_ref[0, hk, 0:HEAD_DIM, rows] = p[:, A_VS + hk * HEAD_DIM:A_VS + (hk + 1) * HEAD_DIM].T.astype(BF16)
            vs_ref[0, hk, HEAD_DIM:, rows] = jnp.ones((V_AUG_ROWS - HEAD_DIM, sub), BF16)
            vw_ref[0, hk, :, rows] = p[:, A_VW + hk * HEAD_DIM:A_VW + (hk + 1) * HEAD_DIM].T.astype(BF16)


def nsa_proj(x, mod, w, rc, ra, rb):
    B, T, D = x.shape
    tm = min(512, T)
    assert T // SLC_BLOCK <= LANES
    qw = NSA_HEADS * HEAD_DIM
    qspec = pl.BlockSpec((1, tm, qw), lambda b, i: (b, i, 0))
    tspec = pl.BlockSpec((tm, LANES), lambda b, i: (i, 0))
    return pl.pallas_call(
        functools.partial(_nsa_proj_kernel, sub=min(128, tm)),
        grid=(B, T // tm),
        in_specs=[pl.BlockSpec((1, tm, D), lambda b, i: (b, i, 0)),
                  pl.BlockSpec((1, 6, D), lambda b, i: (b, 0, 0)),
                  pl.BlockSpec((D, A_W), lambda b, i: (0, 0)),
                  tspec, tspec, tspec],
        out_specs=[qspec, qspec,
                   pl.BlockSpec((1, NSA_KV_HEADS, tm, HEAD_DIM + LANES), lambda b, i: (b, 0, i, 0)),
                   pl.BlockSpec((1, NSA_KV_HEADS, V_AUG_ROWS, tm), lambda b, i: (b, 0, 0, i)),
                   pl.BlockSpec((1, NSA_KV_HEADS, tm, HEAD_DIM), lambda b, i: (b, 0, i, 0)),
                   pl.BlockSpec((1, NSA_KV_HEADS, HEAD_DIM, tm), lambda b, i: (b, 0, 0, i))],
        out_shape=[jax.ShapeDtypeStruct((B, T, qw), BF16),
                   jax.ShapeDtypeStruct((B, T, qw), BF16),
                   jax.ShapeDtypeStruct((B, NSA_KV_HEADS, T, HEAD_DIM + LANES), BF16),
                   jax.ShapeDtypeStruct((B, NSA_KV_HEADS, V_AUG_ROWS, T), BF16),
                   jax.ShapeDtypeStruct((B, NSA_KV_HEADS, T, HEAD_DIM), BF16),
                   jax.ShapeDtypeStruct((B, NSA_KV_HEADS, HEAD_DIM, T), BF16)],
        compiler_params=_params("parallel", "parallel"),
        name="nsa_proj",
    )(x, mod, w, rc, ra, rb)


def _compress_kernel(a_ref, pe_ref, w1_ref, w2_ref, o_ref, ot_ref, *, n_half):
    half = CMP_STRIDE

    def part(l0):
        acc = jnp.zeros((n_half, CMP_HIDDEN), F32)
        for l in range(half):
            rows = a_ref[0, pl.ds(l, n_half, stride=half), :] + pe_ref[0, l0 + l:l0 + l + 1, :]
            acc += jnp.dot(rows.astype(BF16), w1_ref[0, (l0 + l) * HEAD_DIM:(l0 + l + 1) * HEAD_DIM, :],
                           preferred_element_type=F32)
        return acc

    first = part(0)
    second = part(half)
    hid = first + pltpu.roll(second, n_half - 1, 0)
    row = lax.broadcasted_iota(jnp.int32, (n_half, 1), 0)
    hid = jnp.where(row < n_half - 1, hid, 0.0)
    act = jax.nn.gelu(hid)
    out = jnp.dot(act.astype(BF16), w2_ref[0], preferred_element_type=F32)
    o_ref[0, 0, 0] = out.astype(BF16)
    ot_ref[0, 0, 0] = out.T.astype(BF16)


def nsa_compress(proj, pe, w1, w2):
    B, T, _ = proj.shape
    n_half = T // CMP_STRIDE
    return pl.pallas_call(
        functools.partial(_compress_kernel, n_half=n_half),
        grid=(B, 2, NSA_KV_HEADS),
        in_specs=[pl.BlockSpec((1, T, HEAD_DIM), lambda b, s, h: (b, 0, C_KC // HEAD_DIM + s * NSA_KV_HEADS + h)),
                  pl.BlockSpec((1, CMP_BLOCK, HEAD_DIM), lambda b, s, h: (s, 0, 0)),
                  pl.BlockSpec((1, CMP_BLOCK * HEAD_DIM, CMP_HIDDEN), lambda b, s, h: (s, 0, 0)),
                  pl.BlockSpec((1, CMP_HIDDEN, HEAD_DIM), lambda b, s, h: (s, 0, 0))],
        out_specs=[pl.BlockSpec((1, 1, 1, n_half, HEAD_DIM), lambda b, s, h: (b, s, h, 0, 0)),
                   pl.BlockSpec((1, 1, 1, HEAD_DIM, n_half), lambda b, s, h: (b, s, h, 0, 0))],
        out_shape=[jax.ShapeDtypeStruct((B, 2, NSA_KV_HEADS, n_half, HEAD_DIM), BF16),
                   jax.ShapeDtypeStruct((B, 2, NSA_KV_HEADS, HEAD_DIM, n_half), BF16)],
        compiler_params=_params("parallel", "parallel", "parallel"),
        name="nsa_compress",
    )(proj, pe, w1, w2)


def _nsa_kernel(qr_ref, qn_ref, g_ref, kc_ref, vct_ref, ks_ref, vst_ref, kw_ref, vwt_ref, ovt_ref, o_ref,
                *, TQ, TK, WK, NC, NB, NH):
    G = NSA_GROUP
    R = G * TQ
    NBP = ovt_ref.shape[0]
    i = pl.program_id(2)
    t0 = i * TQ
    m_floor = 0.5 * NEG_BIG

    def stack(ref, h):
        x = ref[0]
        return jnp.concatenate([x[:, (h * G + g) * HEAD_DIM:(h * G + g + 1) * HEAD_DIM] for g in range(G)], axis=0)

    def tile_g(x):
        return jnp.concatenate([x] * G, axis=1)

    tT = t0 + lax.broadcasted_iota(jnp.int32, (1, TQ), 1)
    cur = tT // SLC_BLOCK
    n_sel = min(SLC_TOPN, NB)

    HS = range(NH)
    qr = [stack(qr_ref, h) for h in HS]
    qn = [stack(qn_ref, h) for h in HS]

    ws = pl.multiple_of(jnp.maximum(t0 + TQ - WK, 0), LANES)
    s_c = [lax.dot_general(kc_ref[0, 0, h], qn[h], NT_DIMS, preferred_element_type=F32) for h in HS]
    s_w = [lax.dot_general(kw_ref[0, h, pl.ds(ws, WK), :], qr[h], NT_DIMS, preferred_element_type=F32)
           for h in HS]
    cend = lax.broadcasted_iota(jnp.int32, (NC, 1), 0) * CMP_STRIDE + (CMP_BLOCK - 1)
    bias_c = tile_g(jnp.where(cend <= tT, 0.0, NEG_BIG))
    dist = tT - (ws + lax.broadcasted_iota(jnp.int32, (WK, 1), 0))
    bias_w = tile_g(jnp.where(dist >= 0, jnp.where(dist < WINDOW, 0.0, NEG_BIG), NEG_BIG))

    p_cb, e_wb, l_w = [], [], []
    for h in HS:
        sc = s_c[h] + bias_c
        m_c = jnp.maximum(jnp.max(sc, axis=0, keepdims=True), m_floor)
        e_c = jnp.exp2(sc - m_c)
        den = jnp.sum(e_c, axis=0, keepdims=True)
        p_cb.append((e_c * jnp.where(den > 0.0, 1.0 / den, 0.0)).astype(BF16))
        sw = s_w[h] + bias_w
        e_w = jnp.exp2(sw - jnp.max(sw, axis=0, keepdims=True))
        l_w.append(jnp.sum(e_w, axis=0, keepdims=True))
        e_wb.append(e_w.astype(BF16))

    o_c = [jnp.dot(vct_ref[0, 0, h], p_cb[h], preferred_element_type=F32) for h in HS]
    imp4 = [jnp.dot(ovt_ref[...], p_cb[h], preferred_element_type=F32) for h in HS]
    o_w = [jnp.dot(vwt_ref[0, h, :, pl.ds(ws, WK)], e_wb[h], preferred_element_type=F32) * (1.0 / l_w[h])
           for h in HS]

    jj = lax.broadcasted_iota(jnp.int32, (NBP, TQ), 0)
    forced = jnp.where(jj == 0, 1, jnp.where(jj == cur, 1, jnp.where(jj == cur - 1, 1, 0)))
    val = []
    for h in HS:
        impT = imp4[h][:, 0:TQ]
        for g in range(1, G):
            impT = impT + imp4[h][:, g * TQ:(g + 1) * TQ]
        val.append(jnp.where(jj > cur, -jnp.inf, jnp.where(forced > 0, FORCE_SCORE, impT)))
    rank = [jnp.zeros((NBP, TQ), jnp.int32) for h in HS]
    for j2 in range(NB):
        later = jnp.where(jj > j2, 1, 0)
        for h in HS:
            row = val[h][j2:j2 + 1, :]
            rank[h] = rank[h] + jnp.where(row > val[h], 1, jnp.where(row == val[h], later, 0))
    fronts = []
    for h in HS:
        sel_neg = jnp.where(jj > cur, -MASK_BIG, jnp.where(rank[h] < n_sel, 0.0, -MASK_BIG))
        if NBP < LANES:
            sel_neg = jnp.concatenate([sel_neg, jnp.zeros((LANES - NBP, TQ), F32)], axis=0)
        sel_q = sel_neg.T.astype(BF16)
        q_aug = jnp.concatenate([qr[h], jnp.concatenate([sel_q] * G, axis=0)], axis=1)
        fronts.append((o_c[h], o_w[h], q_aug))

    def sel_scores(h, kt):
        k0 = pl.multiple_of(kt * TK, TK)
        k = ks_ref[0, h, pl.ds(k0, TK), :]
        s = lax.dot_general(k, fronts[h][2], NT_DIMS, preferred_element_type=F32)
        return s, jnp.max(s, axis=0, keepdims=True)

    def sel_accumulate(h, kt, s, smax, m, acc):
        k0 = pl.multiple_of(kt * TK, TK)
        vt = vst_ref[0, h, :, pl.ds(k0, TK)]
        m_new = jnp.maximum(m, smax)
        p = jnp.exp2(s - m_new).astype(BF16)
        acc = jnp.exp2(m - m_new) * acc + jnp.dot(vt, p, preferred_element_type=F32)
        return m_new, acc

    def sel_body(kt, carry):
        out = []
        for h in range(NH):
            s, smax, m, acc = carry[h]
            nxt = sel_scores(h, kt + 1)
            out.append(nxt + sel_accumulate(h, kt, s, smax, m, acc))
        return tuple(out)

    n_kt = (t0 + TQ + TK - 1) // TK
    m0 = jnp.full((1, R), m_floor, F32)
    a0 = jnp.zeros((V_AUG_ROWS, R), F32)
    state = lax.fori_loop(0, n_kt - 1, sel_body, tuple(sel_scores(h, 0) + (m0, a0) for h in range(NH)))
    krow = (n_kt - 1) * TK + lax.broadcasted_iota(jnp.int32, (TK, 1), 0)
    causal = tile_g(jnp.where(krow <= tT, 0.0, NEG_BIG))
    k_last = pl.multiple_of((n_kt - 1) * TK, TK)
    s_fin = [state[h][0] + causal for h in HS]
    m_fin = [jnp.maximum(state[h][2], jnp.max(s_fin[h], axis=0, keepdims=True)) for h in HS]
    p_fin = [jnp.exp2(s_fin[h] - m_fin[h]).astype(BF16) for h in HS]
    acc_fin = [jnp.exp2(state[h][2] - m_fin[h]) * state[h][3]
               + jnp.dot(vst_ref[0, h, :, pl.ds(k_last, TK)], p_fin[h], preferred_element_type=F32) for h in HS]
    gate_all = jax.nn.sigmoid(g_ref[0]).T
    assert NH == NSA_KV_HEADS
    gates = [gate_all[3 * G * h:3 * G * (h + 1), :] for h in HS]
    for h in HS:
        o_s = acc_fin[h][0:HEAD_DIM] * (1.0 / acc_fin[h][HEAD_DIM:HEAD_DIM + 1])
        o_c, o_w, _ = fronts[h]
        gate = gates[h]
        for g in range(G):
            cols = slice(g * TQ, (g + 1) * TQ)
            o = (gate[3 * g:3 * g + 1, :] * o_c[:, cols] + gate[3 * g + 1:3 * g + 2, :] * o_s[:, cols]
                 + gate[3 * g + 2:3 * g + 3, :] * o_w[:, cols])
            o_ref[0, :, (h * G + g) * HEAD_DIM:(h * G + g + 1) * HEAD_DIM] = o.T.astype(BF16)


def nsa_attention(qr, qn, gates, kc, vct, ks, vst, kw, vwt, overlap_t):
    B, T, _ = qr.shape
    TQ = 128
    TK = min(NSA_TK, T)
    WK = min(WINDOW + TQ, T)
    NC = kc.shape[3]
    NB = T // SLC_BLOCK
    NH = NSA_HEADS_PER_STEP
    gw = NH * NSA_GROUP * HEAD_DIM
    qspec = pl.BlockSpec((1, TQ, gw), lambda b, h, i: (b, i, h))
    kspec = pl.BlockSpec((1, NH, T, HEAD_DIM), lambda b, h, i: (b, h, 0, 0))
    vtspec = pl.BlockSpec((1, NH, HEAD_DIM, T), lambda b, h, i: (b, h, 0, 0))
    return pl.pallas_call(
        functools.partial(_nsa_kernel, TQ=TQ, TK=TK, WK=WK, NC=NC, NB=NB, NH=NH),
        grid=(B, NSA_KV_HEADS // NH, T // TQ),
        in_specs=[qspec, qspec,
                  pl.BlockSpec((1, TQ, LANES), lambda b, h, i: (b, i, C_MISC // LANES)),
                  pl.BlockSpec((1, 1, NH, NC, HEAD_DIM), lambda b, h, i: (b, 0, h, 0, 0)),
                  pl.BlockSpec((1, 1, NH, HEAD_DIM, NC), lambda b, h, i: (b, 1, h, 0, 0)),
                  pl.BlockSpec((1, NH, T, HEAD_DIM + LANES), lambda b, h, i: (b, h, 0, 0)),
                  pl.BlockSpec((1, NH, V_AUG_ROWS, T), lambda b, h, i: (b, h, 0, 0)),
                  kspec, vtspec,
                  pl.BlockSpec(overlap_t.shape, lambda b, h, i: (0, 0))],
        out_specs=pl.BlockSpec((1, TQ, gw), lambda b, h, i: (b, i, h)),
        out_shape=jax.ShapeDtypeStruct((B, T, NSA_HEADS * HEAD_DIM), BF16),
        compiler_params=_params("parallel", "parallel", "arbitrary"),
        name="nsa_attention",
    )(qr, qn, gates, kc, vct, ks, vst, kw, vwt, overlap_t)


def _gla_kernel(q_ref, k_ref, v_ref, gg_ref, misc_ref, wa_ref, ba_ref, nw_ref, o_ref, s_ref, la_ref, *, TC):
    C = GLA_CHUNK

    @pl.when(pl.program_id(1) == 0)
    def _():
        s_ref[...] = jnp.zeros_like(s_ref)

    z = jnp.dot(misc_ref[0].astype(BF16), wa_ref[...], preferred_element_type=F32) + ba_ref[...]
    la_ref[...] = (jnp.minimum(z, 0.0) - jnp.log1p(jnp.exp(-jnp.abs(z)))) * (1.0 / GLA_GATE_NORM)
    ri = lax.broadcasted_iota(jnp.int32, (C, C), 0)
    ci = lax.broadcasted_iota(jnp.int32, (C, C), 1)
    causal = ri >= ci
    tri = jnp.where(causal, 1.0, 0.0)
    nw = nw_ref[...]

    n_c = TC // C
    pairs = [(c, h) for c in range(n_c) for h in range(GLA_HEADS)]
    rows = [slice(c * C, (c + 1) * C) for c in range(n_c)]
    b_all = [jnp.dot(tri, la_ref[rows[c], :], preferred_element_type=F32, precision=lax.Precision.HIGHEST)
             for c in range(n_c)]
    qd, kv, decay, intra = {}, {}, {}, {}
    for c, h in pairs:
        b = b_all[c][:, h * GLA_DK:(h + 1) * GLA_DK]
        bl = b[C - 1:C, :]
        q = q_ref[0, rows[c], h * GLA_DK:(h + 1) * GLA_DK] * (GLA_DK ** -0.5)
        k = k_ref[0, rows[c], h * GLA_DK:(h + 1) * GLA_DK]
        v = v_ref[0, rows[c], h * GLA_DV:(h + 1) * GLA_DV].astype(BF16)
        qd[c, h] = (q * jnp.exp(b)).astype(BF16)
        ki = (k * jnp.exp(-b)).astype(BF16)
        kst = (k * jnp.exp(bl - b)).T.astype(BF16)
        decay[c, h] = jnp.exp(b.T[:, C - 1:C])
        a = lax.dot_general(qd[c, h], ki, NT_DIMS, preferred_element_type=F32)
        a = jnp.where(causal, a, 0.0).astype(BF16)
        intra[c, h] = jnp.dot(a, v, preferred_element_type=F32)
        kv[c, h] = jnp.dot(kst, v, preferred_element_type=F32)
    state = [s_ref[h] for h in range(GLA_HEADS)]
    for c, h in pairs:
        o = intra[c, h] + jnp.dot(qd[c, h], state[h].astype(BF16), preferred_element_type=F32)
        state[h] = state[h] * decay[c, h] + kv[c, h]
        o = o * lax.rsqrt(jnp.mean(o * o, axis=-1, keepdims=True) + NORM_EPS) * nw
        gg = gg_ref[0, rows[c], h * GLA_DV:(h + 1) * GLA_DV]
        o_ref[0, rows[c], h * GLA_DV:(h + 1) * GLA_DV] = (o * (gg * jax.nn.sigmoid(gg))).astype(BF16)
    for h in range(GLA_HEADS):
        s_ref[h] = state[h]


def gla_mixer(proj, wa_pad, ba, nw):
    B, T, _ = proj.shape
    TC = min(512, T)
    qk_w = GLA_HEADS * GLA_DK
    v_w = GLA_HEADS * GLA_DV
    return pl.pallas_call(
        functools.partial(_gla_kernel, TC=TC),
        grid=(B, T // TC),
        in_specs=[pl.BlockSpec((1, TC, qk_w), lambda b, i: (b, i, C_GQ // qk_w)),
                  pl.BlockSpec((1, TC, qk_w), lambda b, i: (b, i, C_GK // qk_w)),
                  pl.BlockSpec((1, TC, v_w), lambda b, i: (b, i, C_GV // v_w)),
                  pl.BlockSpec((1, TC, v_w), lambda b, i: (b, i, C_GG // v_w)),
                  pl.BlockSpec((1, TC, LANES), lambda b, i: (b, i, C_MISC // LANES)),
                  pl.BlockSpec((LANES, qk_w), lambda b, i: (0, 0)),
                  pl.BlockSpec((1, qk_w), lambda b, i: (0, 0)),
                  pl.BlockSpec((1, GLA_DV), lambda b, i: (0, 0))],
        out_specs=pl.BlockSpec((1, TC, v_w), lambda b, i: (b, i, 0)),
        out_shape=jax.ShapeDtypeStruct((B, T, v_w), BF16),
        scratch_shapes=[pltpu.VMEM((GLA_HEADS, GLA_DK, GLA_DV), F32),
                        pltpu.VMEM((TC, qk_w), F32)],
        compiler_params=_params("parallel", "arbitrary"),
        name="gla_mixer",
    )(proj, proj, proj, proj, proj, wa_pad, ba, nw)


def _outproj_kernel(n_ref, g_ref, w_ref, x_ref, m_ref, lg_ref, lb_ref, o_ref, *, half):
    sub = 128
    for r in range(o_ref.shape[1] // sub):
        rows = slice(r * sub, (r + 1) * sub)
        y = (jnp.dot(n_ref[0, rows, :], w_ref[0:half, :], preferred_element_type=F32)
             + jnp.dot(g_ref[0, rows, :], w_ref[half:, :], preferred_element_type=F32))
        z = DEEPNORM_ALPHA * x_ref[0, rows, :] + (1.0 + m_ref[0, 2:3, :]) * y
        o_ref[0, rows, :] = _layer_norm(z, lg_ref[...], lb_ref[...])


def out_proj_ln(nsa_o, gla_o, w, x, mod, lg, lb):
    B, T, D = x.shape
    half = nsa_o.shape[-1]
    tm = min(512, T)
    vec = pl.BlockSpec((1, D), lambda b, i: (0, 0))
    return pl.pallas_call(
        functools.partial(_outproj_kernel, half=half),
        grid=(B, T // tm),
        in_specs=[pl.BlockSpec((1, tm, half), lambda b, i: (b, i, 0)),
                  pl.BlockSpec((1, tm, gla_o.shape[-1]), lambda b, i: (b, i, 0)),
                  pl.BlockSpec(w.shape, lambda b, i: (0, 0)),
                  pl.BlockSpec((1, tm, D), lambda b, i: (b, i, 0)),
                  pl.BlockSpec((1, 6, D), lambda b, i: (b, 0, 0)),
                  vec, vec],
        out_specs=pl.BlockSpec((1, tm, D), lambda b, i: (b, i, 0)),
        out_shape=jax.ShapeDtypeStruct((B, T, D), F32),
        compiler_params=_params("parallel", "parallel"),
        name="out_proj_ln",
    )(nsa_o, gla_o, w, x, mod, lg, lb)


FFN_TM = 1024
FFN_TF = 256
FFN_SUB = 512


def _ffn_kernel(x_ref, m_ref, wg_ref, wu_ref, wd_ref, lg_ref, lb_ref, o_ref, h_sc, *, sub):
    j = pl.program_id(2)
    n_sub = o_ref.shape[1] // sub

    @pl.when(j == 0)
    def _():
        for sb in range(n_sub):
            rows = slice(sb * sub, (sb + 1) * sub)
            h_sc[rows, :] = (x_ref[0, rows, :] * (1.0 + m_ref[0, 4:5, :]) + m_ref[0, 3:4, :]).astype(BF16)
        o_ref[...] = jnp.zeros_like(o_ref)

    wg = wg_ref[...].astype(BF16)
    wu = wu_ref[...].astype(BF16)
    wd = wd_ref[...].astype(BF16)
    for sb in range(n_sub):
        rows = slice(sb * sub, (sb + 1) * sub)
        h = h_sc[rows, :]
        a = jnp.dot(h, wg, preferred_element_type=F32)
        u = jnp.dot(h, wu, preferred_element_type=F32)
        o_ref[0, rows, :] += jnp.dot((a * jax.nn.sigmoid(a) * u).astype(BF16), wd, preferred_element_type=F32)

    @pl.when(j == pl.num_programs(2) - 1)
    def _():
        for sb in range(n_sub):
            rows = slice(sb * sub, (sb + 1) * sub)
            z = DEEPNORM_ALPHA * x_ref[0, rows, :] + (1.0 + m_ref[0, 5:6, :]) * o_ref[0, rows, :]
            o_ref[0, rows, :] = _layer_norm(z, lg_ref[...], lb_ref[...])


def ffn_ln(x, mod, wg, wu, wd, lg, lb):
    B, T, D = x.shape
    F = wg.shape[1]
    tm = min(FFN_TM, T)
    tf = FFN_TF
    vec = pl.BlockSpec((1, D), lambda b, i, j: (0, 0))
    return pl.pallas_call(
        functools.partial(_ffn_kernel, sub=min(FFN_SUB, tm)),
        grid=(B, T // tm, F // tf),
        in_specs=[pl.BlockSpec((1, tm, D), lambda b, i, j: (b, i, 0)),
                  pl.BlockSpec((1, 6, D), lambda b, i, j: (b, 0, 0)),
                  pl.BlockSpec((D, tf), lambda b, i, j: (0, j)),
                  pl.BlockSpec((D, tf), lambda b, i, j: (0, j)),
                  pl.BlockSpec((tf, D), lambda b, i, j: (j, 0)),
                  vec, vec],
        out_specs=pl.BlockSpec((1, tm, D), lambda b, i, j: (b, i, 0)),
        out_shape=jax.ShapeDtypeStruct((B, T, D), F32),
        scratch_shapes=[pltpu.VMEM((tm, D), BF16)],
        compiler_params=pltpu.CompilerParams(dimension_semantics=("parallel", "parallel", "arbitrary"),
                                             vmem_limit_bytes=BIG_VMEM_LIMIT_BYTES),
        name="ffn_ln",
    )(x, mod, wg, wu, wd, lg, lb)


def _router_kernel(x_ref, m_ref, wr_ref, h_ref, lg_ref):
    h = x_ref[0] * (1.0 + m_ref[0, 4:5, :]) + m_ref[0, 3:4, :]
    h_hi = h.astype(BF16)
    h_lo = (h - h_hi.astype(F32)).astype(BF16)
    h_ref[0] = h_hi
    lg_ref[0] = (jnp.dot(h_hi, wr_ref[0], preferred_element_type=F32)
                 + jnp.dot(h_lo, wr_ref[0], preferred_element_type=F32)
                 + jnp.dot(h_hi, wr_ref[1], preferred_element_type=F32))


def moe_router(x, mod, wr_pad):
    B, T, D = x.shape
    tm = min(512, T)
    return pl.pallas_call(
        _router_kernel,
        grid=(B, T // tm),
        in_specs=[pl.BlockSpec((1, tm, D), lambda b, i: (b, i, 0)),
                  pl.BlockSpec((1, 6, D), lambda b, i: (b, 0, 0)),
                  pl.BlockSpec((2, D, LANES), lambda b, i: (0, 0, 0))],
        out_specs=[pl.BlockSpec((1, tm, D), lambda b, i: (b, i, 0)),
                   pl.BlockSpec((1, tm, LANES), lambda b, i: (b, i, 0))],
        out_shape=[jax.ShapeDtypeStruct((B, T, D), BF16),
                   jax.ShapeDtypeStruct((B, T, LANES), F32)],
        compiler_params=_params("parallel", "parallel"),
        name="moe_router",
    )(x, mod, wr_pad)


def _moe_kernel(te_ref, nv_ref, x_ref, wg_ref, wu_ref, wd_ref, o_ref, acc_ref, *, sub):
    i = pl.program_id(0)
    j = pl.program_id(1)
    nv = nv_ref[i]

    @pl.when(j == 0)
    def _():
        acc_ref[...] = jnp.zeros_like(acc_ref)

    n_sub = acc_ref.shape[0] // sub

    def swiglu_rows(rows, wg, wu, wd):
        h = x_ref[rows, :]
        a = jnp.dot(h, wg, preferred_element_type=F32)
        u = jnp.dot(h, wu, preferred_element_type=F32)
        acc_ref[rows, :] += jnp.dot((a * jax.nn.sigmoid(a) * u).astype(BF16), wd, preferred_element_type=F32)

    for n_real in range(1, n_sub + 1):
        @pl.when(nv == n_real)
        def _():
            wg = wg_ref[0].astype(BF16)
            wu = wu_ref[0].astype(BF16)
            wd = wd_ref[0].astype(BF16)
            for sb in range(n_real):
                swiglu_rows(slice(sb * sub, (sb + 1) * sub), wg, wu, wd)

    @pl.when(j == pl.num_programs(1) - 1)
    def _():
        o_ref[...] = acc_ref[...].astype(o_ref.dtype)


def moe_experts(tile_e, tile_nv, xg, wg, wu, wd, tm, sub):
    M, D = xg.shape
    F = wg.shape[2]
    tf = 512
    nj = F // tf
    n_tiles = M // tm

    def wj(i, j, nv):
        return jnp.where(nv[i] > 0, j, nj - 1)

    return pl.pallas_call(
        functools.partial(_moe_kernel, sub=sub),
        grid_spec=pltpu.PrefetchScalarGridSpec(
            num_scalar_prefetch=2,
            grid=(n_tiles, nj),
            in_specs=[pl.BlockSpec((tm, D), lambda i, j, te, nv: (i, 0)),
                      pl.BlockSpec((1, D, tf), lambda i, j, te, nv: (te[i], 0, wj(i, j, nv))),
                      pl.BlockSpec((1, D, tf), lambda i, j, te, nv: (te[i], 0, wj(i, j, nv))),
                      pl.BlockSpec((1, tf, D), lambda i, j, te, nv: (te[i], wj(i, j, nv), 0))],
            out_specs=pl.BlockSpec((tm, D), lambda i, j, te, nv: (i, 0)),
            scratch_shapes=[pltpu.VMEM((tm, D), F32)]),
        out_shape=jax.ShapeDtypeStruct((M, D), BF16),
        compiler_params=pltpu.CompilerParams(dimension_semantics=("parallel", "arbitrary"),
                                             vmem_limit_bytes=BIG_VMEM_LIMIT_BYTES),
        name="moe_experts",
    )(tile_e, tile_nv, xg, wg, wu, wd)


def _combine_kernel(y0_ref, y1_ref, cw_ref, x_ref, m_ref, lg_ref, lb_ref, o_ref):
    cw = cw_ref[0]
    y = y0_ref[0].astype(F32) * cw[:, 0:1] + y1_ref[0].astype(F32) * cw[:, 1:2]
    z = DEEPNORM_ALPHA * x_ref[0] + (1.0 + m_ref[0, 5:6, :]) * y
    o_ref[0] = _layer_norm(z, lg_ref[...], lb_ref[...])


def moe_combine_ln(y0, y1, cw, x, mod, lg, lb):
    B, T, D = x.shape
    tm = min(512, T)
    row = pl.BlockSpec((1, tm, D), lambda b, i: (b, i, 0))
    vec = pl.BlockSpec((1, D), lambda b, i: (0, 0))
    return pl.pallas_call(
        _combine_kernel,
        grid=(B, T // tm),
        in_specs=[row, row, pl.BlockSpec((1, tm, LANES), lambda b, i: (b, i, 0)), row,
                  pl.BlockSpec((1, 6, D), lambda b, i: (b, 0, 0)), vec, vec],
        out_specs=row,
        out_shape=jax.ShapeDtypeStruct((B, T, D), F32),
        compiler_params=_params("parallel", "parallel"),
        name="moe_combine_ln",
    )(y0, y1, cw, x, mod, lg, lb)


MOE_TM = 1024
MOE_SUB = 256


def moe_layer(x, mod, w_router, wg, wu, wd, lg, lb):
    B, T, D = x.shape
    N = B * T
    A = N * TOP_K
    tm = MOE_TM
    wr_pad = jnp.zeros((D, LANES), F32).at[:, :N_EXPERTS].set(w_router)
    wr_hi = wr_pad.astype(BF16)
    wr_lo = (wr_pad - wr_hi.astype(F32)).astype(BF16)
    h, logits = moe_router(x, mod, jnp.stack([wr_hi, wr_lo]))
    logits = logits.reshape(N, LANES)[:, :N_EXPERTS]
    top_val, top_idx = lax.top_k(logits, TOP_K)
    comb = jax.nn.softmax(top_val, axis=-1)
    flat_e = top_idx.reshape(-1).astype(jnp.int32)
    onehot = (flat_e[:, None] == jnp.arange(N_EXPERTS, dtype=jnp.int32)[None, :]).astype(jnp.int32)
    csum = jnp.cumsum(onehot, axis=0)
    counts = csum[-1]
    padded = (counts + tm - 1) // tm * tm
    pad_end = jnp.cumsum(padded)
    pad_start = pad_end - padded
    slot = jnp.sum(onehot * (csum + pad_start[None, :]), axis=1) - 1
    n_tiles = -(-A // tm) + N_EXPERTS
    tile_start = jnp.arange(n_tiles, dtype=jnp.int32) * tm
    tile_e = jnp.minimum(jnp.searchsorted(pad_end, tile_start, side='right'), N_EXPERTS - 1).astype(jnp.int32)
    valid = jnp.clip(pad_start[tile_e] + counts[tile_e] - tile_start, 0, tm)
    tile_nv = ((valid + MOE_SUB - 1) // MOE_SUB).astype(jnp.int32)
    order = jnp.argsort(flat_e)
    seg_start = jnp.cumsum(counts) - counts
    slot_id = jnp.arange(n_tiles * tm, dtype=jnp.int32)
    within = slot_id - jnp.repeat(tile_start, tm)
    src = jnp.repeat(seg_start[tile_e] + tile_start - pad_start[tile_e], tm) + within
    slot_tok = jnp.where(within < jnp.repeat(valid, tm),
                         order[jnp.clip(src, 0, A - 1)].astype(jnp.int32) // TOP_K, slot_id % N)
    n_used = pad_end[-1] // tm
    tile_e = jnp.where(jnp.arange(n_tiles) < n_used, tile_e, tile_e[jnp.maximum(n_used - 1, 0)])
    xg = h.reshape(N, D)[slot_tok]
    y = moe_experts(tile_e, tile_nv, xg, wg, wu, wd, tm, MOE_SUB)
    slot_of = slot.reshape(N, TOP_K)
    y0 = y[slot_of[:, 0]].reshape(B, T, D)
    y1 = y[slot_of[:, 1]].reshape(B, T, D)
    cw = jnp.zeros((N, LANES), F32).at[:, :TOP_K].set(comb).reshape(B, T, LANES)
    return moe_combine_ln(y0, y1, cw, x, mod, lg, lb)


def _rope_tables(T):
    half = ROPE_DIMS // 2
    inv = ROPE_THETA ** (-jnp.arange(half, dtype=F32) * 2.0 / ROPE_DIMS)
    ang = jnp.arange(T).astype(F32)[:, None] * inv[None, :]
    cos, sin = jnp.cos(ang), jnp.sin(ang)
    z = jnp.zeros((T, LANES - ROPE_DIMS), F32)
    zh = jnp.zeros((T, half), F32)
    rc = jnp.concatenate([cos, cos, jnp.ones((T, LANES - ROPE_DIMS), F32)], axis=1)
    ra = jnp.concatenate([-sin, zh, z], axis=1)
    rb = jnp.concatenate([zh, sin, z], axis=1)
    return rc, ra, rb


def _overlap_matrix_t(NC, NB):
    nbp = -(-NB // 16) * 16
    c0 = np.arange(NC)[None, :] * CMP_STRIDE
    b0 = np.arange(nbp)[:, None] * SLC_BLOCK
    ov = (c0 < b0 + SLC_BLOCK) & (c0 + CMP_BLOCK > b0) & (np.arange(nbp)[:, None] < NB)
    return jnp.asarray(ov.astype(np.float32), dtype=BF16)


def _reorder_w_in(w):
    D = w.shape[0]
    o_ng = 1024 + 6 * 256
    o_gq = o_ng + 24
    o_gk = o_gq + 512
    o_gv = o_gk + 512
    o_ga = o_gv + 1024
    o_gg = o_ga + GLA_GATE_RANK
    misc = jnp.concatenate([w[:, o_ng:o_gq], w[:, o_ga:o_gg],
                            jnp.zeros((D, LANES - 24 - GLA_GATE_RANK), w.dtype)], axis=1)
    kv = 1024
    w_a = [w[:, 0:1024], w[:, kv + 512:kv + 768], w[:, kv + 1024:kv + 1280],
           w[:, kv + 768:kv + 1024], w[:, kv + 1280:kv + 1536]]
    w_b = [w[:, o_gv:o_ga], w[:, o_gg:o_gg + 1024], w[:, o_gq:o_gk], w[:, o_gk:o_gv],
           w[:, kv:kv + 512], misc]
    return jnp.concatenate(w_a, axis=1).astype(BF16), jnp.concatenate(w_b, axis=1).astype(BF16)


def hybrid_mixer_ln(x, mod, w_in_r, cmp_pe, cmp_w1, cmp_w2, wa_pad, ba, nw, w_out, lg, lb, tables, overlap):
    w_a, w_b = w_in_r
    qr, qn, ks, vs, kw, vw = nsa_proj(x, mod, w_a, *tables)
    proj = in_proj(x, mod, w_b)
    kc, vct = nsa_compress(proj, cmp_pe, cmp_w1, cmp_w2)
    nsa_o = nsa_attention(qr, qn, proj, kc, vct, ks, vs, kw, vw, overlap)
    gla_o = gla_mixer(proj, wa_pad, ba, nw)
    return out_proj_ln(nsa_o, gla_o, w_out, x, mod, lg, lb)


def kernel(x, c, w_ada, b_ada, w_in, cmp_pos_k, cmp_w1_k, cmp_w2_k, cmp_pos_v, cmp_w1_v, cmp_w2_v, gla_w_a2, gla_b_a, gla_norm_w, w_out, ln_mix_g, ln_mix_b, ln_ffn_g, ln_ffn_b, ffn_w_gate, ffn_w_up, ffn_w_down, moe_router, moe_w_gate, moe_w_up, moe_w_down):
    B, T, D = x.shape
    L = w_ada.shape[0]
    c_pad = jnp.zeros((8, D), F32).at[:B].set(c)
    mod_all = ada_mod(c_pad, w_ada, b_ada.reshape(L, 1, 6 * D))[:, :B].reshape(L, B, 6, D)
    tables = _rope_tables(T)
    overlap = _overlap_matrix_t(T // CMP_STRIDE, T // SLC_BLOCK)
    for layer in range(L):
        mod = mod_all[layer]
        wa_pad = jnp.zeros((LANES, GLA_HEADS * GLA_DK), F32).at[MISC_GA:MISC_GA + GLA_GATE_RANK].set(
            gla_w_a2[layer]).astype(BF16)
        x = hybrid_mixer_ln(
            x, mod, _reorder_w_in(w_in[layer]),
            jnp.stack([cmp_pos_k[layer], cmp_pos_v[layer]]),
            jnp.stack([cmp_w1_k[layer], cmp_w1_v[layer]]).astype(BF16),
            jnp.stack([cmp_w2_k[layer], cmp_w2_v[layer]]).astype(BF16),
            wa_pad, gla_b_a[layer].reshape(1, -1), gla_norm_w[layer].reshape(1, -1),
            w_out[layer].astype(BF16), ln_mix_g[layer].reshape(1, D), ln_mix_b[layer].reshape(1, D),
            tables, overlap)
        lg = ln_ffn_g[layer].reshape(1, D)
        lb = ln_ffn_b[layer].reshape(1, D)
        i = layer // 2
        if layer % 2 == 0:
            x = ffn_ln(x, mod, ffn_w_gate[i], ffn_w_up[i], ffn_w_down[i], lg, lb)
        else:
            x = moe_layer(x, mod, moe_router[i], moe_w_gate[i], moe_w_up[i], moe_w_down[i], lg, lb)
    return x
```

```python
import functools

import numpy as np
import jax
import jax.numpy as jnp
from jax import lax
from jax.experimental import pallas as pl
from jax.experimental.pallas import tpu as pltpu

F32 = jnp.float32
BF16 = jnp.bfloat16

D_MODEL = 2048
DEPTH = 2
HEAD_DIM = 128
NSA_HEADS = 8
NSA_KV_HEADS = 2
NSA_GROUP = 4
CMP_BLOCK = 32
CMP_STRIDE = 16
CMP_HIDDEN = 256
SLC_BLOCK = 64
SLC_TOPN = 16
WINDOW = 512
FORCE_SCORE = 1e9
GLA_DV = 256
GLA_HEADS = 4
GLA_DK = 128
GLA_GATE_RANK = 16
GLA_GATE_NORM = 16.0
GLA_CHUNK = 64
ROPE_THETA = 500000.0
ROPE_DIMS = 32
N_EXPERTS = 8
TOP_K = 2
LN_EPS = 1e-5
NORM_EPS = 1e-6
DEEPNORM_ALPHA = (2 * DEPTH) ** 0.25
NEG_BIG = -1e30
MASK_BIG = 2.0 ** 100
V_AUG_ROWS = HEAD_DIM + 16
NSA_HEADS_PER_STEP = 2
NSA_TK = 512
LOG2_E = 1.4426950408889634

VMEM_LIMIT_BYTES = 56 * 1024 * 1024
BIG_VMEM_LIMIT_BYTES = 60 * 1024 * 1024
LANES = 128

A_NQ = 0
A_KS = 1024
A_KW = 1280
A_VS = 1536
A_VW = 1792
A_W = 2048
C_GV = 0
C_GG = 1024
C_GQ = 2048
C_GK = 2560
C_KC = 3072
C_VC = 3328
C_MISC = 3584
PROJ_W = 3712
MISC_GA = 24

NT_DIMS = (((1,), (1,)), ((), ()))


def _params(*sem):
    return pltpu.CompilerParams(dimension_semantics=sem, vmem_limit_bytes=VMEM_LIMIT_BYTES)


def _layer_norm(z, g, b):
    mu = jnp.mean(z, axis=-1, keepdims=True)
    zc = z - mu
    var = jnp.mean(zc * zc, axis=-1, keepdims=True)
    return zc * lax.rsqrt(var + LN_EPS) * g + b


def _ada_kernel(c_ref, w_ref, b_ref, o_ref):
    c = c_ref[...]
    cond = c * jax.nn.sigmoid(c)
    o_ref[0] = jnp.dot(cond.astype(BF16), w_ref[0].astype(BF16),
                       preferred_element_type=F32) + b_ref[0]


def ada_mod(c_pad, w_ada, b_ada):
    L, D, N = w_ada.shape
    tn = 1024
    return pl.pallas_call(
        _ada_kernel,
        grid=(L, N // tn),
        in_specs=[pl.BlockSpec((8, D), lambda l, j: (0, 0)),
                  pl.BlockSpec((1, D, tn), lambda l, j: (l, 0, j)),
                  pl.BlockSpec((1, 1, tn), lambda l, j: (l, 0, j))],
        out_specs=pl.BlockSpec((1, 8, tn), lambda l, j: (l, 0, j)),
        out_shape=jax.ShapeDtypeStruct((L, 8, N), F32),
        compiler_params=_params("parallel", "parallel"),
        name="ada_mod",
    )(c_pad, w_ada, b_ada)


def _inproj_kernel(x_ref, m_ref, w_ref, o_ref):
    h = x_ref[0] * (1.0 + m_ref[0, 1:2, :]) + m_ref[0, 0:1, :]
    o_ref[0] = jnp.dot(h.astype(BF16), w_ref[...], preferred_element_type=F32)


def in_proj(x, mod, w):
    B, T, D = x.shape
    N = w.shape[1]
    tm = min(512, T)
    return pl.pallas_call(
        _inproj_kernel,
        grid=(B, T // tm),
        in_specs=[pl.BlockSpec((1, tm, D), lambda b, i: (b, i, 0)),
                  pl.BlockSpec((1, 6, D), lambda b, i: (b, 0, 0)),
                  pl.BlockSpec((D, N), lambda b, i: (0, 0))],
        out_specs=pl.BlockSpec((1, tm, N), lambda b, i: (b, i, 0)),
        out_shape=jax.ShapeDtypeStruct((B, T, N), F32),
        compiler_params=_params("parallel", "parallel"),
        name="in_proj",
    )(x, mod, w)


def _nsa_proj_kernel(x_ref, m_ref, w_ref, rc_ref, ra_ref, rb_ref,
                     qr_ref, qn_ref, ks_ref, vs_ref, kw_ref, vw_ref, *, sub):
    tm = x_ref.shape[1]
    scale = HEAD_DIM ** -0.5 * LOG2_E
    for r in range(tm // sub):
        rows = slice(r * sub, (r + 1) * sub)
        rc = rc_ref[rows, :]
        ra = ra_ref[rows, :]
        rb = rb_ref[rows, :]

        def rope(xh):
            return (xh * rc + pltpu.roll(xh, LANES - ROPE_DIMS // 2, 1) * ra
                    + pltpu.roll(xh, ROPE_DIMS // 2, 1) * rb)

        h = x_ref[0, rows, :] * (1.0 + m_ref[0, 1:2, :]) + m_ref[0, 0:1, :]
        p = jnp.dot(h.astype(BF16), w_ref[...], preferred_element_type=F32)
        for hq in range(NSA_HEADS):
            xh = p[:, A_NQ + hq * HEAD_DIM:A_NQ + (hq + 1) * HEAD_DIM]
            qr_ref[0, rows, hq * HEAD_DIM:(hq + 1) * HEAD_DIM] = (rope(xh) * scale).astype(BF16)
            qn_ref[0, rows, hq * HEAD_DIM:(hq + 1) * HEAD_DIM] = (xh * scale).astype(BF16)
        key_blk = (pl.program_id(1) * tm + r * sub + lax.broadcasted_iota(jnp.int32, (sub, LANES), 0)) // SLC_BLOCK
        blk_onehot = jnp.where(key_blk == lax.broadcasted_iota(jnp.int32, (sub, LANES), 1), 1.0, 0.0).astype(BF16)
        for hk in range(NSA_KV_HEADS):
            ks_ref[0, hk, rows, 0:HEAD_DIM] = rope(p[:, A_KS + hk * HEAD_DIM:A_KS + (hk + 1) * HEAD_DIM]).astype(BF16)
            ks_ref[0, hk, rows, HEAD_DIM:] = blk_onehot
            kw_ref[0, hk, rows, :] = rope(p[:, A_KW + hk * HEAD_DIM:A_KW + (hk + 1) * HEAD_DIM]).astype(BF16)
            vs_ref[0, hk, 0:HEAD_DIM, rows] = p[:, A_VS + hk * HEAD_DIM:A_VS + (hk + 1) * HEAD_DIM].T.astype(BF16)
            vs_ref[0, hk, HEAD_DIM:, rows] = jnp.ones((V_AUG_ROWS - HEAD_DIM, sub), BF16)
            vw_ref[0, hk, :, rows] = p[:, A_VW + hk * HEAD_DIM:A_VW + (hk + 1) * HEAD_DIM].T.astype(BF16)


def nsa_proj(x, mod, w, rc, ra, rb):
    B, T, D = x.shape
    tm = min(512, T)
    assert T // SLC_BLOCK <= LANES
    qw = NSA_HEADS * HEAD_DIM
    qspec = pl.BlockSpec((1, tm, qw), lambda b, i: (b, i, 0))
    tspec = pl.BlockSpec((tm, LANES), lambda b, i: (i, 0))
    return pl.pallas_call(
        functools.partial(_nsa_proj_kernel, sub=min(128, tm)),
        grid=(B, T // tm),
        in_specs=[pl.BlockSpec((1, tm, D), lambda b, i: (b, i, 0)),
                  pl.BlockSpec((1, 6, D), lambda b, i: (b, 0, 0)),
                  pl.BlockSpec((D, A_W), lambda b, i: (0, 0)),
                  tspec, tspec, tspec],
        out_specs=[qspec, qspec,
                   pl.BlockSpec((1, NSA_KV_HEADS, tm, HEAD_DIM + LANES), lambda b, i: (b, 0, i, 0)),
                   pl.BlockSpec((1, NSA_KV_HEADS, V_AUG_ROWS, tm), lambda b, i: (b, 0, 0, i)),
                   pl.BlockSpec((1, NSA_KV_HEADS, tm, HEAD_DIM), lambda b, i: (b, 0, i, 0)),
                   pl.BlockSpec((1, NSA_KV_HEADS, HEAD_DIM, tm), lambda b, i: (b, 0, 0, i))],
        out_shape=[jax.ShapeDtypeStruct((B, T, qw), BF16),
                   jax.ShapeDtypeStruct((B, T, qw), BF16),
                   jax.ShapeDtypeStruct((B, NSA_KV_HEADS, T, HEAD_DIM + LANES), BF16),
                   jax.ShapeDtypeStruct((B, NSA_KV_HEADS, V_AUG_ROWS, T), BF16),
                   jax.ShapeDtypeStruct((B, NSA_KV_HEADS, T, HEAD_DIM), BF16),
                   jax.ShapeDtypeStruct((B, NSA_KV_HEADS, HEAD_DIM, T), BF16)],
        compiler_params=_params("parallel", "parallel"),
        name="nsa_proj",
    )(x, mod, w, rc, ra, rb)


def _compress_kernel(a_ref, pe_ref, w1_ref, w2_ref, o_ref, ot_ref, *, n_half):
    half = CMP_STRIDE

    def part(l0):
        acc = jnp.zeros((n_half, CMP_HIDDEN), F32)
        for l in range(half):
            rows = a_ref[0, pl.ds(l, n_half, stride=half), :] + pe_ref[0, l0 + l:l0 + l + 1, :]
            acc += jnp.dot(rows.astype(BF16), w1_ref[0, (l0 + l) * HEAD_DIM:(l0 + l + 1) * HEAD_DIM, :],
                           preferred_element_type=F32)
        return acc

    first = part(0)
    second = part(half)
    hid = first + pltpu.roll(second, n_half - 1, 0)
    row = lax.broadcasted_iota(jnp.int32, (n_half, 1), 0)
    hid = jnp.where(row < n_half - 1, hid, 0.0)
    act = jax.nn.gelu(hid)
    out = jnp.dot(act.astype(BF16), w2_ref[0], preferred_element_type=F32)
    o_ref[0, 0, 0] = out.astype(BF16)
    ot_ref[0, 0, 0] = out.T.astype(BF16)


def nsa_compress(proj, pe, w1, w2):
    B, T, _ = proj.shape
    n_half = T // CMP_STRIDE
    return pl.pallas_call(
        functools.partial(_compress_kernel, n_half=n_half),
        grid=(B, 2, NSA_KV_HEADS),
        in_specs=[pl.BlockSpec((1, T, HEAD_DIM), lambda b, s, h: (b, 0, C_KC // HEAD_DIM + s * NSA_KV_HEADS + h)),
                  pl.BlockSpec((1, CMP_BLOCK, HEAD_DIM), lambda b, s, h: (s, 0, 0)),
                  pl.BlockSpec((1, CMP_BLOCK * HEAD_DIM, CMP_HIDDEN), lambda b, s, h: (s, 0, 0)),
                  pl.BlockSpec((1, CMP_HIDDEN, HEAD_DIM), lambda b, s, h: (s, 0, 0))],
        out_specs=[pl.BlockSpec((1, 1, 1, n_half, HEAD_DIM), lambda b, s, h: (b, s, h, 0, 0)),
                   pl.BlockSpec((1, 1, 1, HEAD_DIM, n_half), lambda b, s, h: (b, s, h, 0, 0))],
        out_shape=[jax.ShapeDtypeStruct((B, 2, NSA_KV_HEADS, n_half, HEAD_DIM), BF16),
                   jax.ShapeDtypeStruct((B, 2, NSA_KV_HEADS, HEAD_DIM, n_half), BF16)],
        compiler_params=_params("parallel", "parallel", "parallel"),
        name="nsa_compress",
    )(proj, pe, w1, w2)


def _nsa_kernel(qr_ref, qn_ref, g_ref, kc_ref, vct_ref, ks_ref, vst_ref, kw_ref, vwt_ref, ovt_ref, o_ref,
                s_buf, smax_buf, m_buf, acc_buf, *, TQ, TK, WK, NC, NB, NH):
    G = NSA_GROUP
    R = G * TQ
    NBP = ovt_ref.shape[0]
    i = pl.program_id(2)
    t0 = i * TQ
    m_floor = 0.5 * NEG_BIG

    def stack(ref, h):
        x = ref[0]
        return jnp.concatenate([x[:, (h * G + g) * HEAD_DIM:(h * G + g + 1) * HEAD_DIM] for g in range(G)], axis=0)

    def tile_g(x):
        return jnp.concatenate([x] * G, axis=1)

    tT = t0 + lax.broadcasted_iota(jnp.int32, (1, TQ), 1)
    cur = tT // SLC_BLOCK
    n_sel = min(SLC_TOPN, NB)

    HS = range(NH)
    qr = [stack(qr_ref, h) for h in HS]
    qn = [stack(qn_ref, h) for h in HS]

    ws = pl.multiple_of(jnp.maximum(t0 + TQ - WK, 0), LANES)
    s_c = [lax.dot_general(kc_ref[0, 0, h], qn[h], NT_DIMS, preferred_element_type=F32) for h in HS]
    s_w = [lax.dot_general(kw_ref[0, h, pl.ds(ws, WK), :], qr[h], NT_DIMS, preferred_element_type=F32)
           for h in HS]
    cend = lax.broadcasted_iota(jnp.int32, (NC, 1), 0) * CMP_STRIDE + (CMP_BLOCK - 1)
    bias_c = tile_g(jnp.where(cend <= tT, 0.0, NEG_BIG))
    dist = tT - (ws + lax.broadcasted_iota(jnp.int32, (WK, 1), 0))
    bias_w = tile_g(jnp.where(dist >= 0, jnp.where(dist < WINDOW, 0.0, NEG_BIG), NEG_BIG))

    p_cb, e_wb, l_w = [], [], []
    for h in HS:
        sc = s_c[h] + bias_c
        m_c = jnp.maximum(jnp.max(sc, axis=0, keepdims=True), m_floor)
        e_c = jnp.exp2(sc - m_c)
        den = jnp.sum(e_c, axis=0, keepdims=True)
        p_cb.append((e_c * jnp.where(den > 0.0, 1.0 / den, 0.0)).astype(BF16))
        sw = s_w[h] + bias_w
        e_w = jnp.exp2(sw - jnp.max(sw, axis=0, keepdims=True))
        l_w.append(jnp.sum(e_w, axis=0, keepdims=True))
        e_wb.append(e_w.astype(BF16))

    o_c = [jnp.dot(vct_ref[0, 0, h], p_cb[h], preferred_element_type=F32) for h in HS]
    imp4 = [jnp.dot(ovt_ref[...], p_cb[h], preferred_element_type=F32) for h in HS]
    o_w = [jnp.dot(vwt_ref[0, h, :, pl.ds(ws, WK)], e_wb[h], preferred_element_type=F32) * (1.0 / l_w[h])
           for h in HS]

    jj = lax.broadcasted_iota(jnp.int32, (NBP, TQ), 0)
    forced = jnp.where(jj == 0, 1, jnp.where(jj == cur, 1, jnp.where(jj == cur - 1, 1, 0)))
    val = []
    for h in HS:
        impT = imp4[h][:, 0:TQ]
        for g in range(1, G):
            impT = impT + imp4[h][:, g * TQ:(g + 1) * TQ]
        val.append(jnp.where(jj > cur, -jnp.inf, jnp.where(forced > 0, FORCE_SCORE, impT)))
    rank = [jnp.zeros((NBP, TQ), jnp.int32) for h in HS]
    for j2 in range(NB):
        later = jnp.where(jj > j2, 1, 0)
        for h in HS:
            row = val[h][j2:j2 + 1, :]
            rank[h] = rank[h] + jnp.where(row > val[h], 1, jnp.where(row == val[h], later, 0))
    fronts = []
    for h in HS:
        sel_neg = jnp.where(jj > cur, -MASK_BIG, jnp.where(rank[h] < n_sel, 0.0, -MASK_BIG))
        if NBP < LANES:
            sel_neg = jnp.concatenate([sel_neg, jnp.zeros((LANES - NBP, TQ), F32)], axis=0)
        sel_q = sel_neg.T.astype(BF16)
        q_aug = jnp.concatenate([qr[h], jnp.concatenate([sel_q] * G, axis=0)], axis=1)
        fronts.append((o_c[h], o_w[h], q_aug))

    def sel_scores(h, kt, slot):
        k0 = pl.multiple_of(kt * TK, TK)
        k = ks_ref[0, h, pl.ds(k0, TK), :]
        s = lax.dot_general(k, fronts[h][2], NT_DIMS, preferred_element_type=F32)
        s_buf[h, slot] = s
        smax_buf[h, slot] = jnp.max(s, axis=0, keepdims=True)

    def sel_accumulate(h, kt, slot):
        k0 = pl.multiple_of(kt * TK, TK)
        vt = vst_ref[0, h, :, pl.ds(k0, TK)]
        m = m_buf[h]
        m_new = jnp.maximum(m, smax_buf[h, slot])
        p = jnp.exp2(s_buf[h, slot] - m_new).astype(BF16)
        acc_buf[h] = jnp.exp2(m - m_new) * acc_buf[h] + jnp.dot(vt, p, preferred_element_type=F32)
        m_buf[h] = m_new

    n_kt = (t0 + TQ + TK - 1) // TK
    n_loop = n_kt - 1
    for h in HS:
        m_buf[h] = jnp.full((1, R), m_floor, F32)
        acc_buf[h] = jnp.zeros((V_AUG_ROWS, R), F32)
        sel_scores(h, 0, 0)

    def pair_body(k, carry):
        for h in HS:
            sel_scores(h, 2 * k + 1, 1)
            sel_accumulate(h, 2 * k, 0)

        @pl.when(2 * k + 1 < n_loop)
        def _():
            for h in HS:
                sel_scores(h, 2 * k + 2, 0)
                sel_accumulate(h, 2 * k + 1, 1)
        return carry

    lax.fori_loop(0, (n_loop + 1) // 2, pair_body, 0)
    krow = n_loop * TK + lax.broadcasted_iota(jnp.int32, (TK, 1), 0)
    causal = tile_g(jnp.where(krow <= tT, 0.0, NEG_BIG))
    k_last = pl.multiple_of(n_loop * TK, TK)
    last_slot = n_loop % 2
    s_fin = [s_buf[h, last_slot] + causal for h in HS]
    m_fin = [jnp.maximum(m_buf[h], jnp.max(s_fin[h], axis=0, keepdims=True)) for h in HS]
    p_fin = [jnp.exp2(s_fin[h] - m_fin[h]).astype(BF16) for h in HS]
    acc_fin = [jnp.exp2(m_buf[h] - m_fin[h]) * acc_buf[h]
               + jnp.dot(vst_ref[0, h, :, pl.ds(k_last, TK)], p_fin[h], preferred_element_type=F32) for h in HS]
    gate_all = jax.nn.sigmoid(g_ref[0]).T
    assert NH == NSA_KV_HEADS
    gates = [gate_all[3 * G * h:3 * G * (h + 1), :] for h in HS]
    for h in HS:
        o_s = acc_fin[h][0:HEAD_DIM] * (1.0 / acc_fin[h][HEAD_DIM:HEAD_DIM + 1])
        o_c, o_w, _ = fronts[h]
        gate = gates[h]
        for g in range(G):
            cols = slice(g * TQ, (g + 1) * TQ)
            o = (gate[3 * g:3 * g + 1, :] * o_c[:, cols] + gate[3 * g + 1:3 * g + 2, :] * o_s[:, cols]
                 + gate[3 * g + 2:3 * g + 3, :] * o_w[:, cols])
            o_ref[0, :, (h * G + g) * HEAD_DIM:(h * G + g + 1) * HEAD_DIM] = o.T.astype(BF16)


def nsa_attention(qr, qn, gates, kc, vct, ks, vst, kw, vwt, overlap_t):
    B, T, _ = qr.shape
    TQ = 128
    TK = min(NSA_TK, T)
    WK = min(WINDOW + TQ, T)
    NC = kc.shape[3]
    NB = T // SLC_BLOCK
    NH = NSA_HEADS_PER_STEP
    gw = NH * NSA_GROUP * HEAD_DIM
    qspec = pl.BlockSpec((1, TQ, gw), lambda b, h, i: (b, i, h))
    kspec = pl.BlockSpec((1, NH, T, HEAD_DIM), lambda b, h, i: (b, h, 0, 0))
    vtspec = pl.BlockSpec((1, NH, HEAD_DIM, T), lambda b, h, i: (b, h, 0, 0))
    return pl.pallas_call(
        functools.partial(_nsa_kernel, TQ=TQ, TK=TK, WK=WK, NC=NC, NB=NB, NH=NH),
        grid=(B, NSA_KV_HEADS // NH, T // TQ),
        in_specs=[qspec, qspec,
                  pl.BlockSpec((1, TQ, LANES), lambda b, h, i: (b, i, C_MISC // LANES)),
                  pl.BlockSpec((1, 1, NH, NC, HEAD_DIM), lambda b, h, i: (b, 0, h, 0, 0)),
                  pl.BlockSpec((1, 1, NH, HEAD_DIM, NC), lambda b, h, i: (b, 1, h, 0, 0)),
                  pl.BlockSpec((1, NH, T, HEAD_DIM + LANES), lambda b, h, i: (b, h, 0, 0)),
                  pl.BlockSpec((1, NH, V_AUG_ROWS, T), lambda b, h, i: (b, h, 0, 0)),
                  kspec, vtspec,
                  pl.BlockSpec(overlap_t.shape, lambda b, h, i: (0, 0))],
        out_specs=pl.BlockSpec((1, TQ, gw), lambda b, h, i: (b, i, h)),
        out_shape=jax.ShapeDtypeStruct((B, T, NSA_HEADS * HEAD_DIM), BF16),
        scratch_shapes=[pltpu.VMEM((NH, 2, TK, NSA_GROUP * TQ), F32),
                        pltpu.VMEM((NH, 2, 1, NSA_GROUP * TQ), F32),
                        pltpu.VMEM((NH, 1, NSA_GROUP * TQ), F32),
                        pltpu.VMEM((NH, V_AUG_ROWS, NSA_GROUP * TQ), F32)],
        compiler_params=_params("parallel", "parallel", "arbitrary"),
        name="nsa_attention",
    )(qr, qn, gates, kc, vct, ks, vst, kw, vwt, overlap_t)


def _gla_kernel(q_ref, k_ref, v_ref, gg_ref, misc_ref, wa_ref, ba_ref, nw_ref, o_ref, s_ref, la_ref, *, TC):
    C = GLA_CHUNK

    @pl.when(pl.program_id(1) == 0)
    def _():
        s_ref[...] = jnp.zeros_like(s_ref)

    z = jnp.dot(misc_ref[0].astype(BF16), wa_ref[...], preferred_element_type=F32) + ba_ref[...]
    la_ref[...] = (jnp.minimum(z, 0.0) - jnp.log1p(jnp.exp(-jnp.abs(z)))) * (1.0 / GLA_GATE_NORM)
    ri = lax.broadcasted_iota(jnp.int32, (C, C), 0)
    ci = lax.broadcasted_iota(jnp.int32, (C, C), 1)
    causal = ri >= ci
    tri = jnp.where(causal, 1.0, 0.0)
    nw = nw_ref[...]

    n_c = TC // C
    pairs = [(c, h) for c in range(n_c) for h in range(GLA_HEADS)]
    rows = [slice(c * C, (c + 1) * C) for c in range(n_c)]
    b_all = [jnp.dot(tri, la_ref[rows[c], :], preferred_element_type=F32, precision=lax.Precision.HIGHEST)
             for c in range(n_c)]
    qd, kv, decay, intra = {}, {}, {}, {}
    for c, h in pairs:
        b = b_all[c][:, h * GLA_DK:(h + 1) * GLA_DK]
        bl = b[C - 1:C, :]
        q = q_ref[0, rows[c], h * GLA_DK:(h + 1) * GLA_DK] * (GLA_DK ** -0.5)
        k = k_ref[0, rows[c], h * GLA_DK:(h + 1) * GLA_DK]
        v = v_ref[0, rows[c], h * GLA_DV:(h + 1) * GLA_DV].astype(BF16)
        qd[c, h] = (q * jnp.exp(b)).astype(BF16)
        ki = (k * jnp.exp(-b)).astype(BF16)
        kst = (k * jnp.exp(bl - b)).T.astype(BF16)
        decay[c, h] = jnp.exp(b.T[:, C - 1:C])
        a = lax.dot_general(qd[c, h], ki, NT_DIMS, preferred_element_type=F32)
        a = jnp.where(causal, a, 0.0).astype(BF16)
        intra[c, h] = jnp.dot(a, v, preferred_element_type=F32)
        kv[c, h] = jnp.dot(kst, v, preferred_element_type=F32)
    state = [s_ref[h] for h in range(GLA_HEADS)]
    for c, h in pairs:
        o = intra[c, h] + jnp.dot(qd[c, h], state[h].astype(BF16), preferred_element_type=F32)
        state[h] = state[h] * decay[c, h] + kv[c, h]
        o = o * lax.rsqrt(jnp.mean(o * o, axis=-1, keepdims=True) + NORM_EPS) * nw
        gg = gg_ref[0, rows[c], h * GLA_DV:(h + 1) * GLA_DV]
        o_ref[0, rows[c], h * GLA_DV:(h + 1) * GLA_DV] = (o * (gg * jax.nn.sigmoid(gg))).astype(BF16)
    for h in range(GLA_HEADS):
        s_ref[h] = state[h]


def gla_mixer(proj, wa_pad, ba, nw):
    B, T, _ = proj.shape
    TC = min(512, T)
    qk_w = GLA_HEADS * GLA_DK
    v_w = GLA_HEADS * GLA_DV
    return pl.pallas_call(
        functools.partial(_gla_kernel, TC=TC),
        grid=(B, T // TC),
        in_specs=[pl.BlockSpec((1, TC, qk_w), lambda b, i: (b, i, C_GQ // qk_w)),
                  pl.BlockSpec((1, TC, qk_w), lambda b, i: (b, i, C_GK // qk_w)),
                  pl.BlockSpec((1, TC, v_w), lambda b, i: (b, i, C_GV // v_w)),
                  pl.BlockSpec((1, TC, v_w), lambda b, i: (b, i, C_GG // v_w)),
                  pl.BlockSpec((1, TC, LANES), lambda b, i: (b, i, C_MISC // LANES)),
                  pl.BlockSpec((LANES, qk_w), lambda b, i: (0, 0)),
                  pl.BlockSpec((1, qk_w), lambda b, i: (0, 0)),
                  pl.BlockSpec((1, GLA_DV), lambda b, i: (0, 0))],
        out_specs=pl.BlockSpec((1, TC, v_w), lambda b, i: (b, i, 0)),
        out_shape=jax.ShapeDtypeStruct((B, T, v_w), BF16),
        scratch_shapes=[pltpu.VMEM((GLA_HEADS, GLA_DK, GLA_DV), F32),
                        pltpu.VMEM((TC, qk_w), F32)],
        compiler_params=_params("parallel", "arbitrary"),
        name="gla_mixer",
    )(proj, proj, proj, proj, proj, wa_pad, ba, nw)


def _outproj_kernel(n_ref, g_ref, w_ref, x_ref, m_ref, lg_ref, lb_ref, o_ref, *, half):
    sub = 128
    for r in range(o_ref.shape[1] // sub):
        rows = slice(r * sub, (r + 1) * sub)
        y = (jnp.dot(n_ref[0, rows, :], w_ref[0:half, :], preferred_element_type=F32)
             + jnp.dot(g_ref[0, rows, :], w_ref[half:, :], preferred_element_type=F32))
        z = DEEPNORM_ALPHA * x_ref[0, rows, :] + (1.0 + m_ref[0, 2:3, :]) * y
        o_ref[0, rows, :] = _layer_norm(z, lg_ref[...], lb_ref[...])


def out_proj_ln(nsa_o, gla_o, w, x, mod, lg, lb):
    B, T, D = x.shape
    half = nsa_o.shape[-1]
    tm = min(512, T)
    vec = pl.BlockSpec((1, D), lambda b, i: (0, 0))
    return pl.pallas_call(
        functools.partial(_outproj_kernel, half=half),
        grid=(B, T // tm),
        in_specs=[pl.BlockSpec((1, tm, half), lambda b, i: (b, i, 0)),
                  pl.BlockSpec((1, tm, gla_o.shape[-1]), lambda b, i: (b, i, 0)),
                  pl.BlockSpec(w.shape, lambda b, i: (0, 0)),
                  pl.BlockSpec((1, tm, D), lambda b, i: (b, i, 0)),
                  pl.BlockSpec((1, 6, D), lambda b, i: (b, 0, 0)),
                  vec, vec],
        out_specs=pl.BlockSpec((1, tm, D), lambda b, i: (b, i, 0)),
        out_shape=jax.ShapeDtypeStruct((B, T, D), F32),
        compiler_params=_params("parallel", "parallel"),
        name="out_proj_ln",
    )(nsa_o, gla_o, w, x, mod, lg, lb)


FFN_TM = 1024
FFN_TF = 256
FFN_SUB = 512


def _ffn_kernel(x_ref, m_ref, wg_ref, wu_ref, wd_ref, lg_ref, lb_ref, o_ref, h_sc, *, sub):
    j = pl.program_id(2)
    n_sub = o_ref.shape[1] // sub

    @pl.when(j == 0)
    def _():
        for sb in range(n_sub):
            rows = slice(sb * sub, (sb + 1) * sub)
            h_sc[rows, :] = (x_ref[0, rows, :] * (1.0 + m_ref[0, 4:5, :]) + m_ref[0, 3:4, :]).astype(BF16)
        o_ref[...] = jnp.zeros_like(o_ref)

    wg = wg_ref[...].astype(BF16)
    wu = wu_ref[...].astype(BF16)
    wd = wd_ref[...].astype(BF16)
    for sb in range(n_sub):
        rows = slice(sb * sub, (sb + 1) * sub)
        h = h_sc[rows, :]
        a = jnp.dot(h, wg, preferred_element_type=F32)
        u = jnp.dot(h, wu, preferred_element_type=F32)
        o_ref[0, rows, :] += jnp.dot((a * jax.nn.sigmoid(a) * u).astype(BF16), wd, preferred_element_type=F32)

    @pl.when(j == pl.num_programs(2) - 1)
    def _():
        for sb in range(n_sub):
            rows = slice(sb * sub, (sb + 1) * sub)
            z = DEEPNORM_ALPHA * x_ref[0, rows, :] + (1.0 + m_ref[0, 5:6, :]) * o_ref[0, rows, :]
            o_ref[0, rows, :] = _layer_norm(z, lg_ref[...], lb_ref[...])


def ffn_ln(x, mod, wg, wu, wd, lg, lb):
    B, T, D = x.shape
    F = wg.shape[1]
    tm = min(FFN_TM, T)
    tf = FFN_TF
    vec = pl.BlockSpec((1, D), lambda b, i, j: (0, 0))
    return pl.pallas_call(
        functools.partial(_ffn_kernel, sub=min(FFN_SUB, tm)),
        grid=(B, T // tm, F // tf),
        in_specs=[pl.BlockSpec((1, tm, D), lambda b, i, j: (b, i, 0)),
                  pl.BlockSpec((1, 6, D), lambda b, i, j: (b, 0, 0)),
                  pl.BlockSpec((D, tf), lambda b, i, j: (0, j)),
                  pl.BlockSpec((D, tf), lambda b, i, j: (0, j)),
                  pl.BlockSpec((tf, D), lambda b, i, j: (j, 0)),
                  vec, vec],
        out_specs=pl.BlockSpec((1, tm, D), lambda b, i, j: (b, i, 0)),
        out_shape=jax.ShapeDtypeStruct((B, T, D), F32),
        scratch_shapes=[pltpu.VMEM((tm, D), BF16)],
        compiler_params=pltpu.CompilerParams(dimension_semantics=("parallel", "parallel", "arbitrary"),
                                             vmem_limit_bytes=BIG_VMEM_LIMIT_BYTES),
        name="ffn_ln",
    )(x, mod, wg, wu, wd, lg, lb)


def _router_kernel(x_ref, m_ref, wr_ref, h_ref, lg_ref):
    h = x_ref[0] * (1.0 + m_ref[0, 4:5, :]) + m_ref[0, 3:4, :]
    h_hi = h.astype(BF16)
    h_lo = (h - h_hi.astype(F32)).astype(BF16)
    h_ref[0] = h_hi
    lg_ref[0] = (jnp.dot(h_hi, wr_ref[0], preferred_element_type=F32)
                 + jnp.dot(h_lo, wr_ref[0], preferred_element_type=F32)
                 + jnp.dot(h_hi, wr_ref[1], preferred_element_type=F32))


def moe_router(x, mod, wr_pad):
    B, T, D = x.shape
    tm = min(512, T)
    return pl.pallas_call(
        _router_kernel,
        grid=(B, T // tm),
        in_specs=[pl.BlockSpec((1, tm, D), lambda b, i: (b, i, 0)),
                  pl.BlockSpec((1, 6, D), lambda b, i: (b, 0, 0)),
                  pl.BlockSpec((2, D, LANES), lambda b, i: (0, 0, 0))],
        out_specs=[pl.BlockSpec((1, tm, D), lambda b, i: (b, i, 0)),
                   pl.BlockSpec((1, tm, LANES), lambda b, i: (b, i, 0))],
        out_shape=[jax.ShapeDtypeStruct((B, T, D), BF16),
                   jax.ShapeDtypeStruct((B, T, LANES), F32)],
        compiler_params=_params("parallel", "parallel"),
        name="moe_router",
    )(x, mod, wr_pad)


def _moe_kernel(te_ref, nv_ref, x_ref, wg_ref, wu_ref, wd_ref, o_ref, acc_ref, *, sub):
    i = pl.program_id(0)
    j = pl.program_id(1)
    nv = nv_ref[i]

    @pl.when(j == 0)
    def _():
        acc_ref[...] = jnp.zeros_like(acc_ref)

    n_sub = acc_ref.shape[0] // sub

    def swiglu_rows(rows, wg, wu, wd):
        h = x_ref[rows, :]
        a = jnp.dot(h, wg, preferred_element_type=F32)
        u = jnp.dot(h, wu, preferred_element_type=F32)
        acc_ref[rows, :] += jnp.dot((a * jax.nn.sigmoid(a) * u).astype(BF16), wd, preferred_element_type=F32)

    for n_real in range(1, n_sub + 1):
        @pl.when(nv == n_real)
        def _():
            wg = wg_ref[0].astype(BF16)
            wu = wu_ref[0].astype(BF16)
            wd = wd_ref[0].astype(BF16)
            for r0 in range(0, n_real * sub, MOE_CHAIN):
                swiglu_rows(slice(r0, min(r0 + MOE_CHAIN, n_real * sub)), wg, wu, wd)

    @pl.when(j == pl.num_programs(1) - 1)
    def _():
        o_ref[...] = acc_ref[...].astype(o_ref.dtype)


def moe_experts(tile_e, tile_nv, xg, wg, wu, wd, tm, sub):
    M, D = xg.shape
    F = wg.shape[2]
    tf = 512
    nj = F // tf
    n_tiles = M // tm

    def wj(i, j, nv):
        return jnp.where(nv[i] > 0, j, nj - 1)

    return pl.pallas_call(
        functools.partial(_moe_kernel, sub=sub),
        grid_spec=pltpu.PrefetchScalarGridSpec(
            num_scalar_prefetch=2,
            grid=(n_tiles, nj),
            in_specs=[pl.BlockSpec((tm, D), lambda i, j, te, nv: (i, 0)),
                      pl.BlockSpec((1, D, tf), lambda i, j, te, nv: (te[i], 0, wj(i, j, nv))),
                      pl.BlockSpec((1, D, tf), lambda i, j, te, nv: (te[i], 0, wj(i, j, nv))),
                      pl.BlockSpec((1, tf, D), lambda i, j, te, nv: (te[i], wj(i, j, nv), 0))],
            out_specs=pl.BlockSpec((tm, D), lambda i, j, te, nv: (i, 0)),
            scratch_shapes=[pltpu.VMEM((tm, D), F32)]),
        out_shape=jax.ShapeDtypeStruct((M, D), BF16),
        compiler_params=pltpu.CompilerParams(dimension_semantics=("parallel", "arbitrary"),
                                             vmem_limit_bytes=BIG_VMEM_LIMIT_BYTES),
        name="moe_experts",
    )(tile_e, tile_nv, xg, wg, wu, wd)


def _combine_kernel(y0_ref, y1_ref, cw_ref, x_ref, m_ref, lg_ref, lb_ref, o_ref):
    cw = cw_ref[0]
    y = y0_ref[0].astype(F32) * cw[:, 0:1] + y1_ref[0].astype(F32) * cw[:, 1:2]
    z = DEEPNORM_ALPHA * x_ref[0] + (1.0 + m_ref[0, 5:6, :]) * y
    o_ref[0] = _layer_norm(z, lg_ref[...], lb_ref[...])


def moe_combine_ln(y0, y1, cw, x, mod, lg, lb):
    B, T, D = x.shape
    tm = min(512, T)
    row = pl.BlockSpec((1, tm, D), lambda b, i: (b, i, 0))
    vec = pl.BlockSpec((1, D), lambda b, i: (0, 0))
    return pl.pallas_call(
        _combine_kernel,
        grid=(B, T // tm),
        in_specs=[row, row, pl.BlockSpec((1, tm, LANES), lambda b, i: (b, i, 0)), row,
                  pl.BlockSpec((1, 6, D), lambda b, i: (b, 0, 0)), vec, vec],
        out_specs=row,
        out_shape=jax.ShapeDtypeStruct((B, T, D), F32),
        compiler_params=_params("parallel", "parallel"),
        name="moe_combine_ln",
    )(y0, y1, cw, x, mod, lg, lb)


MOE_TM = 1024
MOE_SUB = 256
MOE_CHAIN = 1024


def moe_layer(x, mod, w_router, wg, wu, wd, lg, lb):
    B, T, D = x.shape
    N = B * T
    A = N * TOP_K
    tm = MOE_TM
    wr_pad = jnp.zeros((D, LANES), F32).at[:, :N_EXPERTS].set(w_router)
    wr_hi = wr_pad.astype(BF16)
    wr_lo = (wr_pad - wr_hi.astype(F32)).astype(BF16)
    h, logits = moe_router(x, mod, jnp.stack([wr_hi, wr_lo]))
    logits = logits.reshape(N, LANES)[:, :N_EXPERTS]
    top_val, top_idx = lax.top_k(logits, TOP_K)
    comb = jax.nn.softmax(top_val, axis=-1)
    flat_e = top_idx.reshape(-1).astype(jnp.int32)
    onehot = (flat_e[:, None] == jnp.arange(N_EXPERTS, dtype=jnp.int32)[None, :]).astype(jnp.int32)
    csum = jnp.cumsum(onehot, axis=0)
    counts = csum[-1]
    padded = (counts + tm - 1) // tm * tm
    pad_end = jnp.cumsum(padded)
    pad_start = pad_end - padded
    slot = jnp.sum(onehot * (csum + pad_start[None, :]), axis=1) - 1
    n_tiles = -(-A // tm) + N_EXPERTS
    tile_start = jnp.arange(n_tiles, dtype=jnp.int32) * tm
    tile_e = jnp.minimum(jnp.searchsorted(pad_end, tile_start, side='right'), N_EXPERTS - 1).astype(jnp.int32)
    valid = jnp.clip(pad_start[tile_e] + counts[tile_e] - tile_start, 0, tm)
    tile_nv = ((valid + MOE_SUB - 1) // MOE_SUB).astype(jnp.int32)
    order = jnp.argsort(flat_e)
    seg_start = jnp.cumsum(counts) - counts
    slot_id = jnp.arange(n_tiles * tm, dtype=jnp.int32)
    within = slot_id - jnp.repeat(tile_start, tm)
    src = jnp.repeat(seg_start[tile_e] + tile_start - pad_start[tile_e], tm) + within
    slot_tok = jnp.where(within < jnp.repeat(valid, tm),
                         order[jnp.clip(src, 0, A - 1)].astype(jnp.int32) // TOP_K, slot_id % N)
    n_used = pad_end[-1] // tm
    tile_e = jnp.where(jnp.arange(n_tiles) < n_used, tile_e, tile_e[jnp.maximum(n_used - 1, 0)])
    xg = h.reshape(N, D)[slot_tok]
    y = moe_experts(tile_e, tile_nv, xg, wg, wu, wd, tm, MOE_SUB)
    slot_of = slot.reshape(N, TOP_K)
    y0 = y[slot_of[:, 0]].reshape(B, T, D)
    y1 = y[slot_of[:, 1]].reshape(B, T, D)
    cw = jnp.zeros((N, LANES), F32).at[:, :TOP_K].set(comb).reshape(B, T, LANES)
    return moe_combine_ln(y0, y1, cw, x, mod, lg, lb)


def _rope_tables(T):
    half = ROPE_DIMS // 2
    inv = ROPE_THETA ** (-jnp.arange(half, dtype=F32) * 2.0 / ROPE_DIMS)
    ang = jnp.arange(T).astype(F32)[:, None] * inv[None, :]
    cos, sin = jnp.cos(ang), jnp.sin(ang)
    z = jnp.zeros((T, LANES - ROPE_DIMS), F32)
    zh = jnp.zeros((T, half), F32)
    rc = jnp.concatenate([cos, cos, jnp.ones((T, LANES - ROPE_DIMS), F32)], axis=1)
    ra = jnp.concatenate([-sin, zh, z], axis=1)
    rb = jnp.concatenate([zh, sin, z], axis=1)
    return rc, ra, rb


def _overlap_matrix_t(NC, NB):
    nbp = -(-NB // 16) * 16
    c0 = np.arange(NC)[None, :] * CMP_STRIDE
    b0 = np.arange(nbp)[:, None] * SLC_BLOCK
    ov = (c0 < b0 + SLC_BLOCK) & (c0 + CMP_BLOCK > b0) & (np.arange(nbp)[:, None] < NB)
    return jnp.asarray(ov.astype(np.float32), dtype=BF16)


def _reorder_w_in(w):
    D = w.shape[0]
    o_ng = 1024 + 6 * 256
    o_gq = o_ng + 24
    o_gk = o_gq + 512
    o_gv = o_gk + 512
    o_ga = o_gv + 1024
    o_gg = o_ga + GLA_GATE_RANK
    misc = jnp.concatenate([w[:, o_ng:o_gq], w[:, o_ga:o_gg],
                            jnp.zeros((D, LANES - 24 - GLA_GATE_RANK), w.dtype)], axis=1)
    kv = 1024
    w_a = [w[:, 0:1024], w[:, kv + 512:kv + 768], w[:, kv + 1024:kv + 1280],
           w[:, kv + 768:kv + 1024], w[:, kv + 1280:kv + 1536]]
    w_b = [w[:, o_gv:o_ga], w[:, o_gg:o_gg + 1024], w[:, o_gq:o_gk], w[:, o_gk:o_gv],
           w[:, kv:kv + 512], misc]
    return jnp.concatenate(w_a, axis=1).astype(BF16), jnp.concatenate(w_b, axis=1).astype(BF16)


def hybrid_mixer_ln(x, mod, w_in_r, cmp_pe, cmp_w1, cmp_w2, wa_pad, ba, nw, w_out, lg, lb, tables, overlap):
    w_a, w_b = w_in_r
    qr, qn, ks, vs, kw, vw = nsa_proj(x, mod, w_a, *tables)
    proj = in_proj(x, mod, w_b)
    kc, vct = nsa_compress(proj, cmp_pe, cmp_w1, cmp_w2)
    nsa_o = nsa_attention(qr, qn, proj, kc, vct, ks, vs, kw, vw, overlap)
    gla_o = gla_mixer(proj, wa_pad, ba, nw)
    return out_proj_ln(nsa_o, gla_o, w_out, x, mod, lg, lb)


def kernel(x, c, w_ada, b_ada, w_in, cmp_pos_k, cmp_w1_k, cmp_w2_k, cmp_pos_v, cmp_w1_v, cmp_w2_v, gla_w_a2, gla_b_a, gla_norm_w, w_out, ln_mix_g, ln_mix_b, ln_ffn_g, ln_ffn_b, ffn_w_gate, ffn_w_up, ffn_w_down, moe_router, moe_w_gate, moe_w_up, moe_w_down):
    B, T, D = x.shape
    L = w_ada.shape[0]
    c_pad = jnp.zeros((8, D), F32).at[:B].set(c)
    mod_all = ada_mod(c_pad, w_ada, b_ada.reshape(L, 1, 6 * D))[:, :B].reshape(L, B, 6, D)
    tables = _rope_tables(T)
    overlap = _overlap_matrix_t(T // CMP_STRIDE, T // SLC_BLOCK)
    for layer in range(L):
        mod = mod_all[layer]
        wa_pad = jnp.zeros((LANES, GLA_HEADS * GLA_DK), F32).at[MISC_GA:MISC_GA + GLA_GATE_RANK].set(
            gla_w_a2[layer]).astype(BF16)
        x = hybrid_mixer_ln(
            x, mod, _reorder_w_in(w_in[layer]),
            jnp.stack([cmp_pos_k[layer], cmp_pos_v[layer]]),
            jnp.stack([cmp_w1_k[layer], cmp_w1_v[layer]]).astype(BF16),
            jnp.stack([cmp_w2_k[layer], cmp_w2_v[layer]]).astype(BF16),
            wa_pad, gla_b_a[layer].reshape(1, -1), gla_norm_w[layer].reshape(1, -1),
            w_out[layer].astype(BF16), ln_mix_g[layer].reshape(1, D), ln_mix_b[layer].reshape(1, D),
            tables, overlap)
        lg = ln_ffn_g[layer].reshape(1, D)
        lb = ln_ffn_b[layer].reshape(1, D)
        i = layer // 2
        if layer % 2 == 0:
            x = ffn_ln(x, mod, ffn_w_gate[i], ffn_w_up[i], ffn_w_down[i], lg, lb)
        else:
            x = moe_layer(x, mod, moe_router[i], moe_w_gate[i], moe_w_up[i], moe_w_down[i], lg, lb)
    return x
```

```python
import functools

import numpy as np
import jax
import jax.numpy as jnp
from jax import lax
from jax.experimental import pallas as pl
from jax.experimental.pallas import tpu as pltpu

F32 = jnp.float32
BF16 = jnp.bfloat16

D_MODEL = 2048
DEPTH = 2
HEAD_DIM = 128
NSA_HEADS = 8
NSA_KV_HEADS = 2
NSA_GROUP = 4
CMP_BLOCK = 32
CMP_STRIDE = 16
CMP_HIDDEN = 256
SLC_BLOCK = 64
SLC_TOPN = 16
WINDOW = 512
FORCE_SCORE = 1e9
GLA_DV = 256
GLA_HEADS = 4
GLA_DK = 128
GLA_GATE_RANK = 16
GLA_GATE_NORM = 16.0
GLA_CHUNK = 64
ROPE_THETA = 500000.0
ROPE_DIMS = 32
N_EXPERTS = 8
TOP_K = 2
LN_EPS = 1e-5
NORM_EPS = 1e-6
DEEPNORM_ALPHA = (2 * DEPTH) ** 0.25
NEG_BIG = -1e30
MASK_BIG = 2.0 ** 100
V_AUG_ROWS = HEAD_DIM + 16
NSA_HEADS_PER_STEP = 2
NSA_TK = 512
LOG2_E = 1.4426950408889634

VMEM_LIMIT_BYTES = 56 * 1024 * 1024
BIG_VMEM_LIMIT_BYTES = 60 * 1024 * 1024
LANES = 128

A_NQ = 0
A_KS = 1024
A_KW = 1280
A_VS = 1536
A_VW = 1792
A_W = 2048
C_GV = 0
C_GG = 1024
C_GQ = 2048
C_GK = 2560
C_KC = 3072
C_VC = 3328
C_MISC = 3584
PROJ_W = 3712
MISC_GA = 24

NT_DIMS = (((1,), (1,)), ((), ()))


def _params(*sem):
    return pltpu.CompilerParams(dimension_semantics=sem, vmem_limit_bytes=VMEM_LIMIT_BYTES)


def _layer_norm(z, g, b):
    mu = jnp.mean(z, axis=-1, keepdims=True)
    zc = z - mu
    var = jnp.mean(zc * zc, axis=-1, keepdims=True)
    return zc * lax.rsqrt(var + LN_EPS) * g + b


def _ada_kernel(c_ref, w_ref, b_ref, o_ref):
    c = c_ref[...]
    cond = c * jax.nn.sigmoid(c)
    o_ref[0] = jnp.dot(cond.astype(BF16), w_ref[0].astype(BF16),
                       preferred_element_type=F32) + b_ref[0]


def ada_mod(c_pad, w_ada, b_ada):
    L, D, N = w_ada.shape
    tn = 1024
    return pl.pallas_call(
        _ada_kernel,
        grid=(L, N // tn),
        in_specs=[pl.BlockSpec((8, D), lambda l, j: (0, 0)),
                  pl.BlockSpec((1, D, tn), lambda l, j: (l, 0, j)),
                  pl.BlockSpec((1, 1, tn), lambda l, j: (l, 0, j))],
        out_specs=pl.BlockSpec((1, 8, tn), lambda l, j: (l, 0, j)),
        out_shape=jax.ShapeDtypeStruct((L, 8, N), F32),
        compiler_params=_params("parallel", "parallel"),
        name="ada_mod",
    )(c_pad, w_ada, b_ada)


def _inproj_kernel(x_ref, m_ref, w_ref, o_ref):
    h = x_ref[0] * (1.0 + m_ref[0, 1:2, :]) + m_ref[0, 0:1, :]
    o_ref[0] = jnp.dot(h.astype(BF16), w_ref[...], preferred_element_type=F32)


def in_proj(x, mod, w):
    B, T, D = x.shape
    N = w.shape[1]
    tm = min(512, T)
    return pl.pallas_call(
        _inproj_kernel,
        grid=(B, T // tm),
        in_specs=[pl.BlockSpec((1, tm, D), lambda b, i: (b, i, 0)),
                  pl.BlockSpec((1, 6, D), lambda b, i: (b, 0, 0)),
                  pl.BlockSpec((D, N), lambda b, i: (0, 0))],
        out_specs=pl.BlockSpec((1, tm, N), lambda b, i: (b, i, 0)),
        out_shape=jax.ShapeDtypeStruct((B, T, N), F32),
        compiler_params=_params("parallel", "parallel"),
        name="in_proj",
    )(x, mod, w)


def _nsa_proj_kernel(x_ref, m_ref, w_ref, rc_ref, ra_ref, rb_ref,
                     qr_ref, qn_ref, ks_ref, vs_ref, kw_ref, vw_ref, *, sub):
    tm = x_ref.shape[1]
    scale = HEAD_DIM ** -0.5 * LOG2_E
    for r in range(tm // sub):
        rows = slice(r * sub, (r + 1) * sub)
        rc = rc_ref[rows, :]
        ra = ra_ref[rows, :]
        rb = rb_ref[rows, :]

        def rope(xh):
            return (xh * rc + pltpu.roll(xh, LANES - ROPE_DIMS // 2, 1) * ra
                    + pltpu.roll(xh, ROPE_DIMS // 2, 1) * rb)

        h = x_ref[0, rows, :] * (1.0 + m_ref[0, 1:2, :]) + m_ref[0, 0:1, :]
        p = jnp.dot(h.astype(BF16), w_ref[...], preferred_element_type=F32)
        for hq in range(NSA_HEADS):
            xh = p[:, A_NQ + hq * HEAD_DIM:A_NQ + (hq + 1) * HEAD_DIM]
            qr_ref[0, rows, hq * HEAD_DIM:(hq + 1) * HEAD_DIM] = (rope(xh) * scale).astype(BF16)
            qn_ref[0, rows, hq * HEAD_DIM:(hq + 1) * HEAD_DIM] = (xh * scale).astype(BF16)
        key_blk = (pl.program_id(1) * tm + r * sub + lax.broadcasted_iota(jnp.int32, (sub, LANES), 0)) // SLC_BLOCK
        blk_onehot = jnp.where(key_blk == lax.broadcasted_iota(jnp.int32, (sub, LANES), 1), 1.0, 0.0).astype(BF16)
        for hk in range(NSA_KV_HEADS):
            ks_ref[0, hk, rows, 0:HEAD_DIM] = rope(p[:, A_KS + hk * HEAD_DIM:A_KS + (hk + 1) * HEAD_DIM]).astype(BF16)
            ks_ref[0, hk, rows, HEAD_DIM:] = blk_onehot
            kw_ref[0, hk, rows, :] = rope(p[:, A_KW + hk * HEAD_DIM:A_KW + (hk + 1) * HEAD_DIM]).astype(BF16)
            vs_ref[0, hk, 0:HEAD_DIM, rows] = p[:, A_VS + hk * HEAD_DIM:A_VS + (hk + 1) * HEAD_DIM].T.astype(BF16)
            vs_ref[0, hk, HEAD_DIM:, rows] = jnp.ones((V_AUG_ROWS - HEAD_DIM, sub), BF16)
            vw_ref[0, hk, :, rows] = p[:, A_VW + hk * HEAD_DIM:A_VW + (hk + 1) * HEAD_DIM].T.astype(BF16)


def nsa_proj(x, mod, w, rc, ra, rb):
    B, T, D = x.shape
    tm = min(512, T)
    assert T // SLC_BLOCK <= LANES
    qw = NSA_HEADS * HEAD_DIM
    qspec = pl.BlockSpec((1, tm, qw), lambda b, i: (b, i, 0))
    tspec = pl.BlockSpec((tm, LANES), lambda b, i: (i, 0))
    return pl.pallas_call(
        functools.partial(_nsa_proj_kernel, sub=min(128, tm)),
        grid=(B, T // tm),
        in_specs=[pl.BlockSpec((1, tm, D), lambda b, i: (b, i, 0)),
                  pl.BlockSpec((1, 6, D), lambda b, i: (b, 0, 0)),
                  pl.BlockSpec((D, A_W), lambda b, i: (0, 0)),
                  tspec, tspec, tspec],
        out_specs=[qspec, qspec,
                   pl.BlockSpec((1, NSA_KV_HEADS, tm, HEAD_DIM + LANES), lambda b, i: (b, 0, i, 0)),
                   pl.BlockSpec((1, NSA_KV_HEADS, V_AUG_ROWS, tm), lambda b, i: (b, 0, 0, i)),
                   pl.BlockSpec((1, NSA_KV_HEADS, tm, HEAD_DIM), lambda b, i: (b, 0, i, 0)),
                   pl.BlockSpec((1, NSA_KV_HEADS, HEAD_DIM, tm), lambda b, i: (b, 0, 0, i))],
        out_shape=[jax.ShapeDtypeStruct((B, T, qw), BF16),
                   jax.ShapeDtypeStruct((B, T, qw), BF16),
                   jax.ShapeDtypeStruct((B, NSA_KV_HEADS, T, HEAD_DIM + LANES), BF16),
                   jax.ShapeDtypeStruct((B, NSA_KV_HEADS, V_AUG_ROWS, T), BF16),
                   jax.ShapeDtypeStruct((B, NSA_KV_HEADS, T, HEAD_DIM), BF16),
                   jax.ShapeDtypeStruct((B, NSA_KV_HEADS, HEAD_DIM, T), BF16)],
        compiler_params=_params("parallel", "parallel"),
        name="nsa_proj",
    )(x, mod, w, rc, ra, rb)


def _compress_kernel(a_ref, pe_ref, w1_ref, w2_ref, o_ref, ot_ref, *, n_half):
    half = CMP_STRIDE

    def part(l0):
        acc = jnp.zeros((n_half, CMP_HIDDEN), F32)
        for l in range(half):
            rows = a_ref[0, pl.ds(l, n_half, stride=half), :] + pe_ref[0, l0 + l:l0 + l + 1, :]
            acc += jnp.dot(rows.astype(BF16), w1_ref[0, (l0 + l) * HEAD_DIM:(l0 + l + 1) * HEAD_DIM, :],
                           preferred_element_type=F32)
        return acc

    first = part(0)
    second = part(half)
    hid = first + pltpu.roll(second, n_half - 1, 0)
    row = lax.broadcasted_iota(jnp.int32, (n_half, 1), 0)
    hid = jnp.where(row < n_half - 1, hid, 0.0)
    act = jax.nn.gelu(hid)
    out = jnp.dot(act.astype(BF16), w2_ref[0], preferred_element_type=F32)
    o_ref[0, 0, 0] = out.astype(BF16)
    ot_ref[0, 0, 0] = out.T.astype(BF16)


def nsa_compress(proj, pe, w1, w2):
    B, T, _ = proj.shape
    n_half = T // CMP_STRIDE
    return pl.pallas_call(
        functools.partial(_compress_kernel, n_half=n_half),
        grid=(B, 2, NSA_KV_HEADS),
        in_specs=[pl.BlockSpec((1, T, HEAD_DIM), lambda b, s, h: (b, 0, C_KC // HEAD_DIM + s * NSA_KV_HEADS + h)),
                  pl.BlockSpec((1, CMP_BLOCK, HEAD_DIM), lambda b, s, h: (s, 0, 0)),
                  pl.BlockSpec((1, CMP_BLOCK * HEAD_DIM, CMP_HIDDEN), lambda b, s, h: (s, 0, 0)),
                  pl.BlockSpec((1, CMP_HIDDEN, HEAD_DIM), lambda b, s, h: (s, 0, 0))],
        out_specs=[pl.BlockSpec((1, 1, 1, n_half, HEAD_DIM), lambda b, s, h: (b, s, h, 0, 0)),
                   pl.BlockSpec((1, 1, 1, HEAD_DIM, n_half), lambda b, s, h: (b, s, h, 0, 0))],
        out_shape=[jax.ShapeDtypeStruct((B, 2, NSA_KV_HEADS, n_half, HEAD_DIM), BF16),
                   jax.ShapeDtypeStruct((B, 2, NSA_KV_HEADS, HEAD_DIM, n_half), BF16)],
        compiler_params=_params("parallel", "parallel", "parallel"),
        name="nsa_compress",
    )(proj, pe, w1, w2)


def _nsa_kernel(qr_ref, qn_ref, g_ref, kc_ref, vct_ref, ks_ref, vst_ref, kw_ref, vwt_ref, ovt_ref, o_ref,
                s_buf, smax_buf, m_buf, acc_buf, *, TQ, TK, WK, NC, NB, NH):
    G = NSA_GROUP
    R = G * TQ
    NBP = ovt_ref.shape[0]
    i = pl.program_id(2)
    t0 = i * TQ
    m_floor = 0.5 * NEG_BIG

    def stack(ref, h):
        x = ref[0]
        return jnp.concatenate([x[:, (h * G + g) * HEAD_DIM:(h * G + g + 1) * HEAD_DIM] for g in range(G)], axis=0)

    def tile_g(x):
        return jnp.concatenate([x] * G, axis=1)

    tT = t0 + lax.broadcasted_iota(jnp.int32, (1, TQ), 1)
    cur = tT // SLC_BLOCK
    n_sel = min(SLC_TOPN, NB)

    HS = range(NH)
    qr = [stack(qr_ref, h) for h in HS]
    qn = [stack(qn_ref, h) for h in HS]

    ws = pl.multiple_of(jnp.maximum(t0 + TQ - WK, 0), LANES)
    s_c = [lax.dot_general(kc_ref[0, 0, h], qn[h], NT_DIMS, preferred_element_type=F32) for h in HS]
    s_w = [lax.dot_general(kw_ref[0, h, pl.ds(ws, WK), :], qr[h], NT_DIMS, preferred_element_type=F32)
           for h in HS]
    cend = lax.broadcasted_iota(jnp.int32, (NC, 1), 0) * CMP_STRIDE + (CMP_BLOCK - 1)
    bias_c = tile_g(jnp.where(cend <= tT, 0.0, NEG_BIG))
    dist = tT - (ws + lax.broadcasted_iota(jnp.int32, (WK, 1), 0))
    bias_w = tile_g(jnp.where(dist >= 0, jnp.where(dist < WINDOW, 0.0, NEG_BIG), NEG_BIG))

    p_cb, e_wb, l_w = [], [], []
    for h in HS:
        sc = s_c[h] + bias_c
        m_c = jnp.maximum(jnp.max(sc, axis=0, keepdims=True), m_floor)
        e_c = jnp.exp2(sc - m_c)
        den = jnp.sum(e_c, axis=0, keepdims=True)
        p_cb.append((e_c * jnp.where(den > 0.0, 1.0 / den, 0.0)).astype(BF16))
        sw = s_w[h] + bias_w
        e_w = jnp.exp2(sw - jnp.max(sw, axis=0, keepdims=True))
        l_w.append(jnp.sum(e_w, axis=0, keepdims=True))
        e_wb.append(e_w.astype(BF16))

    o_c = [jnp.dot(vct_ref[0, 0, h], p_cb[h], preferred_element_type=F32) for h in HS]
    imp4 = [jnp.dot(ovt_ref[...], p_cb[h], preferred_element_type=F32) for h in HS]
    o_w = [jnp.dot(vwt_ref[0, h, :, pl.ds(ws, WK)], e_wb[h], preferred_element_type=F32) * (1.0 / l_w[h])
           for h in HS]

    jj = lax.broadcasted_iota(jnp.int32, (NBP, TQ), 0)
    forced = jnp.where(jj == 0, 1, jnp.where(jj == cur, 1, jnp.where(jj == cur - 1, 1, 0)))
    val = []
    for h in HS:
        impT = imp4[h][:, 0:TQ]
        for g in range(1, G):
            impT = impT + imp4[h][:, g * TQ:(g + 1) * TQ]
        val.append(jnp.where(jj > cur, -jnp.inf, jnp.where(forced > 0, FORCE_SCORE, impT)))
    SUBL = 8
    chunks = [[val[h][c * SUBL:(c + 1) * SUBL, :] for c in range(NBP // SUBL)] for h in HS]
    rank_c = [[jnp.zeros((SUBL, TQ), jnp.int32) for c in range(NBP // SUBL)] for h in HS]
    jj_c = lax.broadcasted_iota(jnp.int32, (SUBL, TQ), 0)
    for j2 in range(NB):
        for h in HS:
            row = val[h][j2:j2 + 1, :]
            for c in range(NBP // SUBL):
                v = chunks[h][c]
                if c * SUBL > j2:
                    beat = jnp.where(row >= v, 1, 0)
                elif (c + 1) * SUBL - 1 < j2:
                    beat = jnp.where(row > v, 1, 0)
                else:
                    later = jnp.where(jj_c + c * SUBL > j2, 1, 0)
                    beat = jnp.where(row > v, 1, jnp.where(row == v, later, 0))
                rank_c[h][c] = rank_c[h][c] + beat
    rank = [jnp.concatenate(rank_c[h], axis=0) for h in HS]
    fronts = []
    for h in HS:
        sel_neg = jnp.where(jj > cur, -MASK_BIG, jnp.where(rank[h] < n_sel, 0.0, -MASK_BIG))
        if NBP < LANES:
            sel_neg = jnp.concatenate([sel_neg, jnp.zeros((LANES - NBP, TQ), F32)], axis=0)
        sel_q = sel_neg.T.astype(BF16)
        q_aug = jnp.concatenate([qr[h], jnp.concatenate([sel_q] * G, axis=0)], axis=1)
        fronts.append((o_c[h], o_w[h], q_aug))

    def sel_scores(h, kt, slot):
        k0 = pl.multiple_of(kt * TK, TK)
        k = ks_ref[0, h, pl.ds(k0, TK), :]
        s = lax.dot_general(k, fronts[h][2], NT_DIMS, preferred_element_type=F32)
        s_buf[h, slot] = s
        smax_buf[h, slot] = jnp.max(s, axis=0, keepdims=True)

    def sel_accumulate(h, kt, slot):
        k0 = pl.multiple_of(kt * TK, TK)
        vt = vst_ref[0, h, :, pl.ds(k0, TK)]
        m = m_buf[h]
        m_new = jnp.maximum(m, smax_buf[h, slot])
        p = jnp.exp2(s_buf[h, slot] - m_new).astype(BF16)
        acc_buf[h] = jnp.exp2(m - m_new) * acc_buf[h] + jnp.dot(vt, p, preferred_element_type=F32)
        m_buf[h] = m_new

    n_kt = (t0 + TQ + TK - 1) // TK
    n_loop = n_kt - 1
    for h in HS:
        m_buf[h] = jnp.full((1, R), m_floor, F32)
        acc_buf[h] = jnp.zeros((V_AUG_ROWS, R), F32)
        sel_scores(h, 0, 0)

    def pair_body(k, carry):
        for h in HS:
            sel_scores(h, 2 * k + 1, 1)
            sel_accumulate(h, 2 * k, 0)

        @pl.when(2 * k + 1 < n_loop)
        def _():
            for h in HS:
                sel_scores(h, 2 * k + 2, 0)
                sel_accumulate(h, 2 * k + 1, 1)
        return carry

    lax.fori_loop(0, (n_loop + 1) // 2, pair_body, 0)
    krow = n_loop * TK + lax.broadcasted_iota(jnp.int32, (TK, 1), 0)
    causal = tile_g(jnp.where(krow <= tT, 0.0, NEG_BIG))
    k_last = pl.multiple_of(n_loop * TK, TK)
    last_slot = n_loop % 2
    s_fin = [s_buf[h, last_slot] + causal for h in HS]
    m_fin = [jnp.maximum(m_buf[h], jnp.max(s_fin[h], axis=0, keepdims=True)) for h in HS]
    p_fin = [jnp.exp2(s_fin[h] - m_fin[h]).astype(BF16) for h in HS]
    acc_fin = [jnp.exp2(m_buf[h] - m_fin[h]) * acc_buf[h]
               + jnp.dot(vst_ref[0, h, :, pl.ds(k_last, TK)], p_fin[h], preferred_element_type=F32) for h in HS]
    gate_all = jax.nn.sigmoid(g_ref[0]).T
    assert NH == NSA_KV_HEADS
    gates = [gate_all[3 * G * h:3 * G * (h + 1), :] for h in HS]
    for h in HS:
        o_s = acc_fin[h][0:HEAD_DIM] * (1.0 / acc_fin[h][HEAD_DIM:HEAD_DIM + 1])
        o_c, o_w, _ = fronts[h]
        gate = gates[h]
        for g in range(G):
            cols = slice(g * TQ, (g + 1) * TQ)
            o = (gate[3 * g:3 * g + 1, :] * o_c[:, cols] + gate[3 * g + 1:3 * g + 2, :] * o_s[:, cols]
                 + gate[3 * g + 2:3 * g + 3, :] * o_w[:, cols])
            o_ref[0, :, (h * G + g) * HEAD_DIM:(h * G + g + 1) * HEAD_DIM] = o.T.astype(BF16)


def nsa_attention(qr, qn, gates, kc, vct, ks, vst, kw, vwt, overlap_t):
    B, T, _ = qr.shape
    TQ = 128
    TK = min(NSA_TK, T)
    WK = min(WINDOW + TQ, T)
    NC = kc.shape[3]
    NB = T // SLC_BLOCK
    NH = NSA_HEADS_PER_STEP
    gw = NH * NSA_GROUP * HEAD_DIM
    qspec = pl.BlockSpec((1, TQ, gw), lambda b, h, i: (b, i, h))
    kspec = pl.BlockSpec((1, NH, T, HEAD_DIM), lambda b, h, i: (b, h, 0, 0))
    vtspec = pl.BlockSpec((1, NH, HEAD_DIM, T), lambda b, h, i: (b, h, 0, 0))
    return pl.pallas_call(
        functools.partial(_nsa_kernel, TQ=TQ, TK=TK, WK=WK, NC=NC, NB=NB, NH=NH),
        grid=(B, NSA_KV_HEADS // NH, T // TQ),
        in_specs=[qspec, qspec,
                  pl.BlockSpec((1, TQ, LANES), lambda b, h, i: (b, i, C_MISC // LANES)),
                  pl.BlockSpec((1, 1, NH, NC, HEAD_DIM), lambda b, h, i: (b, 0, h, 0, 0)),
                  pl.BlockSpec((1, 1, NH, HEAD_DIM, NC), lambda b, h, i: (b, 1, h, 0, 0)),
                  pl.BlockSpec((1, NH, T, HEAD_DIM + LANES), lambda b, h, i: (b, h, 0, 0)),
                  pl.BlockSpec((1, NH, V_AUG_ROWS, T), lambda b, h, i: (b, h, 0, 0)),
                  kspec, vtspec,
                  pl.BlockSpec(overlap_t.shape, lambda b, h, i: (0, 0))],
        out_specs=pl.BlockSpec((1, TQ, gw), lambda b, h, i: (b, i, h)),
        out_shape=jax.ShapeDtypeStruct((B, T, NSA_HEADS * HEAD_DIM), BF16),
        scratch_shapes=[pltpu.VMEM((NH, 2, TK, NSA_GROUP * TQ), F32),
                        pltpu.VMEM((NH, 2, 1, NSA_GROUP * TQ), F32),
                        pltpu.VMEM((NH, 1, NSA_GROUP * TQ), F32),
                        pltpu.VMEM((NH, V_AUG_ROWS, NSA_GROUP * TQ), F32)],
        compiler_params=_params("parallel", "parallel", "arbitrary"),
        name="nsa_attention",
    )(qr, qn, gates, kc, vct, ks, vst, kw, vwt, overlap_t)


def _gla_kernel(q_ref, k_ref, v_ref, gg_ref, misc_ref, wa_ref, ba_ref, nw_ref, o_ref, s_ref, la_ref, *, TC):
    C = GLA_CHUNK

    @pl.when(pl.program_id(1) == 0)
    def _():
        s_ref[...] = jnp.zeros_like(s_ref)

    z = jnp.dot(misc_ref[0].astype(BF16), wa_ref[...], preferred_element_type=F32) + ba_ref[...]
    la_ref[...] = (jnp.minimum(z, 0.0) - jnp.log1p(jnp.exp(-jnp.abs(z)))) * (1.0 / GLA_GATE_NORM)
    ri = lax.broadcasted_iota(jnp.int32, (C, C), 0)
    ci = lax.broadcasted_iota(jnp.int32, (C, C), 1)
    causal = ri >= ci
    tri = jnp.where(causal, 1.0, 0.0)
    nw = nw_ref[...]

    n_c = TC // C
    pairs = [(c, h) for c in range(n_c) for h in range(GLA_HEADS)]
    rows = [slice(c * C, (c + 1) * C) for c in range(n_c)]
    b_all = [jnp.dot(tri, la_ref[rows[c], :], preferred_element_type=F32, precision=lax.Precision.HIGHEST)
             for c in range(n_c)]
    qd, kv, decay, intra = {}, {}, {}, {}
    for c, h in pairs:
        b = b_all[c][:, h * GLA_DK:(h + 1) * GLA_DK]
        bl = b[C - 1:C, :]
        q = q_ref[0, rows[c], h * GLA_DK:(h + 1) * GLA_DK] * (GLA_DK ** -0.5)
        k = k_ref[0, rows[c], h * GLA_DK:(h + 1) * GLA_DK]
        v = v_ref[0, rows[c], h * GLA_DV:(h + 1) * GLA_DV].astype(BF16)
        qd[c, h] = (q * jnp.exp(b)).astype(BF16)
        ki = (k * jnp.exp(-b)).astype(BF16)
        kst = (k * jnp.exp(bl - b)).T.astype(BF16)
        decay[c, h] = jnp.exp(b.T[:, C - 1:C])
        a = lax.dot_general(qd[c, h], ki, NT_DIMS, preferred_element_type=F32)
        a = jnp.where(causal, a, 0.0).astype(BF16)
        intra[c, h] = jnp.dot(a, v, preferred_element_type=F32)
        kv[c, h] = jnp.dot(kst, v, preferred_element_type=F32)
    state = [s_ref[h] for h in range(GLA_HEADS)]
    for c, h in pairs:
        o = intra[c, h] + jnp.dot(qd[c, h], state[h].astype(BF16), preferred_element_type=F32)
        state[h] = state[h] * decay[c, h] + kv[c, h]
        o = o * lax.rsqrt(jnp.mean(o * o, axis=-1, keepdims=True) + NORM_EPS) * nw
        gg = gg_ref[0, rows[c], h * GLA_DV:(h + 1) * GLA_DV]
        o_ref[0, rows[c], h * GLA_DV:(h + 1) * GLA_DV] = (o * (gg * jax.nn.sigmoid(gg))).astype(BF16)
    for h in range(GLA_HEADS):
        s_ref[h] = state[h]


def gla_mixer(proj, wa_pad, ba, nw):
    B, T, _ = proj.shape
    TC = min(512, T)
    qk_w = GLA_HEADS * GLA_DK
    v_w = GLA_HEADS * GLA_DV
    return pl.pallas_call(
        functools.partial(_gla_kernel, TC=TC),
        grid=(B, T // TC),
        in_specs=[pl.BlockSpec((1, TC, qk_w), lambda b, i: (b, i, C_GQ // qk_w)),
                  pl.BlockSpec((1, TC, qk_w), lambda b, i: (b, i, C_GK // qk_w)),
                  pl.BlockSpec((1, TC, v_w), lambda b, i: (b, i, C_GV // v_w)),
                  pl.BlockSpec((1, TC, v_w), lambda b, i: (b, i, C_GG // v_w)),
                  pl.BlockSpec((1, TC, LANES), lambda b, i: (b, i, C_MISC // LANES)),
                  pl.BlockSpec((LANES, qk_w), lambda b, i: (0, 0)),
                  pl.BlockSpec((1, qk_w), lambda b, i: (0, 0)),
                  pl.BlockSpec((1, GLA_DV), lambda b, i: (0, 0))],
        out_specs=pl.BlockSpec((1, TC, v_w), lambda b, i: (b, i, 0)),
        out_shape=jax.ShapeDtypeStruct((B, T, v_w), BF16),
        scratch_shapes=[pltpu.VMEM((GLA_HEADS, GLA_DK, GLA_DV), F32),
                        pltpu.VMEM((TC, qk_w), F32)],
        compiler_params=_params("parallel", "arbitrary"),
        name="gla_mixer",
    )(proj, proj, proj, proj, proj, wa_pad, ba, nw)


def _outproj_kernel(n_ref, g_ref, w_ref, x_ref, m_ref, lg_ref, lb_ref, o_ref, *, half):
    sub = 128
    for r in range(o_ref.shape[1] // sub):
        rows = slice(r * sub, (r + 1) * sub)
        y = (jnp.dot(n_ref[0, rows, :], w_ref[0:half, :], preferred_element_type=F32)
             + jnp.dot(g_ref[0, rows, :], w_ref[half:, :], preferred_element_type=F32))
        z = DEEPNORM_ALPHA * x_ref[0, rows, :] + (1.0 + m_ref[0, 2:3, :]) * y
        o_ref[0, rows, :] = _layer_norm(z, lg_ref[...], lb_ref[...])


def out_proj_ln(nsa_o, gla_o, w, x, mod, lg, lb):
    B, T, D = x.shape
    half = nsa_o.shape[-1]
    tm = min(512, T)
    vec = pl.BlockSpec((1, D), lambda b, i: (0, 0))
    return pl.pallas_call(
        functools.partial(_outproj_kernel, half=half),
        grid=(B, T // tm),
        in_specs=[pl.BlockSpec((1, tm, half), lambda b, i: (b, i, 0)),
                  pl.BlockSpec((1, tm, gla_o.shape[-1]), lambda b, i: (b, i, 0)),
                  pl.BlockSpec(w.shape, lambda b, i: (0, 0)),
                  pl.BlockSpec((1, tm, D), lambda b, i: (b, i, 0)),
                  pl.BlockSpec((1, 6, D), lambda b, i: (b, 0, 0)),
                  vec, vec],
        out_specs=pl.BlockSpec((1, tm, D), lambda b, i: (b, i, 0)),
        out_shape=jax.ShapeDtypeStruct((B, T, D), F32),
        compiler_params=_params("parallel", "parallel"),
        name="out_proj_ln",
    )(nsa_o, gla_o, w, x, mod, lg, lb)


FFN_TM = 1024
FFN_TF = 256
FFN_SUB = 512


def _ffn_kernel(x_ref, m_ref, wg_ref, wu_ref, wd_ref, lg_ref, lb_ref, o_ref, h_sc, *, sub):
    j = pl.program_id(2)
    last = pl.num_programs(2) - 1
    tm = o_ref.shape[1]

    def step(first, final):
        rows_per = sub // 2 if (first or final) else sub
        wg = wg_ref[...].astype(BF16)
        wu = wu_ref[...].astype(BF16)
        wd = wd_ref[...].astype(BF16)
        for sb in range(tm // rows_per):
            rows = slice(sb * rows_per, (sb + 1) * rows_per)
            if first:
                h_sc[rows, :] = (x_ref[0, rows, :] * (1.0 + m_ref[0, 4:5, :]) + m_ref[0, 3:4, :]).astype(BF16)
            h = h_sc[rows, :]
            a = jnp.dot(h, wg, preferred_element_type=F32)
            u = jnp.dot(h, wu, preferred_element_type=F32)
            y = jnp.dot((a * jax.nn.sigmoid(a) * u).astype(BF16), wd, preferred_element_type=F32)
            if first:
                o_ref[0, rows, :] = y
            elif final:
                z = DEEPNORM_ALPHA * x_ref[0, rows, :] + (1.0 + m_ref[0, 5:6, :]) * (o_ref[0, rows, :] + y)
                o_ref[0, rows, :] = _layer_norm(z, lg_ref[...], lb_ref[...])
            else:
                o_ref[0, rows, :] += y

    pl.when(j == 0)(lambda: step(True, False))
    pl.when(jnp.logical_and(j > 0, j < last))(lambda: step(False, False))
    pl.when(j == last)(lambda: step(False, True))


def ffn_ln(x, mod, wg, wu, wd, lg, lb):
    B, T, D = x.shape
    F = wg.shape[1]
    tm = min(FFN_TM, T)
    tf = FFN_TF
    vec = pl.BlockSpec((1, D), lambda b, i, j: (0, 0))
    return pl.pallas_call(
        functools.partial(_ffn_kernel, sub=min(FFN_SUB, tm)),
        grid=(B, T // tm, F // tf),
        in_specs=[pl.BlockSpec((1, tm, D), lambda b, i, j: (b, i, 0)),
                  pl.BlockSpec((1, 6, D), lambda b, i, j: (b, 0, 0)),
                  pl.BlockSpec((D, tf), lambda b, i, j: (0, j)),
                  pl.BlockSpec((D, tf), lambda b, i, j: (0, j)),
                  pl.BlockSpec((tf, D), lambda b, i, j: (j, 0)),
                  vec, vec],
        out_specs=pl.BlockSpec((1, tm, D), lambda b, i, j: (b, i, 0)),
        out_shape=jax.ShapeDtypeStruct((B, T, D), F32),
        scratch_shapes=[pltpu.VMEM((tm, D), BF16)],
        compiler_params=pltpu.CompilerParams(dimension_semantics=("parallel", "parallel", "arbitrary"),
                                             vmem_limit_bytes=BIG_VMEM_LIMIT_BYTES),
        name="ffn_ln",
    )(x, mod, wg, wu, wd, lg, lb)


def _router_kernel(x_ref, m_ref, wr_ref, h_ref, lg_ref):
    h = x_ref[0] * (1.0 + m_ref[0, 4:5, :]) + m_ref[0, 3:4, :]
    h_hi = h.astype(BF16)
    h_lo = (h - h_hi.astype(F32)).astype(BF16)
    h_ref[0] = h_hi
    lg_ref[0] = (jnp.dot(h_hi, wr_ref[0], preferred_element_type=F32)
                 + jnp.dot(h_lo, wr_ref[0], preferred_element_type=F32)
                 + jnp.dot(h_hi, wr_ref[1], preferred_element_type=F32))


def moe_router(x, mod, wr_pad):
    B, T, D = x.shape
    tm = min(512, T)
    return pl.pallas_call(
        _router_kernel,
        grid=(B, T // tm),
        in_specs=[pl.BlockSpec((1, tm, D), lambda b, i: (b, i, 0)),
                  pl.BlockSpec((1, 6, D), lambda b, i: (b, 0, 0)),
                  pl.BlockSpec((2, D, LANES), lambda b, i: (0, 0, 0))],
        out_specs=[pl.BlockSpec((1, tm, D), lambda b, i: (b, i, 0)),
                   pl.BlockSpec((1, tm, LANES), lambda b, i: (b, i, 0))],
        out_shape=[jax.ShapeDtypeStruct((B, T, D), BF16),
                   jax.ShapeDtypeStruct((B, T, LANES), F32)],
        compiler_params=_params("parallel", "parallel"),
        name="moe_router",
    )(x, mod, wr_pad)


def _moe_kernel(te_ref, nv_ref, x_ref, wg_ref, wu_ref, wd_ref, o_ref, acc_ref, *, sub):
    i = pl.program_id(0)
    j = pl.program_id(1)
    nv = nv_ref[i]

    @pl.when(j == 0)
    def _():
        acc_ref[...] = jnp.zeros_like(acc_ref)

    n_sub = acc_ref.shape[0] // sub

    def swiglu_rows(rows, wg, wu, wd):
        h = x_ref[rows, :]
        a = jnp.dot(h, wg, preferred_element_type=F32)
        u = jnp.dot(h, wu, preferred_element_type=F32)
        acc_ref[rows, :] += jnp.dot((a * jax.nn.sigmoid(a) * u).astype(BF16), wd, preferred_element_type=F32)

    for n_real in range(1, n_sub + 1):
        @pl.when(nv == n_real)
        def _():
            wg = wg_ref[0].astype(BF16)
            wu = wu_ref[0].astype(BF16)
            wd = wd_ref[0].astype(BF16)
            for r0 in range(0, n_real * sub, MOE_CHAIN):
                swiglu_rows(slice(r0, min(r0 + MOE_CHAIN, n_real * sub)), wg, wu, wd)

    @pl.when(j == pl.num_programs(1) - 1)
    def _():
        o_ref[...] = acc_ref[...].astype(o_ref.dtype)


def moe_experts(tile_e, tile_nv, xg, wg, wu, wd, tm, sub):
    M, D = xg.shape
    F = wg.shape[2]
    tf = 512
    nj = F // tf
    n_tiles = M // tm

    def wj(i, j, nv):
        return jnp.where(nv[i] > 0, j, nj - 1)

    return pl.pallas_call(
        functools.partial(_moe_kernel, sub=sub),
        grid_spec=pltpu.PrefetchScalarGridSpec(
            num_scalar_prefetch=2,
            grid=(n_tiles, nj),
            in_specs=[pl.BlockSpec((tm, D), lambda i, j, te, nv: (i, 0)),
                      pl.BlockSpec((1, D, tf), lambda i, j, te, nv: (te[i], 0, wj(i, j, nv))),
                      pl.BlockSpec((1, D, tf), lambda i, j, te, nv: (te[i], 0, wj(i, j, nv))),
                      pl.BlockSpec((1, tf, D), lambda i, j, te, nv: (te[i], wj(i, j, nv), 0))],
            out_specs=pl.BlockSpec((tm, D), lambda i, j, te, nv: (i, 0)),
            scratch_shapes=[pltpu.VMEM((tm, D), F32)]),
        out_shape=jax.ShapeDtypeStruct((M, D), BF16),
        compiler_params=pltpu.CompilerParams(dimension_semantics=("parallel", "arbitrary"),
                                             vmem_limit_bytes=BIG_VMEM_LIMIT_BYTES),
        name="moe_experts",
    )(tile_e, tile_nv, xg, wg, wu, wd)


def _combine_kernel(y0_ref, y1_ref, cw_ref, x_ref, m_ref, lg_ref, lb_ref, o_ref):
    cw = cw_ref[0]
    y = y0_ref[0].astype(F32) * cw[:, 0:1] + y1_ref[0].astype(F32) * cw[:, 1:2]
    z = DEEPNORM_ALPHA * x_ref[0] + (1.0 + m_ref[0, 5:6, :]) * y
    o_ref[0] = _layer_norm(z, lg_ref[...], lb_ref[...])


def moe_combine_ln(y0, y1, cw, x, mod, lg, lb):
    B, T, D = x.shape
    tm = min(512, T)
    row = pl.BlockSpec((1, tm, D), lambda b, i: (b, i, 0))
    vec = pl.BlockSpec((1, D), lambda b, i: (0, 0))
    return pl.pallas_call(
        _combine_kernel,
        grid=(B, T // tm),
        in_specs=[row, row, pl.BlockSpec((1, tm, LANES), lambda b, i: (b, i, 0)), row,
                  pl.BlockSpec((1, 6, D), lambda b, i: (b, 0, 0)), vec, vec],
        out_specs=row,
        out_shape=jax.ShapeDtypeStruct((B, T, D), F32),
        compiler_params=_params("parallel", "parallel"),
        name="moe_combine_ln",
    )(y0, y1, cw, x, mod, lg, lb)


MOE_TM = 1024
MOE_SUB = 256
MOE_CHAIN = 1024


def moe_layer(x, mod, w_router, wg, wu, wd, lg, lb):
    B, T, D = x.shape
    N = B * T
    A = N * TOP_K
    tm = MOE_TM
    wr_pad = jnp.zeros((D, LANES), F32).at[:, :N_EXPERTS].set(w_router)
    wr_hi = wr_pad.astype(BF16)
    wr_lo = (wr_pad - wr_hi.astype(F32)).astype(BF16)
    h, logits = moe_router(x, mod, jnp.stack([wr_hi, wr_lo]))
    logits = logits.reshape(N, LANES)[:, :N_EXPERTS]
    top_val, top_idx = lax.top_k(logits, TOP_K)
    comb = jax.nn.softmax(top_val, axis=-1)
    flat_e = top_idx.reshape(-1).astype(jnp.int32)
    onehot = (flat_e[:, None] == jnp.arange(N_EXPERTS, dtype=jnp.int32)[None, :]).astype(jnp.int32)
    csum = jnp.cumsum(onehot, axis=0)
    counts = csum[-1]
    padded = (counts + tm - 1) // tm * tm
    pad_end = jnp.cumsum(padded)
    pad_start = pad_end - padded
    slot = jnp.sum(onehot * (csum + pad_start[None, :]), axis=1) - 1
    n_tiles = -(-A // tm) + N_EXPERTS
    tile_start = jnp.arange(n_tiles, dtype=jnp.int32) * tm
    tile_e = jnp.minimum(jnp.searchsorted(pad_end, tile_start, side='right'), N_EXPERTS - 1).astype(jnp.int32)
    valid = jnp.clip(pad_start[tile_e] + counts[tile_e] - tile_start, 0, tm)
    tile_nv = ((valid + MOE_SUB - 1) // MOE_SUB).astype(jnp.int32)
    order = jnp.argsort(flat_e)
    seg_start = jnp.cumsum(counts) - counts
    slot_id = jnp.arange(n_tiles * tm, dtype=jnp.int32)
    within = slot_id - jnp.repeat(tile_start, tm)
    src = jnp.repeat(seg_start[tile_e] + tile_start - pad_start[tile_e], tm) + within
    slot_tok = jnp.where(within < jnp.repeat(valid, tm),
                         order[jnp.clip(src, 0, A - 1)].astype(jnp.int32) // TOP_K, slot_id % N)
    n_used = pad_end[-1] // tm
    tile_e = jnp.where(jnp.arange(n_tiles) < n_used, tile_e, tile_e[jnp.maximum(n_used - 1, 0)])
    xg = h.reshape(N, D)[slot_tok]
    y = moe_experts(tile_e, tile_nv, xg, wg, wu, wd, tm, MOE_SUB)
    slot_of = slot.reshape(N, TOP_K)
    y0 = y[slot_of[:, 0]].reshape(B, T, D)
    y1 = y[slot_of[:, 1]].reshape(B, T, D)
    cw = jnp.zeros((N, LANES), F32).at[:, :TOP_K].set(comb).reshape(B, T, LANES)
    return moe_combine_ln(y0, y1, cw, x, mod, lg, lb)


def _rope_tables(T):
    half = ROPE_DIMS // 2
    inv = ROPE_THETA ** (-jnp.arange(half, dtype=F32) * 2.0 / ROPE_DIMS)
    ang = jnp.arange(T).astype(F32)[:, None] * inv[None, :]
    cos, sin = jnp.cos(ang), jnp.sin(ang)
    z = jnp.zeros((T, LANES - ROPE_DIMS), F32)
    zh = jnp.zeros((T, half), F32)
    rc = jnp.concatenate([cos, cos, jnp.ones((T, LANES - ROPE_DIMS), F32)], axis=1)
    ra = jnp.concatenate([-sin, zh, z], axis=1)
    rb = jnp.concatenate([zh, sin, z], axis=1)
    return rc, ra, rb


def _overlap_matrix_t(NC, NB):
    nbp = -(-NB // 16) * 16
    c0 = np.arange(NC)[None, :] * CMP_STRIDE
    b0 = np.arange(nbp)[:, None] * SLC_BLOCK
    ov = (c0 < b0 + SLC_BLOCK) & (c0 + CMP_BLOCK > b0) & (np.arange(nbp)[:, None] < NB)
    return jnp.asarray(ov.astype(np.float32), dtype=BF16)


def _reorder_w_in(w):
    D = w.shape[0]
    o_ng = 1024 + 6 * 256
    o_gq = o_ng + 24
    o_gk = o_gq + 512
    o_gv = o_gk + 512
    o_ga = o_gv + 1024
    o_gg = o_ga + GLA_GATE_RANK
    misc = jnp.concatenate([w[:, o_ng:o_gq], w[:, o_ga:o_gg],
                            jnp.zeros((D, LANES - 24 - GLA_GATE_RANK), w.dtype)], axis=1)
    kv = 1024
    w_a = [w[:, 0:1024], w[:, kv + 512:kv + 768], w[:, kv + 1024:kv + 1280],
           w[:, kv + 768:kv + 1024], w[:, kv + 1280:kv + 1536]]
    w_b = [w[:, o_gv:o_ga], w[:, o_gg:o_gg + 1024], w[:, o_gq:o_gk], w[:, o_gk:o_gv],
           w[:, kv:kv + 512], misc]
    return jnp.concatenate(w_a, axis=1).astype(BF16), jnp.concatenate(w_b, axis=1).astype(BF16)


def hybrid_mixer_ln(x, mod, w_in_r, cmp_pe, cmp_w1, cmp_w2, wa_pad, ba, nw, w_out, lg, lb, tables, overlap):
    w_a, w_b = w_in_r
    qr, qn, ks, vs, kw, vw = nsa_proj(x, mod, w_a, *tables)
    proj = in_proj(x, mod, w_b)
    kc, vct = nsa_compress(proj, cmp_pe, cmp_w1, cmp_w2)
    nsa_o = nsa_attention(qr, qn, proj, kc, vct, ks, vs, kw, vw, overlap)
    gla_o = gla_mixer(proj, wa_pad, ba, nw)
    return out_proj_ln(nsa_o, gla_o, w_out, x, mod, lg, lb)


def kernel(x, c, w_ada, b_ada, w_in, cmp_pos_k, cmp_w1_k, cmp_w2_k, cmp_pos_v, cmp_w1_v, cmp_w2_v, gla_w_a2, gla_b_a, gla_norm_w, w_out, ln_mix_g, ln_mix_b, ln_ffn_g, ln_ffn_b, ffn_w_gate, ffn_w_up, ffn_w_down, moe_router, moe_w_gate, moe_w_up, moe_w_down):
    B, T, D = x.shape
    L = w_ada.shape[0]
    c_pad = jnp.zeros((8, D), F32).at[:B].set(c)
    mod_all = ada_mod(c_pad, w_ada, b_ada.reshape(L, 1, 6 * D))[:, :B].reshape(L, B, 6, D)
    tables = _rope_tables(T)
    overlap = _overlap_matrix_t(T // CMP_STRIDE, T // SLC_BLOCK)
    for layer in range(L):
        mod = mod_all[layer]
        wa_pad = jnp.zeros((LANES, GLA_HEADS * GLA_DK), F32).at[MISC_GA:MISC_GA + GLA_GATE_RANK].set(
            gla_w_a2[layer]).astype(BF16)
        x = hybrid_mixer_ln(
            x, mod, _reorder_w_in(w_in[layer]),
            jnp.stack([cmp_pos_k[layer], cmp_pos_v[layer]]),
            jnp.stack([cmp_w1_k[layer], cmp_w1_v[layer]]).astype(BF16),
            jnp.stack([cmp_w2_k[layer], cmp_w2_v[layer]]).astype(BF16),
            wa_pad, gla_b_a[layer].reshape(1, -1), gla_norm_w[layer].reshape(1, -1),
            w_out[layer].astype(BF16), ln_mix_g[layer].reshape(1, D), ln_mix_b[layer].reshape(1, D),
            tables, overlap)
        lg = ln_ffn_g[layer].reshape(1, D)
        lb = ln_ffn_b[layer].reshape(1, D)
        i = layer // 2
        if layer % 2 == 0:
            x = ffn_ln(x, mod, ffn_w_gate[i], ffn_w_up[i], ffn_w_down[i], lg, lb)
        else:
            x = moe_layer(x, mod, moe_router[i], moe_w_gate[i], moe_w_up[i], moe_w_down[i], lg, lb)
    return x
```

```python
import functools

import numpy as np
import jax
import jax.numpy as jnp
from jax import lax
from jax.experimental import pallas as pl
from jax.experimental.pallas import tpu as pltpu

F32 = jnp.float32
BF16 = jnp.bfloat16

D_MODEL = 2048
DEPTH = 2
HEAD_DIM = 128
NSA_HEADS = 8
NSA_KV_HEADS = 2
NSA_GROUP = 4
CMP_BLOCK = 32
CMP_STRIDE = 16
CMP_HIDDEN = 256
SLC_BLOCK = 64
SLC_TOPN = 16
WINDOW = 512
FORCE_SCORE = 1e9
GLA_DV = 256
GLA_HEADS = 4
GLA_DK = 128
GLA_GATE_RANK = 16
GLA_GATE_NORM = 16.0
GLA_CHUNK = 64
ROPE_THETA = 500000.0
ROPE_DIMS = 32
N_EXPERTS = 8
TOP_K = 2
LN_EPS = 1e-5
NORM_EPS = 1e-6
DEEPNORM_ALPHA = (2 * DEPTH) ** 0.25
NEG_BIG = -1e30
MASK_BIG = 2.0 ** 100
V_AUG_ROWS = HEAD_DIM + 16
NSA_HEADS_PER_STEP = 2
NSA_TQ = 128
NSA_TK = 512
ROW_TILE = 512
EPILOGUE_SUB = 128
ADA_TN = 2048
MOE_TF = 512
GLA_TILE = 512
LOG2_E = 1.4426950408889634

VMEM_LIMIT_BYTES = 56 * 1024 * 1024
BIG_VMEM_LIMIT_BYTES = 60 * 1024 * 1024
LANES = 128

A_NQ = 0
A_KS = 1024
A_KW = 1280
A_VS = 1536
A_VW = 1792
A_W = 2048
C_GV = 0
C_GG = 1024
C_GQ = 2048
C_GK = 2560
C_KC = 3072
C_VC = 3328
C_MISC = 3584
PROJ_W = 3712
MISC_GA = 24

NT_DIMS = (((1,), (1,)), ((), ()))


def _params(*sem):
    return pltpu.CompilerParams(dimension_semantics=sem, vmem_limit_bytes=VMEM_LIMIT_BYTES)


def _layer_norm(z, g, b):
    mu = jnp.mean(z, axis=-1, keepdims=True)
    zc = z - mu
    var = jnp.mean(zc * zc, axis=-1, keepdims=True)
    return zc * lax.rsqrt(var + LN_EPS) * g + b


def _ada_kernel(c_ref, w_ref, b_ref, o_ref):
    c = c_ref[...]
    cond = c * jax.nn.sigmoid(c)
    o_ref[0] = jnp.dot(cond.astype(BF16), w_ref[0].astype(BF16),
                       preferred_element_type=F32) + b_ref[0]


def ada_mod(c_pad, w_ada, b_ada):
    L, D, N = w_ada.shape
    tn = ADA_TN
    return pl.pallas_call(
        _ada_kernel,
        grid=(L, N // tn),
        in_specs=[pl.BlockSpec((8, D), lambda l, j: (0, 0)),
                  pl.BlockSpec((1, D, tn), lambda l, j: (l, 0, j)),
                  pl.BlockSpec((1, 1, tn), lambda l, j: (l, 0, j))],
        out_specs=pl.BlockSpec((1, 8, tn), lambda l, j: (l, 0, j)),
        out_shape=jax.ShapeDtypeStruct((L, 8, N), F32),
        compiler_params=_params("parallel", "parallel"),
        name="ada_mod",
    )(c_pad, w_ada, b_ada)


def _inproj_kernel(x_ref, m_ref, w_ref, o_ref):
    h = x_ref[0] * (1.0 + m_ref[0, 1:2, :]) + m_ref[0, 0:1, :]
    o_ref[0] = jnp.dot(h.astype(BF16), w_ref[...], preferred_element_type=F32)


def in_proj(x, mod, w):
    B, T, D = x.shape
    N = w.shape[1]
    tm = min(ROW_TILE, T)
    return pl.pallas_call(
        _inproj_kernel,
        grid=(B, T // tm),
        in_specs=[pl.BlockSpec((1, tm, D), lambda b, i: (b, i, 0)),
                  pl.BlockSpec((1, 6, D), lambda b, i: (b, 0, 0)),
                  pl.BlockSpec((D, N), lambda b, i: (0, 0))],
        out_specs=pl.BlockSpec((1, tm, N), lambda b, i: (b, i, 0)),
        out_shape=jax.ShapeDtypeStruct((B, T, N), F32),
        compiler_params=_params("parallel", "parallel"),
        name="in_proj",
    )(x, mod, w)


def _nsa_proj_kernel(x_ref, m_ref, w_ref, rc_ref, ra_ref, rb_ref,
                     qr_ref, qn_ref, ks_ref, vs_ref, kw_ref, vw_ref, *, sub):
    tm = x_ref.shape[1]
    scale = HEAD_DIM ** -0.5 * LOG2_E
    for r in range(tm // sub):
        rows = slice(r * sub, (r + 1) * sub)
        rc = rc_ref[rows, :]
        ra = ra_ref[rows, :]
        rb = rb_ref[rows, :]

        def rope(xh):
            return (xh * rc + pltpu.roll(xh, LANES - ROPE_DIMS // 2, 1) * ra
                    + pltpu.roll(xh, ROPE_DIMS // 2, 1) * rb)

        h = x_ref[0, rows, :] * (1.0 + m_ref[0, 1:2, :]) + m_ref[0, 0:1, :]
        p = jnp.dot(h.astype(BF16), w_ref[...], preferred_element_type=F32)
        for hq in range(NSA_HEADS):
            xh = p[:, A_NQ + hq * HEAD_DIM:A_NQ + (hq + 1) * HEAD_DIM]
            qr_ref[0, rows, hq * HEAD_DIM:(hq + 1) * HEAD_DIM] = (rope(xh) * scale).astype(BF16)
            qn_ref[0, rows, hq * HEAD_DIM:(hq + 1) * HEAD_DIM] = (xh * scale).astype(BF16)
        key_blk = (pl.program_id(1) * tm + r * sub + lax.broadcasted_iota(jnp.int32, (sub, LANES), 0)) // SLC_BLOCK
        blk_onehot = jnp.where(key_blk == lax.broadcasted_iota(jnp.int32, (sub, LANES), 1), 1.0, 0.0).astype(BF16)
        for hk in range(NSA_KV_HEADS):
            ks_ref[0, hk, rows, 0:HEAD_DIM] = rope(p[:, A_KS + hk * HEAD_DIM:A_KS + (hk + 1) * HEAD_DIM]).astype(BF16)
            ks_ref[0, hk, rows, HEAD_DIM:] = blk_onehot
            kw_ref[0, hk, rows, :] = rope(p[:, A_KW + hk * HEAD_DIM:A_KW + (hk + 1) * HEAD_DIM]).astype(BF16)
            vs_ref[0, hk, 0:HEAD_DIM, rows] = p[:, A_VS + hk * HEAD_DIM:A_VS + (hk + 1) * HEAD_DIM].T.astype(BF16)
            vs_ref[0, hk, HEAD_DIM:, rows] = jnp.ones((V_AUG_ROWS - HEAD_DIM, sub), BF16)
            vw_ref[0, hk, :, rows] = p[:, A_VW + hk * HEAD_DIM:A_VW + (hk + 1) * HEAD_DIM].T.astype(BF16)


def nsa_proj(x, mod, w, rc, ra, rb):
    B, T, D = x.shape
    tm = min(ROW_TILE, T)
    assert T // SLC_BLOCK <= LANES
    qw = NSA_HEADS * HEAD_DIM
    qspec = pl.BlockSpec((1, tm, qw), lambda b, i: (b, i, 0))
    tspec = pl.BlockSpec((tm, LANES), lambda b, i: (i, 0))
    return pl.pallas_call(
        functools.partial(_nsa_proj_kernel, sub=min(EPILOGUE_SUB, tm)),
        grid=(B, T // tm),
        in_specs=[pl.BlockSpec((1, tm, D), lambda b, i: (b, i, 0)),
                  pl.BlockSpec((1, 6, D), lambda b, i: (b, 0, 0)),
                  pl.BlockSpec((D, A_W), lambda b, i: (0, 0)),
                  tspec, tspec, tspec],
        out_specs=[qspec, qspec,
                   pl.BlockSpec((1, NSA_KV_HEADS, tm, HEAD_DIM + LANES), lambda b, i: (b, 0, i, 0)),
                   pl.BlockSpec((1, NSA_KV_HEADS, V_AUG_ROWS, tm), lambda b, i: (b, 0, 0, i)),
                   pl.BlockSpec((1, NSA_KV_HEADS, tm, HEAD_DIM), lambda b, i: (b, 0, i, 0)),
                   pl.BlockSpec((1, NSA_KV_HEADS, HEAD_DIM, tm), lambda b, i: (b, 0, 0, i))],
        out_shape=[jax.ShapeDtypeStruct((B, T, qw), BF16),
                   jax.ShapeDtypeStruct((B, T, qw), BF16),
                   jax.ShapeDtypeStruct((B, NSA_KV_HEADS, T, HEAD_DIM + LANES), BF16),
                   jax.ShapeDtypeStruct((B, NSA_KV_HEADS, V_AUG_ROWS, T), BF16),
                   jax.ShapeDtypeStruct((B, NSA_KV_HEADS, T, HEAD_DIM), BF16),
                   jax.ShapeDtypeStruct((B, NSA_KV_HEADS, HEAD_DIM, T), BF16)],
        compiler_params=_params("parallel", "parallel"),
        name="nsa_proj",
    )(x, mod, w, rc, ra, rb)


def _compress_kernel(a_ref, pe_ref, w1_ref, w2_ref, o_ref, ot_ref, *, n_half):
    half = CMP_STRIDE

    def part(l0):
        acc = jnp.zeros((n_half, CMP_HIDDEN), F32)
        for l in range(half):
            rows = a_ref[0, pl.ds(l, n_half, stride=half), :] + pe_ref[0, l0 + l:l0 + l + 1, :]
            acc += jnp.dot(rows.astype(BF16), w1_ref[0, (l0 + l) * HEAD_DIM:(l0 + l + 1) * HEAD_DIM, :],
                           preferred_element_type=F32)
        return acc

    first = part(0)
    second = part(half)
    hid = first + pltpu.roll(second, n_half - 1, 0)
    row = lax.broadcasted_iota(jnp.int32, (n_half, 1), 0)
    hid = jnp.where(row < n_half - 1, hid, 0.0)
    act = jax.nn.gelu(hid)
    out = jnp.dot(act.astype(BF16), w2_ref[0], preferred_element_type=F32)
    o_ref[0, 0, 0] = out.astype(BF16)
    ot_ref[0, 0, 0] = out.T.astype(BF16)


def nsa_compress(proj, pe, w1, w2):
    B, T, _ = proj.shape
    n_half = T // CMP_STRIDE
    return pl.pallas_call(
        functools.partial(_compress_kernel, n_half=n_half),
        grid=(B, 2, NSA_KV_HEADS),
        in_specs=[pl.BlockSpec((1, T, HEAD_DIM), lambda b, s, h: (b, 0, C_KC // HEAD_DIM + s * NSA_KV_HEADS + h)),
                  pl.BlockSpec((1, CMP_BLOCK, HEAD_DIM), lambda b, s, h: (s, 0, 0)),
                  pl.BlockSpec((1, CMP_BLOCK * HEAD_DIM, CMP_HIDDEN), lambda b, s, h: (s, 0, 0)),
                  pl.BlockSpec((1, CMP_HIDDEN, HEAD_DIM), lambda b, s, h: (s, 0, 0))],
        out_specs=[pl.BlockSpec((1, 1, 1, n_half, HEAD_DIM), lambda b, s, h: (b, s, h, 0, 0)),
                   pl.BlockSpec((1, 1, 1, HEAD_DIM, n_half), lambda b, s, h: (b, s, h, 0, 0))],
        out_shape=[jax.ShapeDtypeStruct((B, 2, NSA_KV_HEADS, n_half, HEAD_DIM), BF16),
                   jax.ShapeDtypeStruct((B, 2, NSA_KV_HEADS, HEAD_DIM, n_half), BF16)],
        compiler_params=_params("parallel", "parallel", "parallel"),
        name="nsa_compress",
    )(proj, pe, w1, w2)


def _nsa_kernel(qr_ref, qn_ref, g_ref, kc_ref, vct_ref, ks_ref, vst_ref, kw_ref, vwt_ref, ovt_ref, o_ref,
                s_buf, smax_buf, m_buf, acc_buf, *, TQ, TK, WK, NC, NB, NH):
    G = NSA_GROUP
    R = G * TQ
    NBP = ovt_ref.shape[0]
    i = pl.program_id(2)
    t0 = i * TQ
    m_floor = 0.5 * NEG_BIG

    def stack(ref, h):
        x = ref[0]
        return jnp.concatenate([x[:, (h * G + g) * HEAD_DIM:(h * G + g + 1) * HEAD_DIM] for g in range(G)], axis=0)

    def tile_g(x):
        return jnp.concatenate([x] * G, axis=1)

    tT = t0 + lax.broadcasted_iota(jnp.int32, (1, TQ), 1)
    cur = tT // SLC_BLOCK
    n_sel = min(SLC_TOPN, NB)

    HS = range(NH)
    qr = [stack(qr_ref, h) for h in HS]
    qn = [stack(qn_ref, h) for h in HS]

    ws = pl.multiple_of(jnp.maximum(t0 + TQ - WK, 0), LANES)
    s_c = [lax.dot_general(kc_ref[0, 0, h], qn[h], NT_DIMS, preferred_element_type=F32) for h in HS]
    s_w = [lax.dot_general(kw_ref[0, h, pl.ds(ws, WK), :], qr[h], NT_DIMS, preferred_element_type=F32)
           for h in HS]
    cend = lax.broadcasted_iota(jnp.int32, (NC, 1), 0) * CMP_STRIDE + (CMP_BLOCK - 1)
    bias_c = tile_g(jnp.where(cend <= tT, 0.0, NEG_BIG))
    dist = tT - (ws + lax.broadcasted_iota(jnp.int32, (WK, 1), 0))
    bias_w = tile_g(jnp.where(dist >= 0, jnp.where(dist < WINDOW, 0.0, NEG_BIG), NEG_BIG))

    p_cb, e_wb, l_w = [], [], []
    for h in HS:
        sc = s_c[h] + bias_c
        m_c = jnp.maximum(jnp.max(sc, axis=0, keepdims=True), m_floor)
        e_c = jnp.exp2(sc - m_c)
        den = jnp.sum(e_c, axis=0, keepdims=True)
        p_cb.append((e_c * jnp.where(den > 0.0, 1.0 / den, 0.0)).astype(BF16))
        sw = s_w[h] + bias_w
        e_w = jnp.exp2(sw - jnp.max(sw, axis=0, keepdims=True))
        l_w.append(jnp.sum(e_w, axis=0, keepdims=True))
        e_wb.append(e_w.astype(BF16))

    o_c = [jnp.dot(vct_ref[0, 0, h], p_cb[h], preferred_element_type=F32) for h in HS]
    imp4 = [jnp.dot(ovt_ref[...], p_cb[h], preferred_element_type=F32) for h in HS]
    o_w = [jnp.dot(vwt_ref[0, h, :, pl.ds(ws, WK)], e_wb[h], preferred_element_type=F32) * (1.0 / l_w[h])
           for h in HS]

    jj = lax.broadcasted_iota(jnp.int32, (NBP, TQ), 0)
    forced = jnp.where(jj == 0, 1, jnp.where(jj == cur, 1, jnp.where(jj == cur - 1, 1, 0)))
    val = []
    for h in HS:
        impT = imp4[h][:, 0:TQ]
        for g in range(1, G):
            impT = impT + imp4[h][:, g * TQ:(g + 1) * TQ]
        val.append(jnp.where(jj > cur, -jnp.inf, jnp.where(forced > 0, FORCE_SCORE, impT)))
    SUBL = 8
    chunks = [[val[h][c * SUBL:(c + 1) * SUBL, :] for c in range(NBP // SUBL)] for h in HS]
    rank_c = [[jnp.zeros((SUBL, TQ), jnp.int32) for c in range(NBP // SUBL)] for h in HS]
    jj_c = lax.broadcasted_iota(jnp.int32, (SUBL, TQ), 0)
    for j2 in range(NB):
        for h in HS:
            row = val[h][j2:j2 + 1, :]
            for c in range(NBP // SUBL):
                v = chunks[h][c]
                if c * SUBL > j2:
                    beat = jnp.where(row >= v, 1, 0)
                elif (c + 1) * SUBL - 1 < j2:
                    beat = jnp.where(row > v, 1, 0)
                else:
                    later = jnp.where(jj_c + c * SUBL > j2, 1, 0)
                    beat = jnp.where(row > v, 1, jnp.where(row == v, later, 0))
                rank_c[h][c] = rank_c[h][c] + beat
    rank = [jnp.concatenate(rank_c[h], axis=0) for h in HS]
    fronts = []
    for h in HS:
        sel_neg = jnp.where(jj > cur, -MASK_BIG, jnp.where(rank[h] < n_sel, 0.0, -MASK_BIG))
        if NBP < LANES:
            sel_neg = jnp.concatenate([sel_neg, jnp.zeros((LANES - NBP, TQ), F32)], axis=0)
        sel_q = sel_neg.T.astype(BF16)
        q_aug = jnp.concatenate([qr[h], jnp.concatenate([sel_q] * G, axis=0)], axis=1)
        fronts.append((o_c[h], o_w[h], q_aug))

    def sel_scores(h, kt, slot):
        k0 = pl.multiple_of(kt * TK, TK)
        k = ks_ref[0, h, pl.ds(k0, TK), :]
        s = lax.dot_general(k, fronts[h][2], NT_DIMS, preferred_element_type=F32)
        s_buf[h, slot] = s
        smax_buf[h, slot] = jnp.max(s, axis=0, keepdims=True)

    def sel_accumulate(h, kt, slot):
        k0 = pl.multiple_of(kt * TK, TK)
        vt = vst_ref[0, h, :, pl.ds(k0, TK)]
        m = m_buf[h]
        m_new = jnp.maximum(m, smax_buf[h, slot])
        p = jnp.exp2(s_buf[h, slot] - m_new).astype(BF16)
        acc_buf[h] = jnp.exp2(m - m_new) * acc_buf[h] + jnp.dot(vt, p, preferred_element_type=F32)
        m_buf[h] = m_new

    n_kt = (t0 + TQ + TK - 1) // TK
    n_loop = n_kt - 1
    for h in HS:
        m_buf[h] = jnp.full((1, R), m_floor, F32)
        acc_buf[h] = jnp.zeros((V_AUG_ROWS, R), F32)
        sel_scores(h, 0, 0)

    def pair_body(k, carry):
        for h in HS:
            sel_scores(h, 2 * k + 1, 1)
            sel_accumulate(h, 2 * k, 0)

        @pl.when(2 * k + 1 < n_loop)
        def _():
            for h in HS:
                sel_scores(h, 2 * k + 2, 0)
                sel_accumulate(h, 2 * k + 1, 1)
        return carry

    lax.fori_loop(0, (n_loop + 1) // 2, pair_body, 0)
    krow = n_loop * TK + lax.broadcasted_iota(jnp.int32, (TK, 1), 0)
    causal = tile_g(jnp.where(krow <= tT, 0.0, NEG_BIG))
    k_last = pl.multiple_of(n_loop * TK, TK)
    last_slot = n_loop % 2
    s_fin = [s_buf[h, last_slot] + causal for h in HS]
    m_fin = [jnp.maximum(m_buf[h], jnp.max(s_fin[h], axis=0, keepdims=True)) for h in HS]
    p_fin = [jnp.exp2(s_fin[h] - m_fin[h]).astype(BF16) for h in HS]
    acc_fin = [jnp.exp2(m_buf[h] - m_fin[h]) * acc_buf[h]
               + jnp.dot(vst_ref[0, h, :, pl.ds(k_last, TK)], p_fin[h], preferred_element_type=F32) for h in HS]
    gate_all = jax.nn.sigmoid(g_ref[0]).T
    assert NH == NSA_KV_HEADS
    gates = [gate_all[3 * G * h:3 * G * (h + 1), :] for h in HS]
    for h in HS:
        o_s = acc_fin[h][0:HEAD_DIM] * (1.0 / acc_fin[h][HEAD_DIM:HEAD_DIM + 1])
        o_c, o_w, _ = fronts[h]
        gate = gates[h]
        for g in range(G):
            cols = slice(g * TQ, (g + 1) * TQ)
            o = (gate[3 * g:3 * g + 1, :] * o_c[:, cols] + gate[3 * g + 1:3 * g + 2, :] * o_s[:, cols]
                 + gate[3 * g + 2:3 * g + 3, :] * o_w[:, cols])
            o_ref[0, :, (h * G + g) * HEAD_DIM:(h * G + g + 1) * HEAD_DIM] = o.T.astype(BF16)


def nsa_attention(qr, qn, gates, kc, vct, ks, vst, kw, vwt, overlap_t):
    B, T, _ = qr.shape
    TQ = NSA_TQ
    TK = min(NSA_TK, T)
    WK = min(WINDOW + TQ, T)
    NC = kc.shape[3]
    NB = T // SLC_BLOCK
    NH = NSA_HEADS_PER_STEP
    gw = NH * NSA_GROUP * HEAD_DIM
    qspec = pl.BlockSpec((1, TQ, gw), lambda b, h, i: (b, i, h))
    kspec = pl.BlockSpec((1, NH, T, HEAD_DIM), lambda b, h, i: (b, h, 0, 0))
    vtspec = pl.BlockSpec((1, NH, HEAD_DIM, T), lambda b, h, i: (b, h, 0, 0))
    return pl.pallas_call(
        functools.partial(_nsa_kernel, TQ=TQ, TK=TK, WK=WK, NC=NC, NB=NB, NH=NH),
        grid=(B, NSA_KV_HEADS // NH, T // TQ),
        in_specs=[qspec, qspec,
                  pl.BlockSpec((1, TQ, LANES), lambda b, h, i: (b, i, C_MISC // LANES)),
                  pl.BlockSpec((1, 1, NH, NC, HEAD_DIM), lambda b, h, i: (b, 0, h, 0, 0)),
                  pl.BlockSpec((1, 1, NH, HEAD_DIM, NC), lambda b, h, i: (b, 1, h, 0, 0)),
                  pl.BlockSpec((1, NH, T, HEAD_DIM + LANES), lambda b, h, i: (b, h, 0, 0)),
                  pl.BlockSpec((1, NH, V_AUG_ROWS, T), lambda b, h, i: (b, h, 0, 0)),
                  kspec, vtspec,
                  pl.BlockSpec(overlap_t.shape, lambda b, h, i: (0, 0))],
        out_specs=pl.BlockSpec((1, TQ, gw), lambda b, h, i: (b, i, h)),
        out_shape=jax.ShapeDtypeStruct((B, T, NSA_HEADS * HEAD_DIM), BF16),
        scratch_shapes=[pltpu.VMEM((NH, 2, TK, NSA_GROUP * TQ), F32),
                        pltpu.VMEM((NH, 2, 1, NSA_GROUP * TQ), F32),
                        pltpu.VMEM((NH, 1, NSA_GROUP * TQ), F32),
                        pltpu.VMEM((NH, V_AUG_ROWS, NSA_GROUP * TQ), F32)],
        compiler_params=_params("parallel", "parallel", "arbitrary"),
        name="nsa_attention",
    )(qr, qn, gates, kc, vct, ks, vst, kw, vwt, overlap_t)


def _gla_kernel(q_ref, k_ref, v_ref, gg_ref, misc_ref, wa_ref, ba_ref, nw_ref, o_ref, s_ref, la_ref, *, TC):
    C = GLA_CHUNK

    @pl.when(pl.program_id(1) == 0)
    def _():
        s_ref[...] = jnp.zeros_like(s_ref)

    z = jnp.dot(misc_ref[0].astype(BF16), wa_ref[...], preferred_element_type=F32) + ba_ref[...]
    la_ref[...] = (jnp.minimum(z, 0.0) - jnp.log1p(jnp.exp(-jnp.abs(z)))) * (1.0 / GLA_GATE_NORM)
    ri = lax.broadcasted_iota(jnp.int32, (C, C), 0)
    ci = lax.broadcasted_iota(jnp.int32, (C, C), 1)
    causal = ri >= ci
    tri = jnp.where(causal, 1.0, 0.0)
    nw = nw_ref[...]

    n_c = TC // C
    pairs = [(c, h) for c in range(n_c) for h in range(GLA_HEADS)]
    rows = [slice(c * C, (c + 1) * C) for c in range(n_c)]
    b_all = [jnp.dot(tri, la_ref[rows[c], :], preferred_element_type=F32, precision=lax.Precision.HIGHEST)
             for c in range(n_c)]
    qd, kv, decay, intra = {}, {}, {}, {}
    for c, h in pairs:
        b = b_all[c][:, h * GLA_DK:(h + 1) * GLA_DK]
        bl = b[C - 1:C, :]
        q = q_ref[0, rows[c], h * GLA_DK:(h + 1) * GLA_DK] * (GLA_DK ** -0.5)
        k = k_ref[0, rows[c], h * GLA_DK:(h + 1) * GLA_DK]
        v = v_ref[0, rows[c], h * GLA_DV:(h + 1) * GLA_DV].astype(BF16)
        qd[c, h] = (q * jnp.exp(b)).astype(BF16)
        ki = (k * jnp.exp(-b)).astype(BF16)
        kst = (k * jnp.exp(bl - b)).T.astype(BF16)
        decay[c, h] = jnp.exp(b.T[:, C - 1:C])
        a = lax.dot_general(qd[c, h], ki, NT_DIMS, preferred_element_type=F32)
        a = jnp.where(causal, a, 0.0).astype(BF16)
        intra[c, h] = jnp.dot(a, v, preferred_element_type=F32)
        kv[c, h] = jnp.dot(kst, v, preferred_element_type=F32)
    state = [s_ref[h] for h in range(GLA_HEADS)]
    for c, h in pairs:
        o = intra[c, h] + jnp.dot(qd[c, h], state[h].astype(BF16), preferred_element_type=F32)
        state[h] = state[h] * decay[c, h] + kv[c, h]
        o = o * lax.rsqrt(jnp.mean(o * o, axis=-1, keepdims=True) + NORM_EPS) * nw
        gg = gg_ref[0, rows[c], h * GLA_DV:(h + 1) * GLA_DV]
        o_ref[0, rows[c], h * GLA_DV:(h + 1) * GLA_DV] = (o * (gg * jax.nn.sigmoid(gg))).astype(BF16)
    for h in range(GLA_HEADS):
        s_ref[h] = state[h]


def gla_mixer(proj, wa_pad, ba, nw):
    B, T, _ = proj.shape
    TC = min(GLA_TILE, T)
    qk_w = GLA_HEADS * GLA_DK
    v_w = GLA_HEADS * GLA_DV
    return pl.pallas_call(
        functools.partial(_gla_kernel, TC=TC),
        grid=(B, T // TC),
        in_specs=[pl.BlockSpec((1, TC, qk_w), lambda b, i: (b, i, C_GQ // qk_w)),
                  pl.BlockSpec((1, TC, qk_w), lambda b, i: (b, i, C_GK // qk_w)),
                  pl.BlockSpec((1, TC, v_w), lambda b, i: (b, i, C_GV // v_w)),
                  pl.BlockSpec((1, TC, v_w), lambda b, i: (b, i, C_GG // v_w)),
                  pl.BlockSpec((1, TC, LANES), lambda b, i: (b, i, C_MISC // LANES)),
                  pl.BlockSpec((LANES, qk_w), lambda b, i: (0, 0)),
                  pl.BlockSpec((1, qk_w), lambda b, i: (0, 0)),
                  pl.BlockSpec((1, GLA_DV), lambda b, i: (0, 0))],
        out_specs=pl.BlockSpec((1, TC, v_w), lambda b, i: (b, i, 0)),
        out_shape=jax.ShapeDtypeStruct((B, T, v_w), BF16),
        scratch_shapes=[pltpu.VMEM((GLA_HEADS, GLA_DK, GLA_DV), F32),
                        pltpu.VMEM((TC, qk_w), F32)],
        compiler_params=_params("parallel", "arbitrary"),
        name="gla_mixer",
    )(proj, proj, proj, proj, proj, wa_pad, ba, nw)


def _outproj_kernel(n_ref, g_ref, w_ref, x_ref, m_ref, lg_ref, lb_ref, o_ref, *, half):
    sub = min(EPILOGUE_SUB, o_ref.shape[1])
    for r in range(o_ref.shape[1] // sub):
        rows = slice(r * sub, (r + 1) * sub)
        y = (jnp.dot(n_ref[0, rows, :], w_ref[0:half, :], preferred_element_type=F32)
             + jnp.dot(g_ref[0, rows, :], w_ref[half:, :], preferred_element_type=F32))
        z = DEEPNORM_ALPHA * x_ref[0, rows, :] + (1.0 + m_ref[0, 2:3, :]) * y
        o_ref[0, rows, :] = _layer_norm(z, lg_ref[...], lb_ref[...])


def out_proj_ln(nsa_o, gla_o, w, x, mod, lg, lb):
    B, T, D = x.shape
    half = nsa_o.shape[-1]
    tm = min(ROW_TILE, T)
    vec = pl.BlockSpec((1, D), lambda b, i: (0, 0))
    return pl.pallas_call(
        functools.partial(_outproj_kernel, half=half),
        grid=(B, T // tm),
        in_specs=[pl.BlockSpec((1, tm, half), lambda b, i: (b, i, 0)),
                  pl.BlockSpec((1, tm, gla_o.shape[-1]), lambda b, i: (b, i, 0)),
                  pl.BlockSpec(w.shape, lambda b, i: (0, 0)),
                  pl.BlockSpec((1, tm, D), lambda b, i: (b, i, 0)),
                  pl.BlockSpec((1, 6, D), lambda b, i: (b, 0, 0)),
                  vec, vec],
        out_specs=pl.BlockSpec((1, tm, D), lambda b, i: (b, i, 0)),
        out_shape=jax.ShapeDtypeStruct((B, T, D), F32),
        compiler_params=_params("parallel", "parallel"),
        name="out_proj_ln",
    )(nsa_o, gla_o, w, x, mod, lg, lb)


FFN_TM = 1024
FFN_TF = 256
FFN_SUB = 512


def _ffn_kernel(x_ref, m_ref, wg_ref, wu_ref, wd_ref, lg_ref, lb_ref, o_ref, h_sc, *, sub):
    j = pl.program_id(2)
    last = pl.num_programs(2) - 1
    tm = o_ref.shape[1]

    def step(first, final):
        rows_per = sub // 2 if (first or final) else sub
        wg = wg_ref[...].astype(BF16)
        wu = wu_ref[...].astype(BF16)
        wd = wd_ref[...].astype(BF16)
        for sb in range(tm // rows_per):
            rows = slice(sb * rows_per, (sb + 1) * rows_per)
            if first:
                h_sc[rows, :] = (x_ref[0, rows, :] * (1.0 + m_ref[0, 4:5, :]) + m_ref[0, 3:4, :]).astype(BF16)
            h = h_sc[rows, :]
            a = jnp.dot(h, wg, preferred_element_type=F32)
            u = jnp.dot(h, wu, preferred_element_type=F32)
            y = jnp.dot((a * jax.nn.sigmoid(a) * u).astype(BF16), wd, preferred_element_type=F32)
            if first:
                o_ref[0, rows, :] = y
            elif final:
                z = DEEPNORM_ALPHA * x_ref[0, rows, :] + (1.0 + m_ref[0, 5:6, :]) * (o_ref[0, rows, :] + y)
                o_ref[0, rows, :] = _layer_norm(z, lg_ref[...], lb_ref[...])
            else:
                o_ref[0, rows, :] += y

    pl.when(j == 0)(lambda: step(True, False))
    pl.when(jnp.logical_and(j > 0, j < last))(lambda: step(False, False))
    pl.when(j == last)(lambda: step(False, True))


def ffn_ln(x, mod, wg, wu, wd, lg, lb):
    B, T, D = x.shape
    F = wg.shape[1]
    tm = min(FFN_TM, T)
    tf = FFN_TF
    vec = pl.BlockSpec((1, D), lambda b, i, j: (0, 0))
    return pl.pallas_call(
        functools.partial(_ffn_kernel, sub=min(FFN_SUB, tm)),
        grid=(B, T // tm, F // tf),
        in_specs=[pl.BlockSpec((1, tm, D), lambda b, i, j: (b, i, 0)),
                  pl.BlockSpec((1, 6, D), lambda b, i, j: (b, 0, 0)),
                  pl.BlockSpec((D, tf), lambda b, i, j: (0, j)),
                  pl.BlockSpec((D, tf), lambda b, i, j: (0, j)),
                  pl.BlockSpec((tf, D), lambda b, i, j: (j, 0)),
                  vec, vec],
        out_specs=pl.BlockSpec((1, tm, D), lambda b, i, j: (b, i, 0)),
        out_shape=jax.ShapeDtypeStruct((B, T, D), F32),
        scratch_shapes=[pltpu.VMEM((tm, D), BF16)],
        compiler_params=pltpu.CompilerParams(dimension_semantics=("parallel", "parallel", "arbitrary"),
                                             vmem_limit_bytes=BIG_VMEM_LIMIT_BYTES),
        name="ffn_ln",
    )(x, mod, wg, wu, wd, lg, lb)


def _router_kernel(x_ref, m_ref, wr_ref, h_ref, lg_ref):
    h = x_ref[0] * (1.0 + m_ref[0, 4:5, :]) + m_ref[0, 3:4, :]
    h_hi = h.astype(BF16)
    h_lo = (h - h_hi.astype(F32)).astype(BF16)
    h_ref[0] = h_hi
    lg_ref[0] = (jnp.dot(h_hi, wr_ref[0], preferred_element_type=F32)
                 + jnp.dot(h_lo, wr_ref[0], preferred_element_type=F32)
                 + jnp.dot(h_hi, wr_ref[1], preferred_element_type=F32))


def moe_router(x, mod, wr_pad):
    B, T, D = x.shape
    tm = min(ROW_TILE, T)
    return pl.pallas_call(
        _router_kernel,
        grid=(B, T // tm),
        in_specs=[pl.BlockSpec((1, tm, D), lambda b, i: (b, i, 0)),
                  pl.BlockSpec((1, 6, D), lambda b, i: (b, 0, 0)),
                  pl.BlockSpec((2, D, LANES), lambda b, i: (0, 0, 0))],
        out_specs=[pl.BlockSpec((1, tm, D), lambda b, i: (b, i, 0)),
                   pl.BlockSpec((1, tm, LANES), lambda b, i: (b, i, 0))],
        out_shape=[jax.ShapeDtypeStruct((B, T, D), BF16),
                   jax.ShapeDtypeStruct((B, T, LANES), F32)],
        compiler_params=_params("parallel", "parallel"),
        name="moe_router",
    )(x, mod, wr_pad)


def _moe_kernel(te_ref, nv_ref, x_ref, wg_ref, wu_ref, wd_ref, o_ref, acc_ref, *, sub):
    i = pl.program_id(0)
    j = pl.program_id(1)
    nv = nv_ref[i]

    @pl.when(j == 0)
    def _():
        acc_ref[...] = jnp.zeros_like(acc_ref)

    n_sub = acc_ref.shape[0] // sub

    def swiglu_rows(rows, wg, wu, wd):
        h = x_ref[rows, :]
        a = jnp.dot(h, wg, preferred_element_type=F32)
        u = jnp.dot(h, wu, preferred_element_type=F32)
        acc_ref[rows, :] += jnp.dot((a * jax.nn.sigmoid(a) * u).astype(BF16), wd, preferred_element_type=F32)

    for n_real in range(1, n_sub + 1):
        @pl.when(nv == n_real)
        def _():
            wg = wg_ref[0].astype(BF16)
            wu = wu_ref[0].astype(BF16)
            wd = wd_ref[0].astype(BF16)
            for r0 in range(0, n_real * sub, MOE_CHAIN):
                swiglu_rows(slice(r0, min(r0 + MOE_CHAIN, n_real * sub)), wg, wu, wd)

    @pl.when(j == pl.num_programs(1) - 1)
    def _():
        o_ref[...] = acc_ref[...].astype(o_ref.dtype)


def moe_experts(tile_e, tile_nv, xg, wg, wu, wd, tm, sub):
    M, D = xg.shape
    F = wg.shape[2]
    tf = MOE_TF
    nj = F // tf
    n_tiles = M // tm

    def wj(i, j, nv):
        return jnp.where(nv[i] > 0, j, nj - 1)

    return pl.pallas_call(
        functools.partial(_moe_kernel, sub=sub),
        grid_spec=pltpu.PrefetchScalarGridSpec(
            num_scalar_prefetch=2,
            grid=(n_tiles, nj),
            in_specs=[pl.BlockSpec((tm, D), lambda i, j, te, nv: (i, 0)),
                      pl.BlockSpec((1, D, tf), lambda i, j, te, nv: (te[i], 0, wj(i, j, nv))),
                      pl.BlockSpec((1, D, tf), lambda i, j, te, nv: (te[i], 0, wj(i, j, nv))),
                      pl.BlockSpec((1, tf, D), lambda i, j, te, nv: (te[i], wj(i, j, nv), 0))],
            out_specs=pl.BlockSpec((tm, D), lambda i, j, te, nv: (i, 0)),
            scratch_shapes=[pltpu.VMEM((tm, D), F32)]),
        out_shape=jax.ShapeDtypeStruct((M, D), BF16),
        compiler_params=pltpu.CompilerParams(dimension_semantics=("parallel", "arbitrary"),
                                             vmem_limit_bytes=BIG_VMEM_LIMIT_BYTES),
        name="moe_experts",
    )(tile_e, tile_nv, xg, wg, wu, wd)


def _combine_kernel(y0_ref, y1_ref, cw_ref, x_ref, m_ref, lg_ref, lb_ref, o_ref):
    cw = cw_ref[0]
    y = y0_ref[0].astype(F32) * cw[:, 0:1] + y1_ref[0].astype(F32) * cw[:, 1:2]
    z = DEEPNORM_ALPHA * x_ref[0] + (1.0 + m_ref[0, 5:6, :]) * y
    o_ref[0] = _layer_norm(z, lg_ref[...], lb_ref[...])


def moe_combine_ln(y0, y1, cw, x, mod, lg, lb):
    B, T, D = x.shape
    tm = min(ROW_TILE, T)
    row = pl.BlockSpec((1, tm, D), lambda b, i: (b, i, 0))
    vec = pl.BlockSpec((1, D), lambda b, i: (0, 0))
    return pl.pallas_call(
        _combine_kernel,
        grid=(B, T // tm),
        in_specs=[row, row, pl.BlockSpec((1, tm, LANES), lambda b, i: (b, i, 0)), row,
                  pl.BlockSpec((1, 6, D), lambda b, i: (b, 0, 0)), vec, vec],
        out_specs=row,
        out_shape=jax.ShapeDtypeStruct((B, T, D), F32),
        compiler_params=_params("parallel", "parallel"),
        name="moe_combine_ln",
    )(y0, y1, cw, x, mod, lg, lb)


MOE_TM = 1024
MOE_SUB = 256
MOE_CHAIN = 1024


def moe_layer(x, mod, w_router, wg, wu, wd, lg, lb):
    B, T, D = x.shape
    N = B * T
    A = N * TOP_K
    tm = MOE_TM
    wr_pad = jnp.zeros((D, LANES), F32).at[:, :N_EXPERTS].set(w_router)
    wr_hi = wr_pad.astype(BF16)
    wr_lo = (wr_pad - wr_hi.astype(F32)).astype(BF16)
    h, logits = moe_router(x, mod, jnp.stack([wr_hi, wr_lo]))
    logits = logits.reshape(N, LANES)[:, :N_EXPERTS]
    top_val, top_idx = lax.top_k(logits, TOP_K)
    comb = jax.nn.softmax(top_val, axis=-1)
    flat_e = top_idx.reshape(-1).astype(jnp.int32)
    onehot = (flat_e[:, None] == jnp.arange(N_EXPERTS, dtype=jnp.int32)[None, :]).astype(jnp.int32)
    csum = jnp.cumsum(onehot, axis=0)
    counts = csum[-1]
    padded = (counts + tm - 1) // tm * tm
    pad_end = jnp.cumsum(padded)
    pad_start = pad_end - padded
    slot = jnp.sum(onehot * (csum + pad_start[None, :]), axis=1) - 1
    n_tiles = -(-A // tm) + N_EXPERTS
    tile_start = jnp.arange(n_tiles, dtype=jnp.int32) * tm
    tile_e = jnp.minimum(jnp.searchsorted(pad_end, tile_start, side='right'), N_EXPERTS - 1).astype(jnp.int32)
    valid = jnp.clip(pad_start[tile_e] + counts[tile_e] - tile_start, 0, tm)
    tile_nv = ((valid + MOE_SUB - 1) // MOE_SUB).astype(jnp.int32)
    order = jnp.argsort(flat_e)
    seg_start = jnp.cumsum(counts) - counts
    slot_id = jnp.arange(n_tiles * tm, dtype=jnp.int32)
    within = slot_id - jnp.repeat(tile_start, tm)
    src = jnp.repeat(seg_start[tile_e] + tile_start - pad_start[tile_e], tm) + within
    slot_tok = jnp.where(within < jnp.repeat(valid, tm),
                         order[jnp.clip(src, 0, A - 1)].astype(jnp.int32) // TOP_K, slot_id % N)
    n_used = pad_end[-1] // tm
    tile_e = jnp.where(jnp.arange(n_tiles) < n_used, tile_e, tile_e[jnp.maximum(n_used - 1, 0)])
    xg = h.reshape(N, D)[slot_tok]
    y = moe_experts(tile_e, tile_nv, xg, wg, wu, wd, tm, MOE_SUB)
    slot_of = slot.reshape(N, TOP_K)
    y0 = y[slot_of[:, 0]].reshape(B, T, D)
    y1 = y[slot_of[:, 1]].reshape(B, T, D)
    cw = jnp.zeros((N, LANES), F32).at[:, :TOP_K].set(comb).reshape(B, T, LANES)
    return moe_combine_ln(y0, y1, cw, x, mod, lg, lb)


def _rope_tables(T):
    half = ROPE_DIMS // 2
    inv = ROPE_THETA ** (-jnp.arange(half, dtype=F32) * 2.0 / ROPE_DIMS)
    ang = jnp.arange(T).astype(F32)[:, None] * inv[None, :]
    cos, sin = jnp.cos(ang), jnp.sin(ang)
    z = jnp.zeros((T, LANES - ROPE_DIMS), F32)
    zh = jnp.zeros((T, half), F32)
    rc = jnp.concatenate([cos, cos, jnp.ones((T, LANES - ROPE_DIMS), F32)], axis=1)
    ra = jnp.concatenate([-sin, zh, z], axis=1)
    rb = jnp.concatenate([zh, sin, z], axis=1)
    return rc, ra, rb


def _overlap_matrix_t(NC, NB):
    nbp = -(-NB // 16) * 16
    c0 = np.arange(NC)[None, :] * CMP_STRIDE
    b0 = np.arange(nbp)[:, None] * SLC_BLOCK
    ov = (c0 < b0 + SLC_BLOCK) & (c0 + CMP_BLOCK > b0) & (np.arange(nbp)[:, None] < NB)
    return jnp.asarray(ov.astype(np.float32), dtype=BF16)


def _reorder_w_in(w):
    D = w.shape[0]
    o_ng = 1024 + 6 * 256
    o_gq = o_ng + 24
    o_gk = o_gq + 512
    o_gv = o_gk + 512
    o_ga = o_gv + 1024
    o_gg = o_ga + GLA_GATE_RANK
    misc = jnp.concatenate([w[:, o_ng:o_gq], w[:, o_ga:o_gg],
                            jnp.zeros((D, LANES - 24 - GLA_GATE_RANK), w.dtype)], axis=1)
    kv = 1024
    w_a = [w[:, 0:1024], w[:, kv + 512:kv + 768], w[:, kv + 1024:kv + 1280],
           w[:, kv + 768:kv + 1024], w[:, kv + 1280:kv + 1536]]
    w_b = [w[:, o_gv:o_ga], w[:, o_gg:o_gg + 1024], w[:, o_gq:o_gk], w[:, o_gk:o_gv],
           w[:, kv:kv + 512], misc]
    return jnp.concatenate(w_a, axis=1).astype(BF16), jnp.concatenate(w_b, axis=1).astype(BF16)


def hybrid_mixer_ln(x, mod, w_in_r, cmp_pe, cmp_w1, cmp_w2, wa_pad, ba, nw, w_out, lg, lb, tables, overlap):
    w_a, w_b = w_in_r
    qr, qn, ks, vs, kw, vw = nsa_proj(x, mod, w_a, *tables)
    proj = in_proj(x, mod, w_b)
    kc, vct = nsa_compress(proj, cmp_pe, cmp_w1, cmp_w2)
    nsa_o = nsa_attention(qr, qn, proj, kc, vct, ks, vs, kw, vw, overlap)
    gla_o = gla_mixer(proj, wa_pad, ba, nw)
    return out_proj_ln(nsa_o, gla_o, w_out, x, mod, lg, lb)


def kernel(x, c, w_ada, b_ada, w_in, cmp_pos_k, cmp_w1_k, cmp_w2_k, cmp_pos_v, cmp_w1_v, cmp_w2_v, gla_w_a2, gla_b_a, gla_norm_w, w_out, ln_mix_g, ln_mix_b, ln_ffn_g, ln_ffn_b, ffn_w_gate, ffn_w_up, ffn_w_down, moe_router, moe_w_gate, moe_w_up, moe_w_down):
    B, T, D = x.shape
    L = w_ada.shape[0]
    c_pad = jnp.zeros((8, D), F32).at[:B].set(c)
    mod_all = ada_mod(c_pad, w_ada, b_ada.reshape(L, 1, 6 * D))[:, :B].reshape(L, B, 6, D)
    tables = _rope_tables(T)
    overlap = _overlap_matrix_t(T // CMP_STRIDE, T // SLC_BLOCK)
    for layer in range(L):
        mod = mod_all[layer]
        wa_pad = jnp.zeros((LANES, GLA_HEADS * GLA_DK), F32).at[MISC_GA:MISC_GA + GLA_GATE_RANK].set(
            gla_w_a2[layer]).astype(BF16)
        x = hybrid_mixer_ln(
            x, mod, _reorder_w_in(w_in[layer]),
            jnp.stack([cmp_pos_k[layer], cmp_pos_v[layer]]),
            jnp.stack([cmp_w1_k[layer], cmp_w1_v[layer]]).astype(BF16),
            jnp.stack([cmp_w2_k[layer], cmp_w2_v[layer]]).astype(BF16),
            wa_pad, gla_b_a[layer].reshape(1, -1), gla_norm_w[layer].reshape(1, -1),
            w_out[layer].astype(BF16), ln_mix_g[layer].reshape(1, D), ln_mix_b[layer].reshape(1, D),
            tables, overlap)
        lg = ln_ffn_g[layer].reshape(1, D)
        lb = ln_ffn_b[layer].reshape(1, D)
        i = layer // 2
        if layer % 2 == 0:
            x = ffn_ln(x, mod, ffn_w_gate[i], ffn_w_up[i], ffn_w_down[i], lg, lb)
        else:
            x = moe_layer(x, mod, moe_router[i], moe_w_gate[i], moe_w_up[i], moe_w_down[i], lg, lb)
    return x
```

```python
import functools

import numpy as np
import jax
import jax.numpy as jnp
from jax import lax
from jax.experimental import pallas as pl
from jax.experimental.pallas import tpu as pltpu

F32 = jnp.float32
BF16 = jnp.bfloat16

D_MODEL = 2048
DEPTH = 2
HEAD_DIM = 128
NSA_HEADS = 8
NSA_KV_HEADS = 2
NSA_GROUP = 4
CMP_BLOCK = 32
CMP_STRIDE = 16
CMP_HIDDEN = 256
SLC_BLOCK = 64
SLC_TOPN = 16
WINDOW = 512
FORCE_SCORE = 1e9
GLA_DV = 256
GLA_HEADS = 4
GLA_DK = 128
GLA_GATE_RANK = 16
GLA_GATE_NORM = 16.0
GLA_CHUNK = 64
ROPE_THETA = 500000.0
ROPE_DIMS = 32
N_EXPERTS = 8
TOP_K = 2
LN_EPS = 1e-5
NORM_EPS = 1e-6
DEEPNORM_ALPHA = (2 * DEPTH) ** 0.25
NEG_BIG = -1e30
MASK_BIG = 2.0 ** 100
V_AUG_ROWS = HEAD_DIM + 16
NSA_HEADS_PER_STEP = 2
NSA_TQ = 128
NSA_TK = 512
ROW_TILE = 512
EPILOGUE_SUB = 128
ADA_TN = 2048
MOE_TF = 512
GLA_TILE = 512
LOG2_E = 1.4426950408889634

VMEM_LIMIT_BYTES = 56 * 1024 * 1024
BIG_VMEM_LIMIT_BYTES = 60 * 1024 * 1024
LANES = 128

A_NQ = 0
A_KS = 1024
A_KW = 1280
A_VS = 1536
A_VW = 1792
A_W = 2048
C_GV = 0
C_GG = 1024
C_GQ = 2048
C_GK = 2560
C_KC = 3072
C_VC = 3328
C_MISC = 3584
PROJ_W = 3712
MISC_GA = 24

NT_DIMS = (((1,), (1,)), ((), ()))


def _params(*sem):
    return pltpu.CompilerParams(dimension_semantics=sem, vmem_limit_bytes=VMEM_LIMIT_BYTES)


def _layer_norm(z, g, b):
    mu = jnp.mean(z, axis=-1, keepdims=True)
    zc = z - mu
    var = jnp.mean(zc * zc, axis=-1, keepdims=True)
    return zc * lax.rsqrt(var + LN_EPS) * g + b


def _ada_kernel(c_ref, w_ref, b_ref, o_ref):
    c = c_ref[...]
    cond = c * jax.nn.sigmoid(c)
    o_ref[0] = jnp.dot(cond.astype(BF16), w_ref[0].astype(BF16),
                       preferred_element_type=F32) + b_ref[0]


def ada_mod(c_pad, w_ada, b_ada):
    L, D, N = w_ada.shape
    tn = ADA_TN
    return pl.pallas_call(
        _ada_kernel,
        grid=(L, N // tn),
        in_specs=[pl.BlockSpec((8, D), lambda l, j: (0, 0)),
                  pl.BlockSpec((1, D, tn), lambda l, j: (l, 0, j)),
                  pl.BlockSpec((1, 1, tn), lambda l, j: (l, 0, j))],
        out_specs=pl.BlockSpec((1, 8, tn), lambda l, j: (l, 0, j)),
        out_shape=jax.ShapeDtypeStruct((L, 8, N), F32),
        compiler_params=_params("parallel", "parallel"),
        name="ada_mod",
    )(c_pad, w_ada, b_ada)


def _inproj_kernel(x_ref, m_ref, w_ref, o_ref):
    h = x_ref[0] * (1.0 + m_ref[0, 1:2, :]) + m_ref[0, 0:1, :]
    o_ref[0] = jnp.dot(h.astype(BF16), w_ref[...], preferred_element_type=F32)


def in_proj(x, mod, w):
    B, T, D = x.shape
    N = w.shape[1]
    tm = min(ROW_TILE, T)
    return pl.pallas_call(
        _inproj_kernel,
        grid=(B, T // tm),
        in_specs=[pl.BlockSpec((1, tm, D), lambda b, i: (b, i, 0)),
                  pl.BlockSpec((1, 6, D), lambda b, i: (b, 0, 0)),
                  pl.BlockSpec((D, N), lambda b, i: (0, 0))],
        out_specs=pl.BlockSpec((1, tm, N), lambda b, i: (b, i, 0)),
        out_shape=jax.ShapeDtypeStruct((B, T, N), F32),
        compiler_params=_params("parallel", "parallel"),
        name="in_proj",
    )(x, mod, w)


def _nsa_proj_kernel(x_ref, m_ref, w_ref, rc_ref, ra_ref, rb_ref,
                     qr_ref, qn_ref, ks_ref, vs_ref, kw_ref, vw_ref, *, sub):
    tm = x_ref.shape[1]
    scale = HEAD_DIM ** -0.5 * LOG2_E
    for r in range(tm // sub):
        rows = slice(r * sub, (r + 1) * sub)
        rc = rc_ref[rows, :]
        ra = ra_ref[rows, :]
        rb = rb_ref[rows, :]

        def rope(xh):
            return (xh * rc + pltpu.roll(xh, LANES - ROPE_DIMS // 2, 1) * ra
                    + pltpu.roll(xh, ROPE_DIMS // 2, 1) * rb)

        h = x_ref[0, rows, :] * (1.0 + m_ref[0, 1:2, :]) + m_ref[0, 0:1, :]
        p = jnp.dot(h.astype(BF16), w_ref[...], preferred_element_type=F32)
        for hq in range(NSA_HEADS):
            xh = p[:, A_NQ + hq * HEAD_DIM:A_NQ + (hq + 1) * HEAD_DIM]
            qr_ref[0, rows, hq * HEAD_DIM:(hq + 1) * HEAD_DIM] = (rope(xh) * scale).astype(BF16)
            qn_ref[0, rows, hq * HEAD_DIM:(hq + 1) * HEAD_DIM] = (xh * scale).astype(BF16)
        key_blk = (pl.program_id(1) * tm + r * sub + lax.broadcasted_iota(jnp.int32, (sub, LANES), 0)) // SLC_BLOCK
        blk_onehot = jnp.where(key_blk == lax.broadcasted_iota(jnp.int32, (sub, LANES), 1), 1.0, 0.0).astype(BF16)
        for hk in range(NSA_KV_HEADS):
            ks_ref[0, hk, rows, 0:HEAD_DIM] = rope(p[:, A_KS + hk * HEAD_DIM:A_KS + (hk + 1) * HEAD_DIM]).astype(BF16)
            ks_ref[0, hk, rows, HEAD_DIM:] = blk_onehot
            kw_ref[0, hk, rows, :] = rope(p[:, A_KW + hk * HEAD_DIM:A_KW + (hk + 1) * HEAD_DIM]).astype(BF16)
            vs_ref[0, hk, 0:HEAD_DIM, rows] = p[:, A_VS + hk * HEAD_DIM:A_VS + (hk + 1) * HEAD_DIM].T.astype(BF16)
            vs_ref[0, hk, HEAD_DIM:, rows] = jnp.ones((V_AUG_ROWS - HEAD_DIM, sub), BF16)
            vw_ref[0, hk, :, rows] = p[:, A_VW + hk * HEAD_DIM:A_VW + (hk + 1) * HEAD_DIM].T.astype(BF16)


def nsa_proj(x, mod, w, rc, ra, rb):
    B, T, D = x.shape
    tm = min(ROW_TILE, T)
    assert T // SLC_BLOCK <= LANES
    qw = NSA_HEADS * HEAD_DIM
    qspec = pl.BlockSpec((1, tm, qw), lambda b, i: (b, i, 0))
    tspec = pl.BlockSpec((tm, LANES), lambda b, i: (i, 0))
    return pl.pallas_call(
        functools.partial(_nsa_proj_kernel, sub=min(EPILOGUE_SUB, tm)),
        grid=(B, T // tm),
        in_specs=[pl.BlockSpec((1, tm, D), lambda b, i: (b, i, 0)),
                  pl.BlockSpec((1, 6, D), lambda b, i: (b, 0, 0)),
                  pl.BlockSpec((D, A_W), lambda b, i: (0, 0)),
                  tspec, tspec, tspec],
        out_specs=[qspec, qspec,
                   pl.BlockSpec((1, NSA_KV_HEADS, tm, HEAD_DIM + LANES), lambda b, i: (b, 0, i, 0)),
                   pl.BlockSpec((1, NSA_KV_HEADS, V_AUG_ROWS, tm), lambda b, i: (b, 0, 0, i)),
                   pl.BlockSpec((1, NSA_KV_HEADS, tm, HEAD_DIM), lambda b, i: (b, 0, i, 0)),
                   pl.BlockSpec((1, NSA_KV_HEADS, HEAD_DIM, tm), lambda b, i: (b, 0, 0, i))],
        out_shape=[jax.ShapeDtypeStruct((B, T, qw), BF16),
                   jax.ShapeDtypeStruct((B, T, qw), BF16),
                   jax.ShapeDtypeStruct((B, NSA_KV_HEADS, T, HEAD_DIM + LANES), BF16),
                   jax.ShapeDtypeStruct((B, NSA_KV_HEADS, V_AUG_ROWS, T), BF16),
                   jax.ShapeDtypeStruct((B, NSA_KV_HEADS, T, HEAD_DIM), BF16),
                   jax.ShapeDtypeStruct((B, NSA_KV_HEADS, HEAD_DIM, T), BF16)],
        compiler_params=_params("parallel", "parallel"),
        name="nsa_proj",
    )(x, mod, w, rc, ra, rb)


def _compress_kernel(a_ref, pe_ref, w1_ref, w2_ref, o_ref, ot_ref, *, n_half):
    half = CMP_STRIDE

    def part(l0):
        acc = jnp.zeros((n_half, CMP_HIDDEN), F32)
        for l in range(half):
            rows = a_ref[0, pl.ds(l, n_half, stride=half), :] + pe_ref[0, l0 + l:l0 + l + 1, :]
            acc += jnp.dot(rows.astype(BF16), w1_ref[0, (l0 + l) * HEAD_DIM:(l0 + l + 1) * HEAD_DIM, :],
                           preferred_element_type=F32)
        return acc

    first = part(0)
    second = part(half)
    hid = first + pltpu.roll(second, n_half - 1, 0)
    row = lax.broadcasted_iota(jnp.int32, (n_half, 1), 0)
    hid = jnp.where(row < n_half - 1, hid, 0.0)
    act = jax.nn.gelu(hid)
    out = jnp.dot(act.astype(BF16), w2_ref[0], preferred_element_type=F32)
    o_ref[0, 0, 0] = out.astype(BF16)
    ot_ref[0, 0, 0] = out.T.astype(BF16)


def nsa_compress(proj, pe, w1, w2):
    B, T, _ = proj.shape
    n_half = T // CMP_STRIDE
    return pl.pallas_call(
        functools.partial(_compress_kernel, n_half=n_half),
        grid=(B, 2, NSA_KV_HEADS),
        in_specs=[pl.BlockSpec((1, T, HEAD_DIM), lambda b, s, h: (b, 0, C_KC // HEAD_DIM + s * NSA_KV_HEADS + h)),
                  pl.BlockSpec((1, CMP_BLOCK, HEAD_DIM), lambda b, s, h: (s, 0, 0)),
                  pl.BlockSpec((1, CMP_BLOCK * HEAD_DIM, CMP_HIDDEN), lambda b, s, h: (s, 0, 0)),
                  pl.BlockSpec((1, CMP_HIDDEN, HEAD_DIM), lambda b, s, h: (s, 0, 0))],
        out_specs=[pl.BlockSpec((1, 1, 1, n_half, HEAD_DIM), lambda b, s, h: (b, s, h, 0, 0)),
                   pl.BlockSpec((1, 1, 1, HEAD_DIM, n_half), lambda b, s, h: (b, s, h, 0, 0))],
        out_shape=[jax.ShapeDtypeStruct((B, 2, NSA_KV_HEADS, n_half, HEAD_DIM), BF16),
                   jax.ShapeDtypeStruct((B, 2, NSA_KV_HEADS, HEAD_DIM, n_half), BF16)],
        compiler_params=_params("parallel", "parallel", "parallel"),
        name="nsa_compress",
    )(proj, pe, w1, w2)


def _nsa_kernel(qr_ref, qn_ref, g_ref, kc_ref, vct_ref, ks_ref, vst_ref, kw_ref, vwt_ref, ovt_ref, o_ref,
                s_buf, smax_buf, m_buf, acc_buf, *, TQ, TK, WK, NC, NB, NH):
    G = NSA_GROUP
    R = G * TQ
    NBP = ovt_ref.shape[0]
    i = pl.program_id(2)
    t0 = i * TQ
    m_floor = 0.5 * NEG_BIG

    def stack(ref, h):
        x = ref[0]
        return jnp.concatenate([x[:, (h * G + g) * HEAD_DIM:(h * G + g + 1) * HEAD_DIM] for g in range(G)], axis=0)

    def tile_g(x):
        return jnp.concatenate([x] * G, axis=1)

    tT = t0 + lax.broadcasted_iota(jnp.int32, (1, TQ), 1)
    cur = tT // SLC_BLOCK
    n_sel = min(SLC_TOPN, NB)

    HS = range(NH)
    qr = [stack(qr_ref, h) for h in HS]
    qn = [stack(qn_ref, h) for h in HS]

    ws = pl.multiple_of(jnp.maximum(t0 + TQ - WK, 0), LANES)
    s_c = [lax.dot_general(kc_ref[0, 0, h], qn[h], NT_DIMS, preferred_element_type=F32) for h in HS]
    s_w = [lax.dot_general(kw_ref[0, h, pl.ds(ws, WK), :], qr[h], NT_DIMS, preferred_element_type=F32)
           for h in HS]
    cend = lax.broadcasted_iota(jnp.int32, (NC, 1), 0) * CMP_STRIDE + (CMP_BLOCK - 1)
    bias_c = tile_g(jnp.where(cend <= tT, 0.0, NEG_BIG))
    dist = tT - (ws + lax.broadcasted_iota(jnp.int32, (WK, 1), 0))
    bias_w = tile_g(jnp.where(dist >= 0, jnp.where(dist < WINDOW, 0.0, NEG_BIG), NEG_BIG))

    p_cb, e_wb, l_w = [], [], []
    for h in HS:
        sc = s_c[h] + bias_c
        m_c = jnp.maximum(jnp.max(sc, axis=0, keepdims=True), m_floor)
        e_c = jnp.exp2(sc - m_c)
        den = jnp.sum(e_c, axis=0, keepdims=True)
        p_cb.append((e_c * jnp.where(den > 0.0, 1.0 / den, 0.0)).astype(BF16))
        sw = s_w[h] + bias_w
        e_w = jnp.exp2(sw - jnp.max(sw, axis=0, keepdims=True))
        l_w.append(jnp.sum(e_w, axis=0, keepdims=True))
        e_wb.append(e_w.astype(BF16))

    o_c = [jnp.dot(vct_ref[0, 0, h], p_cb[h], preferred_element_type=F32) for h in HS]
    imp4 = [jnp.dot(ovt_ref[...], p_cb[h], preferred_element_type=F32) for h in HS]
    o_w = [jnp.dot(vwt_ref[0, h, :, pl.ds(ws, WK)], e_wb[h], preferred_element_type=F32) * (1.0 / l_w[h])
           for h in HS]

    jj = lax.broadcasted_iota(jnp.int32, (NBP, TQ), 0)
    forced = jnp.where(jj == 0, 1, jnp.where(jj == cur, 1, jnp.where(jj == cur - 1, 1, 0)))
    val = []
    for h in HS:
        impT = imp4[h][:, 0:TQ]
        for g in range(1, G):
            impT = impT + imp4[h][:, g * TQ:(g + 1) * TQ]
        val.append(jnp.where(jj > cur, -jnp.inf, jnp.where(forced > 0, FORCE_SCORE, impT)))
    SUBL = 8
    chunks = [[val[h][c * SUBL:(c + 1) * SUBL, :] for c in range(NBP // SUBL)] for h in HS]
    rank_c = [[jnp.zeros((SUBL, TQ), jnp.int32) for c in range(NBP // SUBL)] for h in HS]
    jj_c = lax.broadcasted_iota(jnp.int32, (SUBL, TQ), 0)
    for j2 in range(NB):
        for h in HS:
            row = val[h][j2:j2 + 1, :]
            for c in range(NBP // SUBL):
                v = chunks[h][c]
                if c * SUBL > j2:
                    beat = jnp.where(row >= v, 1, 0)
                elif (c + 1) * SUBL - 1 < j2:
                    beat = jnp.where(row > v, 1, 0)
                else:
                    later = jnp.where(jj_c + c * SUBL > j2, 1, 0)
                    beat = jnp.where(row > v, 1, jnp.where(row == v, later, 0))
                rank_c[h][c] = rank_c[h][c] + beat
    rank = [jnp.concatenate(rank_c[h], axis=0) for h in HS]
    fronts = []
    for h in HS:
        sel_neg = jnp.where(jj > cur, -MASK_BIG, jnp.where(rank[h] < n_sel, 0.0, -MASK_BIG))
        if NBP < LANES:
            sel_neg = jnp.concatenate([sel_neg, jnp.zeros((LANES - NBP, TQ), F32)], axis=0)
        sel_q = sel_neg.T.astype(BF16)
        q_aug = jnp.concatenate([qr[h], jnp.concatenate([sel_q] * G, axis=0)], axis=1)
        fronts.append((o_c[h], o_w[h], q_aug))

    def sel_scores(h, kt, slot):
        k0 = pl.multiple_of(kt * TK, TK)
        k = ks_ref[0, h, pl.ds(k0, TK), :]
        s = lax.dot_general(k, fronts[h][2], NT_DIMS, preferred_element_type=F32)
        s_buf[h, slot] = s
        smax_buf[h, slot] = jnp.max(s, axis=0, keepdims=True)

    def sel_accumulate(h, kt, slot):
        k0 = pl.multiple_of(kt * TK, TK)
        vt = vst_ref[0, h, :, pl.ds(k0, TK)]
        m = m_buf[h]
        m_new = jnp.maximum(m, smax_buf[h, slot])
        p = jnp.exp2(s_buf[h, slot] - m_new).astype(BF16)
        acc_buf[h] = jnp.exp2(m - m_new) * acc_buf[h] + jnp.dot(vt, p, preferred_element_type=F32)
        m_buf[h] = m_new

    n_kt = (t0 + TQ + TK - 1) // TK
    n_loop = n_kt - 1
    for h in HS:
        m_buf[h] = jnp.full((1, R), m_floor, F32)
        acc_buf[h] = jnp.zeros((V_AUG_ROWS, R), F32)
        sel_scores(h, 0, 0)

    def pair_body(k, carry):
        for h in HS:
            sel_scores(h, 2 * k + 1, 1)
            sel_accumulate(h, 2 * k, 0)

        @pl.when(2 * k + 1 < n_loop)
        def _():
            for h in HS:
                sel_scores(h, 2 * k + 2, 0)
                sel_accumulate(h, 2 * k + 1, 1)
        return carry

    lax.fori_loop(0, (n_loop + 1) // 2, pair_body, 0)
    krow = n_loop * TK + lax.broadcasted_iota(jnp.int32, (TK, 1), 0)
    causal = tile_g(jnp.where(krow <= tT, 0.0, NEG_BIG))
    k_last = pl.multiple_of(n_loop * TK, TK)
    last_slot = n_loop % 2
    s_fin = [s_buf[h, last_slot] + causal for h in HS]
    m_fin = [jnp.maximum(m_buf[h], jnp.max(s_fin[h], axis=0, keepdims=True)) for h in HS]
    p_fin = [jnp.exp2(s_fin[h] - m_fin[h]).astype(BF16) for h in HS]
    acc_fin = [jnp.exp2(m_buf[h] - m_fin[h]) * acc_buf[h]
               + jnp.dot(vst_ref[0, h, :, pl.ds(k_last, TK)], p_fin[h], preferred_element_type=F32) for h in HS]
    gate_all = jax.nn.sigmoid(g_ref[0]).T
    assert NH == NSA_KV_HEADS
    gates = [gate_all[3 * G * h:3 * G * (h + 1), :] for h in HS]
    for h in HS:
        o_s = acc_fin[h][0:HEAD_DIM] * (1.0 / acc_fin[h][HEAD_DIM:HEAD_DIM + 1])
        o_c, o_w, _ = fronts[h]
        gate = gates[h]
        for g in range(G):
            cols = slice(g * TQ, (g + 1) * TQ)
            o = (gate[3 * g:3 * g + 1, :] * o_c[:, cols] + gate[3 * g + 1:3 * g + 2, :] * o_s[:, cols]
                 + gate[3 * g + 2:3 * g + 3, :] * o_w[:, cols])
            o_ref[0, :, (h * G + g) * HEAD_DIM:(h * G + g + 1) * HEAD_DIM] = o.T.astype(BF16)


def nsa_attention(qr, qn, gates, kc, vct, ks, vst, kw, vwt, overlap_t):
    B, T, _ = qr.shape
    TQ = NSA_TQ
    TK = min(NSA_TK, T)
    WK = min(WINDOW + TQ, T)
    NC = kc.shape[3]
    NB = T // SLC_BLOCK
    NH = NSA_HEADS_PER_STEP
    gw = NH * NSA_GROUP * HEAD_DIM
    qspec = pl.BlockSpec((1, TQ, gw), lambda b, h, i: (b, i, h))
    kspec = pl.BlockSpec((1, NH, T, HEAD_DIM), lambda b, h, i: (b, h, 0, 0))
    vtspec = pl.BlockSpec((1, NH, HEAD_DIM, T), lambda b, h, i: (b, h, 0, 0))
    return pl.pallas_call(
        functools.partial(_nsa_kernel, TQ=TQ, TK=TK, WK=WK, NC=NC, NB=NB, NH=NH),
        grid=(B, NSA_KV_HEADS // NH, T // TQ),
        in_specs=[qspec, qspec,
                  pl.BlockSpec((1, TQ, LANES), lambda b, h, i: (b, i, C_MISC // LANES)),
                  pl.BlockSpec((1, 1, NH, NC, HEAD_DIM), lambda b, h, i: (b, 0, h, 0, 0)),
                  pl.BlockSpec((1, 1, NH, HEAD_DIM, NC), lambda b, h, i: (b, 1, h, 0, 0)),
                  pl.BlockSpec((1, NH, T, HEAD_DIM + LANES), lambda b, h, i: (b, h, 0, 0)),
                  pl.BlockSpec((1, NH, V_AUG_ROWS, T), lambda b, h, i: (b, h, 0, 0)),
                  kspec, vtspec,
                  pl.BlockSpec(overlap_t.shape, lambda b, h, i: (0, 0))],
        out_specs=pl.BlockSpec((1, TQ, gw), lambda b, h, i: (b, i, h)),
        out_shape=jax.ShapeDtypeStruct((B, T, NSA_HEADS * HEAD_DIM), BF16),
        scratch_shapes=[pltpu.VMEM((NH, 2, TK, NSA_GROUP * TQ), F32),
                        pltpu.VMEM((NH, 2, 1, NSA_GROUP * TQ), F32),
                        pltpu.VMEM((NH, 1, NSA_GROUP * TQ), F32),
                        pltpu.VMEM((NH, V_AUG_ROWS, NSA_GROUP * TQ), F32)],
        compiler_params=_params("parallel", "parallel", "arbitrary"),
        name="nsa_attention",
    )(qr, qn, gates, kc, vct, ks, vst, kw, vwt, overlap_t)


def _gla_kernel(q_ref, k_ref, v_ref, gg_ref, misc_ref, wa_ref, ba_ref, nw_ref, o_ref, s_ref, la_ref, *, TC):
    C = GLA_CHUNK

    @pl.when(pl.program_id(1) == 0)
    def _():
        s_ref[...] = jnp.zeros_like(s_ref)

    z = jnp.dot(misc_ref[0].astype(BF16), wa_ref[...], preferred_element_type=F32) + ba_ref[...]
    la_ref[...] = (jnp.minimum(z, 0.0) - jnp.log1p(jnp.exp(-jnp.abs(z)))) * (1.0 / GLA_GATE_NORM)
    ri = lax.broadcasted_iota(jnp.int32, (C, C), 0)
    ci = lax.broadcasted_iota(jnp.int32, (C, C), 1)
    causal = ri >= ci
    tri = jnp.where(causal, 1.0, 0.0)
    nw = nw_ref[...]

    n_c = TC // C
    pairs = [(c, h) for c in range(n_c) for h in range(GLA_HEADS)]
    rows = [slice(c * C, (c + 1) * C) for c in range(n_c)]
    b_all = [jnp.dot(tri, la_ref[rows[c], :], preferred_element_type=F32, precision=lax.Precision.HIGHEST)
             for c in range(n_c)]
    qd, kv, decay, intra = {}, {}, {}, {}
    for c, h in pairs:
        b = b_all[c][:, h * GLA_DK:(h + 1) * GLA_DK]
        bl = b[C - 1:C, :]
        q = q_ref[0, rows[c], h * GLA_DK:(h + 1) * GLA_DK] * (GLA_DK ** -0.5)
        k = k_ref[0, rows[c], h * GLA_DK:(h + 1) * GLA_DK]
        v = v_ref[0, rows[c], h * GLA_DV:(h + 1) * GLA_DV].astype(BF16)
        qd[c, h] = (q * jnp.exp(b)).astype(BF16)
        ki = (k * jnp.exp(-b)).astype(BF16)
        kst = (k * jnp.exp(bl - b)).T.astype(BF16)
        decay[c, h] = jnp.exp(b.T[:, C - 1:C])
        a = lax.dot_general(qd[c, h], ki, NT_DIMS, preferred_element_type=F32)
        a = jnp.where(causal, a, 0.0).astype(BF16)
        intra[c, h] = jnp.dot(a, v, preferred_element_type=F32)
        kv[c, h] = jnp.dot(kst, v, preferred_element_type=F32)
    state = [s_ref[h] for h in range(GLA_HEADS)]
    for c, h in pairs:
        o = intra[c, h] + jnp.dot(qd[c, h], state[h].astype(BF16), preferred_element_type=F32)
        state[h] = state[h] * decay[c, h] + kv[c, h]
        o = o * lax.rsqrt(jnp.mean(o * o, axis=-1, keepdims=True) + NORM_EPS) * nw
        gg = gg_ref[0, rows[c], h * GLA_DV:(h + 1) * GLA_DV]
        o_ref[0, rows[c], h * GLA_DV:(h + 1) * GLA_DV] = (o * (gg * jax.nn.sigmoid(gg))).astype(BF16)
    for h in range(GLA_HEADS):
        s_ref[h] = state[h]


def gla_mixer(proj, wa_pad, ba, nw):
    B, T, _ = proj.shape
    TC = min(GLA_TILE, T)
    qk_w = GLA_HEADS * GLA_DK
    v_w = GLA_HEADS * GLA_DV
    return pl.pallas_call(
        functools.partial(_gla_kernel, TC=TC),
        grid=(B, T // TC),
        in_specs=[pl.BlockSpec((1, TC, qk_w), lambda b, i: (b, i, C_GQ // qk_w)),
                  pl.BlockSpec((1, TC, qk_w), lambda b, i: (b, i, C_GK // qk_w)),
                  pl.BlockSpec((1, TC, v_w), lambda b, i: (b, i, C_GV // v_w)),
                  pl.BlockSpec((1, TC, v_w), lambda b, i: (b, i, C_GG // v_w)),
                  pl.BlockSpec((1, TC, LANES), lambda b, i: (b, i, C_MISC // LANES)),
                  pl.BlockSpec((LANES, qk_w), lambda b, i: (0, 0)),
                  pl.BlockSpec((1, qk_w), lambda b, i: (0, 0)),
                  pl.BlockSpec((1, GLA_DV), lambda b, i: (0, 0))],
        out_specs=pl.BlockSpec((1, TC, v_w), lambda b, i: (b, i, 0)),
        out_shape=jax.ShapeDtypeStruct((B, T, v_w), BF16),
        scratch_shapes=[pltpu.VMEM((GLA_HEADS, GLA_DK, GLA_DV), F32),
                        pltpu.VMEM((TC, qk_w), F32)],
        compiler_params=_params("parallel", "arbitrary"),
        name="gla_mixer",
    )(proj, proj, proj, proj, proj, wa_pad, ba, nw)


def _outproj_kernel(n_ref, g_ref, w_ref, x_ref, m_ref, lg_ref, lb_ref, o_ref, *, half):
    sub = min(EPILOGUE_SUB, o_ref.shape[1])
    for r in range(o_ref.shape[1] // sub):
        rows = slice(r * sub, (r + 1) * sub)
        y = (jnp.dot(n_ref[0, rows, :], w_ref[0:half, :], preferred_element_type=F32)
             + jnp.dot(g_ref[0, rows, :], w_ref[half:, :], preferred_element_type=F32))
        z = DEEPNORM_ALPHA * x_ref[0, rows, :] + (1.0 + m_ref[0, 2:3, :]) * y
        o_ref[0, rows, :] = _layer_norm(z, lg_ref[...], lb_ref[...])


def out_proj_ln(nsa_o, gla_o, w, x, mod, lg, lb):
    B, T, D = x.shape
    half = nsa_o.shape[-1]
    tm = min(ROW_TILE, T)
    vec = pl.BlockSpec((1, D), lambda b, i: (0, 0))
    return pl.pallas_call(
        functools.partial(_outproj_kernel, half=half),
        grid=(B, T // tm),
        in_specs=[pl.BlockSpec((1, tm, half), lambda b, i: (b, i, 0)),
                  pl.BlockSpec((1, tm, gla_o.shape[-1]), lambda b, i: (b, i, 0)),
                  pl.BlockSpec(w.shape, lambda b, i: (0, 0)),
                  pl.BlockSpec((1, tm, D), lambda b, i: (b, i, 0)),
                  pl.BlockSpec((1, 6, D), lambda b, i: (b, 0, 0)),
                  vec, vec],
        out_specs=pl.BlockSpec((1, tm, D), lambda b, i: (b, i, 0)),
        out_shape=jax.ShapeDtypeStruct((B, T, D), F32),
        compiler_params=_params("parallel", "parallel"),
        name="out_proj_ln",
    )(nsa_o, gla_o, w, x, mod, lg, lb)


FFN_TM = 1024
FFN_TF = 256
FFN_SUB = 512


def _ffn_kernel(x_ref, m_ref, wg_ref, wu_ref, wd_ref, lg_ref, lb_ref, o_ref, h_sc, *, sub):
    j = pl.program_id(2)
    last = pl.num_programs(2) - 1
    tm = o_ref.shape[1]

    def step(first, final):
        rows_per = sub // 2 if (first or final) else sub
        wg = wg_ref[...].astype(BF16)
        wu = wu_ref[...].astype(BF16)
        wd = wd_ref[...].astype(BF16)
        for sb in range(tm // rows_per):
            rows = slice(sb * rows_per, (sb + 1) * rows_per)
            if first:
                h_sc[rows, :] = (x_ref[0, rows, :] * (1.0 + m_ref[0, 4:5, :]) + m_ref[0, 3:4, :]).astype(BF16)
            h = h_sc[rows, :]
            a = jnp.dot(h, wg, preferred_element_type=F32)
            u = jnp.dot(h, wu, preferred_element_type=F32)
            y = jnp.dot((a * jax.nn.sigmoid(a) * u).astype(BF16), wd, preferred_element_type=F32)
            if first:
                o_ref[0, rows, :] = y
            elif final:
                z = DEEPNORM_ALPHA * x_ref[0, rows, :] + (1.0 + m_ref[0, 5:6, :]) * (o_ref[0, rows, :] + y)
                o_ref[0, rows, :] = _layer_norm(z, lg_ref[...], lb_ref[...])
            else:
                o_ref[0, rows, :] += y

    pl.when(j == 0)(lambda: step(True, False))
    pl.when(jnp.logical_and(j > 0, j < last))(lambda: step(False, False))
    pl.when(j == last)(lambda: step(False, True))


def ffn_ln(x, mod, wg, wu, wd, lg, lb):
    B, T, D = x.shape
    F = wg.shape[1]
    tm = min(FFN_TM, T)
    tf = FFN_TF
    vec = pl.BlockSpec((1, D), lambda b, i, j: (0, 0))
    return pl.pallas_call(
        functools.partial(_ffn_kernel, sub=min(FFN_SUB, tm)),
        grid=(B, T // tm, F // tf),
        in_specs=[pl.BlockSpec((1, tm, D), lambda b, i, j: (b, i, 0)),
                  pl.BlockSpec((1, 6, D), lambda b, i, j: (b, 0, 0)),
                  pl.BlockSpec((D, tf), lambda b, i, j: (0, j)),
                  pl.BlockSpec((D, tf), lambda b, i, j: (0, j)),
                  pl.BlockSpec((tf, D), lambda b, i, j: (j, 0)),
                  vec, vec],
        out_specs=pl.BlockSpec((1, tm, D), lambda b, i, j: (b, i, 0)),
        out_shape=jax.ShapeDtypeStruct((B, T, D), F32),
        scratch_shapes=[pltpu.VMEM((tm, D), BF16)],
        compiler_params=pltpu.CompilerParams(dimension_semantics=("parallel", "parallel", "arbitrary"),
                                             vmem_limit_bytes=BIG_VMEM_LIMIT_BYTES),
        name="ffn_ln",
    )(x, mod, wg, wu, wd, lg, lb)


def _router_kernel(x_ref, m_ref, wr_ref, h_ref, lg_ref):
    h = x_ref[0] * (1.0 + m_ref[0, 4:5, :]) + m_ref[0, 3:4, :]
    h_hi = h.astype(BF16)
    h_lo = (h - h_hi.astype(F32)).astype(BF16)
    h_ref[0] = h_hi
    lg_ref[0] = (jnp.dot(h_hi, wr_ref[0], preferred_element_type=F32)
                 + jnp.dot(h_lo, wr_ref[0], preferred_element_type=F32)
                 + jnp.dot(h_hi, wr_ref[1], preferred_element_type=F32))


def moe_router(x, mod, wr_pad):
    B, T, D = x.shape
    tm = min(ROW_TILE, T)
    return pl.pallas_call(
        _router_kernel,
        grid=(B, T // tm),
        in_specs=[pl.BlockSpec((1, tm, D), lambda b, i: (b, i, 0)),
                  pl.BlockSpec((1, 6, D), lambda b, i: (b, 0, 0)),
                  pl.BlockSpec((2, D, LANES), lambda b, i: (0, 0, 0))],
        out_specs=[pl.BlockSpec((1, tm, D), lambda b, i: (b, i, 0)),
                   pl.BlockSpec((1, tm, LANES), lambda b, i: (b, i, 0))],
        out_shape=[jax.ShapeDtypeStruct((B, T, D), BF16),
                   jax.ShapeDtypeStruct((B, T, LANES), F32)],
        compiler_params=_params("parallel", "parallel"),
        name="moe_router",
    )(x, mod, wr_pad)


def _moe_kernel(te_ref, nv_ref, x_ref, wg_ref, wu_ref, wd_ref, o_ref, acc_ref, *, sub):
    i = pl.program_id(0)
    j = pl.program_id(1)
    nv = nv_ref[i]

    @pl.when(j == 0)
    def _():
        acc_ref[...] = jnp.zeros_like(acc_ref)

    n_sub = acc_ref.shape[0] // sub

    def swiglu_rows(rows, wg, wu, wd):
        h = x_ref[rows, :]
        a = jnp.dot(h, wg, preferred_element_type=F32)
        u = jnp.dot(h, wu, preferred_element_type=F32)
        acc_ref[rows, :] += jnp.dot((a * jax.nn.sigmoid(a) * u).astype(BF16), wd, preferred_element_type=F32)

    for n_real in range(1, n_sub + 1):
        @pl.when(nv == n_real)
        def _():
            wg = wg_ref[0].astype(BF16)
            wu = wu_ref[0].astype(BF16)
            wd = wd_ref[0].astype(BF16)
            for r0 in range(0, n_real * sub, MOE_CHAIN):
                swiglu_rows(slice(r0, min(r0 + MOE_CHAIN, n_real * sub)), wg, wu, wd)

    @pl.when(j == pl.num_programs(1) - 1)
    def _():
        o_ref[...] = acc_ref[...].astype(o_ref.dtype)


def _moe_kernel_into(te_ref, nv_ref, x_ref, wg_ref, wu_ref, wd_ref, y_prev_ref, o_ref, acc_ref, *, sub):
    del y_prev_ref
    _moe_kernel(te_ref, nv_ref, x_ref, wg_ref, wu_ref, wd_ref, o_ref, acc_ref, sub=sub)


def moe_experts(tile_e, tile_nv, xg, wg, wu, wd, tm, sub, tile0, n_total, y_prev=None):
    M, D = xg.shape
    F = wg.shape[2]
    tf = MOE_TF
    nj = F // tf
    n_tiles = M // tm

    def wj(i, j, nv):
        return jnp.where(nv[i] > 0, j, nj - 1)

    in_specs = [pl.BlockSpec((tm, D), lambda i, j, te, nv: (i, 0)),
                pl.BlockSpec((1, D, tf), lambda i, j, te, nv: (te[i], 0, wj(i, j, nv))),
                pl.BlockSpec((1, D, tf), lambda i, j, te, nv: (te[i], 0, wj(i, j, nv))),
                pl.BlockSpec((1, tf, D), lambda i, j, te, nv: (te[i], wj(i, j, nv), 0))]
    operands = [tile_e, tile_nv, xg, wg, wu, wd]
    body, aliases = _moe_kernel, {}
    if y_prev is not None:
        in_specs.append(pl.BlockSpec(memory_space=pl.ANY))
        operands.append(y_prev)
        body, aliases = _moe_kernel_into, {len(operands) - 1: 0}
    return pl.pallas_call(
        functools.partial(body, sub=sub),
        grid_spec=pltpu.PrefetchScalarGridSpec(
            num_scalar_prefetch=2,
            grid=(n_tiles, nj),
            in_specs=in_specs,
            out_specs=pl.BlockSpec((tm, D), lambda i, j, te, nv: (i + tile0, 0)),
            scratch_shapes=[pltpu.VMEM((tm, D), F32)]),
        out_shape=jax.ShapeDtypeStruct((n_total * tm, D), BF16),
        input_output_aliases=aliases,
        compiler_params=pltpu.CompilerParams(dimension_semantics=("parallel", "arbitrary"),
                                             vmem_limit_bytes=BIG_VMEM_LIMIT_BYTES),
        name="moe_experts",
    )(*operands)


def _combine_kernel(y0_ref, y1_ref, cw_ref, x_ref, m_ref, lg_ref, lb_ref, o_ref):
    cw = cw_ref[0]
    y = y0_ref[0].astype(F32) * cw[:, 0:1] + y1_ref[0].astype(F32) * cw[:, 1:2]
    z = DEEPNORM_ALPHA * x_ref[0] + (1.0 + m_ref[0, 5:6, :]) * y
    o_ref[0] = _layer_norm(z, lg_ref[...], lb_ref[...])


def moe_combine_ln(y0, y1, cw, x, mod, lg, lb):
    B, T, D = x.shape
    tm = min(ROW_TILE, T)
    row = pl.BlockSpec((1, tm, D), lambda b, i: (b, i, 0))
    vec = pl.BlockSpec((1, D), lambda b, i: (0, 0))
    return pl.pallas_call(
        _combine_kernel,
        grid=(B, T // tm),
        in_specs=[row, row, pl.BlockSpec((1, tm, LANES), lambda b, i: (b, i, 0)), row,
                  pl.BlockSpec((1, 6, D), lambda b, i: (b, 0, 0)), vec, vec],
        out_specs=row,
        out_shape=jax.ShapeDtypeStruct((B, T, D), F32),
        compiler_params=_params("parallel", "parallel"),
        name="moe_combine_ln",
    )(y0, y1, cw, x, mod, lg, lb)


MOE_TM = 1024
MOE_SUB = 256
MOE_CHAIN = 1024


def moe_layer(x, mod, w_router, wg, wu, wd, lg, lb):
    B, T, D = x.shape
    N = B * T
    A = N * TOP_K
    tm = MOE_TM
    wr_pad = jnp.zeros((D, LANES), F32).at[:, :N_EXPERTS].set(w_router)
    wr_hi = wr_pad.astype(BF16)
    wr_lo = (wr_pad - wr_hi.astype(F32)).astype(BF16)
    h, logits = moe_router(x, mod, jnp.stack([wr_hi, wr_lo]))
    logits = logits.reshape(N, LANES)[:, :N_EXPERTS]
    top_val, top_idx = lax.top_k(logits, TOP_K)
    comb = jax.nn.softmax(top_val, axis=-1)
    flat_e = top_idx.reshape(-1).astype(jnp.int32)
    onehot = (flat_e[:, None] == jnp.arange(N_EXPERTS, dtype=jnp.int32)[None, :]).astype(jnp.int32)
    csum = jnp.cumsum(onehot, axis=0)
    counts = csum[-1]
    padded = (counts + tm - 1) // tm * tm
    pad_end = jnp.cumsum(padded)
    pad_start = pad_end - padded
    slot = jnp.sum(onehot * (csum + pad_start[None, :]), axis=1) - 1
    n_tiles = -(-A // tm) + N_EXPERTS
    tile_start = jnp.arange(n_tiles, dtype=jnp.int32) * tm
    tile_e = jnp.minimum(jnp.searchsorted(pad_end, tile_start, side='right'), N_EXPERTS - 1).astype(jnp.int32)
    valid = jnp.clip(pad_start[tile_e] + counts[tile_e] - tile_start, 0, tm)
    tile_nv = ((valid + MOE_SUB - 1) // MOE_SUB).astype(jnp.int32)
    order = jnp.argsort(flat_e)
    seg_start = jnp.cumsum(counts) - counts
    slot_id = jnp.arange(n_tiles * tm, dtype=jnp.int32)
    within = slot_id - jnp.repeat(tile_start, tm)
    src = jnp.repeat(seg_start[tile_e] + tile_start - pad_start[tile_e], tm) + within
    slot_tok = jnp.where(within < jnp.repeat(valid, tm),
                         order[jnp.clip(src, 0, A - 1)].astype(jnp.int32) // TOP_K, slot_id % N)
    n_used = pad_end[-1] // tm
    tile_e = jnp.where(jnp.arange(n_tiles) < n_used, tile_e, tile_e[jnp.maximum(n_used - 1, 0)])
    h2 = h.reshape(N, D)
    y = None
    for lo, hi in ((0, n_tiles // 2), (n_tiles // 2, n_tiles)):
        xg = h2[slot_tok[lo * tm:hi * tm]]
        y = moe_experts(tile_e[lo:hi], tile_nv[lo:hi], xg, wg, wu, wd, tm, MOE_SUB, lo, n_tiles, y_prev=y)
    slot_of = slot.reshape(N, TOP_K)
    y0 = y[slot_of[:, 0]].reshape(B, T, D)
    y1 = y[slot_of[:, 1]].reshape(B, T, D)
    cw = jnp.zeros((N, LANES), F32).at[:, :TOP_K].set(comb).reshape(B, T, LANES)
    return moe_combine_ln(y0, y1, cw, x, mod, lg, lb)


def _rope_tables(T):
    half = ROPE_DIMS // 2
    inv = ROPE_THETA ** (-jnp.arange(half, dtype=F32) * 2.0 / ROPE_DIMS)
    ang = jnp.arange(T).astype(F32)[:, None] * inv[None, :]
    cos, sin = jnp.cos(ang), jnp.sin(ang)
    z = jnp.zeros((T, LANES - ROPE_DIMS), F32)
    zh = jnp.zeros((T, half), F32)
    rc = jnp.concatenate([cos, cos, jnp.ones((T, LANES - ROPE_DIMS), F32)], axis=1)
    ra = jnp.concatenate([-sin, zh, z], axis=1)
    rb = jnp.concatenate([zh, sin, z], axis=1)
    return rc, ra, rb


def _overlap_matrix_t(NC, NB):
    nbp = -(-NB // 16) * 16
    c0 = np.arange(NC)[None, :] * CMP_STRIDE
    b0 = np.arange(nbp)[:, None] * SLC_BLOCK
    ov = (c0 < b0 + SLC_BLOCK) & (c0 + CMP_BLOCK > b0) & (np.arange(nbp)[:, None] < NB)
    return jnp.asarray(ov.astype(np.float32), dtype=BF16)


def _reorder_w_in(w):
    D = w.shape[0]
    o_ng = 1024 + 6 * 256
    o_gq = o_ng + 24
    o_gk = o_gq + 512
    o_gv = o_gk + 512
    o_ga = o_gv + 1024
    o_gg = o_ga + GLA_GATE_RANK
    misc = jnp.concatenate([w[:, o_ng:o_gq], w[:, o_ga:o_gg],
                            jnp.zeros((D, LANES - 24 - GLA_GATE_RANK), w.dtype)], axis=1)
    kv = 1024
    w_a = [w[:, 0:1024], w[:, kv + 512:kv + 768], w[:, kv + 1024:kv + 1280],
           w[:, kv + 768:kv + 1024], w[:, kv + 1280:kv + 1536]]
    w_b = [w[:, o_gv:o_ga], w[:, o_gg:o_gg + 1024], w[:, o_gq:o_gk], w[:, o_gk:o_gv],
           w[:, kv:kv + 512], misc]
    return jnp.concatenate(w_a, axis=1).astype(BF16), jnp.concatenate(w_b, axis=1).astype(BF16)


def hybrid_mixer_ln(x, mod, w_in_r, cmp_pe, cmp_w1, cmp_w2, wa_pad, ba, nw, w_out, lg, lb, tables, overlap):
    w_a, w_b = w_in_r
    qr, qn, ks, vs, kw, vw = nsa_proj(x, mod, w_a, *tables)
    proj = in_proj(x, mod, w_b)
    kc, vct = nsa_compress(proj, cmp_pe, cmp_w1, cmp_w2)
    nsa_o = nsa_attention(qr, qn, proj, kc, vct, ks, vs, kw, vw, overlap)
    gla_o = gla_mixer(proj, wa_pad, ba, nw)
    return out_proj_ln(nsa_o, gla_o, w_out, x, mod, lg, lb)


def kernel(x, c, w_ada, b_ada, w_in, cmp_pos_k, cmp_w1_k, cmp_w2_k, cmp_pos_v, cmp_w1_v, cmp_w2_v, gla_w_a2, gla_b_a, gla_norm_w, w_out, ln_mix_g, ln_mix_b, ln_ffn_g, ln_ffn_b, ffn_w_gate, ffn_w_up, ffn_w_down, moe_router, moe_w_gate, moe_w_up, moe_w_down):
    B, T, D = x.shape
    L = w_ada.shape[0]
    c_pad = jnp.zeros((8, D), F32).at[:B].set(c)
    mod_all = ada_mod(c_pad, w_ada, b_ada.reshape(L, 1, 6 * D))[:, :B].reshape(L, B, 6, D)
    tables = _rope_tables(T)
    overlap = _overlap_matrix_t(T // CMP_STRIDE, T // SLC_BLOCK)
    for layer in range(L):
        mod = mod_all[layer]
        wa_pad = jnp.zeros((LANES, GLA_HEADS * GLA_DK), F32).at[MISC_GA:MISC_GA + GLA_GATE_RANK].set(
            gla_w_a2[layer]).astype(BF16)
        x = hybrid_mixer_ln(
            x, mod, _reorder_w_in(w_in[layer]),
            jnp.stack([cmp_pos_k[layer], cmp_pos_v[layer]]),
            jnp.stack([cmp_w1_k[layer], cmp_w1_v[layer]]).astype(BF16),
            jnp.stack([cmp_w2_k[layer], cmp_w2_v[layer]]).astype(BF16),
            wa_pad, gla_b_a[layer].reshape(1, -1), gla_norm_w[layer].reshape(1, -1),
            w_out[layer].astype(BF16), ln_mix_g[layer].reshape(1, D), ln_mix_b[layer].reshape(1, D),
            tables, overlap)
        lg = ln_ffn_g[layer].reshape(1, D)
        lb = ln_ffn_b[layer].reshape(1, D)
        i = layer // 2
        if layer % 2 == 0:
            x = ffn_ln(x, mod, ffn_w_gate[i], ffn_w_up[i], ffn_w_down[i], lg, lb)
        else:
            x = moe_layer(x, mod, moe_router[i], moe_w_gate[i], moe_w_up[i], moe_w_down[i], lg, lb)
    return x
```

```python
import functools

import numpy as np
import jax
import jax.numpy as jnp
from jax import lax
from jax.experimental import pallas as pl
from jax.experimental.pallas import tpu as pltpu

F32 = jnp.float32
BF16 = jnp.bfloat16

D_MODEL = 2048
DEPTH = 2
HEAD_DIM = 128
NSA_HEADS = 8
NSA_KV_HEADS = 2
NSA_GROUP = 4
CMP_BLOCK = 32
CMP_STRIDE = 16
CMP_HIDDEN = 256
SLC_BLOCK = 64
SLC_TOPN = 16
WINDOW = 512
FORCE_SCORE = 1e9
GLA_DV = 256
GLA_HEADS = 4
GLA_DK = 128
GLA_GATE_RANK = 16
GLA_GATE_NORM = 16.0
GLA_CHUNK = 64
ROPE_THETA = 500000.0
ROPE_DIMS = 32
N_EXPERTS = 8
TOP_K = 2
LN_EPS = 1e-5
NORM_EPS = 1e-6
DEEPNORM_ALPHA = (2 * DEPTH) ** 0.25
NEG_BIG = -1e30
MASK_BIG = 2.0 ** 100
V_AUG_ROWS = HEAD_DIM + 16
NSA_HEADS_PER_STEP = 2
NSA_TQ = 128
NSA_TK = 512
ROW_TILE = 512
EPILOGUE_SUB = 128
ADA_TN = 2048
MOE_TF = 512
GLA_TILE = 512
LOG2_E = 1.4426950408889634

VMEM_LIMIT_BYTES = 56 * 1024 * 1024
BIG_VMEM_LIMIT_BYTES = 60 * 1024 * 1024
LANES = 128

A_NQ = 0
A_KS = 1024
A_KW = 1280
A_VS = 1536
A_VW = 1792
A_W = 2048
C_GV = 0
C_GG = 1024
C_GQ = 2048
C_GK = 2560
C_KC = 3072
C_VC = 3328
C_MISC = 3584
PROJ_W = 3712
MISC_GA = 24

NT_DIMS = (((1,), (1,)), ((), ()))


def _params(*sem):
    return pltpu.CompilerParams(dimension_semantics=sem, vmem_limit_bytes=VMEM_LIMIT_BYTES)


def _layer_norm(z, g, b):
    mu = jnp.mean(z, axis=-1, keepdims=True)
    zc = z - mu
    var = jnp.mean(zc * zc, axis=-1, keepdims=True)
    return zc * lax.rsqrt(var + LN_EPS) * g + b


def _ada_kernel(c_ref, w_ref, b_ref, o_ref):
    c = c_ref[...]
    cond = c * jax.nn.sigmoid(c)
    o_ref[0] = jnp.dot(cond.astype(BF16), w_ref[0].astype(BF16),
                       preferred_element_type=F32) + b_ref[0]


def ada_mod(c_pad, w_ada, b_ada):
    L, D, N = w_ada.shape
    tn = ADA_TN
    return pl.pallas_call(
        _ada_kernel,
        grid=(L, N // tn),
        in_specs=[pl.BlockSpec((8, D), lambda l, j: (0, 0)),
                  pl.BlockSpec((1, D, tn), lambda l, j: (l, 0, j)),
                  pl.BlockSpec((1, 1, tn), lambda l, j: (l, 0, j))],
        out_specs=pl.BlockSpec((1, 8, tn), lambda l, j: (l, 0, j)),
        out_shape=jax.ShapeDtypeStruct((L, 8, N), F32),
        compiler_params=_params("parallel", "parallel"),
        name="ada_mod",
    )(c_pad, w_ada, b_ada)


def _inproj_kernel(x_ref, m_ref, w_ref, o_ref):
    h = x_ref[0] * (1.0 + m_ref[0, 1:2, :]) + m_ref[0, 0:1, :]
    o_ref[0] = jnp.dot(h.astype(BF16), w_ref[...], preferred_element_type=F32)


def in_proj(x, mod, w):
    B, T, D = x.shape
    N = w.shape[1]
    tm = min(ROW_TILE, T)
    return pl.pallas_call(
        _inproj_kernel,
        grid=(B, T // tm),
        in_specs=[pl.BlockSpec((1, tm, D), lambda b, i: (b, i, 0)),
                  pl.BlockSpec((1, 6, D), lambda b, i: (b, 0, 0)),
                  pl.BlockSpec((D, N), lambda b, i: (0, 0))],
        out_specs=pl.BlockSpec((1, tm, N), lambda b, i: (b, i, 0)),
        out_shape=jax.ShapeDtypeStruct((B, T, N), F32),
        compiler_params=_params("parallel", "parallel"),
        name="in_proj",
    )(x, mod, w)


def _nsa_proj_kernel(x_ref, m_ref, w_ref, rc_ref, ra_ref, rb_ref,
                     qr_ref, qn_ref, ks_ref, vs_ref, kw_ref, vw_ref, *, sub):
    tm = x_ref.shape[1]
    scale = HEAD_DIM ** -0.5 * LOG2_E
    for r in range(tm // sub):
        rows = slice(r * sub, (r + 1) * sub)
        rc = rc_ref[rows, :]
        ra = ra_ref[rows, :]
        rb = rb_ref[rows, :]

        def rope(xh):
            return (xh * rc + pltpu.roll(xh, LANES - ROPE_DIMS // 2, 1) * ra
                    + pltpu.roll(xh, ROPE_DIMS // 2, 1) * rb)

        h = x_ref[0, rows, :] * (1.0 + m_ref[0, 1:2, :]) + m_ref[0, 0:1, :]
        p = jnp.dot(h.astype(BF16), w_ref[...], preferred_element_type=F32)
        for hq in range(NSA_HEADS):
            xh = p[:, A_NQ + hq * HEAD_DIM:A_NQ + (hq + 1) * HEAD_DIM]
            qr_ref[0, rows, hq * HEAD_DIM:(hq + 1) * HEAD_DIM] = (rope(xh) * scale).astype(BF16)
            qn_ref[0, rows, hq * HEAD_DIM:(hq + 1) * HEAD_DIM] = (xh * scale).astype(BF16)
        key_blk = (pl.program_id(1) * tm + r * sub + lax.broadcasted_iota(jnp.int32, (sub, LANES), 0)) // SLC_BLOCK
        blk_onehot = jnp.where(key_blk == lax.broadcasted_iota(jnp.int32, (sub, LANES), 1), 1.0, 0.0).astype(BF16)
        for hk in range(NSA_KV_HEADS):
            ks_ref[0, hk, rows, 0:HEAD_DIM] = rope(p[:, A_KS + hk * HEAD_DIM:A_KS + (hk + 1) * HEAD_DIM]).astype(BF16)
            ks_ref[0, hk, rows, HEAD_DIM:] = blk_onehot
            kw_ref[0, hk, rows, :] = rope(p[:, A_KW + hk * HEAD_DIM:A_KW + (hk + 1) * HEAD_DIM]).astype(BF16)
            vs_ref[0, hk, 0:HEAD_DIM, rows] = p[:, A_VS + hk * HEAD_DIM:A_VS + (hk + 1) * HEAD_DIM].T.astype(BF16)
            vs_ref[0, hk, HEAD_DIM:, rows] = jnp.ones((V_AUG_ROWS - HEAD_DIM, sub), BF16)
            vw_ref[0, hk, :, rows] = p[:, A_VW + hk * HEAD_DIM:A_VW + (hk + 1) * HEAD_DIM].T.astype(BF16)


def nsa_proj(x, mod, w, rc, ra, rb):
    B, T, D = x.shape
    tm = min(ROW_TILE, T)
    assert T // SLC_BLOCK <= LANES
    qw = NSA_HEADS * HEAD_DIM
    qspec = pl.BlockSpec((1, tm, qw), lambda b, i: (b, i, 0))
    tspec = pl.BlockSpec((tm, LANES), lambda b, i: (i, 0))
    return pl.pallas_call(
        functools.partial(_nsa_proj_kernel, sub=min(EPILOGUE_SUB, tm)),
        grid=(B, T // tm),
        in_specs=[pl.BlockSpec((1, tm, D), lambda b, i: (b, i, 0)),
                  pl.BlockSpec((1, 6, D), lambda b, i: (b, 0, 0)),
                  pl.BlockSpec((D, A_W), lambda b, i: (0, 0)),
                  tspec, tspec, tspec],
        out_specs=[qspec, qspec,
                   pl.BlockSpec((1, NSA_KV_HEADS, tm, HEAD_DIM + LANES), lambda b, i: (b, 0, i, 0)),
                   pl.BlockSpec((1, NSA_KV_HEADS, V_AUG_ROWS, tm), lambda b, i: (b, 0, 0, i)),
                   pl.BlockSpec((1, NSA_KV_HEADS, tm, HEAD_DIM), lambda b, i: (b, 0, i, 0)),
                   pl.BlockSpec((1, NSA_KV_HEADS, HEAD_DIM, tm), lambda b, i: (b, 0, 0, i))],
        out_shape=[jax.ShapeDtypeStruct((B, T, qw), BF16),
                   jax.ShapeDtypeStruct((B, T, qw), BF16),
                   jax.ShapeDtypeStruct((B, NSA_KV_HEADS, T, HEAD_DIM + LANES), BF16),
                   jax.ShapeDtypeStruct((B, NSA_KV_HEADS, V_AUG_ROWS, T), BF16),
                   jax.ShapeDtypeStruct((B, NSA_KV_HEADS, T, HEAD_DIM), BF16),
                   jax.ShapeDtypeStruct((B, NSA_KV_HEADS, HEAD_DIM, T), BF16)],
        compiler_params=_params("parallel", "parallel"),
        name="nsa_proj",
    )(x, mod, w, rc, ra, rb)


def _compress_kernel(a_ref, pe_ref, w1_ref, w2_ref, o_ref, ot_ref, *, n_half):
    half = CMP_STRIDE

    def part(l0):
        acc = jnp.zeros((n_half, CMP_HIDDEN), F32)
        for l in range(half):
            rows = a_ref[0, pl.ds(l, n_half, stride=half), :] + pe_ref[0, l0 + l:l0 + l + 1, :]
            acc += jnp.dot(rows.astype(BF16), w1_ref[0, (l0 + l) * HEAD_DIM:(l0 + l + 1) * HEAD_DIM, :],
                           preferred_element_type=F32)
        return acc

    first = part(0)
    second = part(half)
    hid = first + pltpu.roll(second, n_half - 1, 0)
    row = lax.broadcasted_iota(jnp.int32, (n_half, 1), 0)
    hid = jnp.where(row < n_half - 1, hid, 0.0)
    act = jax.nn.gelu(hid)
    out = jnp.dot(act.astype(BF16), w2_ref[0], preferred_element_type=F32)
    o_ref[0, 0, 0] = out.astype(BF16)
    ot_ref[0, 0, 0] = out.T.astype(BF16)


def nsa_compress(proj, pe, w1, w2):
    B, T, _ = proj.shape
    n_half = T // CMP_STRIDE
    return pl.pallas_call(
        functools.partial(_compress_kernel, n_half=n_half),
        grid=(B, 2, NSA_KV_HEADS),
        in_specs=[pl.BlockSpec((1, T, HEAD_DIM), lambda b, s, h: (b, 0, C_KC // HEAD_DIM + s * NSA_KV_HEADS + h)),
                  pl.BlockSpec((1, CMP_BLOCK, HEAD_DIM), lambda b, s, h: (s, 0, 0)),
                  pl.BlockSpec((1, CMP_BLOCK * HEAD_DIM, CMP_HIDDEN), lambda b, s, h: (s, 0, 0)),
                  pl.BlockSpec((1, CMP_HIDDEN, HEAD_DIM), lambda b, s, h: (s, 0, 0))],
        out_specs=[pl.BlockSpec((1, 1, 1, n_half, HEAD_DIM), lambda b, s, h: (b, s, h, 0, 0)),
                   pl.BlockSpec((1, 1, 1, HEAD_DIM, n_half), lambda b, s, h: (b, s, h, 0, 0))],
        out_shape=[jax.ShapeDtypeStruct((B, 2, NSA_KV_HEADS, n_half, HEAD_DIM), BF16),
                   jax.ShapeDtypeStruct((B, 2, NSA_KV_HEADS, HEAD_DIM, n_half), BF16)],
        compiler_params=_params("parallel", "parallel", "parallel"),
        name="nsa_compress",
    )(proj, pe, w1, w2)


def _nsa_kernel(qr_ref, qn_ref, g_ref, kc_ref, vct_ref, ks_ref, vst_ref, kw_ref, vwt_ref, ovt_ref, o_ref,
                s_buf, smax_buf, m_buf, acc_buf, *, TQ, TK, WK, NC, NB, NH):
    G = NSA_GROUP
    R = G * TQ
    NBP = ovt_ref.shape[0]
    i = pl.program_id(2)
    t0 = i * TQ
    m_floor = 0.5 * NEG_BIG

    def stack(ref, h):
        x = ref[0]
        return jnp.concatenate([x[:, (h * G + g) * HEAD_DIM:(h * G + g + 1) * HEAD_DIM] for g in range(G)], axis=0)

    def tile_g(x):
        return jnp.concatenate([x] * G, axis=1)

    tT = t0 + lax.broadcasted_iota(jnp.int32, (1, TQ), 1)
    cur = tT // SLC_BLOCK
    n_sel = min(SLC_TOPN, NB)

    HS = range(NH)
    qr = [stack(qr_ref, h) for h in HS]
    qn = [stack(qn_ref, h) for h in HS]

    ws = pl.multiple_of(jnp.maximum(t0 + TQ - WK, 0), LANES)
    s_c = [lax.dot_general(kc_ref[0, 0, h], qn[h], NT_DIMS, preferred_element_type=F32) for h in HS]
    s_w = [lax.dot_general(kw_ref[0, h, pl.ds(ws, WK), :], qr[h], NT_DIMS, preferred_element_type=F32)
           for h in HS]
    cend = lax.broadcasted_iota(jnp.int32, (NC, 1), 0) * CMP_STRIDE + (CMP_BLOCK - 1)
    bias_c = tile_g(jnp.where(cend <= tT, 0.0, NEG_BIG))
    dist = tT - (ws + lax.broadcasted_iota(jnp.int32, (WK, 1), 0))
    bias_w = tile_g(jnp.where(dist >= 0, jnp.where(dist < WINDOW, 0.0, NEG_BIG), NEG_BIG))

    p_cb, e_wb, l_w = [], [], []
    for h in HS:
        sc = s_c[h] + bias_c
        m_c = jnp.maximum(jnp.max(sc, axis=0, keepdims=True), m_floor)
        e_c = jnp.exp2(sc - m_c)
        den = jnp.sum(e_c, axis=0, keepdims=True)
        p_cb.append((e_c * jnp.where(den > 0.0, 1.0 / den, 0.0)).astype(BF16))
        sw = s_w[h] + bias_w
        e_w = jnp.exp2(sw - jnp.max(sw, axis=0, keepdims=True))
        l_w.append(jnp.sum(e_w, axis=0, keepdims=True))
        e_wb.append(e_w.astype(BF16))

    o_c = [jnp.dot(vct_ref[0, 0, h], p_cb[h], preferred_element_type=F32) for h in HS]
    imp4 = [jnp.dot(ovt_ref[...], p_cb[h], preferred_element_type=F32) for h in HS]
    o_w = [jnp.dot(vwt_ref[0, h, :, pl.ds(ws, WK)], e_wb[h], preferred_element_type=F32) * (1.0 / l_w[h])
           for h in HS]

    jj = lax.broadcasted_iota(jnp.int32, (NBP, TQ), 0)
    forced = jnp.where(jj == 0, 1, jnp.where(jj == cur, 1, jnp.where(jj == cur - 1, 1, 0)))
    val = []
    for h in HS:
        impT = imp4[h][:, 0:TQ]
        for g in range(1, G):
            impT = impT + imp4[h][:, g * TQ:(g + 1) * TQ]
        val.append(jnp.where(jj > cur, -jnp.inf, jnp.where(forced > 0, FORCE_SCORE, impT)))
    SUBL = 8
    chunks = [[val[h][c * SUBL:(c + 1) * SUBL, :] for c in range(NBP // SUBL)] for h in HS]
    rank_c = [[jnp.zeros((SUBL, TQ), jnp.int32) for c in range(NBP // SUBL)] for h in HS]
    jj_c = lax.broadcasted_iota(jnp.int32, (SUBL, TQ), 0)
    for j2 in range(NB):
        for h in HS:
            row = val[h][j2:j2 + 1, :]
            for c in range(NBP // SUBL):
                v = chunks[h][c]
                if c * SUBL > j2:
                    beat = jnp.where(row >= v, 1, 0)
                elif (c + 1) * SUBL - 1 < j2:
                    beat = jnp.where(row > v, 1, 0)
                else:
                    later = jnp.where(jj_c + c * SUBL > j2, 1, 0)
                    beat = jnp.where(row > v, 1, jnp.where(row == v, later, 0))
                rank_c[h][c] = rank_c[h][c] + beat
    rank = [jnp.concatenate(rank_c[h], axis=0) for h in HS]
    fronts = []
    for h in HS:
        sel_neg = jnp.where(jj > cur, -MASK_BIG, jnp.where(rank[h] < n_sel, 0.0, -MASK_BIG))
        if NBP < LANES:
            sel_neg = jnp.concatenate([sel_neg, jnp.zeros((LANES - NBP, TQ), F32)], axis=0)
        sel_q = sel_neg.T.astype(BF16)
        q_aug = jnp.concatenate([qr[h], jnp.concatenate([sel_q] * G, axis=0)], axis=1)
        fronts.append((o_c[h], o_w[h], q_aug))

    def sel_scores(h, kt, slot):
        k0 = pl.multiple_of(kt * TK, TK)
        k = ks_ref[0, h, pl.ds(k0, TK), :]
        s = lax.dot_general(k, fronts[h][2], NT_DIMS, preferred_element_type=F32)
        s_buf[h, slot] = s
        smax_buf[h, slot] = jnp.max(s, axis=0, keepdims=True)

    def sel_accumulate(h, kt, slot):
        k0 = pl.multiple_of(kt * TK, TK)
        vt = vst_ref[0, h, :, pl.ds(k0, TK)]
        m = m_buf[h]
        m_new = jnp.maximum(m, smax_buf[h, slot])
        p = jnp.exp2(s_buf[h, slot] - m_new).astype(BF16)
        acc_buf[h] = jnp.exp2(m - m_new) * acc_buf[h] + jnp.dot(vt, p, preferred_element_type=F32)
        m_buf[h] = m_new

    n_kt = (t0 + TQ + TK - 1) // TK
    n_loop = n_kt - 1
    for h in HS:
        m_buf[h] = jnp.full((1, R), m_floor, F32)
        acc_buf[h] = jnp.zeros((V_AUG_ROWS, R), F32)
        sel_scores(h, 0, 0)

    def pair_body(k, carry):
        for h in HS:
            sel_scores(h, 2 * k + 1, 1)
            sel_accumulate(h, 2 * k, 0)

        @pl.when(2 * k + 1 < n_loop)
        def _():
            for h in HS:
                sel_scores(h, 2 * k + 2, 0)
                sel_accumulate(h, 2 * k + 1, 1)
        return carry

    lax.fori_loop(0, (n_loop + 1) // 2, pair_body, 0)
    krow = n_loop * TK + lax.broadcasted_iota(jnp.int32, (TK, 1), 0)
    causal = tile_g(jnp.where(krow <= tT, 0.0, NEG_BIG))
    k_last = pl.multiple_of(n_loop * TK, TK)
    last_slot = n_loop % 2
    s_fin = [s_buf[h, last_slot] + causal for h in HS]
    m_fin = [jnp.maximum(m_buf[h], jnp.max(s_fin[h], axis=0, keepdims=True)) for h in HS]
    p_fin = [jnp.exp2(s_fin[h] - m_fin[h]).astype(BF16) for h in HS]
    acc_fin = [jnp.exp2(m_buf[h] - m_fin[h]) * acc_buf[h]
               + jnp.dot(vst_ref[0, h, :, pl.ds(k_last, TK)], p_fin[h], preferred_element_type=F32) for h in HS]
    gate_all = jax.nn.sigmoid(g_ref[0]).T
    assert NH == NSA_KV_HEADS
    gates = [gate_all[3 * G * h:3 * G * (h + 1), :] for h in HS]
    for h in HS:
        o_s = acc_fin[h][0:HEAD_DIM] * (1.0 / acc_fin[h][HEAD_DIM:HEAD_DIM + 1])
        o_c, o_w, _ = fronts[h]
        gate = gates[h]
        for g in range(G):
            cols = slice(g * TQ, (g + 1) * TQ)
            o = (gate[3 * g:3 * g + 1, :] * o_c[:, cols] + gate[3 * g + 1:3 * g + 2, :] * o_s[:, cols]
                 + gate[3 * g + 2:3 * g + 3, :] * o_w[:, cols])
            o_ref[0, :, (h * G + g) * HEAD_DIM:(h * G + g + 1) * HEAD_DIM] = o.T.astype(BF16)


def nsa_attention(qr, qn, gates, kc, vct, ks, vst, kw, vwt, overlap_t):
    B, T, _ = qr.shape
    TQ = NSA_TQ
    TK = min(NSA_TK, T)
    WK = min(WINDOW + TQ, T)
    NC = kc.shape[3]
    NB = T // SLC_BLOCK
    NH = NSA_HEADS_PER_STEP
    gw = NH * NSA_GROUP * HEAD_DIM
    qspec = pl.BlockSpec((1, TQ, gw), lambda b, h, i: (b, i, h))
    kspec = pl.BlockSpec((1, NH, T, HEAD_DIM), lambda b, h, i: (b, h, 0, 0))
    vtspec = pl.BlockSpec((1, NH, HEAD_DIM, T), lambda b, h, i: (b, h, 0, 0))
    return pl.pallas_call(
        functools.partial(_nsa_kernel, TQ=TQ, TK=TK, WK=WK, NC=NC, NB=NB, NH=NH),
        grid=(B, NSA_KV_HEADS // NH, T // TQ),
        in_specs=[qspec, qspec,
                  pl.BlockSpec((1, TQ, LANES), lambda b, h, i: (b, i, C_MISC // LANES)),
                  pl.BlockSpec((1, 1, NH, NC, HEAD_DIM), lambda b, h, i: (b, 0, h, 0, 0)),
                  pl.BlockSpec((1, 1, NH, HEAD_DIM, NC), lambda b, h, i: (b, 1, h, 0, 0)),
                  pl.BlockSpec((1, NH, T, HEAD_DIM + LANES), lambda b, h, i: (b, h, 0, 0)),
                  pl.BlockSpec((1, NH, V_AUG_ROWS, T), lambda b, h, i: (b, h, 0, 0)),
                  kspec, vtspec,
                  pl.BlockSpec(overlap_t.shape, lambda b, h, i: (0, 0))],
        out_specs=pl.BlockSpec((1, TQ, gw), lambda b, h, i: (b, i, h)),
        out_shape=jax.ShapeDtypeStruct((B, T, NSA_HEADS * HEAD_DIM), BF16),
        scratch_shapes=[pltpu.VMEM((NH, 2, TK, NSA_GROUP * TQ), F32),
                        pltpu.VMEM((NH, 2, 1, NSA_GROUP * TQ), F32),
                        pltpu.VMEM((NH, 1, NSA_GROUP * TQ), F32),
                        pltpu.VMEM((NH, V_AUG_ROWS, NSA_GROUP * TQ), F32)],
        compiler_params=_params("parallel", "parallel", "arbitrary"),
        name="nsa_attention",
    )(qr, qn, gates, kc, vct, ks, vst, kw, vwt, overlap_t)


def _gla_kernel(q_ref, k_ref, v_ref, gg_ref, misc_ref, wa_ref, ba_ref, nw_ref, o_ref, s_ref, la_ref, *, TC):
    C = GLA_CHUNK

    @pl.when(pl.program_id(1) == 0)
    def _():
        s_ref[...] = jnp.zeros_like(s_ref)

    z = jnp.dot(misc_ref[0].astype(BF16), wa_ref[...], preferred_element_type=F32) + ba_ref[...]
    la_ref[...] = (jnp.minimum(z, 0.0) - jnp.log1p(jnp.exp(-jnp.abs(z)))) * (1.0 / GLA_GATE_NORM)
    ri = lax.broadcasted_iota(jnp.int32, (C, C), 0)
    ci = lax.broadcasted_iota(jnp.int32, (C, C), 1)
    causal = ri >= ci
    tri = jnp.where(causal, 1.0, 0.0)
    nw = nw_ref[...]

    n_c = TC // C
    pairs = [(c, h) for c in range(n_c) for h in range(GLA_HEADS)]
    rows = [slice(c * C, (c + 1) * C) for c in range(n_c)]
    b_all = [jnp.dot(tri, la_ref[rows[c], :], preferred_element_type=F32, precision=lax.Precision.HIGHEST)
             for c in range(n_c)]
    qd, kv, decay, intra = {}, {}, {}, {}
    for c, h in pairs:
        b = b_all[c][:, h * GLA_DK:(h + 1) * GLA_DK]
        bl = b[C - 1:C, :]
        q = q_ref[0, rows[c], h * GLA_DK:(h + 1) * GLA_DK] * (GLA_DK ** -0.5)
        k = k_ref[0, rows[c], h * GLA_DK:(h + 1) * GLA_DK]
        v = v_ref[0, rows[c], h * GLA_DV:(h + 1) * GLA_DV].astype(BF16)
        qd[c, h] = (q * jnp.exp(b)).astype(BF16)
        ki = (k * jnp.exp(-b)).astype(BF16)
        kst = (k * jnp.exp(bl - b)).T.astype(BF16)
        decay[c, h] = jnp.exp(b.T[:, C - 1:C])
        a = lax.dot_general(qd[c, h], ki, NT_DIMS, preferred_element_type=F32)
        a = jnp.where(causal, a, 0.0).astype(BF16)
        intra[c, h] = jnp.dot(a, v, preferred_element_type=F32)
        kv[c, h] = jnp.dot(kst, v, preferred_element_type=F32)
    state = [s_ref[h] for h in range(GLA_HEADS)]
    for c, h in pairs:
        o = intra[c, h] + jnp.dot(qd[c, h], state[h].astype(BF16), preferred_element_type=F32)
        state[h] = state[h] * decay[c, h] + kv[c, h]
        o = o * lax.rsqrt(jnp.mean(o * o, axis=-1, keepdims=True) + NORM_EPS) * nw
        gg = gg_ref[0, rows[c], h * GLA_DV:(h + 1) * GLA_DV]
        o_ref[0, rows[c], h * GLA_DV:(h + 1) * GLA_DV] = (o * (gg * jax.nn.sigmoid(gg))).astype(BF16)
    for h in range(GLA_HEADS):
        s_ref[h] = state[h]


def gla_mixer(proj, wa_pad, ba, nw):
    B, T, _ = proj.shape
    TC = min(GLA_TILE, T)
    qk_w = GLA_HEADS * GLA_DK
    v_w = GLA_HEADS * GLA_DV
    return pl.pallas_call(
        functools.partial(_gla_kernel, TC=TC),
        grid=(B, T // TC),
        in_specs=[pl.BlockSpec((1, TC, qk_w), lambda b, i: (b, i, C_GQ // qk_w)),
                  pl.BlockSpec((1, TC, qk_w), lambda b, i: (b, i, C_GK // qk_w)),
                  pl.BlockSpec((1, TC, v_w), lambda b, i: (b, i, C_GV // v_w)),
                  pl.BlockSpec((1, TC, v_w), lambda b, i: (b, i, C_GG // v_w)),
                  pl.BlockSpec((1, TC, LANES), lambda b, i: (b, i, C_MISC // LANES)),
                  pl.BlockSpec((LANES, qk_w), lambda b, i: (0, 0)),
                  pl.BlockSpec((1, qk_w), lambda b, i: (0, 0)),
                  pl.BlockSpec((1, GLA_DV), lambda b, i: (0, 0))],
        out_specs=pl.BlockSpec((1, TC, v_w), lambda b, i: (b, i, 0)),
        out_shape=jax.ShapeDtypeStruct((B, T, v_w), BF16),
        scratch_shapes=[pltpu.VMEM((GLA_HEADS, GLA_DK, GLA_DV), F32),
                        pltpu.VMEM((TC, qk_w), F32)],
        compiler_params=_params("parallel", "arbitrary"),
        name="gla_mixer",
    )(proj, proj, proj, proj, proj, wa_pad, ba, nw)


def _outproj_kernel(n_ref, g_ref, w_ref, x_ref, m_ref, lg_ref, lb_ref, o_ref, *, half):
    sub = min(EPILOGUE_SUB, o_ref.shape[1])
    for r in range(o_ref.shape[1] // sub):
        rows = slice(r * sub, (r + 1) * sub)
        y = (jnp.dot(n_ref[0, rows, :], w_ref[0:half, :], preferred_element_type=F32)
             + jnp.dot(g_ref[0, rows, :], w_ref[half:, :], preferred_element_type=F32))
        z = DEEPNORM_ALPHA * x_ref[0, rows, :] + (1.0 + m_ref[0, 2:3, :]) * y
        o_ref[0, rows, :] = _layer_norm(z, lg_ref[...], lb_ref[...])


def out_proj_ln(nsa_o, gla_o, w, x, mod, lg, lb):
    B, T, D = x.shape
    half = nsa_o.shape[-1]
    tm = min(ROW_TILE, T)
    vec = pl.BlockSpec((1, D), lambda b, i: (0, 0))
    return pl.pallas_call(
        functools.partial(_outproj_kernel, half=half),
        grid=(B, T // tm),
        in_specs=[pl.BlockSpec((1, tm, half), lambda b, i: (b, i, 0)),
                  pl.BlockSpec((1, tm, gla_o.shape[-1]), lambda b, i: (b, i, 0)),
                  pl.BlockSpec(w.shape, lambda b, i: (0, 0)),
                  pl.BlockSpec((1, tm, D), lambda b, i: (b, i, 0)),
                  pl.BlockSpec((1, 6, D), lambda b, i: (b, 0, 0)),
                  vec, vec],
        out_specs=pl.BlockSpec((1, tm, D), lambda b, i: (b, i, 0)),
        out_shape=jax.ShapeDtypeStruct((B, T, D), F32),
        compiler_params=_params("parallel", "parallel"),
        name="out_proj_ln",
    )(nsa_o, gla_o, w, x, mod, lg, lb)


FFN_TM = 1024
FFN_TF = 256
FFN_SUB = 512
FFN_RING = 3
FFN_VMEM_LIMIT_BYTES = 62 * 1024 * 1024


def _ffn_kernel(x_ref, m_ref, wg_hbm, wu_hbm, wd_hbm, lg_ref, lb_ref, o_ref, h_sc, wg_buf, wu_buf, wd_buf, sem,
                *, sub, tf):
    j = pl.program_id(2)
    nj = pl.num_programs(2)
    last = nj - 1
    tm = o_ref.shape[1]
    step_id = (pl.program_id(0) * pl.num_programs(1) + pl.program_id(1)) * nj + j
    n_steps = pl.num_programs(0) * pl.num_programs(1) * nj

    def tile_copies(s):
        slot = s % FFN_RING
        col = pl.multiple_of((s % nj) * tf, tf)
        return (pltpu.make_async_copy(wg_hbm.at[:, pl.ds(col, tf)], wg_buf.at[slot], sem.at[0, slot]),
                pltpu.make_async_copy(wu_hbm.at[:, pl.ds(col, tf)], wu_buf.at[slot], sem.at[1, slot]),
                pltpu.make_async_copy(wd_hbm.at[pl.ds(col, tf), :], wd_buf.at[slot], sem.at[2, slot]))

    @pl.when(step_id == 0)
    def _():
        for s0 in range(FFN_RING - 1):
            for cp in tile_copies(s0):
                cp.start()

    @pl.when(step_id + (FFN_RING - 1) < n_steps)
    def _():
        for cp in tile_copies(step_id + (FFN_RING - 1)):
            cp.start()

    for cp in tile_copies(step_id):
        cp.wait()
    slot = step_id % FFN_RING

    def step(first, final):
        rows_per = sub // 2 if (first or final) else sub
        wg = wg_buf[slot].astype(BF16)
        wu = wu_buf[slot].astype(BF16)
        wd = wd_buf[slot].astype(BF16)
        for sb in range(tm // rows_per):
            rows = slice(sb * rows_per, (sb + 1) * rows_per)
            if first:
                h_sc[rows, :] = (x_ref[0, rows, :] * (1.0 + m_ref[0, 4:5, :]) + m_ref[0, 3:4, :]).astype(BF16)
            h = h_sc[rows, :]
            a = jnp.dot(h, wg, preferred_element_type=F32)
            u = jnp.dot(h, wu, preferred_element_type=F32)
            y = jnp.dot((a * jax.nn.sigmoid(a) * u).astype(BF16), wd, preferred_element_type=F32)
            if first:
                o_ref[0, rows, :] = y
            elif final:
                z = DEEPNORM_ALPHA * x_ref[0, rows, :] + (1.0 + m_ref[0, 5:6, :]) * (o_ref[0, rows, :] + y)
                o_ref[0, rows, :] = _layer_norm(z, lg_ref[...], lb_ref[...])
            else:
                o_ref[0, rows, :] += y

    pl.when(j == 0)(lambda: step(True, False))
    pl.when(jnp.logical_and(j > 0, j < last))(lambda: step(False, False))
    pl.when(j == last)(lambda: step(False, True))


def ffn_ln(x, mod, wg, wu, wd, lg, lb):
    B, T, D = x.shape
    F = wg.shape[1]
    tm = min(FFN_TM, T)
    tf = FFN_TF
    vec = pl.BlockSpec((1, D), lambda b, i, j: (0, 0))
    return pl.pallas_call(
        functools.partial(_ffn_kernel, sub=min(FFN_SUB, tm), tf=tf),
        grid=(B, T // tm, F // tf),
        in_specs=[pl.BlockSpec((1, tm, D), lambda b, i, j: (b, i, 0)),
                  pl.BlockSpec((1, 6, D), lambda b, i, j: (b, 0, 0)),
                  pl.BlockSpec(memory_space=pl.ANY),
                  pl.BlockSpec(memory_space=pl.ANY),
                  pl.BlockSpec(memory_space=pl.ANY),
                  vec, vec],
        out_specs=pl.BlockSpec((1, tm, D), lambda b, i, j: (b, i, 0)),
        out_shape=jax.ShapeDtypeStruct((B, T, D), F32),
        scratch_shapes=[pltpu.VMEM((tm, D), BF16),
                        pltpu.VMEM((FFN_RING, D, tf), F32),
                        pltpu.VMEM((FFN_RING, D, tf), F32),
                        pltpu.VMEM((FFN_RING, tf, D), F32),
                        pltpu.SemaphoreType.DMA((3, FFN_RING))],
        compiler_params=pltpu.CompilerParams(dimension_semantics=("arbitrary", "arbitrary", "arbitrary"),
                                             vmem_limit_bytes=FFN_VMEM_LIMIT_BYTES),
        name="ffn_ln",
    )(x, mod, wg, wu, wd, lg, lb)


def _router_kernel(x_ref, m_ref, wr_ref, h_ref, lg_ref):
    h = x_ref[0] * (1.0 + m_ref[0, 4:5, :]) + m_ref[0, 3:4, :]
    h_hi = h.astype(BF16)
    h_lo = (h - h_hi.astype(F32)).astype(BF16)
    h_ref[0] = h_hi
    lg_ref[0] = (jnp.dot(h_hi, wr_ref[0], preferred_element_type=F32)
                 + jnp.dot(h_lo, wr_ref[0], preferred_element_type=F32)
                 + jnp.dot(h_hi, wr_ref[1], preferred_element_type=F32))


def moe_router(x, mod, wr_pad):
    B, T, D = x.shape
    tm = min(ROW_TILE, T)
    return pl.pallas_call(
        _router_kernel,
        grid=(B, T // tm),
        in_specs=[pl.BlockSpec((1, tm, D), lambda b, i: (b, i, 0)),
                  pl.BlockSpec((1, 6, D), lambda b, i: (b, 0, 0)),
                  pl.BlockSpec((2, D, LANES), lambda b, i: (0, 0, 0))],
        out_specs=[pl.BlockSpec((1, tm, D), lambda b, i: (b, i, 0)),
                   pl.BlockSpec((1, tm, LANES), lambda b, i: (b, i, 0))],
        out_shape=[jax.ShapeDtypeStruct((B, T, D), BF16),
                   jax.ShapeDtypeStruct((B, T, LANES), F32)],
        compiler_params=_params("parallel", "parallel"),
        name="moe_router",
    )(x, mod, wr_pad)


def _moe_kernel(te_ref, nv_ref, x_ref, wg_ref, wu_ref, wd_ref, o_ref, acc_ref, *, sub):
    i = pl.program_id(0)
    j = pl.program_id(1)
    nv = nv_ref[i]

    @pl.when(j == 0)
    def _():
        acc_ref[...] = jnp.zeros_like(acc_ref)

    n_sub = acc_ref.shape[0] // sub

    def swiglu_rows(rows, wg, wu, wd):
        h = x_ref[rows, :]
        a = jnp.dot(h, wg, preferred_element_type=F32)
        u = jnp.dot(h, wu, preferred_element_type=F32)
        acc_ref[rows, :] += jnp.dot((a * jax.nn.sigmoid(a) * u).astype(BF16), wd, preferred_element_type=F32)

    for n_real in range(1, n_sub + 1):
        @pl.when(nv == n_real)
        def _():
            wg = wg_ref[0].astype(BF16)
            wu = wu_ref[0].astype(BF16)
            wd = wd_ref[0].astype(BF16)
            for r0 in range(0, n_real * sub, MOE_CHAIN):
                swiglu_rows(slice(r0, min(r0 + MOE_CHAIN, n_real * sub)), wg, wu, wd)

    @pl.when(j == pl.num_programs(1) - 1)
    def _():
        o_ref[...] = acc_ref[...].astype(o_ref.dtype)


def moe_experts(tile_e, tile_nv, xg, wg, wu, wd, tm, sub):
    M, D = xg.shape
    F = wg.shape[2]
    tf = MOE_TF
    nj = F // tf
    n_tiles = M // tm

    def wj(i, j, nv):
        return jnp.where(nv[i] > 0, j, nj - 1)

    return pl.pallas_call(
        functools.partial(_moe_kernel, sub=sub),
        grid_spec=pltpu.PrefetchScalarGridSpec(
            num_scalar_prefetch=2,
            grid=(n_tiles, nj),
            in_specs=[pl.BlockSpec((tm, D), lambda i, j, te, nv: (i, 0)),
                      pl.BlockSpec((1, D, tf), lambda i, j, te, nv: (te[i], 0, wj(i, j, nv))),
                      pl.BlockSpec((1, D, tf), lambda i, j, te, nv: (te[i], 0, wj(i, j, nv))),
                      pl.BlockSpec((1, tf, D), lambda i, j, te, nv: (te[i], wj(i, j, nv), 0))],
            out_specs=pl.BlockSpec((tm, D), lambda i, j, te, nv: (i, 0)),
            scratch_shapes=[pltpu.VMEM((tm, D), F32)]),
        out_shape=jax.ShapeDtypeStruct((M, D), BF16),
        compiler_params=pltpu.CompilerParams(dimension_semantics=("parallel", "arbitrary"),
                                             vmem_limit_bytes=BIG_VMEM_LIMIT_BYTES),
        name="moe_experts",
    )(tile_e, tile_nv, xg, wg, wu, wd)


def _combine_kernel(y0_ref, y1_ref, cw_ref, x_ref, m_ref, lg_ref, lb_ref, o_ref):
    cw = cw_ref[0]
    y = y0_ref[0].astype(F32) * cw[:, 0:1] + y1_ref[0].astype(F32) * cw[:, 1:2]
    z = DEEPNORM_ALPHA * x_ref[0] + (1.0 + m_ref[0, 5:6, :]) * y
    o_ref[0] = _layer_norm(z, lg_ref[...], lb_ref[...])


def moe_combine_ln(y0, y1, cw, x, mod, lg, lb):
    B, T, D = x.shape
    tm = min(ROW_TILE, T)
    row = pl.BlockSpec((1, tm, D), lambda b, i: (b, i, 0))
    vec = pl.BlockSpec((1, D), lambda b, i: (0, 0))
    return pl.pallas_call(
        _combine_kernel,
        grid=(B, T // tm),
        in_specs=[row, row, pl.BlockSpec((1, tm, LANES), lambda b, i: (b, i, 0)), row,
                  pl.BlockSpec((1, 6, D), lambda b, i: (b, 0, 0)), vec, vec],
        out_specs=row,
        out_shape=jax.ShapeDtypeStruct((B, T, D), F32),
        compiler_params=_params("parallel", "parallel"),
        name="moe_combine_ln",
    )(y0, y1, cw, x, mod, lg, lb)


MOE_TM = 1024
MOE_SUB = 256
MOE_CHAIN = 1024


def moe_layer(x, mod, w_router, wg, wu, wd, lg, lb):
    B, T, D = x.shape
    N = B * T
    A = N * TOP_K
    tm = MOE_TM
    wr_pad = jnp.zeros((D, LANES), F32).at[:, :N_EXPERTS].set(w_router)
    wr_hi = wr_pad.astype(BF16)
    wr_lo = (wr_pad - wr_hi.astype(F32)).astype(BF16)
    h, logits = moe_router(x, mod, jnp.stack([wr_hi, wr_lo]))
    logits = logits.reshape(N, LANES)[:, :N_EXPERTS]
    top_val, top_idx = lax.top_k(logits, TOP_K)
    comb = jax.nn.softmax(top_val, axis=-1)
    flat_e = top_idx.reshape(-1).astype(jnp.int32)
    onehot = (flat_e[:, None] == jnp.arange(N_EXPERTS, dtype=jnp.int32)[None, :]).astype(jnp.int32)
    csum = jnp.cumsum(onehot, axis=0)
    counts = csum[-1]
    padded = (counts + tm - 1) // tm * tm
    pad_end = jnp.cumsum(padded)
    pad_start = pad_end - padded
    slot = jnp.sum(onehot * (csum + pad_start[None, :]), axis=1) - 1
    n_tiles = -(-A // tm) + N_EXPERTS
    tile_start = jnp.arange(n_tiles, dtype=jnp.int32) * tm
    tile_e = jnp.minimum(jnp.searchsorted(pad_end, tile_start, side='right'), N_EXPERTS - 1).astype(jnp.int32)
    valid = jnp.clip(pad_start[tile_e] + counts[tile_e] - tile_start, 0, tm)
    tile_nv = ((valid + MOE_SUB - 1) // MOE_SUB).astype(jnp.int32)
    order = jnp.argsort(flat_e)
    seg_start = jnp.cumsum(counts) - counts
    slot_id = jnp.arange(n_tiles * tm, dtype=jnp.int32)
    within = slot_id - jnp.repeat(tile_start, tm)
    src = jnp.repeat(seg_start[tile_e] + tile_start - pad_start[tile_e], tm) + within
    slot_tok = jnp.where(within < jnp.repeat(valid, tm),
                         order[jnp.clip(src, 0, A - 1)].astype(jnp.int32) // TOP_K, slot_id % N)
    n_used = pad_end[-1] // tm
    tile_e = jnp.where(jnp.arange(n_tiles) < n_used, tile_e, tile_e[jnp.maximum(n_used - 1, 0)])
    xg = h.reshape(N, D)[slot_tok]
    y = moe_experts(tile_e, tile_nv, xg, wg, wu, wd, tm, MOE_SUB)
    slot_of = slot.reshape(N, TOP_K)
    y0 = y[slot_of[:, 0]].reshape(B, T, D)
    y1 = y[slot_of[:, 1]].reshape(B, T, D)
    cw = jnp.zeros((N, LANES), F32).at[:, :TOP_K].set(comb).reshape(B, T, LANES)
    return moe_combine_ln(y0, y1, cw, x, mod, lg, lb)


def _rope_tables(T):
    half = ROPE_DIMS // 2
    inv = ROPE_THETA ** (-jnp.arange(half, dtype=F32) * 2.0 / ROPE_DIMS)
    ang = jnp.arange(T).astype(F32)[:, None] * inv[None, :]
    cos, sin = jnp.cos(ang), jnp.sin(ang)
    z = jnp.zeros((T, LANES - ROPE_DIMS), F32)
    zh = jnp.zeros((T, half), F32)
    rc = jnp.concatenate([cos, cos, jnp.ones((T, LANES - ROPE_DIMS), F32)], axis=1)
    ra = jnp.concatenate([-sin, zh, z], axis=1)
    rb = jnp.concatenate([zh, sin, z], axis=1)
    return rc, ra, rb


def _overlap_matrix_t(NC, NB):
    nbp = -(-NB // 16) * 16
    c0 = np.arange(NC)[None, :] * CMP_STRIDE
    b0 = np.arange(nbp)[:, None] * SLC_BLOCK
    ov = (c0 < b0 + SLC_BLOCK) & (c0 + CMP_BLOCK > b0) & (np.arange(nbp)[:, None] < NB)
    return jnp.asarray(ov.astype(np.float32), dtype=BF16)


def _reorder_w_in(w):
    D = w.shape[0]
    o_ng = 1024 + 6 * 256
    o_gq = o_ng + 24
    o_gk = o_gq + 512
    o_gv = o_gk + 512
    o_ga = o_gv + 1024
    o_gg = o_ga + GLA_GATE_RANK
    misc = jnp.concatenate([w[:, o_ng:o_gq], w[:, o_ga:o_gg],
                            jnp.zeros((D, LANES - 24 - GLA_GATE_RANK), w.dtype)], axis=1)
    kv = 1024
    w_a = [w[:, 0:1024], w[:, kv + 512:kv + 768], w[:, kv + 1024:kv + 1280],
           w[:, kv + 768:kv + 1024], w[:, kv + 1280:kv + 1536]]
    w_b = [w[:, o_gv:o_ga], w[:, o_gg:o_gg + 1024], w[:, o_gq:o_gk], w[:, o_gk:o_gv],
           w[:, kv:kv + 512], misc]
    return jnp.concatenate(w_a, axis=1).astype(BF16), jnp.concatenate(w_b, axis=1).astype(BF16)


def hybrid_mixer_ln(x, mod, w_in_r, cmp_pe, cmp_w1, cmp_w2, wa_pad, ba, nw, w_out, lg, lb, tables, overlap):
    w_a, w_b = w_in_r
    qr, qn, ks, vs, kw, vw = nsa_proj(x, mod, w_a, *tables)
    proj = in_proj(x, mod, w_b)
    kc, vct = nsa_compress(proj, cmp_pe, cmp_w1, cmp_w2)
    nsa_o = nsa_attention(qr, qn, proj, kc, vct, ks, vs, kw, vw, overlap)
    gla_o = gla_mixer(proj, wa_pad, ba, nw)
    return out_proj_ln(nsa_o, gla_o, w_out, x, mod, lg, lb)


def kernel(x, c, w_ada, b_ada, w_in, cmp_pos_k, cmp_w1_k, cmp_w2_k, cmp_pos_v, cmp_w1_v, cmp_w2_v, gla_w_a2, gla_b_a, gla_norm_w, w_out, ln_mix_g, ln_mix_b, ln_ffn_g, ln_ffn_b, ffn_w_gate, ffn_w_up, ffn_w_down, moe_router, moe_w_gate, moe_w_up, moe_w_down):
    B, T, D = x.shape
    L = w_ada.shape[0]
    c_pad = jnp.zeros((8, D), F32).at[:B].set(c)
    mod_all = ada_mod(c_pad, w_ada, b_ada.reshape(L, 1, 6 * D))[:, :B].reshape(L, B, 6, D)
    tables = _rope_tables(T)
    overlap = _overlap_matrix_t(T // CMP_STRIDE, T // SLC_BLOCK)
    for layer in range(L):
        mod = mod_all[layer]
        wa_pad = jnp.zeros((LANES, GLA_HEADS * GLA_DK), F32).at[MISC_GA:MISC_GA + GLA_GATE_RANK].set(
            gla_w_a2[layer]).astype(BF16)
        x = hybrid_mixer_ln(
            x, mod, _reorder_w_in(w_in[layer]),
            jnp.stack([cmp_pos_k[layer], cmp_pos_v[layer]]),
            jnp.stack([cmp_w1_k[layer], cmp_w1_v[layer]]).astype(BF16),
            jnp.stack([cmp_w2_k[layer], cmp_w2_v[layer]]).astype(BF16),
            wa_pad, gla_b_a[layer].reshape(1, -1), gla_norm_w[layer].reshape(1, -1),
            w_out[layer].astype(BF16), ln_mix_g[layer].reshape(1, D), ln_mix_b[layer].reshape(1, D),
            tables, overlap)
        lg = ln_ffn_g[layer].reshape(1, D)
        lb = ln_ffn_b[layer].reshape(1, D)
        i = layer // 2
        if layer % 2 == 0:
            x = ffn_ln(x, mod, ffn_w_gate[i], ffn_w_up[i], ffn_w_down[i], lg, lb)
        else:
            x = moe_layer(x, mod, moe_router[i], moe_w_gate[i], moe_w_up[i], moe_w_down[i], lg, lb)
    return x
```

```python
import functools

import numpy as np
import jax
import jax.numpy as jnp
from jax import lax
from jax.experimental import pallas as pl
from jax.experimental.pallas import tpu as pltpu

F32 = jnp.float32
BF16 = jnp.bfloat16

D_MODEL = 2048
DEPTH = 2
HEAD_DIM = 128
NSA_HEADS = 8
NSA_KV_HEADS = 2
NSA_GROUP = 4
CMP_BLOCK = 32
CMP_STRIDE = 16
CMP_HIDDEN = 256
SLC_BLOCK = 64
SLC_TOPN = 16
WINDOW = 512
FORCE_SCORE = 1e9
GLA_DV = 256
GLA_HEADS = 4
GLA_DK = 128
GLA_GATE_RANK = 16
GLA_GATE_NORM = 16.0
GLA_CHUNK = 64
ROPE_THETA = 500000.0
ROPE_DIMS = 32
N_EXPERTS = 8
TOP_K = 2
LN_EPS = 1e-5
NORM_EPS = 1e-6
DEEPNORM_ALPHA = (2 * DEPTH) ** 0.25
NEG_BIG = -1e30
MASK_BIG = 2.0 ** 100
V_AUG_ROWS = HEAD_DIM + 16
NSA_HEADS_PER_STEP = 2
NSA_TQ = 128
NSA_TK = 512
ROW_TILE = 512
EPILOGUE_SUB = 128
ADA_TN = 2048
MOE_TF = 512
GLA_TILE = 512
LOG2_E = 1.4426950408889634

VMEM_LIMIT_BYTES = 56 * 1024 * 1024
BIG_VMEM_LIMIT_BYTES = 60 * 1024 * 1024
LANES = 128

A_NQ = 0
A_KS = 1024
A_KW = 1280
A_VS = 1536
A_VW = 1792
A_W = 2048
C_GV = 0
C_GG = 1024
C_GQ = 2048
C_GK = 2560
C_KC = 3072
C_VC = 3328
C_MISC = 3584
PROJ_W = 3712
MISC_GA = 24

NT_DIMS = (((1,), (1,)), ((), ()))


def _params(*sem):
    return pltpu.CompilerParams(dimension_semantics=sem, vmem_limit_bytes=VMEM_LIMIT_BYTES)


def _layer_norm(z, g, b):
    mu = jnp.mean(z, axis=-1, keepdims=True)
    zc = z - mu
    var = jnp.mean(zc * zc, axis=-1, keepdims=True)
    return zc * lax.rsqrt(var + LN_EPS) * g + b


def _ada_kernel(c_ref, w_ref, b_ref, o_ref):
    c = c_ref[...]
    cond = c * jax.nn.sigmoid(c)
    o_ref[0] = jnp.dot(cond.astype(BF16), w_ref[0].astype(BF16),
                       preferred_element_type=F32) + b_ref[0]


def ada_mod(c_pad, w_ada, b_ada):
    L, D, N = w_ada.shape
    tn = ADA_TN
    return pl.pallas_call(
        _ada_kernel,
        grid=(L, N // tn),
        in_specs=[pl.BlockSpec((8, D), lambda l, j: (0, 0)),
                  pl.BlockSpec((1, D, tn), lambda l, j: (l, 0, j)),
                  pl.BlockSpec((1, 1, tn), lambda l, j: (l, 0, j))],
        out_specs=pl.BlockSpec((1, 8, tn), lambda l, j: (l, 0, j)),
        out_shape=jax.ShapeDtypeStruct((L, 8, N), F32),
        compiler_params=_params("parallel", "parallel"),
        name="ada_mod",
    )(c_pad, w_ada, b_ada)


def _inproj_kernel(x_ref, m_ref, w_ref, o_ref):
    h = x_ref[0] * (1.0 + m_ref[0, 1:2, :]) + m_ref[0, 0:1, :]
    o_ref[0] = jnp.dot(h.astype(BF16), w_ref[...], preferred_element_type=F32)


def in_proj(x, mod, w):
    B, T, D = x.shape
    N = w.shape[1]
    tm = min(ROW_TILE, T)
    return pl.pallas_call(
        _inproj_kernel,
        grid=(B, T // tm),
        in_specs=[pl.BlockSpec((1, tm, D), lambda b, i: (b, i, 0)),
                  pl.BlockSpec((1, 6, D), lambda b, i: (b, 0, 0)),
                  pl.BlockSpec((D, N), lambda b, i: (0, 0))],
        out_specs=pl.BlockSpec((1, tm, N), lambda b, i: (b, i, 0)),
        out_shape=jax.ShapeDtypeStruct((B, T, N), F32),
        compiler_params=_params("parallel", "parallel"),
        name="in_proj",
    )(x, mod, w)


def _nsa_proj_kernel(x_ref, m_ref, w_ref, rc_ref, ra_ref, rb_ref,
                     qr_ref, qn_ref, ks_ref, vs_ref, kw_ref, vw_ref, *, sub):
    tm = x_ref.shape[1]
    scale = HEAD_DIM ** -0.5 * LOG2_E
    for r in range(tm // sub):
        rows = slice(r * sub, (r + 1) * sub)
        rc = rc_ref[rows, :]
        ra = ra_ref[rows, :]
        rb = rb_ref[rows, :]

        def rope(xh):
            return (xh * rc + pltpu.roll(xh, LANES - ROPE_DIMS // 2, 1) * ra
                    + pltpu.roll(xh, ROPE_DIMS // 2, 1) * rb)

        h = x_ref[0, rows, :] * (1.0 + m_ref[0, 1:2, :]) + m_ref[0, 0:1, :]
        p = jnp.dot(h.astype(BF16), w_ref[...], preferred_element_type=F32)
        for hq in range(NSA_HEADS):
            xh = p[:, A_NQ + hq * HEAD_DIM:A_NQ + (hq + 1) * HEAD_DIM]
            qr_ref[0, rows, hq * HEAD_DIM:(hq + 1) * HEAD_DIM] = (rope(xh) * scale).astype(BF16)
            qn_ref[0, rows, hq * HEAD_DIM:(hq + 1) * HEAD_DIM] = (xh * scale).astype(BF16)
        key_blk = (pl.program_id(1) * tm + r * sub + lax.broadcasted_iota(jnp.int32, (sub, LANES), 0)) // SLC_BLOCK
        blk_onehot = jnp.where(key_blk == lax.broadcasted_iota(jnp.int32, (sub, LANES), 1), 1.0, 0.0).astype(BF16)
        for hk in range(NSA_KV_HEADS):
            ks_ref[0, hk, rows, 0:HEAD_DIM] = rope(p[:, A_KS + hk * HEAD_DIM:A_KS + (hk + 1) * HEAD_DIM]).astype(BF16)
            ks_ref[0, hk, rows, HEAD_DIM:] = blk_onehot
            kw_ref[0, hk, rows, :] = rope(p[:, A_KW + hk * HEAD_DIM:A_KW + (hk + 1) * HEAD_DIM]).astype(BF16)
            vs_ref[0, hk, 0:HEAD_DIM, rows] = p[:, A_VS + hk * HEAD_DIM:A_VS + (hk + 1) * HEAD_DIM].T.astype(BF16)
            vs_ref[0, hk, HEAD_DIM:, rows] = jnp.ones((V_AUG_ROWS - HEAD_DIM, sub), BF16)
            vw_ref[0, hk, :, rows] = p[:, A_VW + hk * HEAD_DIM:A_VW + (hk + 1) * HEAD_DIM].T.astype(BF16)


def nsa_proj(x, mod, w, rc, ra, rb):
    B, T, D = x.shape
    tm = min(ROW_TILE, T)
    assert T // SLC_BLOCK <= LANES
    qw = NSA_HEADS * HEAD_DIM
    qspec = pl.BlockSpec((1, tm, qw), lambda b, i: (b, i, 0))
    tspec = pl.BlockSpec((tm, LANES), lambda b, i: (i, 0))
    return pl.pallas_call(
        functools.partial(_nsa_proj_kernel, sub=min(EPILOGUE_SUB, tm)),
        grid=(B, T // tm),
        in_specs=[pl.BlockSpec((1, tm, D), lambda b, i: (b, i, 0)),
                  pl.BlockSpec((1, 6, D), lambda b, i: (b, 0, 0)),
                  pl.BlockSpec((D, A_W), lambda b, i: (0, 0)),
                  tspec, tspec, tspec],
        out_specs=[qspec, qspec,
                   pl.BlockSpec((1, NSA_KV_HEADS, tm, HEAD_DIM + LANES), lambda b, i: (b, 0, i, 0)),
                   pl.BlockSpec((1, NSA_KV_HEADS, V_AUG_ROWS, tm), lambda b, i: (b, 0, 0, i)),
                   pl.BlockSpec((1, NSA_KV_HEADS, tm, HEAD_DIM), lambda b, i: (b, 0, i, 0)),
                   pl.BlockSpec((1, NSA_KV_HEADS, HEAD_DIM, tm), lambda b, i: (b, 0, 0, i))],
        out_shape=[jax.ShapeDtypeStruct((B, T, qw), BF16),
                   jax.ShapeDtypeStruct((B, T, qw), BF16),
                   jax.ShapeDtypeStruct((B, NSA_KV_HEADS, T, HEAD_DIM + LANES), BF16),
                   jax.ShapeDtypeStruct((B, NSA_KV_HEADS, V_AUG_ROWS, T), BF16),
                   jax.ShapeDtypeStruct((B, NSA_KV_HEADS, T, HEAD_DIM), BF16),
                   jax.ShapeDtypeStruct((B, NSA_KV_HEADS, HEAD_DIM, T), BF16)],
        compiler_params=_params("parallel", "parallel"),
        name="nsa_proj",
    )(x, mod, w, rc, ra, rb)


def _compress_kernel(a_ref, pe_ref, w1_ref, w2_ref, o_ref, ot_ref, *, n_half):
    half = CMP_STRIDE

    def part(l0):
        acc = jnp.zeros((n_half, CMP_HIDDEN), F32)
        for l in range(half):
            rows = a_ref[0, pl.ds(l, n_half, stride=half), :] + pe_ref[0, l0 + l:l0 + l + 1, :]
            acc += jnp.dot(rows.astype(BF16), w1_ref[0, (l0 + l) * HEAD_DIM:(l0 + l + 1) * HEAD_DIM, :],
                           preferred_element_type=F32)
        return acc

    first = part(0)
    second = part(half)
    hid = first + pltpu.roll(second, n_half - 1, 0)
    row = lax.broadcasted_iota(jnp.int32, (n_half, 1), 0)
    hid = jnp.where(row < n_half - 1, hid, 0.0)
    act = jax.nn.gelu(hid)
    out = jnp.dot(act.astype(BF16), w2_ref[0], preferred_element_type=F32)
    o_ref[0, 0, 0] = out.astype(BF16)
    ot_ref[0, 0, 0] = out.T.astype(BF16)


def nsa_compress(proj, pe, w1, w2):
    B, T, _ = proj.shape
    n_half = T // CMP_STRIDE
    return pl.pallas_call(
        functools.partial(_compress_kernel, n_half=n_half),
        grid=(B, 2, NSA_KV_HEADS),
        in_specs=[pl.BlockSpec((1, T, HEAD_DIM), lambda b, s, h: (b, 0, C_KC // HEAD_DIM + s * NSA_KV_HEADS + h)),
                  pl.BlockSpec((1, CMP_BLOCK, HEAD_DIM), lambda b, s, h: (s, 0, 0)),
                  pl.BlockSpec((1, CMP_BLOCK * HEAD_DIM, CMP_HIDDEN), lambda b, s, h: (s, 0, 0)),
                  pl.BlockSpec((1, CMP_HIDDEN, HEAD_DIM), lambda b, s, h: (s, 0, 0))],
        out_specs=[pl.BlockSpec((1, 1, 1, n_half, HEAD_DIM), lambda b, s, h: (b, s, h, 0, 0)),
                   pl.BlockSpec((1, 1, 1, HEAD_DIM, n_half), lambda b, s, h: (b, s, h, 0, 0))],
        out_shape=[jax.ShapeDtypeStruct((B, 2, NSA_KV_HEADS, n_half, HEAD_DIM), BF16),
                   jax.ShapeDtypeStruct((B, 2, NSA_KV_HEADS, HEAD_DIM, n_half), BF16)],
        compiler_params=_params("parallel", "parallel", "parallel"),
        name="nsa_compress",
    )(proj, pe, w1, w2)


def _nsa_kernel(qr_ref, qn_ref, g_ref, kc_ref, vct_ref, ks_ref, vst_ref, kw_ref, vwt_ref, ovt_ref, o_ref,
                s_buf, smax_buf, m_buf, acc_buf, *, TQ, TK, WK, NC, NB, NH):
    G = NSA_GROUP
    R = G * TQ
    NBP = ovt_ref.shape[0]
    i = pl.program_id(2)
    t0 = i * TQ
    m_floor = 0.5 * NEG_BIG

    def stack(ref, h):
        x = ref[0]
        return jnp.concatenate([x[:, (h * G + g) * HEAD_DIM:(h * G + g + 1) * HEAD_DIM] for g in range(G)], axis=0)

    def tile_g(x):
        return jnp.concatenate([x] * G, axis=1)

    tT = t0 + lax.broadcasted_iota(jnp.int32, (1, TQ), 1)
    cur = tT // SLC_BLOCK
    n_sel = min(SLC_TOPN, NB)

    HS = range(NH)
    qr = [stack(qr_ref, h) for h in HS]
    qn = [stack(qn_ref, h) for h in HS]

    ws = pl.multiple_of(jnp.maximum(t0 + TQ - WK, 0), LANES)
    s_c = [lax.dot_general(kc_ref[0, 0, h], qn[h], NT_DIMS, preferred_element_type=F32) for h in HS]
    s_w = [lax.dot_general(kw_ref[0, h, pl.ds(ws, WK), :], qr[h], NT_DIMS, preferred_element_type=F32)
           for h in HS]
    cend = lax.broadcasted_iota(jnp.int32, (NC, 1), 0) * CMP_STRIDE + (CMP_BLOCK - 1)
    bias_c = tile_g(jnp.where(cend <= tT, 0.0, NEG_BIG))
    dist = tT - (ws + lax.broadcasted_iota(jnp.int32, (WK, 1), 0))
    bias_w = tile_g(jnp.where(dist >= 0, jnp.where(dist < WINDOW, 0.0, NEG_BIG), NEG_BIG))

    p_cb, e_wb, l_w = [], [], []
    for h in HS:
        sc = s_c[h] + bias_c
        m_c = jnp.maximum(jnp.max(sc, axis=0, keepdims=True), m_floor)
        e_c = jnp.exp2(sc - m_c)
        den = jnp.sum(e_c, axis=0, keepdims=True)
        p_cb.append((e_c * jnp.where(den > 0.0, 1.0 / den, 0.0)).astype(BF16))
        sw = s_w[h] + bias_w
        e_w = jnp.exp2(sw - jnp.max(sw, axis=0, keepdims=True))
        l_w.append(jnp.sum(e_w, axis=0, keepdims=True))
        e_wb.append(e_w.astype(BF16))

    o_c = [jnp.dot(vct_ref[0, 0, h], p_cb[h], preferred_element_type=F32) for h in HS]
    imp4 = [jnp.dot(ovt_ref[...], p_cb[h], preferred_element_type=F32) for h in HS]
    o_w = [jnp.dot(vwt_ref[0, h, :, pl.ds(ws, WK)], e_wb[h], preferred_element_type=F32) * (1.0 / l_w[h])
           for h in HS]

    jj = lax.broadcasted_iota(jnp.int32, (NBP, TQ), 0)
    forced = jnp.where(jj == 0, 1, jnp.where(jj == cur, 1, jnp.where(jj == cur - 1, 1, 0)))
    val = []
    for h in HS:
        impT = imp4[h][:, 0:TQ]
        for g in range(1, G):
            impT = impT + imp4[h][:, g * TQ:(g + 1) * TQ]
        val.append(jnp.where(jj > cur, -jnp.inf, jnp.where(forced > 0, FORCE_SCORE, impT)))
    SUBL = 8
    chunks = [[val[h][c * SUBL:(c + 1) * SUBL, :] for c in range(NBP // SUBL)] for h in HS]
    rank_c = [[jnp.zeros((SUBL, TQ), jnp.int32) for c in range(NBP // SUBL)] for h in HS]
    jj_c = lax.broadcasted_iota(jnp.int32, (SUBL, TQ), 0)
    for j2 in range(NB):
        for h in HS:
            row = val[h][j2:j2 + 1, :]
            for c in range(NBP // SUBL):
                v = chunks[h][c]
                if c * SUBL > j2:
                    beat = jnp.where(row >= v, 1, 0)
                elif (c + 1) * SUBL - 1 < j2:
                    beat = jnp.where(row > v, 1, 0)
                else:
                    later = jnp.where(jj_c + c * SUBL > j2, 1, 0)
                    beat = jnp.where(row > v, 1, jnp.where(row == v, later, 0))
                rank_c[h][c] = rank_c[h][c] + beat
    rank = [jnp.concatenate(rank_c[h], axis=0) for h in HS]
    fronts = []
    for h in HS:
        sel_neg = jnp.where(jj > cur, -MASK_BIG, jnp.where(rank[h] < n_sel, 0.0, -MASK_BIG))
        if NBP < LANES:
            sel_neg = jnp.concatenate([sel_neg, jnp.zeros((LANES - NBP, TQ), F32)], axis=0)
        sel_q = sel_neg.T.astype(BF16)
        q_aug = jnp.concatenate([qr[h], jnp.concatenate([sel_q] * G, axis=0)], axis=1)
        fronts.append((o_c[h], o_w[h], q_aug))

    def sel_scores(h, kt, slot):
        k0 = pl.multiple_of(kt * TK, TK)
        k = ks_ref[0, h, pl.ds(k0, TK), :]
        s = lax.dot_general(k, fronts[h][2], NT_DIMS, preferred_element_type=F32)
        s_buf[h, slot] = s
        smax_buf[h, slot] = jnp.max(s, axis=0, keepdims=True)

    def sel_accumulate(h, kt, slot):
        k0 = pl.multiple_of(kt * TK, TK)
        vt = vst_ref[0, h, :, pl.ds(k0, TK)]
        m = m_buf[h]
        m_new = jnp.maximum(m, smax_buf[h, slot])
        p = jnp.exp2(s_buf[h, slot] - m_new).astype(BF16)
        acc_buf[h] = jnp.exp2(m - m_new) * acc_buf[h] + jnp.dot(vt, p, preferred_element_type=F32)
        m_buf[h] = m_new

    n_kt = (t0 + TQ + TK - 1) // TK
    n_loop = n_kt - 1
    for h in HS:
        m_buf[h] = jnp.full((1, R), m_floor, F32)
        acc_buf[h] = jnp.zeros((V_AUG_ROWS, R), F32)
        sel_scores(h, 0, 0)

    def pair_body(k, carry):
        for h in HS:
            sel_scores(h, 2 * k + 1, 1)
            sel_accumulate(h, 2 * k, 0)

        @pl.when(2 * k + 1 < n_loop)
        def _():
            for h in HS:
                sel_scores(h, 2 * k + 2, 0)
                sel_accumulate(h, 2 * k + 1, 1)
        return carry

    lax.fori_loop(0, (n_loop + 1) // 2, pair_body, 0)
    krow = n_loop * TK + lax.broadcasted_iota(jnp.int32, (TK, 1), 0)
    causal = tile_g(jnp.where(krow <= tT, 0.0, NEG_BIG))
    k_last = pl.multiple_of(n_loop * TK, TK)
    last_slot = n_loop % 2
    s_fin = [s_buf[h, last_slot] + causal for h in HS]
    m_fin = [jnp.maximum(m_buf[h], jnp.max(s_fin[h], axis=0, keepdims=True)) for h in HS]
    p_fin = [jnp.exp2(s_fin[h] - m_fin[h]).astype(BF16) for h in HS]
    acc_fin = [jnp.exp2(m_buf[h] - m_fin[h]) * acc_buf[h]
               + jnp.dot(vst_ref[0, h, :, pl.ds(k_last, TK)], p_fin[h], preferred_element_type=F32) for h in HS]
    gate_all = jax.nn.sigmoid(g_ref[0]).T
    assert NH == NSA_KV_HEADS
    gates = [gate_all[3 * G * h:3 * G * (h + 1), :] for h in HS]
    for h in HS:
        o_s = acc_fin[h][0:HEAD_DIM] * (1.0 / acc_fin[h][HEAD_DIM:HEAD_DIM + 1])
        o_c, o_w, _ = fronts[h]
        gate = gates[h]
        for g in range(G):
            cols = slice(g * TQ, (g + 1) * TQ)
            o = (gate[3 * g:3 * g + 1, :] * o_c[:, cols] + gate[3 * g + 1:3 * g + 2, :] * o_s[:, cols]
                 + gate[3 * g + 2:3 * g + 3, :] * o_w[:, cols])
            o_ref[0, :, (h * G + g) * HEAD_DIM:(h * G + g + 1) * HEAD_DIM] = o.T.astype(BF16)


def nsa_attention(qr, qn, gates, kc, vct, ks, vst, kw, vwt, overlap_t):
    B, T, _ = qr.shape
    TQ = NSA_TQ
    TK = min(NSA_TK, T)
    WK = min(WINDOW + TQ, T)
    NC = kc.shape[3]
    NB = T // SLC_BLOCK
    NH = NSA_HEADS_PER_STEP
    gw = NH * NSA_GROUP * HEAD_DIM
    qspec = pl.BlockSpec((1, TQ, gw), lambda b, h, i: (b, i, h))
    kspec = pl.BlockSpec((1, NH, T, HEAD_DIM), lambda b, h, i: (b, h, 0, 0))
    vtspec = pl.BlockSpec((1, NH, HEAD_DIM, T), lambda b, h, i: (b, h, 0, 0))
    return pl.pallas_call(
        functools.partial(_nsa_kernel, TQ=TQ, TK=TK, WK=WK, NC=NC, NB=NB, NH=NH),
        grid=(B, NSA_KV_HEADS // NH, T // TQ),
        in_specs=[qspec, qspec,
                  pl.BlockSpec((1, TQ, LANES), lambda b, h, i: (b, i, C_MISC // LANES)),
                  pl.BlockSpec((1, 1, NH, NC, HEAD_DIM), lambda b, h, i: (b, 0, h, 0, 0)),
                  pl.BlockSpec((1, 1, NH, HEAD_DIM, NC), lambda b, h, i: (b, 1, h, 0, 0)),
                  pl.BlockSpec((1, NH, T, HEAD_DIM + LANES), lambda b, h, i: (b, h, 0, 0)),
                  pl.BlockSpec((1, NH, V_AUG_ROWS, T), lambda b, h, i: (b, h, 0, 0)),
                  kspec, vtspec,
                  pl.BlockSpec(overlap_t.shape, lambda b, h, i: (0, 0))],
        out_specs=pl.BlockSpec((1, TQ, gw), lambda b, h, i: (b, i, h)),
        out_shape=jax.ShapeDtypeStruct((B, T, NSA_HEADS * HEAD_DIM), BF16),
        scratch_shapes=[pltpu.VMEM((NH, 2, TK, NSA_GROUP * TQ), F32),
                        pltpu.VMEM((NH, 2, 1, NSA_GROUP * TQ), F32),
                        pltpu.VMEM((NH, 1, NSA_GROUP * TQ), F32),
                        pltpu.VMEM((NH, V_AUG_ROWS, NSA_GROUP * TQ), F32)],
        compiler_params=_params("parallel", "parallel", "arbitrary"),
        name="nsa_attention",
    )(qr, qn, gates, kc, vct, ks, vst, kw, vwt, overlap_t)


def _gla_kernel(q_ref, k_ref, v_ref, gg_ref, misc_ref, wa_ref, ba_ref, nw_ref, o_ref, s_ref, la_ref, *, TC):
    C = GLA_CHUNK

    @pl.when(pl.program_id(1) == 0)
    def _():
        s_ref[...] = jnp.zeros_like(s_ref)

    z = jnp.dot(misc_ref[0].astype(BF16), wa_ref[...], preferred_element_type=F32) + ba_ref[...]
    la_ref[...] = (jnp.minimum(z, 0.0) - jnp.log1p(jnp.exp(-jnp.abs(z)))) * (1.0 / GLA_GATE_NORM)
    ri = lax.broadcasted_iota(jnp.int32, (C, C), 0)
    ci = lax.broadcasted_iota(jnp.int32, (C, C), 1)
    causal = ri >= ci
    tri = jnp.where(causal, 1.0, 0.0)
    nw = nw_ref[...]

    n_c = TC // C
    pairs = [(c, h) for c in range(n_c) for h in range(GLA_HEADS)]
    rows = [slice(c * C, (c + 1) * C) for c in range(n_c)]
    b_all = [jnp.dot(tri, la_ref[rows[c], :], preferred_element_type=F32, precision=lax.Precision.HIGHEST)
             for c in range(n_c)]
    qd, kv, decay, intra = {}, {}, {}, {}
    for c, h in pairs:
        b = b_all[c][:, h * GLA_DK:(h + 1) * GLA_DK]
        bl = b[C - 1:C, :]
        q = q_ref[0, rows[c], h * GLA_DK:(h + 1) * GLA_DK] * (GLA_DK ** -0.5)
        k = k_ref[0, rows[c], h * GLA_DK:(h + 1) * GLA_DK]
        v = v_ref[0, rows[c], h * GLA_DV:(h + 1) * GLA_DV].astype(BF16)
        qd[c, h] = (q * jnp.exp(b)).astype(BF16)
        ki = (k * jnp.exp(-b)).astype(BF16)
        kst = (k * jnp.exp(bl - b)).T.astype(BF16)
        decay[c, h] = jnp.exp(b.T[:, C - 1:C])
        a = lax.dot_general(qd[c, h], ki, NT_DIMS, preferred_element_type=F32)
        a = jnp.where(causal, a, 0.0).astype(BF16)
        intra[c, h] = jnp.dot(a, v, preferred_element_type=F32)
        kv[c, h] = jnp.dot(kst, v, preferred_element_type=F32)
    state = [s_ref[h] for h in range(GLA_HEADS)]
    for c, h in pairs:
        o = intra[c, h] + jnp.dot(qd[c, h], state[h].astype(BF16), preferred_element_type=F32)
        state[h] = state[h] * decay[c, h] + kv[c, h]
        o = o * lax.rsqrt(jnp.mean(o * o, axis=-1, keepdims=True) + NORM_EPS) * nw
        gg = gg_ref[0, rows[c], h * GLA_DV:(h + 1) * GLA_DV]
        o_ref[0, rows[c], h * GLA_DV:(h + 1) * GLA_DV] = (o * (gg * jax.nn.sigmoid(gg))).astype(BF16)
    for h in range(GLA_HEADS):
        s_ref[h] = state[h]


def gla_mixer(proj, wa_pad, ba, nw):
    B, T, _ = proj.shape
    TC = min(GLA_TILE, T)
    qk_w = GLA_HEADS * GLA_DK
    v_w = GLA_HEADS * GLA_DV
    return pl.pallas_call(
        functools.partial(_gla_kernel, TC=TC),
        grid=(B, T // TC),
        in_specs=[pl.BlockSpec((1, TC, qk_w), lambda b, i: (b, i, C_GQ // qk_w)),
                  pl.BlockSpec((1, TC, qk_w), lambda b, i: (b, i, C_GK // qk_w)),
                  pl.BlockSpec((1, TC, v_w), lambda b, i: (b, i, C_GV // v_w)),
                  pl.BlockSpec((1, TC, v_w), lambda b, i: (b, i, C_GG // v_w)),
                  pl.BlockSpec((1, TC, LANES), lambda b, i: (b, i, C_MISC // LANES)),
                  pl.BlockSpec((LANES, qk_w), lambda b, i: (0, 0)),
                  pl.BlockSpec((1, qk_w), lambda b, i: (0, 0)),
                  pl.BlockSpec((1, GLA_DV), lambda b, i: (0, 0))],
        out_specs=pl.BlockSpec((1, TC, v_w), lambda b, i: (b, i, 0)),
        out_shape=jax.ShapeDtypeStruct((B, T, v_w), BF16),
        scratch_shapes=[pltpu.VMEM((GLA_HEADS, GLA_DK, GLA_DV), F32),
                        pltpu.VMEM((TC, qk_w), F32)],
        compiler_params=_params("parallel", "arbitrary"),
        name="gla_mixer",
    )(proj, proj, proj, proj, proj, wa_pad, ba, nw)


def _outproj_kernel(n_ref, g_ref, w_ref, x_ref, m_ref, lg_ref, lb_ref, o_ref, *, half):
    sub = min(EPILOGUE_SUB, o_ref.shape[1])
    for r in range(o_ref.shape[1] // sub):
        rows = slice(r * sub, (r + 1) * sub)
        y = (jnp.dot(n_ref[0, rows, :], w_ref[0:half, :], preferred_element_type=F32)
             + jnp.dot(g_ref[0, rows, :], w_ref[half:, :], preferred_element_type=F32))
        z = DEEPNORM_ALPHA * x_ref[0, rows, :] + (1.0 + m_ref[0, 2:3, :]) * y
        o_ref[0, rows, :] = _layer_norm(z, lg_ref[...], lb_ref[...])


def out_proj_ln(nsa_o, gla_o, w, x, mod, lg, lb):
    B, T, D = x.shape
    half = nsa_o.shape[-1]
    tm = min(ROW_TILE, T)
    vec = pl.BlockSpec((1, D), lambda b, i: (0, 0))
    return pl.pallas_call(
        functools.partial(_outproj_kernel, half=half),
        grid=(B, T // tm),
        in_specs=[pl.BlockSpec((1, tm, half), lambda b, i: (b, i, 0)),
                  pl.BlockSpec((1, tm, gla_o.shape[-1]), lambda b, i: (b, i, 0)),
                  pl.BlockSpec(w.shape, lambda b, i: (0, 0)),
                  pl.BlockSpec((1, tm, D), lambda b, i: (b, i, 0)),
                  pl.BlockSpec((1, 6, D), lambda b, i: (b, 0, 0)),
                  vec, vec],
        out_specs=pl.BlockSpec((1, tm, D), lambda b, i: (b, i, 0)),
        out_shape=jax.ShapeDtypeStruct((B, T, D), F32),
        compiler_params=_params("parallel", "parallel"),
        name="out_proj_ln",
    )(nsa_o, gla_o, w, x, mod, lg, lb)


FFN_TM = 1024
FFN_TF = 256
FFN_SUB = 512


def _ffn_kernel(x_ref, m_ref, wg_ref, wu_ref, wd_ref, lg_ref, lb_ref, o_ref, h_sc, *, sub):
    j = pl.program_id(2)
    last = pl.num_programs(2) - 1
    tm = o_ref.shape[1]

    def step(first, final):
        rows_per = sub // 2 if (first or final) else sub
        wg = wg_ref[0].astype(BF16)
        wu = wu_ref[0].astype(BF16)
        wd = wd_ref[...].astype(BF16)
        for sb in range(tm // rows_per):
            rows = slice(sb * rows_per, (sb + 1) * rows_per)
            if first:
                h_sc[rows, :] = (x_ref[0, rows, :] * (1.0 + m_ref[0, 4:5, :]) + m_ref[0, 3:4, :]).astype(BF16)
            h = h_sc[rows, :]
            a = jnp.dot(h, wg, preferred_element_type=F32)
            u = jnp.dot(h, wu, preferred_element_type=F32)
            y = jnp.dot((a * jax.nn.sigmoid(a) * u).astype(BF16), wd, preferred_element_type=F32)
            if first:
                o_ref[0, rows, :] = y
            elif final:
                z = DEEPNORM_ALPHA * x_ref[0, rows, :] + (1.0 + m_ref[0, 5:6, :]) * (o_ref[0, rows, :] + y)
                o_ref[0, rows, :] = _layer_norm(z, lg_ref[...], lb_ref[...])
            else:
                o_ref[0, rows, :] += y

    pl.when(j == 0)(lambda: step(True, False))
    pl.when(jnp.logical_and(j > 0, j < last))(lambda: step(False, False))
    pl.when(j == last)(lambda: step(False, True))


def ffn_ln(x, mod, wg, wu, wd, lg, lb):
    B, T, D = x.shape
    F = wg.shape[1]
    tm = min(FFN_TM, T)
    tf = FFN_TF
    vec = pl.BlockSpec((1, D), lambda b, i, j: (0, 0))
    wg = wg.reshape(D, F // tf, tf).transpose(1, 0, 2)
    wu = wu.reshape(D, F // tf, tf).transpose(1, 0, 2)
    return pl.pallas_call(
        functools.partial(_ffn_kernel, sub=min(FFN_SUB, tm)),
        grid=(B, T // tm, F // tf),
        in_specs=[pl.BlockSpec((1, tm, D), lambda b, i, j: (b, i, 0)),
                  pl.BlockSpec((1, 6, D), lambda b, i, j: (b, 0, 0)),
                  pl.BlockSpec((1, D, tf), lambda b, i, j: (j, 0, 0)),
                  pl.BlockSpec((1, D, tf), lambda b, i, j: (j, 0, 0)),
                  pl.BlockSpec((tf, D), lambda b, i, j: (j, 0)),
                  vec, vec],
        out_specs=pl.BlockSpec((1, tm, D), lambda b, i, j: (b, i, 0)),
        out_shape=jax.ShapeDtypeStruct((B, T, D), F32),
        scratch_shapes=[pltpu.VMEM((tm, D), BF16)],
        compiler_params=pltpu.CompilerParams(dimension_semantics=("parallel", "parallel", "arbitrary"),
                                             vmem_limit_bytes=BIG_VMEM_LIMIT_BYTES),
        name="ffn_ln",
    )(x, mod, wg, wu, wd, lg, lb)


def _router_kernel(x_ref, m_ref, wr_ref, h_ref, lg_ref):
    h = x_ref[0] * (1.0 + m_ref[0, 4:5, :]) + m_ref[0, 3:4, :]
    h_hi = h.astype(BF16)
    h_lo = (h - h_hi.astype(F32)).astype(BF16)
    h_ref[0] = h_hi
    lg_ref[0] = (jnp.dot(h_hi, wr_ref[0], preferred_element_type=F32)
                 + jnp.dot(h_lo, wr_ref[0], preferred_element_type=F32)
                 + jnp.dot(h_hi, wr_ref[1], preferred_element_type=F32))


def moe_router(x, mod, wr_pad):
    B, T, D = x.shape
    tm = min(ROW_TILE, T)
    return pl.pallas_call(
        _router_kernel,
        grid=(B, T // tm),
        in_specs=[pl.BlockSpec((1, tm, D), lambda b, i: (b, i, 0)),
                  pl.BlockSpec((1, 6, D), lambda b, i: (b, 0, 0)),
                  pl.BlockSpec((2, D, LANES), lambda b, i: (0, 0, 0))],
        out_specs=[pl.BlockSpec((1, tm, D), lambda b, i: (b, i, 0)),
                   pl.BlockSpec((1, tm, LANES), lambda b, i: (b, i, 0))],
        out_shape=[jax.ShapeDtypeStruct((B, T, D), BF16),
                   jax.ShapeDtypeStruct((B, T, LANES), F32)],
        compiler_params=_params("parallel", "parallel"),
        name="moe_router",
    )(x, mod, wr_pad)


def _moe_kernel(te_ref, nv_ref, x_ref, wg_ref, wu_ref, wd_ref, o_ref, acc_ref, *, sub):
    i = pl.program_id(0)
    j = pl.program_id(1)
    nv = nv_ref[i]

    @pl.when(j == 0)
    def _():
        acc_ref[...] = jnp.zeros_like(acc_ref)

    n_sub = acc_ref.shape[0] // sub

    def swiglu_rows(rows, wg, wu, wd):
        h = x_ref[rows, :]
        a = jnp.dot(h, wg, preferred_element_type=F32)
        u = jnp.dot(h, wu, preferred_element_type=F32)
        acc_ref[rows, :] += jnp.dot((a * jax.nn.sigmoid(a) * u).astype(BF16), wd, preferred_element_type=F32)

    for n_real in range(1, n_sub + 1):
        @pl.when(nv == n_real)
        def _():
            wg = wg_ref[0].astype(BF16)
            wu = wu_ref[0].astype(BF16)
            wd = wd_ref[0].astype(BF16)
            for r0 in range(0, n_real * sub, MOE_CHAIN):
                swiglu_rows(slice(r0, min(r0 + MOE_CHAIN, n_real * sub)), wg, wu, wd)

    @pl.when(j == pl.num_programs(1) - 1)
    def _():
        o_ref[...] = acc_ref[...].astype(o_ref.dtype)


def moe_experts(tile_e, tile_nv, xg, wg, wu, wd, tm, sub):
    M, D = xg.shape
    F = wg.shape[2]
    tf = MOE_TF
    nj = F // tf
    n_tiles = M // tm

    def wj(i, j, nv):
        return jnp.where(nv[i] > 0, j, nj - 1)

    return pl.pallas_call(
        functools.partial(_moe_kernel, sub=sub),
        grid_spec=pltpu.PrefetchScalarGridSpec(
            num_scalar_prefetch=2,
            grid=(n_tiles, nj),
            in_specs=[pl.BlockSpec((tm, D), lambda i, j, te, nv: (i, 0)),
                      pl.BlockSpec((1, D, tf), lambda i, j, te, nv: (te[i], 0, wj(i, j, nv))),
                      pl.BlockSpec((1, D, tf), lambda i, j, te, nv: (te[i], 0, wj(i, j, nv))),
                      pl.BlockSpec((1, tf, D), lambda i, j, te, nv: (te[i], wj(i, j, nv), 0))],
            out_specs=pl.BlockSpec((tm, D), lambda i, j, te, nv: (i, 0)),
            scratch_shapes=[pltpu.VMEM((tm, D), F32)]),
        out_shape=jax.ShapeDtypeStruct((M, D), BF16),
        compiler_params=pltpu.CompilerParams(dimension_semantics=("parallel", "arbitrary"),
                                             vmem_limit_bytes=BIG_VMEM_LIMIT_BYTES),
        name="moe_experts",
    )(tile_e, tile_nv, xg, wg, wu, wd)


def _combine_kernel(y0_ref, y1_ref, cw_ref, x_ref, m_ref, lg_ref, lb_ref, o_ref):
    cw = cw_ref[0]
    y = y0_ref[0].astype(F32) * cw[:, 0:1] + y1_ref[0].astype(F32) * cw[:, 1:2]
    z = DEEPNORM_ALPHA * x_ref[0] + (1.0 + m_ref[0, 5:6, :]) * y
    o_ref[0] = _layer_norm(z, lg_ref[...], lb_ref[...])


def moe_combine_ln(y0, y1, cw, x, mod, lg, lb):
    B, T, D = x.shape
    tm = min(ROW_TILE, T)
    row = pl.BlockSpec((1, tm, D), lambda b, i: (b, i, 0))
    vec = pl.BlockSpec((1, D), lambda b, i: (0, 0))
    return pl.pallas_call(
        _combine_kernel,
        grid=(B, T // tm),
        in_specs=[row, row, pl.BlockSpec((1, tm, LANES), lambda b, i: (b, i, 0)), row,
                  pl.BlockSpec((1, 6, D), lambda b, i: (b, 0, 0)), vec, vec],
        out_specs=row,
        out_shape=jax.ShapeDtypeStruct((B, T, D), F32),
        compiler_params=_params("parallel", "parallel"),
        name="moe_combine_ln",
    )(y0, y1, cw, x, mod, lg, lb)


MOE_TM = 1024
MOE_SUB = 256
MOE_CHAIN = 1024


def moe_layer(x, mod, w_router, wg, wu, wd, lg, lb):
    B, T, D = x.shape
    N = B * T
    A = N * TOP_K
    tm = MOE_TM
    wr_pad = jnp.zeros((D, LANES), F32).at[:, :N_EXPERTS].set(w_router)
    wr_hi = wr_pad.astype(BF16)
    wr_lo = (wr_pad - wr_hi.astype(F32)).astype(BF16)
    h, logits = moe_router(x, mod, jnp.stack([wr_hi, wr_lo]))
    logits = logits.reshape(N, LANES)[:, :N_EXPERTS]
    top_val, top_idx = lax.top_k(logits, TOP_K)
    comb = jax.nn.softmax(top_val, axis=-1)
    flat_e = top_idx.reshape(-1).astype(jnp.int32)
    onehot = (flat_e[:, None] == jnp.arange(N_EXPERTS, dtype=jnp.int32)[None, :]).astype(jnp.int32)
    csum = jnp.cumsum(onehot, axis=0)
    counts = csum[-1]
    padded = (counts + tm - 1) // tm * tm
    pad_end = jnp.cumsum(padded)
    pad_start = pad_end - padded
    slot = jnp.sum(onehot * (csum + pad_start[None, :]), axis=1) - 1
    n_tiles = -(-A // tm) + N_EXPERTS
    tile_start = jnp.arange(n_tiles, dtype=jnp.int32) * tm
    tile_e = jnp.minimum(jnp.searchsorted(pad_end, tile_start, side='right'), N_EXPERTS - 1).astype(jnp.int32)
    valid = jnp.clip(pad_start[tile_e] + counts[tile_e] - tile_start, 0, tm)
    tile_nv = ((valid + MOE_SUB - 1) // MOE_SUB).astype(jnp.int32)
    order = jnp.argsort(flat_e)
    seg_start = jnp.cumsum(counts) - counts
    slot_id = jnp.arange(n_tiles * tm, dtype=jnp.int32)
    within = slot_id - jnp.repeat(tile_start, tm)
    src = jnp.repeat(seg_start[tile_e] + tile_start - pad_start[tile_e], tm) + within
    slot_tok = jnp.where(within < jnp.repeat(valid, tm),
                         order[jnp.clip(src, 0, A - 1)].astype(jnp.int32) // TOP_K, slot_id % N)
    n_used = pad_end[-1] // tm
    tile_e = jnp.where(jnp.arange(n_tiles) < n_used, tile_e, tile_e[jnp.maximum(n_used - 1, 0)])
    xg = h.reshape(N, D)[slot_tok]
    y = moe_experts(tile_e, tile_nv, xg, wg, wu, wd, tm, MOE_SUB)
    slot_of = slot.reshape(N, TOP_K)
    y0 = y[slot_of[:, 0]].reshape(B, T, D)
    y1 = y[slot_of[:, 1]].reshape(B, T, D)
    cw = jnp.zeros((N, LANES), F32).at[:, :TOP_K].set(comb).reshape(B, T, LANES)
    return moe_combine_ln(y0, y1, cw, x, mod, lg, lb)


def _rope_tables(T):
    half = ROPE_DIMS // 2
    inv = ROPE_THETA ** (-jnp.arange(half, dtype=F32) * 2.0 / ROPE_DIMS)
    ang = jnp.arange(T).astype(F32)[:, None] * inv[None, :]
    cos, sin = jnp.cos(ang), jnp.sin(ang)
    z = jnp.zeros((T, LANES - ROPE_DIMS), F32)
    zh = jnp.zeros((T, half), F32)
    rc = jnp.concatenate([cos, cos, jnp.ones((T, LANES - ROPE_DIMS), F32)], axis=1)
    ra = jnp.concatenate([-sin, zh, z], axis=1)
    rb = jnp.concatenate([zh, sin, z], axis=1)
    return rc, ra, rb


def _overlap_matrix_t(NC, NB):
    nbp = -(-NB // 16) * 16
    c0 = np.arange(NC)[None, :] * CMP_STRIDE
    b0 = np.arange(nbp)[:, None] * SLC_BLOCK
    ov = (c0 < b0 + SLC_BLOCK) & (c0 + CMP_BLOCK > b0) & (np.arange(nbp)[:, None] < NB)
    return jnp.asarray(ov.astype(np.float32), dtype=BF16)


def _reorder_w_in(w):
    D = w.shape[0]
    o_ng = 1024 + 6 * 256
    o_gq = o_ng + 24
    o_gk = o_gq + 512
    o_gv = o_gk + 512
    o_ga = o_gv + 1024
    o_gg = o_ga + GLA_GATE_RANK
    misc = jnp.concatenate([w[:, o_ng:o_gq], w[:, o_ga:o_gg],
                            jnp.zeros((D, LANES - 24 - GLA_GATE_RANK), w.dtype)], axis=1)
    kv = 1024
    w_a = [w[:, 0:1024], w[:, kv + 512:kv + 768], w[:, kv + 1024:kv + 1280],
           w[:, kv + 768:kv + 1024], w[:, kv + 1280:kv + 1536]]
    w_b = [w[:, o_gv:o_ga], w[:, o_gg:o_gg + 1024], w[:, o_gq:o_gk], w[:, o_gk:o_gv],
           w[:, kv:kv + 512], misc]
    return jnp.concatenate(w_a, axis=1).astype(BF16), jnp.concatenate(w_b, axis=1).astype(BF16)


def hybrid_mixer_ln(x, mod, w_in_r, cmp_pe, cmp_w1, cmp_w2, wa_pad, ba, nw, w_out, lg, lb, tables, overlap):
    w_a, w_b = w_in_r
    qr, qn, ks, vs, kw, vw = nsa_proj(x, mod, w_a, *tables)
    proj = in_proj(x, mod, w_b)
    kc, vct = nsa_compress(proj, cmp_pe, cmp_w1, cmp_w2)
    nsa_o = nsa_attention(qr, qn, proj, kc, vct, ks, vs, kw, vw, overlap)
    gla_o = gla_mixer(proj, wa_pad, ba, nw)
    return out_proj_ln(nsa_o, gla_o, w_out, x, mod, lg, lb)


def kernel(x, c, w_ada, b_ada, w_in, cmp_pos_k, cmp_w1_k, cmp_w2_k, cmp_pos_v, cmp_w1_v, cmp_w2_v, gla_w_a2, gla_b_a, gla_norm_w, w_out, ln_mix_g, ln_mix_b, ln_ffn_g, ln_ffn_b, ffn_w_gate, ffn_w_up, ffn_w_down, moe_router, moe_w_gate, moe_w_up, moe_w_down):
    B, T, D = x.shape
    L = w_ada.shape[0]
    c_pad = jnp.zeros((8, D), F32).at[:B].set(c)
    mod_all = ada_mod(c_pad, w_ada, b_ada.reshape(L, 1, 6 * D))[:, :B].reshape(L, B, 6, D)
    tables = _rope_tables(T)
    overlap = _overlap_matrix_t(T // CMP_STRIDE, T // SLC_BLOCK)
    for layer in range(L):
        mod = mod_all[layer]
        wa_pad = jnp.zeros((LANES, GLA_HEADS * GLA_DK), F32).at[MISC_GA:MISC_GA + GLA_GATE_RANK].set(
            gla_w_a2[layer]).astype(BF16)
        x = hybrid_mixer_ln(
            x, mod, _reorder_w_in(w_in[layer]),
            jnp.stack([cmp_pos_k[layer], cmp_pos_v[layer]]),
            jnp.stack([cmp_w1_k[layer], cmp_w1_v[layer]]).astype(BF16),
            jnp.stack([cmp_w2_k[layer], cmp_w2_v[layer]]).astype(BF16),
            wa_pad, gla_b_a[layer].reshape(1, -1), gla_norm_w[layer].reshape(1, -1),
            w_out[layer].astype(BF16), ln_mix_g[layer].reshape(1, D), ln_mix_b[layer].reshape(1, D),
            tables, overlap)
        lg = ln_ffn_g[layer].reshape(1, D)
        lb = ln_ffn_b[layer].reshape(1, D)
        i = layer // 2
        if layer % 2 == 0:
            x = ffn_ln(x, mod, ffn_w_gate[i], ffn_w_up[i], ffn_w_down[i], lg, lb)
        else:
            x = moe_layer(x, mod, moe_router[i], moe_w_gate[i], moe_w_up[i], moe_w_down[i], lg, lb)
    return x
```

```python
import functools

import numpy as np
import jax
import jax.numpy as jnp
from jax import lax
from jax.experimental import pallas as pl
from jax.experimental.pallas import tpu as pltpu

F32 = jnp.float32
BF16 = jnp.bfloat16

D_MODEL = 2048
DEPTH = 2
HEAD_DIM = 128
NSA_HEADS = 8
NSA_KV_HEADS = 2
NSA_GROUP = 4
CMP_BLOCK = 32
CMP_STRIDE = 16
CMP_HIDDEN = 256
SLC_BLOCK = 64
SLC_TOPN = 16
WINDOW = 512
FORCE_SCORE = 1e9
GLA_DV = 256
GLA_HEADS = 4
GLA_DK = 128
GLA_GATE_RANK = 16
GLA_GATE_NORM = 16.0
GLA_CHUNK = 64
ROPE_THETA = 500000.0
ROPE_DIMS = 32
N_EXPERTS = 8
TOP_K = 2
LN_EPS = 1e-5
NORM_EPS = 1e-6
DEEPNORM_ALPHA = (2 * DEPTH) ** 0.25
NEG_BIG = -1e30
MASK_BIG = 2.0 ** 100
V_AUG_ROWS = HEAD_DIM + 16
NSA_HEADS_PER_STEP = 2
NSA_TQ = 256
NSA_TK = 512
ROW_TILE = 512
EPILOGUE_SUB = 128
ADA_TN = 2048
MOE_TF = 512
GLA_TILE = 512
LOG2_E = 1.4426950408889634

VMEM_LIMIT_BYTES = 56 * 1024 * 1024
BIG_VMEM_LIMIT_BYTES = 60 * 1024 * 1024
LANES = 128

A_NQ = 0
A_KS = 1024
A_KW = 1280
A_VS = 1536
A_VW = 1792
A_W = 2048
C_GV = 0
C_GG = 1024
C_GQ = 2048
C_GK = 2560
C_KC = 3072
C_VC = 3328
C_MISC = 3584
PROJ_W = 3712
MISC_GA = 24

NT_DIMS = (((1,), (1,)), ((), ()))


def _params(*sem):
    return pltpu.CompilerParams(dimension_semantics=sem, vmem_limit_bytes=VMEM_LIMIT_BYTES)


def _layer_norm(z, g, b):
    mu = jnp.mean(z, axis=-1, keepdims=True)
    zc = z - mu
    var = jnp.mean(zc * zc, axis=-1, keepdims=True)
    return zc * lax.rsqrt(var + LN_EPS) * g + b


def _ada_kernel(c_ref, w_ref, b_ref, o_ref):
    c = c_ref[...]
    cond = c * jax.nn.sigmoid(c)
    o_ref[0] = jnp.dot(cond.astype(BF16), w_ref[0].astype(BF16),
                       preferred_element_type=F32) + b_ref[0]


def ada_mod(c_pad, w_ada, b_ada):
    L, D, N = w_ada.shape
    tn = ADA_TN
    return pl.pallas_call(
        _ada_kernel,
        grid=(L, N // tn),
        in_specs=[pl.BlockSpec((8, D), lambda l, j: (0, 0)),
                  pl.BlockSpec((1, D, tn), lambda l, j: (l, 0, j)),
                  pl.BlockSpec((1, 1, tn), lambda l, j: (l, 0, j))],
        out_specs=pl.BlockSpec((1, 8, tn), lambda l, j: (l, 0, j)),
        out_shape=jax.ShapeDtypeStruct((L, 8, N), F32),
        compiler_params=_params("parallel", "parallel"),
        name="ada_mod",
    )(c_pad, w_ada, b_ada)


def _inproj_kernel(x_ref, m_ref, w_ref, o_ref):
    h = x_ref[0] * (1.0 + m_ref[0, 1:2, :]) + m_ref[0, 0:1, :]
    o_ref[0] = jnp.dot(h.astype(BF16), w_ref[...], preferred_element_type=F32)


def in_proj(x, mod, w):
    B, T, D = x.shape
    N = w.shape[1]
    tm = min(ROW_TILE, T)
    return pl.pallas_call(
        _inproj_kernel,
        grid=(B, T // tm),
        in_specs=[pl.BlockSpec((1, tm, D), lambda b, i: (b, i, 0)),
                  pl.BlockSpec((1, 6, D), lambda b, i: (b, 0, 0)),
                  pl.BlockSpec((D, N), lambda b, i: (0, 0))],
        out_specs=pl.BlockSpec((1, tm, N), lambda b, i: (b, i, 0)),
        out_shape=jax.ShapeDtypeStruct((B, T, N), F32),
        compiler_params=_params("parallel", "parallel"),
        name="in_proj",
    )(x, mod, w)


def _nsa_proj_kernel(x_ref, m_ref, w_ref, rc_ref, ra_ref, rb_ref,
                     qr_ref, qn_ref, ks_ref, vs_ref, kw_ref, vw_ref, *, sub):
    tm = x_ref.shape[1]
    scale = HEAD_DIM ** -0.5 * LOG2_E
    for r in range(tm // sub):
        rows = slice(r * sub, (r + 1) * sub)
        rc = rc_ref[rows, :]
        ra = ra_ref[rows, :]
        rb = rb_ref[rows, :]

        def rope(xh):
            return (xh * rc + pltpu.roll(xh, LANES - ROPE_DIMS // 2, 1) * ra
                    + pltpu.roll(xh, ROPE_DIMS // 2, 1) * rb)

        h = x_ref[0, rows, :] * (1.0 + m_ref[0, 1:2, :]) + m_ref[0, 0:1, :]
        p = jnp.dot(h.astype(BF16), w_ref[...], preferred_element_type=F32)
        for hq in range(NSA_HEADS):
            xh = p[:, A_NQ + hq * HEAD_DIM:A_NQ + (hq + 1) * HEAD_DIM]
            qr_ref[0, rows, hq * HEAD_DIM:(hq + 1) * HEAD_DIM] = (rope(xh) * scale).astype(BF16)
            qn_ref[0, rows, hq * HEAD_DIM:(hq + 1) * HEAD_DIM] = (xh * scale).astype(BF16)
        key_blk = (pl.program_id(1) * tm + r * sub + lax.broadcasted_iota(jnp.int32, (sub, LANES), 0)) // SLC_BLOCK
        blk_onehot = jnp.where(key_blk == lax.broadcasted_iota(jnp.int32, (sub, LANES), 1), 1.0, 0.0).astype(BF16)
        for hk in range(NSA_KV_HEADS):
            ks_ref[0, hk, rows, 0:HEAD_DIM] = rope(p[:, A_KS + hk * HEAD_DIM:A_KS + (hk + 1) * HEAD_DIM]).astype(BF16)
            ks_ref[0, hk, rows, HEAD_DIM:] = blk_onehot
            kw_ref[0, hk, rows, :] = rope(p[:, A_KW + hk * HEAD_DIM:A_KW + (hk + 1) * HEAD_DIM]).astype(BF16)
            vs_ref[0, hk, 0:HEAD_DIM, rows] = p[:, A_VS + hk * HEAD_DIM:A_VS + (hk + 1) * HEAD_DIM].T.astype(BF16)
            vs_ref[0, hk, HEAD_DIM:, rows] = jnp.ones((V_AUG_ROWS - HEAD_DIM, sub), BF16)
            vw_ref[0, hk, :, rows] = p[:, A_VW + hk * HEAD_DIM:A_VW + (hk + 1) * HEAD_DIM].T.astype(BF16)


def nsa_proj(x, mod, w, rc, ra, rb):
    B, T, D = x.shape
    tm = min(ROW_TILE, T)
    assert T // SLC_BLOCK <= LANES
    qw = NSA_HEADS * HEAD_DIM
    qspec = pl.BlockSpec((1, tm, qw), lambda b, i: (b, i, 0))
    tspec = pl.BlockSpec((tm, LANES), lambda b, i: (i, 0))
    return pl.pallas_call(
        functools.partial(_nsa_proj_kernel, sub=min(EPILOGUE_SUB, tm)),
        grid=(B, T // tm),
        in_specs=[pl.BlockSpec((1, tm, D), lambda b, i: (b, i, 0)),
                  pl.BlockSpec((1, 6, D), lambda b, i: (b, 0, 0)),
                  pl.BlockSpec((D, A_W), lambda b, i: (0, 0)),
                  tspec, tspec, tspec],
        out_specs=[qspec, qspec,
                   pl.BlockSpec((1, NSA_KV_HEADS, tm, HEAD_DIM + LANES), lambda b, i: (b, 0, i, 0)),
                   pl.BlockSpec((1, NSA_KV_HEADS, V_AUG_ROWS, tm), lambda b, i: (b, 0, 0, i)),
                   pl.BlockSpec((1, NSA_KV_HEADS, tm, HEAD_DIM), lambda b, i: (b, 0, i, 0)),
                   pl.BlockSpec((1, NSA_KV_HEADS, HEAD_DIM, tm), lambda b, i: (b, 0, 0, i))],
        out_shape=[jax.ShapeDtypeStruct((B, T, qw), BF16),
                   jax.ShapeDtypeStruct((B, T, qw), BF16),
                   jax.ShapeDtypeStruct((B, NSA_KV_HEADS, T, HEAD_DIM + LANES), BF16),
                   jax.ShapeDtypeStruct((B, NSA_KV_HEADS, V_AUG_ROWS, T), BF16),
                   jax.ShapeDtypeStruct((B, NSA_KV_HEADS, T, HEAD_DIM), BF16),
                   jax.ShapeDtypeStruct((B, NSA_KV_HEADS, HEAD_DIM, T), BF16)],
        compiler_params=_params("parallel", "parallel"),
        name="nsa_proj",
    )(x, mod, w, rc, ra, rb)


def _compress_kernel(a_ref, pe_ref, w1_ref, w2_ref, o_ref, ot_ref, *, n_half):
    half = CMP_STRIDE

    def part(l0):
        acc = jnp.zeros((n_half, CMP_HIDDEN), F32)
        for l in range(half):
            rows = a_ref[0, pl.ds(l, n_half, stride=half), :] + pe_ref[0, l0 + l:l0 + l + 1, :]
            acc += jnp.dot(rows.astype(BF16), w1_ref[0, (l0 + l) * HEAD_DIM:(l0 + l + 1) * HEAD_DIM, :],
                           preferred_element_type=F32)
        return acc

    first = part(0)
    second = part(half)
    hid = first + pltpu.roll(second, n_half - 1, 0)
    row = lax.broadcasted_iota(jnp.int32, (n_half, 1), 0)
    hid = jnp.where(row < n_half - 1, hid, 0.0)
    act = jax.nn.gelu(hid)
    out = jnp.dot(act.astype(BF16), w2_ref[0], preferred_element_type=F32)
    o_ref[0, 0, 0] = out.astype(BF16)
    ot_ref[0, 0, 0] = out.T.astype(BF16)


def nsa_compress(proj, pe, w1, w2):
    B, T, _ = proj.shape
    n_half = T // CMP_STRIDE
    return pl.pallas_call(
        functools.partial(_compress_kernel, n_half=n_half),
        grid=(B, 2, NSA_KV_HEADS),
        in_specs=[pl.BlockSpec((1, T, HEAD_DIM), lambda b, s, h: (b, 0, C_KC // HEAD_DIM + s * NSA_KV_HEADS + h)),
                  pl.BlockSpec((1, CMP_BLOCK, HEAD_DIM), lambda b, s, h: (s, 0, 0)),
                  pl.BlockSpec((1, CMP_BLOCK * HEAD_DIM, CMP_HIDDEN), lambda b, s, h: (s, 0, 0)),
                  pl.BlockSpec((1, CMP_HIDDEN, HEAD_DIM), lambda b, s, h: (s, 0, 0))],
        out_specs=[pl.BlockSpec((1, 1, 1, n_half, HEAD_DIM), lambda b, s, h: (b, s, h, 0, 0)),
                   pl.BlockSpec((1, 1, 1, HEAD_DIM, n_half), lambda b, s, h: (b, s, h, 0, 0))],
        out_shape=[jax.ShapeDtypeStruct((B, 2, NSA_KV_HEADS, n_half, HEAD_DIM), BF16),
                   jax.ShapeDtypeStruct((B, 2, NSA_KV_HEADS, HEAD_DIM, n_half), BF16)],
        compiler_params=_params("parallel", "parallel", "parallel"),
        name="nsa_compress",
    )(proj, pe, w1, w2)


def _nsa_kernel(qr_ref, qn_ref, g_ref, kc_ref, vct_ref, ks_ref, vst_ref, kw_ref, vwt_ref, ovt_ref, o_ref,
                s_buf, smax_buf, m_buf, acc_buf, *, TQ, TK, WK, NC, NB, NH):
    G = NSA_GROUP
    R = G * TQ
    NBP = ovt_ref.shape[0]
    i = pl.program_id(2)
    t0 = i * TQ
    m_floor = 0.5 * NEG_BIG

    def stack(ref, h):
        x = ref[0]
        return jnp.concatenate([x[:, (h * G + g) * HEAD_DIM:(h * G + g + 1) * HEAD_DIM] for g in range(G)], axis=0)

    def tile_g(x):
        return jnp.concatenate([x] * G, axis=1)

    tT = t0 + lax.broadcasted_iota(jnp.int32, (1, TQ), 1)
    cur = tT // SLC_BLOCK
    n_sel = min(SLC_TOPN, NB)

    HS = range(NH)
    qr = [stack(qr_ref, h) for h in HS]
    qn = [stack(qn_ref, h) for h in HS]

    ws = pl.multiple_of(jnp.maximum(t0 + TQ - WK, 0), LANES)
    s_c = [lax.dot_general(kc_ref[0, 0, h], qn[h], NT_DIMS, preferred_element_type=F32) for h in HS]
    s_w = [lax.dot_general(kw_ref[0, h, pl.ds(ws, WK), :], qr[h], NT_DIMS, preferred_element_type=F32)
           for h in HS]
    cend = lax.broadcasted_iota(jnp.int32, (NC, 1), 0) * CMP_STRIDE + (CMP_BLOCK - 1)
    bias_c = tile_g(jnp.where(cend <= tT, 0.0, NEG_BIG))
    dist = tT - (ws + lax.broadcasted_iota(jnp.int32, (WK, 1), 0))
    bias_w = tile_g(jnp.where(dist >= 0, jnp.where(dist < WINDOW, 0.0, NEG_BIG), NEG_BIG))

    p_cb, e_wb, l_w = [], [], []
    for h in HS:
        sc = s_c[h] + bias_c
        m_c = jnp.maximum(jnp.max(sc, axis=0, keepdims=True), m_floor)
        e_c = jnp.exp2(sc - m_c)
        den = jnp.sum(e_c, axis=0, keepdims=True)
        p_cb.append((e_c * jnp.where(den > 0.0, 1.0 / den, 0.0)).astype(BF16))
        sw = s_w[h] + bias_w
        e_w = jnp.exp2(sw - jnp.max(sw, axis=0, keepdims=True))
        l_w.append(jnp.sum(e_w, axis=0, keepdims=True))
        e_wb.append(e_w.astype(BF16))

    o_c = [jnp.dot(vct_ref[0, 0, h], p_cb[h], preferred_element_type=F32) for h in HS]
    imp4 = [jnp.dot(ovt_ref[...], p_cb[h], preferred_element_type=F32) for h in HS]
    o_w = [jnp.dot(vwt_ref[0, h, :, pl.ds(ws, WK)], e_wb[h], preferred_element_type=F32) * (1.0 / l_w[h])
           for h in HS]

    jj = lax.broadcasted_iota(jnp.int32, (NBP, TQ), 0)
    forced = jnp.where(jj == 0, 1, jnp.where(jj == cur, 1, jnp.where(jj == cur - 1, 1, 0)))
    val = []
    for h in HS:
        impT = imp4[h][:, 0:TQ]
        for g in range(1, G):
            impT = impT + imp4[h][:, g * TQ:(g + 1) * TQ]
        val.append(jnp.where(jj > cur, -jnp.inf, jnp.where(forced > 0, FORCE_SCORE, impT)))
    SUBL = 8
    chunks = [[val[h][c * SUBL:(c + 1) * SUBL, :] for c in range(NBP // SUBL)] for h in HS]
    rank_c = [[jnp.zeros((SUBL, TQ), jnp.int32) for c in range(NBP // SUBL)] for h in HS]
    jj_c = lax.broadcasted_iota(jnp.int32, (SUBL, TQ), 0)
    for j2 in range(NB):
        for h in HS:
            row = val[h][j2:j2 + 1, :]
            for c in range(NBP // SUBL):
                v = chunks[h][c]
                if c * SUBL > j2:
                    beat = jnp.where(row >= v, 1, 0)
                elif (c + 1) * SUBL - 1 < j2:
                    beat = jnp.where(row > v, 1, 0)
                else:
                    later = jnp.where(jj_c + c * SUBL > j2, 1, 0)
                    beat = jnp.where(row > v, 1, jnp.where(row == v, later, 0))
                rank_c[h][c] = rank_c[h][c] + beat
    rank = [jnp.concatenate(rank_c[h], axis=0) for h in HS]
    fronts = []
    for h in HS:
        sel_neg = jnp.where(jj > cur, -MASK_BIG, jnp.where(rank[h] < n_sel, 0.0, -MASK_BIG))
        if NBP < LANES:
            sel_neg = jnp.concatenate([sel_neg, jnp.zeros((LANES - NBP, TQ), F32)], axis=0)
        sel_q = sel_neg.T.astype(BF16)
        q_aug = jnp.concatenate([qr[h], jnp.concatenate([sel_q] * G, axis=0)], axis=1)
        fronts.append((o_c[h], o_w[h], q_aug))

    def sel_scores(h, kt, slot):
        k0 = pl.multiple_of(kt * TK, TK)
        k = ks_ref[0, h, pl.ds(k0, TK), :]
        s = lax.dot_general(k, fronts[h][2], NT_DIMS, preferred_element_type=F32)
        s_buf[h, slot] = s
        smax_buf[h, slot] = jnp.max(s, axis=0, keepdims=True)

    def sel_accumulate(h, kt, slot):
        k0 = pl.multiple_of(kt * TK, TK)
        vt = vst_ref[0, h, :, pl.ds(k0, TK)]
        m = m_buf[h]
        m_new = jnp.maximum(m, smax_buf[h, slot])
        p = jnp.exp2(s_buf[h, slot] - m_new).astype(BF16)
        acc_buf[h] = jnp.exp2(m - m_new) * acc_buf[h] + jnp.dot(vt, p, preferred_element_type=F32)
        m_buf[h] = m_new

    n_kt = (t0 + TQ + TK - 1) // TK
    n_loop = n_kt - 1
    for h in HS:
        m_buf[h] = jnp.full((1, R), m_floor, F32)
        acc_buf[h] = jnp.zeros((V_AUG_ROWS, R), F32)
        sel_scores(h, 0, 0)

    def pair_body(k, carry):
        for h in HS:
            sel_scores(h, 2 * k + 1, 1)
            sel_accumulate(h, 2 * k, 0)

        @pl.when(2 * k + 1 < n_loop)
        def _():
            for h in HS:
                sel_scores(h, 2 * k + 2, 0)
                sel_accumulate(h, 2 * k + 1, 1)
        return carry

    lax.fori_loop(0, (n_loop + 1) // 2, pair_body, 0)
    krow = n_loop * TK + lax.broadcasted_iota(jnp.int32, (TK, 1), 0)
    causal = tile_g(jnp.where(krow <= tT, 0.0, NEG_BIG))
    k_last = pl.multiple_of(n_loop * TK, TK)
    last_slot = n_loop % 2
    s_fin = [s_buf[h, last_slot] + causal for h in HS]
    m_fin = [jnp.maximum(m_buf[h], jnp.max(s_fin[h], axis=0, keepdims=True)) for h in HS]
    p_fin = [jnp.exp2(s_fin[h] - m_fin[h]).astype(BF16) for h in HS]
    acc_fin = [jnp.exp2(m_buf[h] - m_fin[h]) * acc_buf[h]
               + jnp.dot(vst_ref[0, h, :, pl.ds(k_last, TK)], p_fin[h], preferred_element_type=F32) for h in HS]
    gate_all = jax.nn.sigmoid(g_ref[0]).T
    assert NH == NSA_KV_HEADS
    gates = [gate_all[3 * G * h:3 * G * (h + 1), :] for h in HS]
    for h in HS:
        o_s = acc_fin[h][0:HEAD_DIM] * (1.0 / acc_fin[h][HEAD_DIM:HEAD_DIM + 1])
        o_c, o_w, _ = fronts[h]
        gate = gates[h]
        for g in range(G):
            cols = slice(g * TQ, (g + 1) * TQ)
            o = (gate[3 * g:3 * g + 1, :] * o_c[:, cols] + gate[3 * g + 1:3 * g + 2, :] * o_s[:, cols]
                 + gate[3 * g + 2:3 * g + 3, :] * o_w[:, cols])
            o_ref[0, :, (h * G + g) * HEAD_DIM:(h * G + g + 1) * HEAD_DIM] = o.T.astype(BF16)


def nsa_attention(qr, qn, gates, kc, vct, ks, vst, kw, vwt, overlap_t):
    B, T, _ = qr.shape
    TQ = NSA_TQ
    TK = min(NSA_TK, T)
    WK = min(WINDOW + TQ, T)
    NC = kc.shape[3]
    NB = T // SLC_BLOCK
    NH = NSA_HEADS_PER_STEP
    gw = NH * NSA_GROUP * HEAD_DIM
    qspec = pl.BlockSpec((1, TQ, gw), lambda b, h, i: (b, i, h))
    kspec = pl.BlockSpec((1, NH, T, HEAD_DIM), lambda b, h, i: (b, h, 0, 0))
    vtspec = pl.BlockSpec((1, NH, HEAD_DIM, T), lambda b, h, i: (b, h, 0, 0))
    return pl.pallas_call(
        functools.partial(_nsa_kernel, TQ=TQ, TK=TK, WK=WK, NC=NC, NB=NB, NH=NH),
        grid=(B, NSA_KV_HEADS // NH, T // TQ),
        in_specs=[qspec, qspec,
                  pl.BlockSpec((1, TQ, LANES), lambda b, h, i: (b, i, C_MISC // LANES)),
                  pl.BlockSpec((1, 1, NH, NC, HEAD_DIM), lambda b, h, i: (b, 0, h, 0, 0)),
                  pl.BlockSpec((1, 1, NH, HEAD_DIM, NC), lambda b, h, i: (b, 1, h, 0, 0)),
                  pl.BlockSpec((1, NH, T, HEAD_DIM + LANES), lambda b, h, i: (b, h, 0, 0)),
                  pl.BlockSpec((1, NH, V_AUG_ROWS, T), lambda b, h, i: (b, h, 0, 0)),
                  kspec, vtspec,
                  pl.BlockSpec(overlap_t.shape, lambda b, h, i: (0, 0))],
        out_specs=pl.BlockSpec((1, TQ, gw), lambda b, h, i: (b, i, h)),
        out_shape=jax.ShapeDtypeStruct((B, T, NSA_HEADS * HEAD_DIM), BF16),
        scratch_shapes=[pltpu.VMEM((NH, 2, TK, NSA_GROUP * TQ), F32),
                        pltpu.VMEM((NH, 2, 1, NSA_GROUP * TQ), F32),
                        pltpu.VMEM((NH, 1, NSA_GROUP * TQ), F32),
                        pltpu.VMEM((NH, V_AUG_ROWS, NSA_GROUP * TQ), F32)],
        compiler_params=_params("parallel", "parallel", "arbitrary"),
        name="nsa_attention",
    )(qr, qn, gates, kc, vct, ks, vst, kw, vwt, overlap_t)


def _gla_kernel(q_ref, k_ref, v_ref, gg_ref, misc_ref, wa_ref, ba_ref, nw_ref, o_ref, s_ref, la_ref, *, TC):
    C = GLA_CHUNK

    @pl.when(pl.program_id(1) == 0)
    def _():
        s_ref[...] = jnp.zeros_like(s_ref)

    z = jnp.dot(misc_ref[0].astype(BF16), wa_ref[...], preferred_element_type=F32) + ba_ref[...]
    la_ref[...] = (jnp.minimum(z, 0.0) - jnp.log1p(jnp.exp(-jnp.abs(z)))) * (1.0 / GLA_GATE_NORM)
    ri = lax.broadcasted_iota(jnp.int32, (C, C), 0)
    ci = lax.broadcasted_iota(jnp.int32, (C, C), 1)
    causal = ri >= ci
    tri = jnp.where(causal, 1.0, 0.0)
    nw = nw_ref[...]

    n_c = TC // C
    pairs = [(c, h) for c in range(n_c) for h in range(GLA_HEADS)]
    rows = [slice(c * C, (c + 1) * C) for c in range(n_c)]
    b_all = [jnp.dot(tri, la_ref[rows[c], :], preferred_element_type=F32, precision=lax.Precision.HIGHEST)
             for c in range(n_c)]
    qd, kv, decay, intra = {}, {}, {}, {}
    for c, h in pairs:
        b = b_all[c][:, h * GLA_DK:(h + 1) * GLA_DK]
        bl = b[C - 1:C, :]
        q = q_ref[0, rows[c], h * GLA_DK:(h + 1) * GLA_DK] * (GLA_DK ** -0.5)
        k = k_ref[0, rows[c], h * GLA_DK:(h + 1) * GLA_DK]
        v = v_ref[0, rows[c], h * GLA_DV:(h + 1) * GLA_DV].astype(BF16)
        qd[c, h] = (q * jnp.exp(b)).astype(BF16)
        ki = (k * jnp.exp(-b)).astype(BF16)
        kst = (k * jnp.exp(bl - b)).T.astype(BF16)
        decay[c, h] = jnp.exp(b.T[:, C - 1:C])
        a = lax.dot_general(qd[c, h], ki, NT_DIMS, preferred_element_type=F32)
        a = jnp.where(causal, a, 0.0).astype(BF16)
        intra[c, h] = jnp.dot(a, v, preferred_element_type=F32)
        kv[c, h] = jnp.dot(kst, v, preferred_element_type=F32)
    state = [s_ref[h] for h in range(GLA_HEADS)]
    for c, h in pairs:
        o = intra[c, h] + jnp.dot(qd[c, h], state[h].astype(BF16), preferred_element_type=F32)
        state[h] = state[h] * decay[c, h] + kv[c, h]
        o = o * lax.rsqrt(jnp.mean(o * o, axis=-1, keepdims=True) + NORM_EPS) * nw
        gg = gg_ref[0, rows[c], h * GLA_DV:(h + 1) * GLA_DV]
        o_ref[0, rows[c], h * GLA_DV:(h + 1) * GLA_DV] = (o * (gg * jax.nn.sigmoid(gg))).astype(BF16)
    for h in range(GLA_HEADS):
        s_ref[h] = state[h]


def gla_mixer(proj, wa_pad, ba, nw):
    B, T, _ = proj.shape
    TC = min(GLA_TILE, T)
    qk_w = GLA_HEADS * GLA_DK
    v_w = GLA_HEADS * GLA_DV
    return pl.pallas_call(
        functools.partial(_gla_kernel, TC=TC),
        grid=(B, T // TC),
        in_specs=[pl.BlockSpec((1, TC, qk_w), lambda b, i: (b, i, C_GQ // qk_w)),
                  pl.BlockSpec((1, TC, qk_w), lambda b, i: (b, i, C_GK // qk_w)),
                  pl.BlockSpec((1, TC, v_w), lambda b, i: (b, i, C_GV // v_w)),
                  pl.BlockSpec((1, TC, v_w), lambda b, i: (b, i, C_GG // v_w)),
                  pl.BlockSpec((1, TC, LANES), lambda b, i: (b, i, C_MISC // LANES)),
                  pl.BlockSpec((LANES, qk_w), lambda b, i: (0, 0)),
                  pl.BlockSpec((1, qk_w), lambda b, i: (0, 0)),
                  pl.BlockSpec((1, GLA_DV), lambda b, i: (0, 0))],
        out_specs=pl.BlockSpec((1, TC, v_w), lambda b, i: (b, i, 0)),
        out_shape=jax.ShapeDtypeStruct((B, T, v_w), BF16),
        scratch_shapes=[pltpu.VMEM((GLA_HEADS, GLA_DK, GLA_DV), F32),
                        pltpu.VMEM((TC, qk_w), F32)],
        compiler_params=_params("parallel", "arbitrary"),
        name="gla_mixer",
    )(proj, proj, proj, proj, proj, wa_pad, ba, nw)


def _outproj_kernel(n_ref, g_ref, w_ref, x_ref, m_ref, lg_ref, lb_ref, o_ref, *, half):
    sub = min(EPILOGUE_SUB, o_ref.shape[1])
    for r in range(o_ref.shape[1] // sub):
        rows = slice(r * sub, (r + 1) * sub)
        y = (jnp.dot(n_ref[0, rows, :], w_ref[0:half, :], preferred_element_type=F32)
             + jnp.dot(g_ref[0, rows, :], w_ref[half:, :], preferred_element_type=F32))
        z = DEEPNORM_ALPHA * x_ref[0, rows, :] + (1.0 + m_ref[0, 2:3, :]) * y
        o_ref[0, rows, :] = _layer_norm(z, lg_ref[...], lb_ref[...])


def out_proj_ln(nsa_o, gla_o, w, x, mod, lg, lb):
    B, T, D = x.shape
    half = nsa_o.shape[-1]
    tm = min(ROW_TILE, T)
    vec = pl.BlockSpec((1, D), lambda b, i: (0, 0))
    return pl.pallas_call(
        functools.partial(_outproj_kernel, half=half),
        grid=(B, T // tm),
        in_specs=[pl.BlockSpec((1, tm, half), lambda b, i: (b, i, 0)),
                  pl.BlockSpec((1, tm, gla_o.shape[-1]), lambda b, i: (b, i, 0)),
                  pl.BlockSpec(w.shape, lambda b, i: (0, 0)),
                  pl.BlockSpec((1, tm, D), lambda b, i: (b, i, 0)),
                  pl.BlockSpec((1, 6, D), lambda b, i: (b, 0, 0)),
                  vec, vec],
        out_specs=pl.BlockSpec((1, tm, D), lambda b, i: (b, i, 0)),
        out_shape=jax.ShapeDtypeStruct((B, T, D), F32),
        compiler_params=_params("parallel", "parallel"),
        name="out_proj_ln",
    )(nsa_o, gla_o, w, x, mod, lg, lb)


FFN_TM = 1024
FFN_TF = 256
FFN_SUB = 512


def _ffn_kernel(x_ref, m_ref, wg_ref, wu_ref, wd_ref, lg_ref, lb_ref, o_ref, h_sc, *, sub):
    j = pl.program_id(2)
    last = pl.num_programs(2) - 1
    tm = o_ref.shape[1]

    def step(first, final):
        rows_per = sub // 2 if (first or final) else sub
        wg = wg_ref[...].astype(BF16)
        wu = wu_ref[...].astype(BF16)
        wd = wd_ref[...].astype(BF16)
        for sb in range(tm // rows_per):
            rows = slice(sb * rows_per, (sb + 1) * rows_per)
            if first:
                h_sc[rows, :] = (x_ref[0, rows, :] * (1.0 + m_ref[0, 4:5, :]) + m_ref[0, 3:4, :]).astype(BF16)
            h = h_sc[rows, :]
            a = jnp.dot(h, wg, preferred_element_type=F32)
            u = jnp.dot(h, wu, preferred_element_type=F32)
            y = jnp.dot((a * jax.nn.sigmoid(a) * u).astype(BF16), wd, preferred_element_type=F32)
            if first:
                o_ref[0, rows, :] = y
            elif final:
                z = DEEPNORM_ALPHA * x_ref[0, rows, :] + (1.0 + m_ref[0, 5:6, :]) * (o_ref[0, rows, :] + y)
                o_ref[0, rows, :] = _layer_norm(z, lg_ref[...], lb_ref[...])
            else:
                o_ref[0, rows, :] += y

    pl.when(j == 0)(lambda: step(True, False))
    pl.when(jnp.logical_and(j > 0, j < last))(lambda: step(False, False))
    pl.when(j == last)(lambda: step(False, True))


def ffn_ln(x, mod, wg, wu, wd, lg, lb):
    B, T, D = x.shape
    F = wg.shape[1]
    tm = min(FFN_TM, T)
    tf = FFN_TF
    vec = pl.BlockSpec((1, D), lambda b, i, j: (0, 0))
    return pl.pallas_call(
        functools.partial(_ffn_kernel, sub=min(FFN_SUB, tm)),
        grid=(B, T // tm, F // tf),
        in_specs=[pl.BlockSpec((1, tm, D), lambda b, i, j: (b, i, 0)),
                  pl.BlockSpec((1, 6, D), lambda b, i, j: (b, 0, 0)),
                  pl.BlockSpec((D, tf), lambda b, i, j: (0, j)),
                  pl.BlockSpec((D, tf), lambda b, i, j: (0, j)),
                  pl.BlockSpec((tf, D), lambda b, i, j: (j, 0)),
                  vec, vec],
        out_specs=pl.BlockSpec((1, tm, D), lambda b, i, j: (b, i, 0)),
        out_shape=jax.ShapeDtypeStruct((B, T, D), F32),
        scratch_shapes=[pltpu.VMEM((tm, D), BF16)],
        compiler_params=pltpu.CompilerParams(dimension_semantics=("parallel", "parallel", "arbitrary"),
                                             vmem_limit_bytes=BIG_VMEM_LIMIT_BYTES),
        name="ffn_ln",
    )(x, mod, wg, wu, wd, lg, lb)


def _router_kernel(x_ref, m_ref, wr_ref, h_ref, lg_ref):
    h = x_ref[0] * (1.0 + m_ref[0, 4:5, :]) + m_ref[0, 3:4, :]
    h_hi = h.astype(BF16)
    h_lo = (h - h_hi.astype(F32)).astype(BF16)
    h_ref[0] = h_hi
    lg_ref[0] = (jnp.dot(h_hi, wr_ref[0], preferred_element_type=F32)
                 + jnp.dot(h_lo, wr_ref[0], preferred_element_type=F32)
                 + jnp.dot(h_hi, wr_ref[1], preferred_element_type=F32))


def moe_router(x, mod, wr_pad):
    B, T, D = x.shape
    tm = min(ROW_TILE, T)
    return pl.pallas_call(
        _router_kernel,
        grid=(B, T // tm),
        in_specs=[pl.BlockSpec((1, tm, D), lambda b, i: (b, i, 0)),
                  pl.BlockSpec((1, 6, D), lambda b, i: (b, 0, 0)),
                  pl.BlockSpec((2, D, LANES), lambda b, i: (0, 0, 0))],
        out_specs=[pl.BlockSpec((1, tm, D), lambda b, i: (b, i, 0)),
                   pl.BlockSpec((1, tm, LANES), lambda b, i: (b, i, 0))],
        out_shape=[jax.ShapeDtypeStruct((B, T, D), BF16),
                   jax.ShapeDtypeStruct((B, T, LANES), F32)],
        compiler_params=_params("parallel", "parallel"),
        name="moe_router",
    )(x, mod, wr_pad)


def _moe_kernel(te_ref, nv_ref, x_ref, wg_ref, wu_ref, wd_ref, o_ref, acc_ref, *, sub):
    i = pl.program_id(0)
    j = pl.program_id(1)
    nv = nv_ref[i]

    @pl.when(j == 0)
    def _():
        acc_ref[...] = jnp.zeros_like(acc_ref)

    n_sub = acc_ref.shape[0] // sub

    def swiglu_rows(rows, wg, wu, wd):
        h = x_ref[rows, :]
        a = jnp.dot(h, wg, preferred_element_type=F32)
        u = jnp.dot(h, wu, preferred_element_type=F32)
        acc_ref[rows, :] += jnp.dot((a * jax.nn.sigmoid(a) * u).astype(BF16), wd, preferred_element_type=F32)

    for n_real in range(1, n_sub + 1):
        @pl.when(nv == n_real)
        def _():
            wg = wg_ref[0].astype(BF16)
            wu = wu_ref[0].astype(BF16)
            wd = wd_ref[0].astype(BF16)
            for r0 in range(0, n_real * sub, MOE_CHAIN):
                swiglu_rows(slice(r0, min(r0 + MOE_CHAIN, n_real * sub)), wg, wu, wd)

    @pl.when(j == pl.num_programs(1) - 1)
    def _():
        o_ref[...] = acc_ref[...].astype(o_ref.dtype)


def moe_experts(tile_e, tile_nv, xg, wg, wu, wd, tm, sub):
    M, D = xg.shape
    F = wg.shape[2]
    tf = MOE_TF
    nj = F // tf
    n_tiles = M // tm

    def wj(i, j, nv):
        return jnp.where(nv[i] > 0, j, nj - 1)

    return pl.pallas_call(
        functools.partial(_moe_kernel, sub=sub),
        grid_spec=pltpu.PrefetchScalarGridSpec(
            num_scalar_prefetch=2,
            grid=(n_tiles, nj),
            in_specs=[pl.BlockSpec((tm, D), lambda i, j, te, nv: (i, 0)),
                      pl.BlockSpec((1, D, tf), lambda i, j, te, nv: (te[i], 0, wj(i, j, nv))),
                      pl.BlockSpec((1, D, tf), lambda i, j, te, nv: (te[i], 0, wj(i, j, nv))),
                      pl.BlockSpec((1, tf, D), lambda i, j, te, nv: (te[i], wj(i, j, nv), 0))],
            out_specs=pl.BlockSpec((tm, D), lambda i, j, te, nv: (i, 0)),
            scratch_shapes=[pltpu.VMEM((tm, D), F32)]),
        out_shape=jax.ShapeDtypeStruct((M, D), BF16),
        compiler_params=pltpu.CompilerParams(dimension_semantics=("parallel", "arbitrary"),
                                             vmem_limit_bytes=BIG_VMEM_LIMIT_BYTES),
        name="moe_experts",
    )(tile_e, tile_nv, xg, wg, wu, wd)


def _combine_kernel(y0_ref, y1_ref, cw_ref, x_ref, m_ref, lg_ref, lb_ref, o_ref):
    cw = cw_ref[0]
    y = y0_ref[0].astype(F32) * cw[:, 0:1] + y1_ref[0].astype(F32) * cw[:, 1:2]
    z = DEEPNORM_ALPHA * x_ref[0] + (1.0 + m_ref[0, 5:6, :]) * y
    o_ref[0] = _layer_norm(z, lg_ref[...], lb_ref[...])


def moe_combine_ln(y0, y1, cw, x, mod, lg, lb):
    B, T, D = x.shape
    tm = min(ROW_TILE, T)
    row = pl.BlockSpec((1, tm, D), lambda b, i: (b, i, 0))
    vec = pl.BlockSpec((1, D), lambda b, i: (0, 0))
    return pl.pallas_call(
        _combine_kernel,
        grid=(B, T // tm),
        in_specs=[row, row, pl.BlockSpec((1, tm, LANES), lambda b, i: (b, i, 0)), row,
                  pl.BlockSpec((1, 6, D), lambda b, i: (b, 0, 0)), vec, vec],
        out_specs=row,
        out_shape=jax.ShapeDtypeStruct((B, T, D), F32),
        compiler_params=_params("parallel", "parallel"),
        name="moe_combine_ln",
    )(y0, y1, cw, x, mod, lg, lb)


MOE_TM = 1024
MOE_SUB = 256
MOE_CHAIN = 1024


def moe_layer(x, mod, w_router, wg, wu, wd, lg, lb):
    B, T, D = x.shape
    N = B * T
    A = N * TOP_K
    tm = MOE_TM
    wr_pad = jnp.zeros((D, LANES), F32).at[:, :N_EXPERTS].set(w_router)
    wr_hi = wr_pad.astype(BF16)
    wr_lo = (wr_pad - wr_hi.astype(F32)).astype(BF16)
    h, logits = moe_router(x, mod, jnp.stack([wr_hi, wr_lo]))
    logits = logits.reshape(N, LANES)[:, :N_EXPERTS]
    top_val, top_idx = lax.top_k(logits, TOP_K)
    comb = jax.nn.softmax(top_val, axis=-1)
    flat_e = top_idx.reshape(-1).astype(jnp.int32)
    onehot = (flat_e[:, None] == jnp.arange(N_EXPERTS, dtype=jnp.int32)[None, :]).astype(jnp.int32)
    csum = jnp.cumsum(onehot, axis=0)
    counts = csum[-1]
    padded = (counts + tm - 1) // tm * tm
    pad_end = jnp.cumsum(padded)
    pad_start = pad_end - padded
    slot = jnp.sum(onehot * (csum + pad_start[None, :]), axis=1) - 1
    n_tiles = -(-A // tm) + N_EXPERTS
    tile_start = jnp.arange(n_tiles, dtype=jnp.int32) * tm
    tile_e = jnp.minimum(jnp.searchsorted(pad_end, tile_start, side='right'), N_EXPERTS - 1).astype(jnp.int32)
    valid = jnp.clip(pad_start[tile_e] + counts[tile_e] - tile_start, 0, tm)
    tile_nv = ((valid + MOE_SUB - 1) // MOE_SUB).astype(jnp.int32)
    order = jnp.argsort(flat_e)
    seg_start = jnp.cumsum(counts) - counts
    slot_id = jnp.arange(n_tiles * tm, dtype=jnp.int32)
    within = slot_id - jnp.repeat(tile_start, tm)
    src = jnp.repeat(seg_start[tile_e] + tile_start - pad_start[tile_e], tm) + within
    slot_tok = jnp.where(within < jnp.repeat(valid, tm),
                         order[jnp.clip(src, 0, A - 1)].astype(jnp.int32) // TOP_K, slot_id % N)
    n_used = pad_end[-1] // tm
    tile_e = jnp.where(jnp.arange(n_tiles) < n_used, tile_e, tile_e[jnp.maximum(n_used - 1, 0)])
    xg = h.reshape(N, D)[slot_tok]
    y = moe_experts(tile_e, tile_nv, xg, wg, wu, wd, tm, MOE_SUB)
    slot_of = slot.reshape(N, TOP_K)
    y0 = y[slot_of[:, 0]].reshape(B, T, D)
    y1 = y[slot_of[:, 1]].reshape(B, T, D)
    cw = jnp.zeros((N, LANES), F32).at[:, :TOP_K].set(comb).reshape(B, T, LANES)
    return moe_combine_ln(y0, y1, cw, x, mod, lg, lb)


def _rope_tables(T):
    half = ROPE_DIMS // 2
    inv = ROPE_THETA ** (-jnp.arange(half, dtype=F32) * 2.0 / ROPE_DIMS)
    ang = jnp.arange(T).astype(F32)[:, None] * inv[None, :]
    cos, sin = jnp.cos(ang), jnp.sin(ang)
    z = jnp.zeros((T, LANES - ROPE_DIMS), F32)
    zh = jnp.zeros((T, half), F32)
    rc = jnp.concatenate([cos, cos, jnp.ones((T, LANES - ROPE_DIMS), F32)], axis=1)
    ra = jnp.concatenate([-sin, zh, z], axis=1)
    rb = jnp.concatenate([zh, sin, z], axis=1)
    return rc, ra, rb


def _overlap_matrix_t(NC, NB):
    nbp = -(-NB // 16) * 16
    c0 = np.arange(NC)[None, :] * CMP_STRIDE
    b0 = np.arange(nbp)[:, None] * SLC_BLOCK
    ov = (c0 < b0 + SLC_BLOCK) & (c0 + CMP_BLOCK > b0) & (np.arange(nbp)[:, None] < NB)
    return jnp.asarray(ov.astype(np.float32), dtype=BF16)


def _reorder_w_in(w):
    D = w.shape[0]
    o_ng = 1024 + 6 * 256
    o_gq = o_ng + 24
    o_gk = o_gq + 512
    o_gv = o_gk + 512
    o_ga = o_gv + 1024
    o_gg = o_ga + GLA_GATE_RANK
    misc = jnp.concatenate([w[:, o_ng:o_gq], w[:, o_ga:o_gg],
                            jnp.zeros((D, LANES - 24 - GLA_GATE_RANK), w.dtype)], axis=1)
    kv = 1024
    w_a = [w[:, 0:1024], w[:, kv + 512:kv + 768], w[:, kv + 1024:kv + 1280],
           w[:, kv + 768:kv + 1024], w[:, kv + 1280:kv + 1536]]
    w_b = [w[:, o_gv:o_ga], w[:, o_gg:o_gg + 1024], w[:, o_gq:o_gk], w[:, o_gk:o_gv],
           w[:, kv:kv + 512], misc]
    return jnp.concatenate(w_a, axis=1).astype(BF16), jnp.concatenate(w_b, axis=1).astype(BF16)


def hybrid_mixer_ln(x, mod, w_in_r, cmp_pe, cmp_w1, cmp_w2, wa_pad, ba, nw, w_out, lg, lb, tables, overlap):
    w_a, w_b = w_in_r
    qr, qn, ks, vs, kw, vw = nsa_proj(x, mod, w_a, *tables)
    proj = in_proj(x, mod, w_b)
    kc, vct = nsa_compress(proj, cmp_pe, cmp_w1, cmp_w2)
    nsa_o = nsa_attention(qr, qn, proj, kc, vct, ks, vs, kw, vw, overlap)
    gla_o = gla_mixer(proj, wa_pad, ba, nw)
    return out_proj_ln(nsa_o, gla_o, w_out, x, mod, lg, lb)


def kernel(x, c, w_ada, b_ada, w_in, cmp_pos_k, cmp_w1_k, cmp_w2_k, cmp_pos_v, cmp_w1_v, cmp_w2_v, gla_w_a2, gla_b_a, gla_norm_w, w_out, ln_mix_g, ln_mix_b, ln_ffn_g, ln_ffn_b, ffn_w_gate, ffn_w_up, ffn_w_down, moe_router, moe_w_gate, moe_w_up, moe_w_down):
    B, T, D = x.shape
    L = w_ada.shape[0]
    c_pad = jnp.zeros((8, D), F32).at[:B].set(c)
    mod_all = ada_mod(c_pad, w_ada, b_ada.reshape(L, 1, 6 * D))[:, :B].reshape(L, B, 6, D)
    tables = _rope_tables(T)
    overlap = _overlap_matrix_t(T // CMP_STRIDE, T // SLC_BLOCK)
    for layer in range(L):
        mod = mod_all[layer]
        wa_pad = jnp.zeros((LANES, GLA_HEADS * GLA_DK), F32).at[MISC_GA:MISC_GA + GLA_GATE_RANK].set(
            gla_w_a2[layer]).astype(BF16)
        x = hybrid_mixer_ln(
            x, mod, _reorder_w_in(w_in[layer]),
            jnp.stack([cmp_pos_k[layer], cmp_pos_v[layer]]),
            jnp.stack([cmp_w1_k[layer], cmp_w1_v[layer]]).astype(BF16),
            jnp.stack([cmp_w2_k[layer], cmp_w2_v[layer]]).astype(BF16),
            wa_pad, gla_b_a[layer].reshape(1, -1), gla_norm_w[layer].reshape(1, -1),
            w_out[layer].astype(BF16), ln_mix_g[layer].reshape(1, D), ln_mix_b[layer].reshape(1, D),
            tables, overlap)
        lg = ln_ffn_g[layer].reshape(1, D)
        lb = ln_ffn_b[layer].reshape(1, D)
        i = layer // 2
        if layer % 2 == 0:
            x = ffn_ln(x, mod, ffn_w_gate[i], ffn_w_up[i], ffn_w_down[i], lg, lb)
        else:
            x = moe_layer(x, mod, moe_router[i], moe_w_gate[i], moe_w_up[i], moe_w_down[i], lg, lb)
    return x
```
